```python
import jax, jax.numpy as jnp
from jax import lax

D_MODEL = 2048
BATCH = 2
SEQ = 4096
DEPTH = 1

HEAD_DIM = 128
NSA_HEADS = 16
NSA_KV_GROUPS = 2
NSA_HPG = NSA_HEADS // NSA_KV_GROUPS
CMP_BLOCK = 32
CMP_STRIDE = 16
CMP_HIDDEN = 256
SEL_BLOCK = 64
SEL_TOPK = 16
WINDOW = 512
Q_BLOCK = 128
SGU_WIDTH = 2048
SGU_GROUPS = 8
SGU_CHUNK = 128
MEM_LEN = 256
MEM_HEADS = 4
N_GROUPS = 4
EXPERTS_PER_GROUP = 16
N_EXPERTS = N_GROUPS * EXPERTS_PER_GROUP
EXPERT_TOPK = 2
EXPERT_FF = 512
MOE_BLOCK = 128
ROPE_THETA = 10000.0
LN_EPS = 1e-5
SEL_BIG = 1e9
DN_ALPHA = (2.0 * DEPTH) ** 0.25
DN_BETA = (8.0 * DEPTH) ** -0.25

Q_WIDTH = NSA_HEADS * HEAD_DIM
KV_WIDTH = 3 * 2 * NSA_KV_GROUPS * HEAD_DIM
NSA_GATE_WIDTH = 3 * NSA_HEADS
SGU_IN_WIDTH = 2 * SGU_WIDTH
MERGE_WIDTH = 2 * D_MODEL
IN_WIDTH = Q_WIDTH + KV_WIDTH + NSA_GATE_WIDTH + SGU_IN_WIDTH + MERGE_WIDTH
MEM_WIDTH = MEM_HEADS * HEAD_DIM

kernel_name = 'hybrid_nsa_sgu_hmoe_block'


def layer_norm(x, g, b):
    xf = x.astype(jnp.float32)
    mu = jnp.mean(xf, -1, keepdims=True)
    var = jnp.mean(jnp.square(xf - mu), -1, keepdims=True)
    return ((xf - mu) * lax.rsqrt(var + LN_EPS) * g + b).astype(x.dtype)


def rope_tables(pos):
    inv = ROPE_THETA ** (-jnp.arange(0, HEAD_DIM, 2, dtype=jnp.float32) / HEAD_DIM)
    ang = pos.astype(jnp.float32)[:, None] * inv[None, :]
    return jnp.cos(ang), jnp.sin(ang)


def apply_rope(t, cos, sin):
    t1, t2 = jnp.split(t, 2, axis=-1)
    c = cos[None, :, None, :]
    s = sin[None, :, None, :]
    return jnp.concatenate([t1 * c - t2 * s, t1 * s + t2 * c], -1).astype(t.dtype)


def masked_softmax(s, mask):
    s = jnp.where(mask, s.astype(jnp.float32), -jnp.inf)
    m = jnp.max(s, -1, keepdims=True)
    m = jnp.where(jnp.isfinite(m), m, 0.0)
    p = jnp.exp(s - m)
    return p / jnp.maximum(jnp.sum(p, -1, keepdims=True), 1e-30)


def compress_blocks(t, pe, w1, w2):
    B, S, G, D = t.shape
    c = t.reshape(B, S // CMP_STRIDE, CMP_STRIDE, G, D)
    blocks = jnp.concatenate([c[:, :-1], c[:, 1:]], axis=2)
    blocks = blocks + pe[None, None, :, None, :]
    n_cmp = blocks.shape[1]
    flat = jnp.moveaxis(blocks, 3, 2).reshape(B, n_cmp, G, CMP_BLOCK * D)
    return jax.nn.gelu(flat @ w1) @ w2


def nsa_attention(q, k_cmp, v_cmp, cmp_end, k_slc, v_slc, k_win, v_win, gates):
    B, S, H, D = q.shape
    G, HPG = NSA_KV_GROUPS, NSA_HPG
    n_cmp = k_cmp.shape[1]
    n_sel = S // SEL_BLOCK
    n_pick = min(SEL_TOPK, n_sel)
    nq = S // Q_BLOCK
    ci = CMP_STRIDE * jnp.arange(n_cmp)
    sj = SEL_BLOCK * jnp.arange(n_sel)
    overlap = jnp.clip(jnp.minimum(ci[:, None] + CMP_BLOCK, sj[None, :] + SEL_BLOCK)
                       - jnp.maximum(ci[:, None], sj[None, :]), 0, None)
    sel_map = overlap.astype(jnp.float32) / CMP_BLOCK

    def to_blocks(t):
        return t.reshape(B, n_sel, SEL_BLOCK, G, D).transpose(0, 3, 1, 2, 4).reshape(B, G, n_sel, SEL_BLOCK * D)

    ks_b, vs_b = to_blocks(k_slc), to_blocks(v_slc)
    pad = ((0, 0), (WINDOW, 0), (0, 0), (0, 0))
    kw_p, vw_p = jnp.pad(k_win, pad), jnp.pad(v_win, pad)
    q_blk = q.reshape(B, nq, Q_BLOCK, G, HPG, D).swapaxes(0, 1)
    g_blk = gates.reshape(B, nq, Q_BLOCK, G, HPG, 3).swapaxes(0, 1)
    sel_ids = jnp.arange(n_sel)
    win_off = jnp.arange(WINDOW + Q_BLOCK)
    b_ix = jnp.arange(B)[:, None, None]
    g_ix = jnp.arange(G)[None, :, None]

    def one_block(args):
        c, qc, gc = args
        t = c * Q_BLOCK + jnp.arange(Q_BLOCK)
        s_c = jnp.einsum('btgnd,bkgd->btgnk', qc, k_cmp)
        m_c = cmp_end[None, :] <= t[:, None]
        p_c = masked_softmax(s_c, m_c[None, :, None, None, :])
        o_c = jnp.einsum('btgnk,bkgd->btgnd', p_c.astype(v_cmp.dtype), v_cmp)
        imp = jnp.einsum('btgnk,kj->btgj', p_c, sel_map)
        cur = t // SEL_BLOCK
        visible = sel_ids[None, :] <= cur[:, None]
        forced = (sel_ids[None, :] == 0) | (sel_ids[None, :] == cur[:, None]) | (sel_ids[None, :] == cur[:, None] - 1)
        score = jnp.where(forced[None, :, None, :], SEL_BIG,
                          jnp.where(visible[None, :, None, :], imp, -SEL_BIG))
        _, idx = lax.top_k(score, n_pick)
        idx_bg = jnp.moveaxis(idx, 2, 1).reshape(B, G, Q_BLOCK * n_pick)
        k_sel = ks_b[b_ix, g_ix, idx_bg].reshape(B, G, Q_BLOCK, n_pick * SEL_BLOCK, D)
        v_sel = vs_b[b_ix, g_ix, idx_bg].reshape(B, G, Q_BLOCK, n_pick * SEL_BLOCK, D)
        s_s = jnp.einsum('btgnd,bgtkd->btgnk', qc, k_sel)
        key_pos = (idx[..., None] * SEL_BLOCK + jnp.arange(SEL_BLOCK)).reshape(B, Q_BLOCK, G, n_pick * SEL_BLOCK)
        m_s = key_pos <= t[None, :, None, None]
        p_s = masked_softmax(s_s, m_s[:, :, :, None, :])
        o_s = jnp.einsum('btgnk,bgtkd->btgnd', p_s.astype(v_sel.dtype), v_sel)
        kwc = lax.dynamic_slice_in_dim(kw_p, c * Q_BLOCK, WINDOW + Q_BLOCK, axis=1)
        vwc = lax.dynamic_slice_in_dim(vw_p, c * Q_BLOCK, WINDOW + Q_BLOCK, axis=1)
        s_w = jnp.einsum('btgnd,bkgd->btgnk', qc, kwc)
        kp = c * Q_BLOCK - WINDOW + win_off
        m_w = (kp[None, :] >= 0) & (kp[None, :] <= t[:, None]) & (t[:, None] - kp[None, :] < WINDOW)
        p_w = masked_softmax(s_w, m_w[None, :, None, None, :])
        o_w = jnp.einsum('btgnk,bkgd->btgnd', p_w.astype(vwc.dtype), vwc)
        return gc[..., 0:1] * o_c + gc[..., 1:2] * o_s + gc[..., 2:3] * o_w

    out = lax.map(one_block, (jnp.arange(nq, dtype=jnp.int32), q_blk, g_blk))
    return out.swapaxes(0, 1).reshape(B, S, H * D)


def spatial_gating(u, v, ln_g, ln_b, w_s, b_s):
    B, S, _ = v.shape
    u = jax.nn.gelu(u)
    v = layer_norm(jax.nn.gelu(v), ln_g, ln_b)
    vc = v.reshape(B, S // SGU_CHUNK, SGU_CHUNK, SGU_GROUPS, SGU_WIDTH // SGU_GROUPS)
    causal = jnp.tril(jnp.ones((SGU_CHUNK, SGU_CHUNK), dtype=bool))
    w_causal = jnp.where(causal[None], w_s, jnp.zeros_like(w_s))
    z = jnp.einsum('gts,bcsgd->bctgd', w_causal, vc) + b_s.T[None, None, :, :, None]
    return u * z.reshape(B, S, SGU_WIDTH)


def memory_cross_attention(x, mem, w_xq, w_xk, w_xv, w_xo):
    B, S, _ = x.shape
    M = mem.shape[1]
    q = (x @ w_xq).reshape(B, S, MEM_HEADS, HEAD_DIM) * HEAD_DIM ** -0.5
    k = (mem @ w_xk).reshape(B, M, MEM_HEADS, HEAD_DIM)
    v = (mem @ w_xv).reshape(B, M, MEM_HEADS, HEAD_DIM)
    p = jax.nn.softmax(jnp.einsum('bshd,bmhd->bhsm', q, k).astype(jnp.float32), axis=-1)
    o = jnp.einsum('bhsm,bmhd->bshd', p.astype(v.dtype), v).reshape(B, S, MEM_WIDTH)
    return o @ w_xo


def hierarchical_moe(x, w_router_grp, b_router_grp, w_router_exp, b_router_exp, w_exp_gate, w_exp_up, w_exp_down):
    B, S, D = x.shape
    N = B * S
    xf = x.reshape(N, D)
    p_grp = jax.nn.softmax((xf @ w_router_grp).astype(jnp.float32) + b_router_grp, axis=-1)
    g_grp, grp = lax.top_k(p_grp, 1)
    logit_exp = ((xf @ w_router_exp).astype(jnp.float32) + b_router_exp).reshape(N, N_GROUPS, EXPERTS_PER_GROUP)
    logit_in = logit_exp[jnp.arange(N), grp[:, 0]]
    g_in, local = lax.top_k(jax.nn.softmax(logit_in, axis=-1), EXPERT_TOPK)
    gate = g_grp * g_in / jnp.sum(g_in, -1, keepdims=True)
    expert = grp * EXPERTS_PER_GROUP + local
    NK = N * EXPERT_TOPK
    e_flat = expert.reshape(NK)
    tok_flat = jnp.repeat(jnp.arange(N, dtype=jnp.int32), EXPERT_TOPK)
    w_flat = gate.reshape(NK)
    order = jnp.argsort(e_flat)
    e_s = e_flat[order]
    counts = jnp.bincount(e_flat, length=N_EXPERTS)
    start = jnp.cumsum(counts) - counts
    pcounts = ((counts + MOE_BLOCK - 1) // MOE_BLOCK) * MOE_BLOCK
    pend = jnp.cumsum(pcounts)
    pstart = pend - pcounts
    dest = pstart[e_s] + (jnp.arange(NK) - start[e_s])
    n_blocks = -(-NK // MOE_BLOCK) + N_EXPERTS
    P = n_blocks * MOE_BLOCK
    row_tok = jnp.full((P,), N, dtype=jnp.int32).at[dest].set(tok_flat[order])
    row_w = jnp.zeros((P,), jnp.float32).at[dest].set(w_flat[order])
    blk_exp = jnp.minimum(jnp.searchsorted(pend, jnp.arange(n_blocks) * MOE_BLOCK, side='right'), N_EXPERTS - 1)
    x_pad = jnp.concatenate([xf, jnp.zeros((1, D), xf.dtype)], axis=0)

    def block_ffn(args):
        tok, wr, e = args
        h = x_pad[tok]
        a = jax.nn.silu(h @ w_exp_gate[e]) * (h @ w_exp_up[e])
        return (a @ w_exp_down[e]) * wr[:, None].astype(h.dtype)

    y_rows = lax.map(block_ffn, (row_tok.reshape(n_blocks, MOE_BLOCK), row_w.reshape(n_blocks, MOE_BLOCK), blk_exp))
    out = jnp.zeros((N + 1, D), x.dtype).at[row_tok].add(y_rows.reshape(P, D))[:N]
    return out.reshape(B, S, D)


def hybrid_layer(x, mem, w_in, cmp_pe_k, cmp_w1_k, cmp_w2_k, cmp_pe_v, cmp_w1_v, cmp_w2_v,
                 sgu_ln_g, sgu_ln_b, sgu_w_s, sgu_b_s, w_branch_a, w_branch_b, w_o, ln1_g, ln1_b,
                 w_xq, w_xk, w_xv, w_xo, ln2_g, ln2_b,
                 w_router_grp, b_router_grp, w_router_exp, b_router_exp,
                 w_exp_gate, w_exp_up, w_exp_down, ln3_g, ln3_b):
    B, S, _ = x.shape
    cuts = [Q_WIDTH, Q_WIDTH + KV_WIDTH, Q_WIDTH + KV_WIDTH + NSA_GATE_WIDTH,
            Q_WIDTH + KV_WIDTH + NSA_GATE_WIDTH + SGU_WIDTH,
            Q_WIDTH + KV_WIDTH + NSA_GATE_WIDTH + SGU_IN_WIDTH]
    q, kv, g_nsa, u, v, g_merge = jnp.split(x @ w_in, cuts, axis=-1)
    cos, sin = rope_tables(jnp.arange(S, dtype=jnp.int32))
    q = apply_rope(q.reshape(B, S, NSA_HEADS, HEAD_DIM), cos, sin) * HEAD_DIM ** -0.5
    kv = kv.reshape(B, S, 3, 2, NSA_KV_GROUPS, HEAD_DIM)
    k_cmp = compress_blocks(kv[:, :, 0, 0], cmp_pe_k, cmp_w1_k, cmp_w2_k)
    v_cmp = compress_blocks(kv[:, :, 0, 1], cmp_pe_v, cmp_w1_v, cmp_w2_v)
    cmp_end = CMP_STRIDE * jnp.arange(k_cmp.shape[1], dtype=jnp.int32) + CMP_BLOCK - 1
    ccos, csin = rope_tables(cmp_end)
    k_cmp = apply_rope(k_cmp, ccos, csin)
    k_slc = apply_rope(kv[:, :, 1, 0], cos, sin)
    k_win = apply_rope(kv[:, :, 2, 0], cos, sin)
    o_nsa = nsa_attention(q, k_cmp, v_cmp, cmp_end, k_slc, kv[:, :, 1, 1], k_win, kv[:, :, 2, 1],
                          jax.nn.sigmoid(g_nsa.reshape(B, S, NSA_HEADS, 3)))
    o_sgu = spatial_gating(u, v, sgu_ln_g, sgu_ln_b, sgu_w_s, sgu_b_s)
    g_a, g_b = jnp.split(jax.nn.sigmoid(g_merge), 2, axis=-1)
    merged = g_a * (o_nsa @ w_branch_a) + g_b * (o_sgu @ w_branch_b)
    x = layer_norm(DN_ALPHA * x + merged @ w_o, ln1_g, ln1_b)
    x = layer_norm(DN_ALPHA * x + memory_cross_attention(x, mem, w_xq, w_xk, w_xv, w_xo), ln2_g, ln2_b)
    moe = hierarchical_moe(x, w_router_grp, b_router_grp, w_router_exp, b_router_exp, w_exp_gate, w_exp_up, w_exp_down)
    return layer_norm(DN_ALPHA * x + moe, ln3_g, ln3_b)


def setup_inputs(seed: int = 0) -> dict:
    key = jax.random.key(seed)
    ks = jax.random.split(key, 40)
    L, D = DEPTH, D_MODEL

    def nrm(k, shape, scale):
        return jax.random.normal(k, shape, jnp.float32) * scale

    return {
        'x': nrm(ks[0], (BATCH, SEQ, D), 1.0),
        'mem': nrm(ks[1], (BATCH, MEM_LEN, D), 1.0),
        'w_in': nrm(ks[2], (L, D, IN_WIDTH), D ** -0.5),
        'cmp_pe_k': nrm(ks[3], (L, CMP_BLOCK, HEAD_DIM), 0.1),
        'cmp_w1_k': nrm(ks[4], (L, CMP_BLOCK * HEAD_DIM, CMP_HIDDEN), (CMP_BLOCK * HEAD_DIM) ** -0.5),
        'cmp_w2_k': nrm(ks[5], (L, CMP_HIDDEN, HEAD_DIM), CMP_HIDDEN ** -0.5),
        'cmp_pe_v': nrm(ks[6], (L, CMP_BLOCK, HEAD_DIM), 0.1),
        'cmp_w1_v': nrm(ks[7], (L, CMP_BLOCK * HEAD_DIM, CMP_HIDDEN), (CMP_BLOCK * HEAD_DIM) ** -0.5),
        'cmp_w2_v': nrm(ks[8], (L, CMP_HIDDEN, HEAD_DIM), CMP_HIDDEN ** -0.5),
        'sgu_ln_g': 1.0 + nrm(ks[9], (L, SGU_WIDTH), 0.02),
        'sgu_ln_b': nrm(ks[10], (L, SGU_WIDTH), 0.02),
        'sgu_w_s': nrm(ks[11], (L, SGU_GROUPS, SGU_CHUNK, SGU_CHUNK), SGU_CHUNK ** -0.5),
        'sgu_b_s': 1.0 + nrm(ks[12], (L, SGU_GROUPS, SGU_CHUNK), 0.1),
        'w_branch_a': nrm(ks[13], (L, Q_WIDTH, D), Q_WIDTH ** -0.5),
        'w_branch_b': nrm(ks[14], (L, SGU_WIDTH, D), SGU_WIDTH ** -0.5),
        'w_o': nrm(ks[15], (L, D, D), DN_BETA * D ** -0.5),
        'ln1_g': 1.0 + nrm(ks[16], (L, D), 0.02),
        'ln1_b': nrm(ks[17], (L, D), 0.02),
        'w_xq': nrm(ks[18], (L, D, MEM_WIDTH), D ** -0.5),
        'w_xk': nrm(ks[19], (L, D, MEM_WIDTH), D ** -0.5),
        'w_xv': nrm(ks[20], (L, D, MEM_WIDTH), D ** -0.5),
        'w_xo': nrm(ks[21], (L, MEM_WIDTH, D), DN_BETA * MEM_WIDTH ** -0.5),
        'ln2_g': 1.0 + nrm(ks[22], (L, D), 0.02),
        'ln2_b': nrm(ks[23], (L, D), 0.02),
        'w_router_grp': nrm(ks[24], (L, D, N_GROUPS), D ** -0.5),
        'b_router_grp': nrm(ks[25], (L, N_GROUPS), 0.01),
        'w_router_exp': nrm(ks[26], (L, D, N_EXPERTS), D ** -0.5),
        'b_router_exp': nrm(ks[27], (L, N_EXPERTS), 0.01),
        'w_exp_gate': nrm(ks[28], (L, N_EXPERTS, D, EXPERT_FF), D ** -0.5),
        'w_exp_up': nrm(ks[29], (L, N_EXPERTS, D, EXPERT_FF), D ** -0.5),
        'w_exp_down': nrm(ks[30], (L, N_EXPERTS, EXPERT_FF, D), DN_BETA * EXPERT_FF ** -0.5),
        'ln3_g': 1.0 + nrm(ks[31], (L, D), 0.02),
        'ln3_b': nrm(ks[32], (L, D), 0.02),
    }


def reference(x, mem, w_in, cmp_pe_k, cmp_w1_k, cmp_w2_k, cmp_pe_v, cmp_w1_v, cmp_w2_v,
              sgu_ln_g, sgu_ln_b, sgu_w_s, sgu_b_s, w_branch_a, w_branch_b, w_o, ln1_g, ln1_b,
              w_xq, w_xk, w_xv, w_xo, ln2_g, ln2_b,
              w_router_grp, b_router_grp, w_router_exp, b_router_exp,
              w_exp_gate, w_exp_up, w_exp_down, ln3_g, ln3_b):
    layer_params = (w_in, cmp_pe_k, cmp_w1_k, cmp_w2_k, cmp_pe_v, cmp_w1_v, cmp_w2_v,
                    sgu_ln_g, sgu_ln_b, sgu_w_s, sgu_b_s, w_branch_a, w_branch_b, w_o, ln1_g, ln1_b,
                    w_xq, w_xk, w_xv, w_xo, ln2_g, ln2_b,
                    w_router_grp, b_router_grp, w_router_exp, b_router_exp,
                    w_exp_gate, w_exp_up, w_exp_down, ln3_g, ln3_b)
    h = x
    for l in range(DEPTH):
        h = hybrid_layer(h, mem, *[p[l] for p in layer_params])
    return h
```

```python
import functools
import math

import jax
import jax.numpy as jnp
from jax import lax
from jax.experimental import pallas as pl
from jax.experimental.pallas import tpu as pltpu

D_MODEL = 2048
HEAD_DIM = 128
NSA_HEADS = 16
NSA_KV_GROUPS = 2
NSA_HPG = NSA_HEADS // NSA_KV_GROUPS
CMP_BLOCK = 32
CMP_STRIDE = 16
CMP_HIDDEN = 256
SEL_BLOCK = 64
SEL_TOPK = 16
WINDOW = 512
Q_BLOCK = 128
SGU_WIDTH = 2048
SGU_GROUPS = 8
SGU_CHUNK = 128
MEM_HEADS = 4
N_GROUPS = 4
EXPERTS_PER_GROUP = 16
N_EXPERTS = N_GROUPS * EXPERTS_PER_GROUP
EXPERT_TOPK = 2
EXPERT_FF = 512
MOE_BLOCK = 128
ROPE_THETA = 10000.0
LN_EPS = 1e-5
SEL_BIG = 1e9
DEPTH = 1
DN_ALPHA = (2.0 * DEPTH) ** 0.25

Q_WIDTH = NSA_HEADS * HEAD_DIM
KV_WIDTH = 3 * 2 * NSA_KV_GROUPS * HEAD_DIM
NSA_GATE_WIDTH = 3 * NSA_HEADS
MEM_WIDTH = MEM_HEADS * HEAD_DIM

LANES = 128
VMEM_LIMIT = 56 * 1024 * 1024
MASKED = -1e30
M_INIT = -1e20
BF16 = jnp.bfloat16
F32 = jnp.float32


def _params(*sem):
    return pltpu.CompilerParams(dimension_semantics=sem, vmem_limit_bytes=VMEM_LIMIT)


def _dot(a, b):
    return jnp.dot(a, b, preferred_element_type=F32)


def _dot_nt(a, b):
    return lax.dot_general(a, b, (((1,), (1,)), ((), ())), preferred_element_type=F32)


def _gelu(x):
    return 0.5 * x * (1.0 + jnp.tanh(math.sqrt(2.0 / math.pi) * (x + 0.044715 * (x * x * x))))


def _sigmoid(x):
    return 1.0 / (1.0 + jnp.exp(-x))


def _layer_norm(x, g, b):
    mu = jnp.mean(x, -1, keepdims=True)
    xc = x - mu
    var = jnp.mean(xc * xc, -1, keepdims=True)
    return xc * lax.rsqrt(var + LN_EPS) * g + b


def _rope(t, cos_full, sin_signed):
    return t * cos_full + pltpu.roll(t, HEAD_DIM // 2, axis=1) * sin_signed


def _rope_tables(pos):
    inv = ROPE_THETA ** (-jnp.arange(0, HEAD_DIM, 2, dtype=F32) / HEAD_DIM)
    ang = pos.astype(F32)[:, None] * inv[None, :]
    c, s = jnp.cos(ang), jnp.sin(ang)
    return jnp.concatenate([c, c], -1), jnp.concatenate([-s, s], -1)


def _mm_kernel(a_ref, b_ref, o_ref, *, act):
    y = _dot(a_ref[...], b_ref[...])
    if act == "gelu":
        y = _gelu(y)
    o_ref[...] = y.astype(o_ref.dtype)


def _matmul(a, b, *, tm, tn, out_dtype, act=None):
    m, k = a.shape
    n = b.shape[1]
    return pl.pallas_call(
        functools.partial(_mm_kernel, act=act),
        grid=(m // tm, n // tn),
        in_specs=[pl.BlockSpec((tm, k), lambda i, j: (i, 0)),
                  pl.BlockSpec((k, tn), lambda i, j: (0, j))],
        out_specs=pl.BlockSpec((tm, tn), lambda i, j: (i, j)),
        out_shape=jax.ShapeDtypeStruct((m, n), out_dtype),
        compiler_params=_params("parallel", "parallel"),
        name="matmul_" + (act or "plain"),
    )(a, b)


def _qproj_kernel(a_ref, b_ref, cos_ref, sin_ref, o_ref):
    y = _dot(a_ref[...], b_ref[...])
    cos, sin = cos_ref[...], sin_ref[...]
    scale = HEAD_DIM ** -0.5
    for h in range(y.shape[1] // HEAD_DIM):
        sl = slice(h * HEAD_DIM, (h + 1) * HEAD_DIM)
        o_ref[:, sl] = (_rope(y[:, sl], cos, sin) * scale).astype(o_ref.dtype)


def _q_proj(x_bf, w_q, cos, sin, seq, *, tm=512, tn=512):
    m, k = x_bf.shape
    n = w_q.shape[1]
    per_seq = seq // tm
    return pl.pallas_call(
        _qproj_kernel,
        grid=(m // tm, n // tn),
        in_specs=[pl.BlockSpec((tm, k), lambda i, j: (i, 0)),
                  pl.BlockSpec((k, tn), lambda i, j: (0, j)),
                  pl.BlockSpec((tm, HEAD_DIM), lambda i, j: (i % per_seq, 0)),
                  pl.BlockSpec((tm, HEAD_DIM), lambda i, j: (i % per_seq, 0))],
        out_specs=pl.BlockSpec((tm, tn), lambda i, j: (i, j)),
        out_shape=jax.ShapeDtypeStruct((m, n), BF16),
        compiler_params=_params("parallel", "parallel"),
        name="q_proj",
    )(x_bf, w_q, cos, sin)


_N_CMP_COLS = 2 * NSA_KV_GROUPS * HEAD_DIM
_N_REST_COLS = KV_WIDTH - _N_CMP_COLS
_GATE_COLS = NSA_KV_GROUPS * LANES


def _kvproj_kernel(a_ref, b_ref, cos_ref, sin_ref, cmp_ref, rest_ref, gate_ref):
    y = _dot(a_ref[...], b_ref[...])
    cos, sin = cos_ref[...], sin_ref[...]
    cmp_ref[...] = y[:, :_N_CMP_COLS]
    n_rest = _N_REST_COLS // HEAD_DIM
    for c in range(n_rest):
        t = y[:, _N_CMP_COLS + c * HEAD_DIM:_N_CMP_COLS + (c + 1) * HEAD_DIM]
        is_key = (c % (2 * NSA_KV_GROUPS)) < NSA_KV_GROUPS
        if is_key:
            t = _rope(t, cos, sin)
        rest_ref[:, c * HEAD_DIM:(c + 1) * HEAD_DIM] = t.astype(rest_ref.dtype)
    gate_ref[...] = _sigmoid(y[:, KV_WIDTH:])


def _kv_proj(x_bf, w_kvg, cos, sin, seq, *, tm=512):
    m, k = x_bf.shape
    n = w_kvg.shape[1]
    per_seq = seq // tm
    return pl.pallas_call(
        _kvproj_kernel,
        grid=(m // tm,),
        in_specs=[pl.BlockSpec((tm, k), lambda i: (i, 0)),
                  pl.BlockSpec((k, n), lambda i: (0, 0)),
                  pl.BlockSpec((tm, HEAD_DIM), lambda i: (i % per_seq, 0)),
                  pl.BlockSpec((tm, HEAD_DIM), lambda i: (i % per_seq, 0))],
        out_specs=[pl.BlockSpec((tm, _N_CMP_COLS), lambda i: (i, 0)),
                   pl.BlockSpec((tm, _N_REST_COLS), lambda i: (i, 0)),
                   pl.BlockSpec((tm, _GATE_COLS), lambda i: (i, 0))],
        out_shape=[jax.ShapeDtypeStruct((m, _N_CMP_COLS), F32),
                   jax.ShapeDtypeStruct((m, _N_REST_COLS), BF16),
                   jax.ShapeDtypeStruct((m, _GATE_COLS), F32)],
        compiler_params=_params("parallel"),
        name="kv_gate_proj",
    )(x_bf, w_kvg, cos, sin)


def _compress_kernel(kc_ref, pe_ref, w1_ref, w2_ref, cos_ref, sin_ref, o_ref):
    half = CMP_STRIDE * HEAD_DIM
    kc = kc_ref[0, 0]
    pe = pe_ref[0]
    top = _dot((kc + pe[:, :half]).astype(BF16), w1_ref[0, :half, :])
    bot = _dot((kc + pe[:, half:]).astype(BF16), w1_ref[0, half:, :])
    n_chunks = kc.shape[0]
    h = top + pltpu.roll(bot, n_chunks - 1, axis=0)
    y = _dot(_gelu(h).astype(BF16), w2_ref[0])
    o_ref[0, 0] = _rope(y, cos_ref[0], sin_ref[0]).astype(o_ref.dtype)


def _compress(kc, pe, w1, w2, cos, sin):
    four, bsz, n_chunks, width = kc.shape
    g = NSA_KV_GROUPS
    return pl.pallas_call(
        _compress_kernel,
        grid=(four, bsz),
        in_specs=[pl.BlockSpec((1, 1, n_chunks, width), lambda a, b: (a, b, 0, 0)),
                  pl.BlockSpec((1, 1, 2 * width), lambda a, b: (a // g, 0, 0)),
                  pl.BlockSpec((1, 2 * width, CMP_HIDDEN), lambda a, b: (a // g, 0, 0)),
                  pl.BlockSpec((1, CMP_HIDDEN, HEAD_DIM), lambda a, b: (a // g, 0, 0)),
                  pl.BlockSpec((1, n_chunks, HEAD_DIM), lambda a, b: (a // g, 0, 0)),
                  pl.BlockSpec((1, n_chunks, HEAD_DIM), lambda a, b: (a // g, 0, 0))],
        out_specs=pl.BlockSpec((1, 1, n_chunks, HEAD_DIM), lambda a, b: (a, b, 0, 0)),
        out_shape=jax.ShapeDtypeStruct((four, bsz, n_chunks, HEAD_DIM), BF16),
        compiler_params=_params("parallel", "parallel"),
        name="compress",
    )(kc, pe, w1, w2, cos, sin)


SEL_TILE = 512
_SEL_SHIFT = SEL_BLOCK.bit_length() - 1
assert 1 << _SEL_SHIFT == SEL_BLOCK


def _softmax_rows(s3, mask):
    s3 = jnp.where(mask[None], s3, MASKED)
    m = jnp.max(s3, -1, keepdims=True)
    m = jnp.where(m > 0.5 * MASKED, m, 0.0)
    p = jnp.exp(s3 - m)
    p = jnp.where(mask[None], p, 0.0)
    return p / jnp.maximum(jnp.sum(p, -1, keepdims=True), 1e-30)


def _nsa_kernel(q_ref, kc_ref, vc_ref, ks_ref, vs_ref, kw_ref, vw_ref, gate_ref, selmap_ref, o_ref,
                *, n_sel, n_pick):
    hpg = NSA_HPG
    tq = Q_BLOCK
    c = pl.program_id(2)
    q_t = q_ref[0]
    q8 = jnp.concatenate([q_t[:, n * HEAD_DIM:(n + 1) * HEAD_DIM] for n in range(hpg)], axis=0)
    t_pos = c * tq + lax.broadcasted_iota(jnp.int32, (tq, 1), 0)

    kc = kc_ref[0, 0]
    n_cmp_pad = kc.shape[0]
    k_idx = lax.broadcasted_iota(jnp.int32, (1, n_cmp_pad), 1)
    m_c = (CMP_STRIDE * k_idx + (CMP_BLOCK - 1) <= t_pos) & (k_idx < n_cmp_pad - 1)
    s_c = _dot_nt(q8, kc).reshape(hpg, tq, n_cmp_pad)
    p_c = _softmax_rows(s_c, m_c)
    o_c = _dot(p_c.reshape(hpg * tq, n_cmp_pad).astype(BF16), vc_ref[0, 0])

    p_sum = jnp.sum(p_c, axis=0)
    p_hi = p_sum.astype(BF16)
    p_lo = (p_sum - p_hi.astype(F32)).astype(BF16)
    selmap = selmap_ref[...]
    imp = _dot(p_hi, selmap) + _dot(p_lo, selmap)
    lane = lax.broadcasted_iota(jnp.int32, (tq, LANES), 1)
    cur = t_pos >> _SEL_SHIFT
    forced = (lane == 0) | (lane == cur) | (lane == cur - 1)
    score = jnp.where(forced, SEL_BIG, jnp.where(lane <= cur, imp, -SEL_BIG))
    score = jnp.where(lane < n_sel, score, -3e38)
    lane_f = lane.astype(F32)
    sel = jnp.zeros((tq, LANES), F32)
    for _ in range(n_pick):
        mx = jnp.max(score, axis=1, keepdims=True)
        first = jnp.min(jnp.where(score == mx, lane_f, 1e9), axis=1, keepdims=True)
        pick = lane_f == first
        sel = jnp.where(pick, 1.0, sel)
        score = jnp.where(pick, -3e38, score)
    sel_bf = sel.astype(BF16)

    blk_row = lax.broadcasted_iota(jnp.int32, (LANES, SEL_TILE), 0)
    key_col = lax.broadcasted_iota(jnp.int32, (LANES, SEL_TILE), 1)
    key_row = lax.broadcasted_iota(jnp.int32, (1, SEL_TILE), 1)

    def sel_step(kt, carry):
        m_old, l_old, acc = carry
        base = pl.multiple_of(kt * SEL_TILE, SEL_TILE)
        k = ks_ref[0, pl.ds(base, SEL_TILE), :]
        v = vs_ref[0, pl.ds(base, SEL_TILE), :]
        expand = jnp.where(blk_row == ((base + key_col) >> _SEL_SHIFT), 1.0, 0.0).astype(BF16)
        picked = _dot(sel_bf, expand)
        mask = (picked > 0.5) & (base + key_row <= t_pos)
        s = _dot_nt(q8, k).reshape(hpg, tq, SEL_TILE)
        s = jnp.where(mask[None], s, MASKED)
        m_new = jnp.maximum(m_old, jnp.max(s, -1, keepdims=True))
        alpha = jnp.exp(m_old - m_new)
        p = jnp.exp(s - m_new)
        l_new = alpha * l_old + jnp.sum(p, -1, keepdims=True)
        pv = _dot(p.reshape(hpg * tq, SEL_TILE).astype(BF16), v).reshape(hpg, tq, HEAD_DIM)
        return m_new, l_new, alpha * acc + pv

    n_tiles = ((c + 1) * tq + SEL_TILE - 1) // SEL_TILE
    init = (jnp.full((hpg, tq, 1), M_INIT, F32), jnp.zeros((hpg, tq, 1), F32),
            jnp.zeros((hpg, tq, HEAD_DIM), F32))
    _, l_s, acc_s = lax.fori_loop(0, n_tiles, sel_step, init)
    o_s = (acc_s / jnp.maximum(l_s, 1e-30)).reshape(hpg * tq, HEAD_DIM)

    span = WINDOW + tq
    start = pl.multiple_of(jnp.maximum(c * tq - WINDOW, 0), tq)
    kw = kw_ref[0, pl.ds(start, span), :]
    vw = vw_ref[0, pl.ds(start, span), :]
    kp = start + lax.broadcasted_iota(jnp.int32, (1, span), 1)
    m_w = (kp <= t_pos) & (t_pos - kp < WINDOW)
    p_w = _softmax_rows(_dot_nt(q8, kw).reshape(hpg, tq, span), m_w)
    o_w = _dot(p_w.reshape(hpg * tq, span).astype(BF16), vw)

    gates = gate_ref[0]
    for n in range(hpg):
        rows = slice(n * tq, (n + 1) * tq)
        out = (gates[:, n:n + 1] * o_c[rows]
               + gates[:, hpg + n:hpg + n + 1] * o_s[rows]
               + gates[:, 2 * hpg + n:2 * hpg + n + 1] * o_w[rows])
        o_ref[0, :, n * HEAD_DIM:(n + 1) * HEAD_DIM] = out.astype(o_ref.dtype)


def _nsa(q, cmp_kv, rest, gates, selmap, *, n_sel, n_pick):
    bsz, seq, _ = q.shape
    g = NSA_KV_GROUPS
    n_cmp_pad = cmp_kv.shape[2]
    gw = NSA_HPG * HEAD_DIM
    kv_spec = lambda off: pl.BlockSpec((1, seq, HEAD_DIM), lambda b, gi, c: (b, 0, off + gi))
    return pl.pallas_call(
        functools.partial(_nsa_kernel, n_sel=n_sel, n_pick=n_pick),
        grid=(bsz, g, seq // Q_BLOCK),
        in_specs=[pl.BlockSpec((1, Q_BLOCK, gw), lambda b, gi, c: (b, c, gi)),
                  pl.BlockSpec((1, 1, n_cmp_pad, HEAD_DIM), lambda b, gi, c: (gi, b, 0, 0)),
                  pl.BlockSpec((1, 1, n_cmp_pad, HEAD_DIM), lambda b, gi, c: (g + gi, b, 0, 0)),
                  kv_spec(0), kv_spec(g), kv_spec(2 * g), kv_spec(3 * g),
                  pl.BlockSpec((1, Q_BLOCK, LANES), lambda b, gi, c: (b, c, gi)),
                  pl.BlockSpec(selmap.shape, lambda b, gi, c: (0, 0))],
        out_specs=pl.BlockSpec((1, Q_BLOCK, gw), lambda b, gi, c: (b, c, gi)),
        out_shape=jax.ShapeDtypeStruct(q.shape, BF16),
        compiler_params=_params("parallel", "parallel", "arbitrary"),
        name="nsa_attention",
    )(q, cmp_kv, cmp_kv, rest, rest, rest, rest, gates, selmap)


def _sgu_kernel(u_ref, v_ref, g_ref, b_ref, ws_ref, bs_ref, o_ref, *, chunks):
    gd = SGU_WIDTH // SGU_GROUPS
    row = lax.broadcasted_iota(jnp.int32, (SGU_CHUNK, SGU_CHUNK), 0)
    col = lax.broadcasted_iota(jnp.int32, (SGU_CHUNK, SGU_CHUNK), 1)
    causal = col <= row
    bs = bs_ref[...]
    for ch in range(chunks):
        rows = slice(ch * SGU_CHUNK, (ch + 1) * SGU_CHUNK)
        vn = _layer_norm(v_ref[rows, :], g_ref[...], b_ref[...]).astype(BF16)
        for gi in range(SGU_GROUPS):
            cols = slice(gi * gd, (gi + 1) * gd)
            w = jnp.where(causal, ws_ref[gi], 0.0).astype(BF16)
            z = _dot(w, vn[:, cols]) + bs[:, gi:gi + 1]
            o_ref[rows, cols] = (u_ref[rows, cols] * z).astype(o_ref.dtype)


def _sgu(uv, ln_g, ln_b, w_s, b_s_t, *, chunks=2):
    m = uv.shape[0]
    tm = chunks * SGU_CHUNK
    return pl.pallas_call(
        functools.partial(_sgu_kernel, chunks=chunks),
        grid=(m // tm,),
        in_specs=[pl.BlockSpec((tm, SGU_WIDTH), lambda i: (i, 0)),
                  pl.BlockSpec((tm, SGU_WIDTH), lambda i: (i, 1)),
                  pl.BlockSpec((1, SGU_WIDTH), lambda i: (0, 0)),
                  pl.BlockSpec((1, SGU_WIDTH), lambda i: (0, 0)),
                  pl.BlockSpec(w_s.shape, lambda i: (0, 0, 0)),
                  pl.BlockSpec(b_s_t.shape, lambda i: (0, 0))],
        out_specs=pl.BlockSpec((tm, SGU_WIDTH), lambda i: (i, 0)),
        out_shape=jax.ShapeDtypeStruct((m, SGU_WIDTH), BF16),
        compiler_params=_params("parallel"),
        name="sgu",
    )(uv, uv, ln_g, ln_b, w_s, b_s_t)


def _merge_kernel(x_ref, a_ref, s_ref, wga_ref, wgb_ref, wa_ref, wb_ref, o_ref):
    x = x_ref[...]
    ga = _sigmoid(_dot(x, wga_ref[...]))
    gb = _sigmoid(_dot(x, wgb_ref[...]))
    o_ref[...] = (ga * _dot(a_ref[...], wa_ref[...]) + gb * _dot(s_ref[...], wb_ref[...])).astype(o_ref.dtype)


def _merge(x_bf, o_nsa, o_sgu, w_merge, w_a, w_b, *, tm=1024, tn=256):
    m, k = x_bf.shape
    n = w_a.shape[1]
    nb = n // tn
    row = pl.BlockSpec((tm, k), lambda i, j: (i, 0))
    col = pl.BlockSpec((k, tn), lambda i, j: (0, j))
    return pl.pallas_call(
        _merge_kernel,
        grid=(m // tm, nb),
        in_specs=[row, row, row, col, pl.BlockSpec((k, tn), lambda i, j: (0, nb + j)), col, col],
        out_specs=pl.BlockSpec((tm, tn), lambda i, j: (i, j)),
        out_shape=jax.ShapeDtypeStruct((m, n), BF16),
        compiler_params=_params("parallel", "parallel"),
        name="merge",
    )(x_bf, o_nsa, o_sgu, w_merge, w_merge, w_a, w_b)


def _proj_ln_kernel(a_ref, w_ref, x_ref, g_ref, b_ref, o_ref, obf_ref):
    y = DN_ALPHA * x_ref[...] + _dot(a_ref[...], w_ref[...])
    out = _layer_norm(y, g_ref[...], b_ref[...])
    o_ref[...] = out
    obf_ref[...] = out.astype(BF16)


def _proj_ln(a_bf, w, x, ln_g, ln_b, *, tm=256):
    m, k = a_bf.shape
    n = w.shape[1]
    vec = pl.BlockSpec((1, n), lambda i: (0, 0))
    return pl.pallas_call(
        _proj_ln_kernel,
        grid=(m // tm,),
        in_specs=[pl.BlockSpec((tm, k), lambda i: (i, 0)),
                  pl.BlockSpec((k, n), lambda i: (0, 0)),
                  pl.BlockSpec((tm, n), lambda i: (i, 0)), vec, vec],
        out_specs=[pl.BlockSpec((tm, n), lambda i: (i, 0)), pl.BlockSpec((tm, n), lambda i: (i, 0))],
        out_shape=[jax.ShapeDtypeStruct((m, n), F32), jax.ShapeDtypeStruct((m, n), BF16)],
        compiler_params=_params("parallel"),
        name="proj_ln",
    )(a_bf, w, x, ln_g, ln_b)


def _xattn_kernel(xbf_ref, x_ref, wq_ref, kv_ref, wo_ref, g_ref, b_ref, o_ref):
    q = (_dot(xbf_ref[0], wq_ref[...]) * HEAD_DIM ** -0.5).astype(BF16)
    kv = kv_ref[0]
    heads = []
    for h in range(MEM_HEADS):
        cols = slice(h * HEAD_DIM, (h + 1) * HEAD_DIM)
        s = _dot_nt(q[:, cols], kv[:, cols])
        p = jnp.exp(s - jnp.max(s, -1, keepdims=True))
        p = p / jnp.sum(p, -1, keepdims=True)
        heads.append(_dot(p.astype(BF16), kv[:, MEM_WIDTH + h * HEAD_DIM:MEM_WIDTH + (h + 1) * HEAD_DIM]))
    o = jnp.concatenate(heads, axis=1).astype(BF16)
    y = DN_ALPHA * x_ref[0] + _dot(o, wo_ref[...])
    o_ref[0] = _layer_norm(y, g_ref[...], b_ref[...])


def _xattn(x_bf, x, w_xq, mem_kv, w_xo, ln_g, ln_b, *, tm=256):
    bsz, seq, d = x.shape
    mem_len = mem_kv.shape[1]
    vec = pl.BlockSpec((1, d), lambda b, i: (0, 0))
    return pl.pallas_call(
        _xattn_kernel,
        grid=(bsz, seq // tm),
        in_specs=[pl.BlockSpec((1, tm, d), lambda b, i: (b, i, 0)),
                  pl.BlockSpec((1, tm, d), lambda b, i: (b, i, 0)),
                  pl.BlockSpec(w_xq.shape, lambda b, i: (0, 0)),
                  pl.BlockSpec((1, mem_len, 2 * MEM_WIDTH), lambda b, i: (b, 0, 0)),
                  pl.BlockSpec(w_xo.shape, lambda b, i: (0, 0)), vec, vec],
        out_specs=pl.BlockSpec((1, tm, d), lambda b, i: (b, i, 0)),
        out_shape=jax.ShapeDtypeStruct(x.shape, F32),
        compiler_params=_params("parallel", "parallel"),
        name="xattn_ln",
    )(x_bf, x, w_xq, mem_kv, w_xo, ln_g, ln_b)


def _router_kernel(x_ref, whi_ref, wlo_ref, b_ref, gate_ref, exp_ref):
    x = x_ref[...]
    x_hi = x.astype(BF16)
    x_lo = (x - x_hi.astype(F32)).astype(BF16)
    whi = whi_ref[...]
    logits = _dot(x_hi, whi) + _dot(x_hi, wlo_ref[...]) + _dot(x_lo, whi) + b_ref[...]
    rows = logits.shape[0]
    lane = lax.broadcasted_iota(jnp.int32, (rows, LANES), 1)
    lane_f = lane.astype(F32)
    is_grp = lane < N_GROUPS
    lg = jnp.where(is_grp, logits, MASKED)
    eg = jnp.where(is_grp, jnp.exp(lg - jnp.max(lg, -1, keepdims=True)), 0.0)
    p_grp = eg / jnp.sum(eg, -1, keepdims=True)
    g_grp = jnp.max(p_grp, -1, keepdims=True)
    grp = jnp.min(jnp.where((p_grp == g_grp) & is_grp, lane_f, 1e9), -1, keepdims=True).astype(jnp.int32)
    lo = N_GROUPS + grp * EXPERTS_PER_GROUP
    in_grp = (lane >= lo) & (lane < lo + EXPERTS_PER_GROUP)
    le = jnp.where(in_grp, logits, MASKED)
    ee = jnp.where(in_grp, jnp.exp(le - jnp.max(le, -1, keepdims=True)), 0.0)
    p_exp = jnp.where(in_grp, ee / jnp.sum(ee, -1, keepdims=True), -1.0)
    v1 = jnp.max(p_exp, -1, keepdims=True)
    i1 = jnp.min(jnp.where(p_exp == v1, lane_f, 1e9), -1, keepdims=True)
    p_rest = jnp.where(lane_f == i1, -1.0, p_exp)
    v2 = jnp.max(p_rest, -1, keepdims=True)
    i2 = jnp.min(jnp.where(p_rest == v2, lane_f, 1e9), -1, keepdims=True)
    denom = v1 + v2
    gate_ref[...] = jnp.where(lane == 0, g_grp * v1 / denom, jnp.where(lane == 1, g_grp * v2 / denom, 0.0))
    exp_ref[...] = jnp.where(lane == 0, i1, jnp.where(lane == 1, i2, float(N_GROUPS))).astype(jnp.int32) - N_GROUPS


def _router(x, w_hi, w_lo, bias, *, tm=512):
    m, d = x.shape
    wspec = pl.BlockSpec((d, LANES), lambda i: (0, 0))
    return pl.pallas_call(
        _router_kernel,
        grid=(m // tm,),
        in_specs=[pl.BlockSpec((tm, d), lambda i: (i, 0)), wspec, wspec,
                  pl.BlockSpec((1, LANES), lambda i: (0, 0))],
        out_specs=[pl.BlockSpec((tm, LANES), lambda i: (i, 0)), pl.BlockSpec((tm, LANES), lambda i: (i, 0))],
        out_shape=[jax.ShapeDtypeStruct((m, LANES), F32), jax.ShapeDtypeStruct((m, LANES), jnp.int32)],
        compiler_params=_params("parallel"),
        name="router",
    )(x, w_hi, w_lo, bias)


def _ffn_kernel(blk_exp_ref, blk_on_ref, h_ref, wg_ref, wu_ref, wd_ref, o_ref):
    i = pl.program_id(0)

    @pl.when(blk_on_ref[i] != 0)
    def _():
        h = h_ref[...]
        a = _dot(h, wg_ref[0].astype(BF16))
        a = a * _sigmoid(a) * _dot(h, wu_ref[0].astype(BF16))
        o_ref[...] = _dot(a.astype(BF16), wd_ref[0].astype(BF16))

    @pl.when(blk_on_ref[i] == 0)
    def _():
        o_ref[...] = jnp.zeros_like(o_ref)


def _expert_ffn(blk_exp, blk_on, h_rows, w_gate, w_up, w_down):
    p_rows, d = h_rows.shape
    ff = w_gate.shape[2]
    n_blocks = p_rows // MOE_BLOCK
    grid_spec = pltpu.PrefetchScalarGridSpec(
        num_scalar_prefetch=2,
        grid=(n_blocks,),
        in_specs=[pl.BlockSpec((MOE_BLOCK, d), lambda i, be, on: (i, 0)),
                  pl.BlockSpec((1, d, ff), lambda i, be, on: (be[i], 0, 0)),
                  pl.BlockSpec((1, d, ff), lambda i, be, on: (be[i], 0, 0)),
                  pl.BlockSpec((1, ff, d), lambda i, be, on: (be[i], 0, 0))],
        out_specs=pl.BlockSpec((MOE_BLOCK, d), lambda i, be, on: (i, 0)),
    )
    return pl.pallas_call(
        _ffn_kernel,
        grid_spec=grid_spec,
        out_shape=jax.ShapeDtypeStruct((p_rows, d), F32),
        compiler_params=_params("arbitrary"),
        name="expert_ffn",
    )(blk_exp, blk_on, h_rows, w_gate, w_up, w_down)


def _combine_ln_kernel(x_ref, y0_ref, y1_ref, gate_ref, g_ref, b_ref, o_ref):
    gates = gate_ref[...]
    moe = y0_ref[...] * gates[:, 0:1] + y1_ref[...] * gates[:, 1:2]
    o_ref[...] = _layer_norm(DN_ALPHA * x_ref[...] + moe, g_ref[...], b_ref[...])


def _combine_ln(x, y0, y1, gates, ln_g, ln_b, *, tm=256):
    m, d = x.shape
    row = pl.BlockSpec((tm, d), lambda i: (i, 0))
    vec = pl.BlockSpec((1, d), lambda i: (0, 0))
    return pl.pallas_call(
        _combine_ln_kernel,
        grid=(m // tm,),
        in_specs=[row, row, row, pl.BlockSpec((tm, LANES), lambda i: (i, 0)), vec, vec],
        out_specs=row,
        out_shape=jax.ShapeDtypeStruct((m, d), F32),
        compiler_params=_params("parallel"),
        name="combine_ln",
    )(x, y0, y1, gates, ln_g, ln_b)


def _layer(x, mem, w_in, cmp_pe_k, cmp_w1_k, cmp_w2_k, cmp_pe_v, cmp_w1_v, cmp_w2_v,
           sgu_ln_g, sgu_ln_b, sgu_w_s, sgu_b_s, w_branch_a, w_branch_b, w_o, ln1_g, ln1_b,
           w_xq, w_xk, w_xv, w_xo, ln2_g, ln2_b,
           w_router_grp, b_router_grp, w_router_exp, b_router_exp,
           w_exp_gate, w_exp_up, w_exp_down, ln3_g, ln3_b):
    bsz, seq, d = x.shape
    n_tok = bsz * seq
    g = NSA_KV_GROUPS
    hpg = NSA_HPG
    xf = x.reshape(n_tok, d)
    x_bf = xf.astype(BF16)

    c0 = Q_WIDTH
    c1 = c0 + KV_WIDTH
    c2 = c1 + NSA_GATE_WIDTH
    c3 = c2 + 2 * SGU_WIDTH
    w_q = w_in[:, :c0].astype(BF16)
    w_gate = w_in[:, c1:c2].reshape(d, g, hpg, 3).transpose(0, 1, 3, 2).reshape(d, g, 3 * hpg)
    w_gate = jnp.pad(w_gate, ((0, 0), (0, 0), (0, LANES - 3 * hpg))).reshape(d, g * LANES)
    w_kvg = jnp.concatenate([w_in[:, c0:c1], w_gate], axis=1).astype(BF16)
    w_uv = w_in[:, c2:c3].astype(BF16)
    w_merge = w_in[:, c3:].astype(BF16)

    cos, sin = _rope_tables(jnp.arange(seq, dtype=jnp.int32))
    q = _q_proj(x_bf, w_q, cos, sin, seq).reshape(bsz, seq, Q_WIDTH)
    kv_cmp, kv_rest, nsa_gates = _kv_proj(x_bf, w_kvg, cos, sin, seq)

    n_chunks = seq // CMP_STRIDE
    kc = kv_cmp.reshape(bsz, n_chunks, CMP_STRIDE, 2 * g, HEAD_DIM).transpose(3, 0, 1, 2, 4)
    kc = kc.reshape(2 * g, bsz, n_chunks, CMP_STRIDE * HEAD_DIM)
    pe = jnp.stack([cmp_pe_k.reshape(1, -1), cmp_pe_v.reshape(1, -1)])
    w1 = jnp.stack([cmp_w1_k, cmp_w1_v]).astype(BF16)
    w2 = jnp.stack([cmp_w2_k, cmp_w2_v]).astype(BF16)
    cmp_end = CMP_STRIDE * jnp.arange(n_chunks, dtype=jnp.int32) + CMP_BLOCK - 1
    ccos, csin = _rope_tables(cmp_end)
    cmp_cos = jnp.stack([ccos, jnp.ones_like(ccos)])
    cmp_sin = jnp.stack([csin, jnp.zeros_like(csin)])
    cmp_kv = _compress(kc, pe, w1, w2, cmp_cos, cmp_sin)

    n_sel = seq // SEL_BLOCK
    n_pick = min(SEL_TOPK, n_sel)
    ci = CMP_STRIDE * jnp.arange(n_chunks)
    sj = SEL_BLOCK * jnp.arange(LANES)
    overlap = jnp.clip(jnp.minimum(ci[:, None] + CMP_BLOCK, sj[None, :] + SEL_BLOCK)
                       - jnp.maximum(ci[:, None], sj[None, :]), 0, None)
    selmap = (overlap.astype(F32) / CMP_BLOCK).astype(BF16)
    o_nsa = _nsa(q, cmp_kv, kv_rest.reshape(bsz, seq, _N_REST_COLS),
                 nsa_gates.reshape(bsz, seq, _GATE_COLS), selmap, n_sel=n_sel, n_pick=n_pick)

    uv = _matmul(x_bf, w_uv, tm=512, tn=512, out_dtype=F32, act="gelu")
    o_sgu = _sgu(uv, sgu_ln_g.reshape(1, -1), sgu_ln_b.reshape(1, -1), sgu_w_s, sgu_b_s.T)

    merged = _merge(x_bf, o_nsa.reshape(n_tok, Q_WIDTH), o_sgu, w_merge,
                    w_branch_a.astype(BF16), w_branch_b.astype(BF16))
    x1, x1_bf = _proj_ln(merged, w_o.astype(BF16), xf, ln1_g.reshape(1, -1), ln1_b.reshape(1, -1))

    mem_len = mem.shape[1]
    w_mem = jnp.concatenate([w_xk, w_xv], axis=1).astype(BF16)
    mem_kv = _matmul(mem.reshape(bsz * mem_len, d).astype(BF16), w_mem, tm=256, tn=512, out_dtype=BF16)
    x2 = _xattn(x1_bf.reshape(bsz, seq, d), x1.reshape(bsz, seq, d), w_xq.astype(BF16),
                mem_kv.reshape(bsz, mem_len, 2 * MEM_WIDTH), w_xo.astype(BF16),
                ln2_g.reshape(1, -1), ln2_b.reshape(1, -1)).reshape(n_tok, d)

    w_r = jnp.concatenate([w_router_grp, w_router_exp], axis=1)
    w_r = jnp.pad(w_r, ((0, 0), (0, LANES - w_r.shape[1])))
    w_r_hi = w_r.astype(BF16)
    w_r_lo = (w_r - w_r_hi.astype(F32)).astype(BF16)
    b_r = jnp.concatenate([b_router_grp, b_router_exp])
    b_r = jnp.pad(b_r, (0, LANES - b_r.shape[0])).reshape(1, LANES)
    gates, experts = _router(x2, w_r_hi, w_r_lo, b_r)

    nk = n_tok * EXPERT_TOPK
    e_flat = experts[:, :EXPERT_TOPK].reshape(nk)
    tok_flat = jnp.repeat(jnp.arange(n_tok, dtype=jnp.int32), EXPERT_TOPK)
    order = jnp.argsort(e_flat)
    e_s = e_flat[order]
    counts = jnp.bincount(e_flat, length=N_EXPERTS)
    start = jnp.cumsum(counts) - counts
    pcounts = ((counts + MOE_BLOCK - 1) // MOE_BLOCK) * MOE_BLOCK
    pend = jnp.cumsum(pcounts)
    pstart = pend - pcounts
    dest = (pstart[e_s] + (jnp.arange(nk) - start[e_s])).astype(jnp.int32)
    n_blocks = -(-nk // MOE_BLOCK) + N_EXPERTS
    p_rows = n_blocks * MOE_BLOCK
    row_tok = jnp.full((p_rows,), n_tok, dtype=jnp.int32).at[dest].set(tok_flat[order])
    blk_start = jnp.arange(n_blocks) * MOE_BLOCK
    blk_on = (blk_start < pend[-1]).astype(jnp.int32)
    blk_exp = jnp.minimum(jnp.searchsorted(pend, jnp.minimum(blk_start, pend[-1] - 1), side='right'),
                          N_EXPERTS - 1).astype(jnp.int32)
    dest_of = jnp.zeros((nk,), jnp.int32).at[order].set(dest).reshape(n_tok, EXPERT_TOPK)

    x2_pad = jnp.concatenate([x2.astype(BF16), jnp.zeros((1, d), BF16)], axis=0)
    h_rows = x2_pad[row_tok]
    y_rows = _expert_ffn(blk_exp, blk_on, h_rows, w_exp_gate, w_exp_up, w_exp_down)
    out = _combine_ln(x2, y_rows[dest_of[:, 0]], y_rows[dest_of[:, 1]], gates,
                      ln3_g.reshape(1, -1), ln3_b.reshape(1, -1))
    return out.reshape(bsz, seq, d)


def kernel(x, mem, w_in, cmp_pe_k, cmp_w1_k, cmp_w2_k, cmp_pe_v, cmp_w1_v, cmp_w2_v, sgu_ln_g, sgu_ln_b, sgu_w_s, sgu_b_s, w_branch_a, w_branch_b, w_o, ln1_g, ln1_b, w_xq, w_xk, w_xv, w_xo, ln2_g, ln2_b, w_router_grp, b_router_grp, w_router_exp, b_router_exp, w_exp_gate, w_exp_up, w_exp_down, ln3_g, ln3_b):
    params = (w_in, cmp_pe_k, cmp_w1_k, cmp_w2_k, cmp_pe_v, cmp_w1_v, cmp_w2_v,
              sgu_ln_g, sgu_ln_b, sgu_w_s, sgu_b_s, w_branch_a, w_branch_b, w_o, ln1_g, ln1_b,
              w_xq, w_xk, w_xv, w_xo, ln2_g, ln2_b,
              w_router_grp, b_router_grp, w_router_exp, b_router_exp,
              w_exp_gate, w_exp_up, w_exp_down, ln3_g, ln3_b)
    h = x
    for l in range(DEPTH):
        h = _layer(h, mem, *[p[l] for p in params])
    return h
```

```python
import functools
import math

import jax
import jax.numpy as jnp
from jax import lax
from jax.experimental import pallas as pl
from jax.experimental.pallas import tpu as pltpu

D_MODEL = 2048
HEAD_DIM = 128
NSA_HEADS = 16
NSA_KV_GROUPS = 2
NSA_HPG = NSA_HEADS // NSA_KV_GROUPS
CMP_BLOCK = 32
CMP_STRIDE = 16
CMP_HIDDEN = 256
SEL_BLOCK = 64
SEL_TOPK = 16
WINDOW = 512
Q_BLOCK = 128
SGU_WIDTH = 2048
SGU_GROUPS = 8
SGU_CHUNK = 128
MEM_HEADS = 4
N_GROUPS = 4
EXPERTS_PER_GROUP = 16
N_EXPERTS = N_GROUPS * EXPERTS_PER_GROUP
EXPERT_TOPK = 2
EXPERT_FF = 512
MOE_BLOCK = 128
ROPE_THETA = 10000.0
LN_EPS = 1e-5
SEL_BIG = 1e9
DEPTH = 1
DN_ALPHA = (2.0 * DEPTH) ** 0.25

Q_WIDTH = NSA_HEADS * HEAD_DIM
KV_WIDTH = 3 * 2 * NSA_KV_GROUPS * HEAD_DIM
NSA_GATE_WIDTH = 3 * NSA_HEADS
MEM_WIDTH = MEM_HEADS * HEAD_DIM

LANES = 128
VMEM_LIMIT = 56 * 1024 * 1024
MASKED = -1e30
M_INIT = -1e20
BF16 = jnp.bfloat16
F32 = jnp.float32


def _params(*sem):
    return pltpu.CompilerParams(dimension_semantics=sem, vmem_limit_bytes=VMEM_LIMIT)


def _dot(a, b):
    return jnp.dot(a, b, preferred_element_type=F32)


def _dot_nt(a, b):
    return lax.dot_general(a, b, (((1,), (1,)), ((), ())), preferred_element_type=F32)


def _gelu(x):
    return 0.5 * x * (1.0 + jnp.tanh(math.sqrt(2.0 / math.pi) * (x + 0.044715 * (x * x * x))))


def _sigmoid(x):
    return 1.0 / (1.0 + jnp.exp(-x))


def _layer_norm(x, g, b):
    mu = jnp.mean(x, -1, keepdims=True)
    xc = x - mu
    var = jnp.mean(xc * xc, -1, keepdims=True)
    return xc * lax.rsqrt(var + LN_EPS) * g + b


def _rope(t, cos_full, sin_signed):
    return t * cos_full + pltpu.roll(t, HEAD_DIM // 2, axis=1) * sin_signed


def _rope_tables(pos):
    inv = ROPE_THETA ** (-jnp.arange(0, HEAD_DIM, 2, dtype=F32) / HEAD_DIM)
    ang = pos.astype(F32)[:, None] * inv[None, :]
    c, s = jnp.cos(ang), jnp.sin(ang)
    return jnp.concatenate([c, c], -1), jnp.concatenate([-s, s], -1)


def _mm_kernel(a_ref, b_ref, o_ref, *, act):
    y = _dot(a_ref[...], b_ref[...])
    if act == "gelu":
        y = _gelu(y)
    o_ref[...] = y.astype(o_ref.dtype)


def _matmul(a, b, *, tm, tn, out_dtype, act=None):
    m, k = a.shape
    n = b.shape[1]
    return pl.pallas_call(
        functools.partial(_mm_kernel, act=act),
        grid=(m // tm, n // tn),
        in_specs=[pl.BlockSpec((tm, k), lambda i, j: (i, 0)),
                  pl.BlockSpec((k, tn), lambda i, j: (0, j))],
        out_specs=pl.BlockSpec((tm, tn), lambda i, j: (i, j)),
        out_shape=jax.ShapeDtypeStruct((m, n), out_dtype),
        compiler_params=_params("parallel", "parallel"),
        name="matmul_" + (act or "plain"),
    )(a, b)


def _qproj_kernel(a_ref, b_ref, cos_ref, sin_ref, o_ref):
    y = _dot(a_ref[...], b_ref[...])
    cos, sin = cos_ref[...], sin_ref[...]
    scale = HEAD_DIM ** -0.5 * math.log2(math.e)
    for h in range(y.shape[1] // HEAD_DIM):
        sl = slice(h * HEAD_DIM, (h + 1) * HEAD_DIM)
        o_ref[:, sl] = (_rope(y[:, sl], cos, sin) * scale).astype(o_ref.dtype)


def _q_proj(x_bf, w_q, cos, sin, seq, *, tm=512, tn=512):
    m, k = x_bf.shape
    n = w_q.shape[1]
    per_seq = seq // tm
    return pl.pallas_call(
        _qproj_kernel,
        grid=(m // tm, n // tn),
        in_specs=[pl.BlockSpec((tm, k), lambda i, j: (i, 0)),
                  pl.BlockSpec((k, tn), lambda i, j: (0, j)),
                  pl.BlockSpec((tm, HEAD_DIM), lambda i, j: (i % per_seq, 0)),
                  pl.BlockSpec((tm, HEAD_DIM), lambda i, j: (i % per_seq, 0))],
        out_specs=pl.BlockSpec((tm, tn), lambda i, j: (i, j)),
        out_shape=jax.ShapeDtypeStruct((m, n), BF16),
        compiler_params=_params("parallel", "parallel"),
        name="q_proj",
    )(x_bf, w_q, cos, sin)


_N_CMP_COLS = 2 * NSA_KV_GROUPS * HEAD_DIM
_N_REST_CHUNKS = (KV_WIDTH - _N_CMP_COLS) // HEAD_DIM
_N_REST_COLS = (_N_REST_CHUNKS + NSA_KV_GROUPS) * HEAD_DIM
_GATE_COLS = NSA_KV_GROUPS * LANES
_SEL_SHIFT = SEL_BLOCK.bit_length() - 1
assert 1 << _SEL_SHIFT == SEL_BLOCK


def _kvproj_kernel(a_ref, b_ref, cos_ref, sin_ref, cmp_ref, rest_ref, gate_ref, *, per_seq):
    g = NSA_KV_GROUPS
    y = _dot(a_ref[...], b_ref[...])
    cos, sin = cos_ref[...], sin_ref[...]
    cmp_ref[...] = y[:, :_N_CMP_COLS]
    tm = y.shape[0]
    pos = (pl.program_id(0) % per_seq) * tm + lax.broadcasted_iota(jnp.int32, (tm, LANES), 0)
    lane = lax.broadcasted_iota(jnp.int32, (tm, LANES), 1)
    blk_onehot = jnp.where(lane == (pos >> _SEL_SHIFT), 1.0, 0.0).astype(rest_ref.dtype)
    for c in range(_N_REST_CHUNKS):
        t = y[:, _N_CMP_COLS + c * HEAD_DIM:_N_CMP_COLS + (c + 1) * HEAD_DIM]
        if (c % (2 * g)) < g:
            t = _rope(t, cos, sin)
        dst = 2 * c if c < g else g + c
        rest_ref[:, dst * HEAD_DIM:(dst + 1) * HEAD_DIM] = t.astype(rest_ref.dtype)
        if c < g:
            rest_ref[:, (dst + 1) * HEAD_DIM:(dst + 2) * HEAD_DIM] = blk_onehot
    gate_ref[...] = _sigmoid(y[:, KV_WIDTH:])


def _kv_proj(x_bf, w_kvg, cos, sin, seq, *, tm=512):
    m, k = x_bf.shape
    n = w_kvg.shape[1]
    per_seq = seq // tm
    return pl.pallas_call(
        functools.partial(_kvproj_kernel, per_seq=per_seq),
        grid=(m // tm,),
        in_specs=[pl.BlockSpec((tm, k), lambda i: (i, 0)),
                  pl.BlockSpec((k, n), lambda i: (0, 0)),
                  pl.BlockSpec((tm, HEAD_DIM), lambda i: (i % per_seq, 0)),
                  pl.BlockSpec((tm, HEAD_DIM), lambda i: (i % per_seq, 0))],
        out_specs=[pl.BlockSpec((tm, _N_CMP_COLS), lambda i: (i, 0)),
                   pl.BlockSpec((tm, _N_REST_COLS), lambda i: (i, 0)),
                   pl.BlockSpec((tm, _GATE_COLS), lambda i: (i, 0))],
        out_shape=[jax.ShapeDtypeStruct((m, _N_CMP_COLS), F32),
                   jax.ShapeDtypeStruct((m, _N_REST_COLS), BF16),
                   jax.ShapeDtypeStruct((m, _GATE_COLS), F32)],
        compiler_params=_params("parallel"),
        name="kv_gate_proj",
    )(x_bf, w_kvg, cos, sin)


def _compress_kernel(kc_ref, pe_ref, w1_ref, w2_ref, cos_ref, sin_ref, o_ref):
    half = CMP_STRIDE * HEAD_DIM
    kc = kc_ref[0, 0]
    pe = pe_ref[0]
    top = _dot((kc + pe[:, :half]).astype(BF16), w1_ref[0, :half, :])
    bot = _dot((kc + pe[:, half:]).astype(BF16), w1_ref[0, half:, :])
    n_chunks = kc.shape[0]
    h = top + pltpu.roll(bot, n_chunks - 1, axis=0)
    y = _dot(_gelu(h).astype(BF16), w2_ref[0])
    o_ref[0, 0] = _rope(y, cos_ref[0], sin_ref[0]).astype(o_ref.dtype)


def _compress(kc, pe, w1, w2, cos, sin):
    four, bsz, n_chunks, width = kc.shape
    g = NSA_KV_GROUPS
    return pl.pallas_call(
        _compress_kernel,
        grid=(four, bsz),
        in_specs=[pl.BlockSpec((1, 1, n_chunks, width), lambda a, b: (a, b, 0, 0)),
                  pl.BlockSpec((1, 1, 2 * width), lambda a, b: (a // g, 0, 0)),
                  pl.BlockSpec((1, 2 * width, CMP_HIDDEN), lambda a, b: (a // g, 0, 0)),
                  pl.BlockSpec((1, CMP_HIDDEN, HEAD_DIM), lambda a, b: (a // g, 0, 0)),
                  pl.BlockSpec((1, n_chunks, HEAD_DIM), lambda a, b: (a // g, 0, 0)),
                  pl.BlockSpec((1, n_chunks, HEAD_DIM), lambda a, b: (a // g, 0, 0))],
        out_specs=pl.BlockSpec((1, 1, n_chunks, HEAD_DIM), lambda a, b: (a, b, 0, 0)),
        out_shape=jax.ShapeDtypeStruct((four, bsz, n_chunks, HEAD_DIM), BF16),
        compiler_params=_params("parallel", "parallel"),
        name="compress",
    )(kc, pe, w1, w2, cos, sin)


SEL_TILE = 512
SUBLANES = 8


def _softmax_parts(s3, bias):
    s3 = s3 + bias[None]
    m = jnp.max(s3, -1, keepdims=True)
    m = jnp.where(m > 0.5 * MASKED, m, 0.0)
    p = jnp.exp2(s3 - m)
    return p, 1.0 / jnp.maximum(jnp.sum(p, -1, keepdims=True), 1e-30)


def _topk_unselected(score_t, n_sel, n_pick):
    tq = score_t.shape[1]
    n_grp = n_sel // SUBLANES
    grp = [score_t[g * SUBLANES:(g + 1) * SUBLANES] for g in range(n_grp)]
    row_in_grp = lax.broadcasted_iota(jnp.int32, (SUBLANES, tq), 0)
    later = [jnp.where(row_in_grp > li, 1.0, 0.0) for li in range(SUBLANES)]
    beaten = [jnp.zeros((SUBLANES, tq), F32) for _ in range(n_grp)]
    for i in range(n_sel):
        gi, li = divmod(i, SUBLANES)
        row = grp[gi][li:li + 1, :]
        for g in range(n_grp):
            if g > gi:
                beaten[g] = beaten[g] + jnp.where(row >= grp[g], 1.0, 0.0)
            elif g < gi:
                beaten[g] = beaten[g] + jnp.where(row > grp[g], 1.0, 0.0)
            else:
                beaten[g] = (beaten[g] + jnp.where(row > grp[g], 1.0, 0.0)
                             + jnp.where(row == grp[g], later[li], 0.0))
    return jnp.concatenate([jnp.where(b < n_pick, 0.0, MASKED) for b in beaten], axis=0)


def _nsa_kernel(q_ref, kc_ref, vc_ref, ks_ref, vs_ref, kw_ref, vw_ref, gate_ref, selmap_t_ref, o_ref,
                *, n_sel, n_pick):
    hpg = NSA_HPG
    tq = Q_BLOCK
    c = pl.program_id(2)
    q_t = q_ref[0]
    q8 = jnp.concatenate([q_t[:, n * HEAD_DIM:(n + 1) * HEAD_DIM] for n in range(hpg)], axis=0)
    t_pos = c * tq + lax.broadcasted_iota(jnp.int32, (tq, 1), 0)
    t_lane = c * tq + lax.broadcasted_iota(jnp.int32, (1, tq), 1)

    kc = kc_ref[0, 0]
    n_cmp_pad = kc.shape[0]
    k_idx = lax.broadcasted_iota(jnp.int32, (1, n_cmp_pad), 1)
    vis_c = (CMP_STRIDE * k_idx + (CMP_BLOCK - 1) <= t_pos) & (k_idx < n_cmp_pad - 1)
    p_c, r_c = _softmax_parts(_dot_nt(q8, kc).reshape(hpg, tq, n_cmp_pad), jnp.where(vis_c, 0.0, MASKED))
    o_c = _dot(p_c.reshape(hpg * tq, n_cmp_pad).astype(BF16), vc_ref[0, 0]).reshape(hpg, tq, HEAD_DIM) * r_c

    p_sum = jnp.sum(p_c * r_c, axis=0)
    p_hi = p_sum.astype(BF16)
    p_lo = (p_sum - p_hi.astype(F32)).astype(BF16)
    selmap_t = selmap_t_ref[...]
    imp_t = _dot_nt(selmap_t, p_hi) + _dot_nt(selmap_t, p_lo)
    blk = lax.broadcasted_iota(jnp.int32, (LANES, tq), 0)
    cur = t_lane >> _SEL_SHIFT
    forced = (blk == 0) | (blk == cur) | (blk == cur - 1)
    score_t = jnp.where(forced, SEL_BIG, jnp.where(blk <= cur, imp_t, -SEL_BIG))
    unsel_t = _topk_unselected(score_t, n_sel, n_pick)
    unsel_t = jnp.concatenate([unsel_t, jnp.zeros((LANES - n_sel, tq), F32)], axis=0)
    unsel = unsel_t.T.astype(BF16)
    q_aug = jnp.concatenate([q8, jnp.concatenate([unsel] * hpg, axis=0)], axis=1)

    def sel_tile(base, carry, bias):
        m_old, l_old, acc = carry
        k = ks_ref[0, pl.ds(base, SEL_TILE), :]
        v = vs_ref[0, pl.ds(base, SEL_TILE), :]
        s = _dot_nt(q_aug, k).reshape(hpg, tq, SEL_TILE)
        if bias is not None:
            s = s + bias[None]
        m_new = jnp.maximum(m_old, jnp.max(s, -1, keepdims=True))
        alpha = jnp.exp2(m_old - m_new)
        p = jnp.exp2(s - m_new)
        l_new = alpha * l_old + jnp.sum(p, -1, keepdims=True)
        pv = _dot(p.reshape(hpg * tq, SEL_TILE).astype(BF16), v).reshape(hpg, tq, HEAD_DIM)
        return m_new, l_new, alpha * acc + pv

    n_full = c // (SEL_TILE // tq)
    init = (jnp.full((hpg, tq, 1), M_INIT, F32), jnp.zeros((hpg, tq, 1), F32),
            jnp.zeros((hpg, tq, HEAD_DIM), F32))
    carry = lax.fori_loop(
        0, n_full, lambda kt, cr: sel_tile(pl.multiple_of(kt * SEL_TILE, SEL_TILE), cr, None), init)
    base_last = pl.multiple_of(n_full * SEL_TILE, SEL_TILE)
    kp_last = base_last + lax.broadcasted_iota(jnp.int32, (1, SEL_TILE), 1)
    _, l_s, acc_s = sel_tile(base_last, carry, jnp.where(kp_last <= t_pos, 0.0, MASKED))
    o_s = acc_s * (1.0 / jnp.maximum(l_s, 1e-30))

    span = WINDOW + tq
    start = pl.multiple_of(jnp.maximum(c * tq - WINDOW, 0), tq)
    kw = kw_ref[0, pl.ds(start, span), :]
    vw = vw_ref[0, pl.ds(start, span), :]
    kp = start + lax.broadcasted_iota(jnp.int32, (1, span), 1)
    vis_w = (kp <= t_pos) & (t_pos - kp < WINDOW)
    p_w, r_w = _softmax_parts(_dot_nt(q8, kw).reshape(hpg, tq, span), jnp.where(vis_w, 0.0, MASKED))
    o_w = _dot(p_w.reshape(hpg * tq, span).astype(BF16), vw).reshape(hpg, tq, HEAD_DIM) * r_w

    gates = gate_ref[0]
    for n in range(hpg):
        out = (gates[:, n:n + 1] * o_c[n]
               + gates[:, hpg + n:hpg + n + 1] * o_s[n]
               + gates[:, 2 * hpg + n:2 * hpg + n + 1] * o_w[n])
        o_ref[0, :, n * HEAD_DIM:(n + 1) * HEAD_DIM] = out.astype(o_ref.dtype)


def _nsa(q, cmp_kv, rest, gates, selmap_t, *, n_sel, n_pick):
    bsz, seq, _ = q.shape
    g = NSA_KV_GROUPS
    n_cmp_pad = cmp_kv.shape[2]
    gw = NSA_HPG * HEAD_DIM
    assert n_sel % SUBLANES == 0 and n_sel <= LANES and seq >= WINDOW + Q_BLOCK
    kv_spec = lambda off: pl.BlockSpec((1, seq, HEAD_DIM), lambda b, gi, c: (b, 0, off + gi))
    return pl.pallas_call(
        functools.partial(_nsa_kernel, n_sel=n_sel, n_pick=n_pick),
        grid=(bsz, g, seq // Q_BLOCK),
        in_specs=[pl.BlockSpec((1, Q_BLOCK, gw), lambda b, gi, c: (b, c, gi)),
                  pl.BlockSpec((1, 1, n_cmp_pad, HEAD_DIM), lambda b, gi, c: (gi, b, 0, 0)),
                  pl.BlockSpec((1, 1, n_cmp_pad, HEAD_DIM), lambda b, gi, c: (g + gi, b, 0, 0)),
                  pl.BlockSpec((1, seq, 2 * HEAD_DIM), lambda b, gi, c: (b, 0, gi)),
                  kv_spec(2 * g), kv_spec(3 * g), kv_spec(4 * g),
                  pl.BlockSpec((1, Q_BLOCK, LANES), lambda b, gi, c: (b, c, gi)),
                  pl.BlockSpec(selmap_t.shape, lambda b, gi, c: (0, 0))],
        out_specs=pl.BlockSpec((1, Q_BLOCK, gw), lambda b, gi, c: (b, c, gi)),
        out_shape=jax.ShapeDtypeStruct(q.shape, BF16),
        compiler_params=_params("parallel", "parallel", "arbitrary"),
        name="nsa_attention",
    )(q, cmp_kv, cmp_kv, rest, rest, rest, rest, gates, selmap_t)


def _sgu_kernel(u_ref, v_ref, g_ref, b_ref, ws_ref, bs_ref, o_ref, *, chunks):
    gd = SGU_WIDTH // SGU_GROUPS
    row = lax.broadcasted_iota(jnp.int32, (SGU_CHUNK, SGU_CHUNK), 0)
    col = lax.broadcasted_iota(jnp.int32, (SGU_CHUNK, SGU_CHUNK), 1)
    causal = col <= row
    bs = bs_ref[...]
    for ch in range(chunks):
        rows = slice(ch * SGU_CHUNK, (ch + 1) * SGU_CHUNK)
        vn = _layer_norm(v_ref[rows, :], g_ref[...], b_ref[...]).astype(BF16)
        for gi in range(SGU_GROUPS):
            cols = slice(gi * gd, (gi + 1) * gd)
            w = jnp.where(causal, ws_ref[gi], 0.0).astype(BF16)
            z = _dot(w, vn[:, cols]) + bs[:, gi:gi + 1]
            o_ref[rows, cols] = (u_ref[rows, cols] * z).astype(o_ref.dtype)


def _sgu(uv, ln_g, ln_b, w_s, b_s_t, *, chunks=2):
    m = uv.shape[0]
    tm = chunks * SGU_CHUNK
    return pl.pallas_call(
        functools.partial(_sgu_kernel, chunks=chunks),
        grid=(m // tm,),
        in_specs=[pl.BlockSpec((tm, SGU_WIDTH), lambda i: (i, 0)),
                  pl.BlockSpec((tm, SGU_WIDTH), lambda i: (i, 1)),
                  pl.BlockSpec((1, SGU_WIDTH), lambda i: (0, 0)),
                  pl.BlockSpec((1, SGU_WIDTH), lambda i: (0, 0)),
                  pl.BlockSpec(w_s.shape, lambda i: (0, 0, 0)),
                  pl.BlockSpec(b_s_t.shape, lambda i: (0, 0))],
        out_specs=pl.BlockSpec((tm, SGU_WIDTH), lambda i: (i, 0)),
        out_shape=jax.ShapeDtypeStruct((m, SGU_WIDTH), BF16),
        compiler_params=_params("parallel"),
        name="sgu",
    )(uv, uv, ln_g, ln_b, w_s, b_s_t)


def _merge_kernel(x_ref, a_ref, s_ref, wga_ref, wgb_ref, wa_ref, wb_ref, o_ref):
    x = x_ref[...]
    ga = _sigmoid(_dot(x, wga_ref[...]))
    gb = _sigmoid(_dot(x, wgb_ref[...]))
    o_ref[...] = (ga * _dot(a_ref[...], wa_ref[...]) + gb * _dot(s_ref[...], wb_ref[...])).astype(o_ref.dtype)


def _merge(x_bf, o_nsa, o_sgu, w_merge, w_a, w_b, *, tm=1024, tn=256):
    m, k = x_bf.shape
    n = w_a.shape[1]
    nb = n // tn
    row = pl.BlockSpec((tm, k), lambda i, j: (i, 0))
    col = pl.BlockSpec((k, tn), lambda i, j: (0, j))
    return pl.pallas_call(
        _merge_kernel,
        grid=(m // tm, nb),
        in_specs=[row, row, row, col, pl.BlockSpec((k, tn), lambda i, j: (0, nb + j)), col, col],
        out_specs=pl.BlockSpec((tm, tn), lambda i, j: (i, j)),
        out_shape=jax.ShapeDtypeStruct((m, n), BF16),
        compiler_params=_params("parallel", "parallel"),
        name="merge",
    )(x_bf, o_nsa, o_sgu, w_merge, w_merge, w_a, w_b)


def _proj_ln_kernel(a_ref, w_ref, x_ref, g_ref, b_ref, o_ref, obf_ref):
    y = DN_ALPHA * x_ref[...] + _dot(a_ref[...], w_ref[...])
    out = _layer_norm(y, g_ref[...], b_ref[...])
    o_ref[...] = out
    obf_ref[...] = out.astype(BF16)


def _proj_ln(a_bf, w, x, ln_g, ln_b, *, tm=256):
    m, k = a_bf.shape
    n = w.shape[1]
    vec = pl.BlockSpec((1, n), lambda i: (0, 0))
    return pl.pallas_call(
        _proj_ln_kernel,
        grid=(m // tm,),
        in_specs=[pl.BlockSpec((tm, k), lambda i: (i, 0)),
                  pl.BlockSpec((k, n), lambda i: (0, 0)),
                  pl.BlockSpec((tm, n), lambda i: (i, 0)), vec, vec],
        out_specs=[pl.BlockSpec((tm, n), lambda i: (i, 0)), pl.BlockSpec((tm, n), lambda i: (i, 0))],
        out_shape=[jax.ShapeDtypeStruct((m, n), F32), jax.ShapeDtypeStruct((m, n), BF16)],
        compiler_params=_params("parallel"),
        name="proj_ln",
    )(a_bf, w, x, ln_g, ln_b)


def _xattn_kernel(xbf_ref, x_ref, wq_ref, kv_ref, wo_ref, g_ref, b_ref, o_ref):
    q = (_dot(xbf_ref[0], wq_ref[...]) * HEAD_DIM ** -0.5).astype(BF16)
    kv = kv_ref[0]
    heads = []
    for h in range(MEM_HEADS):
        cols = slice(h * HEAD_DIM, (h + 1) * HEAD_DIM)
        s = _dot_nt(q[:, cols], kv[:, cols])
        p = jnp.exp(s - jnp.max(s, -1, keepdims=True))
        p = p / jnp.sum(p, -1, keepdims=True)
        heads.append(_dot(p.astype(BF16), kv[:, MEM_WIDTH + h * HEAD_DIM:MEM_WIDTH + (h + 1) * HEAD_DIM]))
    o = jnp.concatenate(heads, axis=1).astype(BF16)
    y = DN_ALPHA * x_ref[0] + _dot(o, wo_ref[...])
    o_ref[0] = _layer_norm(y, g_ref[...], b_ref[...])


def _xattn(x_bf, x, w_xq, mem_kv, w_xo, ln_g, ln_b, *, tm=256):
    bsz, seq, d = x.shape
    mem_len = mem_kv.shape[1]
    vec = pl.BlockSpec((1, d), lambda b, i: (0, 0))
    return pl.pallas_call(
        _xattn_kernel,
        grid=(bsz, seq // tm),
        in_specs=[pl.BlockSpec((1, tm, d), lambda b, i: (b, i, 0)),
                  pl.BlockSpec((1, tm, d), lambda b, i: (b, i, 0)),
                  pl.BlockSpec(w_xq.shape, lambda b, i: (0, 0)),
                  pl.BlockSpec((1, mem_len, 2 * MEM_WIDTH), lambda b, i: (b, 0, 0)),
                  pl.BlockSpec(w_xo.shape, lambda b, i: (0, 0)), vec, vec],
        out_specs=pl.BlockSpec((1, tm, d), lambda b, i: (b, i, 0)),
        out_shape=jax.ShapeDtypeStruct(x.shape, F32),
        compiler_params=_params("parallel", "parallel"),
        name="xattn_ln",
    )(x_bf, x, w_xq, mem_kv, w_xo, ln_g, ln_b)


def _router_kernel(x_ref, whi_ref, wlo_ref, b_ref, gate_ref, exp_ref):
    x = x_ref[...]
    x_hi = x.astype(BF16)
    x_lo = (x - x_hi.astype(F32)).astype(BF16)
    whi = whi_ref[...]
    logits = _dot(x_hi, whi) + _dot(x_hi, wlo_ref[...]) + _dot(x_lo, whi) + b_ref[...]
    rows = logits.shape[0]
    lane = lax.broadcasted_iota(jnp.int32, (rows, LANES), 1)
    lane_f = lane.astype(F32)
    is_grp = lane < N_GROUPS
    lg = jnp.where(is_grp, logits, MASKED)
    eg = jnp.where(is_grp, jnp.exp(lg - jnp.max(lg, -1, keepdims=True)), 0.0)
    p_grp = eg / jnp.sum(eg, -1, keepdims=True)
    g_grp = jnp.max(p_grp, -1, keepdims=True)
    grp = jnp.min(jnp.where((p_grp == g_grp) & is_grp, lane_f, 1e9), -1, keepdims=True).astype(jnp.int32)
    lo = N_GROUPS + grp * EXPERTS_PER_GROUP
    in_grp = (lane >= lo) & (lane < lo + EXPERTS_PER_GROUP)
    le = jnp.where(in_grp, logits, MASKED)
    ee = jnp.where(in_grp, jnp.exp(le - jnp.max(le, -1, keepdims=True)), 0.0)
    p_exp = jnp.where(in_grp, ee / jnp.sum(ee, -1, keepdims=True), -1.0)
    v1 = jnp.max(p_exp, -1, keepdims=True)
    i1 = jnp.min(jnp.where(p_exp == v1, lane_f, 1e9), -1, keepdims=True)
    p_rest = jnp.where(lane_f == i1, -1.0, p_exp)
    v2 = jnp.max(p_rest, -1, keepdims=True)
    i2 = jnp.min(jnp.where(p_rest == v2, lane_f, 1e9), -1, keepdims=True)
    denom = v1 + v2
    gate_ref[...] = jnp.where(lane == 0, g_grp * v1 / denom, jnp.where(lane == 1, g_grp * v2 / denom, 0.0))
    exp_ref[...] = jnp.where(lane == 0, i1, jnp.where(lane == 1, i2, float(N_GROUPS))).astype(jnp.int32) - N_GROUPS


def _router(x, w_hi, w_lo, bias, *, tm=512):
    m, d = x.shape
    wspec = pl.BlockSpec((d, LANES), lambda i: (0, 0))
    return pl.pallas_call(
        _router_kernel,
        grid=(m // tm,),
        in_specs=[pl.BlockSpec((tm, d), lambda i: (i, 0)), wspec, wspec,
                  pl.BlockSpec((1, LANES), lambda i: (0, 0))],
        out_specs=[pl.BlockSpec((tm, LANES), lambda i: (i, 0)), pl.BlockSpec((tm, LANES), lambda i: (i, 0))],
        out_shape=[jax.ShapeDtypeStruct((m, LANES), F32), jax.ShapeDtypeStruct((m, LANES), jnp.int32)],
        compiler_params=_params("parallel"),
        name="router",
    )(x, w_hi, w_lo, bias)


def _ffn_kernel(blk_exp_ref, blk_on_ref, h_ref, wg_ref, wu_ref, wd_ref, o_ref):
    i = pl.program_id(0)

    @pl.when(blk_on_ref[i] != 0)
    def _():
        h = h_ref[...].astype(BF16)
        a = _dot(h, wg_ref[0].astype(BF16))
        a = a * _sigmoid(a) * _dot(h, wu_ref[0].astype(BF16))
        o_ref[...] = _dot(a.astype(BF16), wd_ref[0].astype(BF16))

    @pl.when(blk_on_ref[i] == 0)
    def _():
        o_ref[...] = jnp.zeros_like(o_ref)


def _expert_ffn(blk_exp, blk_on, h_rows, w_gate, w_up, w_down):
    p_rows, d = h_rows.shape
    ff = w_gate.shape[2]
    n_blocks = p_rows // MOE_BLOCK
    grid_spec = pltpu.PrefetchScalarGridSpec(
        num_scalar_prefetch=2,
        grid=(n_blocks,),
        in_specs=[pl.BlockSpec((MOE_BLOCK, d), lambda i, be, on: (i, 0)),
                  pl.BlockSpec((1, d, ff), lambda i, be, on: (be[i], 0, 0)),
                  pl.BlockSpec((1, d, ff), lambda i, be, on: (be[i], 0, 0)),
                  pl.BlockSpec((1, ff, d), lambda i, be, on: (be[i], 0, 0))],
        out_specs=pl.BlockSpec((MOE_BLOCK, d), lambda i, be, on: (i, 0)),
    )
    return pl.pallas_call(
        _ffn_kernel,
        grid_spec=grid_spec,
        out_shape=jax.ShapeDtypeStruct((p_rows, d), F32),
        compiler_params=_params("arbitrary"),
        name="expert_ffn",
    )(blk_exp, blk_on, h_rows, w_gate, w_up, w_down)


def _combine_ln_kernel(x_ref, y0_ref, y1_ref, gate_ref, g_ref, b_ref, o_ref):
    gates = gate_ref[...]
    moe = y0_ref[...] * gates[:, 0:1] + y1_ref[...] * gates[:, 1:2]
    o_ref[...] = _layer_norm(DN_ALPHA * x_ref[...] + moe, g_ref[...], b_ref[...])


def _combine_ln(x, y0, y1, gates, ln_g, ln_b, *, tm=256):
    m, d = x.shape
    row = pl.BlockSpec((tm, d), lambda i: (i, 0))
    vec = pl.BlockSpec((1, d), lambda i: (0, 0))
    return pl.pallas_call(
        _combine_ln_kernel,
        grid=(m // tm,),
        in_specs=[row, row, row, pl.BlockSpec((tm, LANES), lambda i: (i, 0)), vec, vec],
        out_specs=row,
        out_shape=jax.ShapeDtypeStruct((m, d), F32),
        compiler_params=_params("parallel"),
        name="combine_ln",
    )(x, y0, y1, gates, ln_g, ln_b)


def _layer(x, mem, w_in, cmp_pe_k, cmp_w1_k, cmp_w2_k, cmp_pe_v, cmp_w1_v, cmp_w2_v,
           sgu_ln_g, sgu_ln_b, sgu_w_s, sgu_b_s, w_branch_a, w_branch_b, w_o, ln1_g, ln1_b,
           w_xq, w_xk, w_xv, w_xo, ln2_g, ln2_b,
           w_router_grp, b_router_grp, w_router_exp, b_router_exp,
           w_exp_gate, w_exp_up, w_exp_down, ln3_g, ln3_b):
    bsz, seq, d = x.shape
    n_tok = bsz * seq
    g = NSA_KV_GROUPS
    hpg = NSA_HPG
    xf = x.reshape(n_tok, d)
    x_bf = xf.astype(BF16)

    c0 = Q_WIDTH
    c1 = c0 + KV_WIDTH
    c2 = c1 + NSA_GATE_WIDTH
    c3 = c2 + 2 * SGU_WIDTH
    w_q = w_in[:, :c0].astype(BF16)
    w_gate = w_in[:, c1:c2].reshape(d, g, hpg, 3).transpose(0, 1, 3, 2).reshape(d, g, 3 * hpg)
    w_gate = jnp.pad(w_gate, ((0, 0), (0, 0), (0, LANES - 3 * hpg))).reshape(d, g * LANES)
    w_kvg = jnp.concatenate([w_in[:, c0:c1], w_gate], axis=1).astype(BF16)
    w_uv = w_in[:, c2:c3].astype(BF16)
    w_merge = w_in[:, c3:].astype(BF16)

    cos, sin = _rope_tables(jnp.arange(seq, dtype=jnp.int32))
    q = _q_proj(x_bf, w_q, cos, sin, seq).reshape(bsz, seq, Q_WIDTH)
    kv_cmp, kv_rest, nsa_gates = _kv_proj(x_bf, w_kvg, cos, sin, seq)

    n_chunks = seq // CMP_STRIDE
    kc = kv_cmp.reshape(bsz, n_chunks, CMP_STRIDE, 2 * g, HEAD_DIM).transpose(3, 0, 1, 2, 4)
    kc = kc.reshape(2 * g, bsz, n_chunks, CMP_STRIDE * HEAD_DIM)
    pe = jnp.stack([cmp_pe_k.reshape(1, -1), cmp_pe_v.reshape(1, -1)])
    w1 = jnp.stack([cmp_w1_k, cmp_w1_v]).astype(BF16)
    w2 = jnp.stack([cmp_w2_k, cmp_w2_v]).astype(BF16)
    cmp_end = CMP_STRIDE * jnp.arange(n_chunks, dtype=jnp.int32) + CMP_BLOCK - 1
    ccos, csin = _rope_tables(cmp_end)
    cmp_cos = jnp.stack([ccos, jnp.ones_like(ccos)])
    cmp_sin = jnp.stack([csin, jnp.zeros_like(csin)])
    cmp_kv = _compress(kc, pe, w1, w2, cmp_cos, cmp_sin)

    n_sel = seq // SEL_BLOCK
    n_pick = min(SEL_TOPK, n_sel)
    ci = CMP_STRIDE * jnp.arange(n_chunks)
    sj = SEL_BLOCK * jnp.arange(LANES)
    overlap = jnp.clip(jnp.minimum(ci[:, None] + CMP_BLOCK, sj[None, :] + SEL_BLOCK)
                       - jnp.maximum(ci[:, None], sj[None, :]), 0, None)
    selmap_t = (overlap.astype(F32) / CMP_BLOCK).astype(BF16).T
    o_nsa = _nsa(q, cmp_kv, kv_rest.reshape(bsz, seq, _N_REST_COLS),
                 nsa_gates.reshape(bsz, seq, _GATE_COLS), selmap_t, n_sel=n_sel, n_pick=n_pick)

    uv = _matmul(x_bf, w_uv, tm=512, tn=512, out_dtype=F32, act="gelu")
    o_sgu = _sgu(uv, sgu_ln_g.reshape(1, -1), sgu_ln_b.reshape(1, -1), sgu_w_s, sgu_b_s.T)

    merged = _merge(x_bf, o_nsa.reshape(n_tok, Q_WIDTH), o_sgu, w_merge,
                    w_branch_a.astype(BF16), w_branch_b.astype(BF16))
    x1, x1_bf = _proj_ln(merged, w_o.astype(BF16), xf, ln1_g.reshape(1, -1), ln1_b.reshape(1, -1))

    mem_len = mem.shape[1]
    w_mem = jnp.concatenate([w_xk, w_xv], axis=1).astype(BF16)
    mem_kv = _matmul(mem.reshape(bsz * mem_len, d).astype(BF16), w_mem, tm=256, tn=512, out_dtype=BF16)
    x2 = _xattn(x1_bf.reshape(bsz, seq, d), x1.reshape(bsz, seq, d), w_xq.astype(BF16),
                mem_kv.reshape(bsz, mem_len, 2 * MEM_WIDTH), w_xo.astype(BF16),
                ln2_g.reshape(1, -1), ln2_b.reshape(1, -1)).reshape(n_tok, d)

    w_r = jnp.concatenate([w_router_grp, w_router_exp], axis=1)
    w_r = jnp.pad(w_r, ((0, 0), (0, LANES - w_r.shape[1])))
    w_r_hi = w_r.astype(BF16)
    w_r_lo = (w_r - w_r_hi.astype(F32)).astype(BF16)
    b_r = jnp.concatenate([b_router_grp, b_router_exp])
    b_r = jnp.pad(b_r, (0, LANES - b_r.shape[0])).reshape(1, LANES)
    gates, experts = _router(x2, w_r_hi, w_r_lo, b_r)

    nk = n_tok * EXPERT_TOPK
    e_flat = experts[:, :EXPERT_TOPK].reshape(nk)
    tok_flat = jnp.repeat(jnp.arange(n_tok, dtype=jnp.int32), EXPERT_TOPK)
    order = jnp.argsort(e_flat)
    e_s = e_flat[order]
    counts = jnp.bincount(e_flat, length=N_EXPERTS)
    start = jnp.cumsum(counts) - counts
    pcounts = ((counts + MOE_BLOCK - 1) // MOE_BLOCK) * MOE_BLOCK
    pend = jnp.cumsum(pcounts)
    pstart = pend - pcounts
    dest = (pstart[e_s] + (jnp.arange(nk) - start[e_s])).astype(jnp.int32)
    n_blocks = -(-nk // MOE_BLOCK) + N_EXPERTS
    p_rows = n_blocks * MOE_BLOCK
    row_tok = jnp.full((p_rows,), n_tok, dtype=jnp.int32).at[dest].set(tok_flat[order])
    blk_start = jnp.arange(n_blocks) * MOE_BLOCK
    blk_on = (blk_start < pend[-1]).astype(jnp.int32)
    blk_exp = jnp.minimum(jnp.searchsorted(pend, jnp.minimum(blk_start, pend[-1] - 1), side='right'),
                          N_EXPERTS - 1).astype(jnp.int32)
    dest_of = jnp.zeros((nk,), jnp.int32).at[order].set(dest).reshape(n_tok, EXPERT_TOPK)

    x2_pad = jnp.concatenate([x2, jnp.zeros((1, d), F32)], axis=0)
    h_rows = x2_pad[row_tok]
    y_rows = _expert_ffn(blk_exp, blk_on, h_rows, w_exp_gate, w_exp_up, w_exp_down)
    out = _combine_ln(x2, y_rows[dest_of[:, 0]], y_rows[dest_of[:, 1]], gates,
                      ln3_g.reshape(1, -1), ln3_b.reshape(1, -1))
    return out.reshape(bsz, seq, d)


def kernel(x, mem, w_in, cmp_pe_k, cmp_w1_k, cmp_w2_k, cmp_pe_v, cmp_w1_v, cmp_w2_v, sgu_ln_g, sgu_ln_b, sgu_w_s, sgu_b_s, w_branch_a, w_branch_b, w_o, ln1_g, ln1_b, w_xq, w_xk, w_xv, w_xo, ln2_g, ln2_b, w_router_grp, b_router_grp, w_router_exp, b_router_exp, w_exp_gate, w_exp_up, w_exp_down, ln3_g, ln3_b):
    params = (w_in, cmp_pe_k, cmp_w1_k, cmp_w2_k, cmp_pe_v, cmp_w1_v, cmp_w2_v,
              sgu_ln_g, sgu_ln_b, sgu_w_s, sgu_b_s, w_branch_a, w_branch_b, w_o, ln1_g, ln1_b,
              w_xq, w_xk, w_xv, w_xo, ln2_g, ln2_b,
              w_router_grp, b_router_grp, w_router_exp, b_router_exp,
              w_exp_gate, w_exp_up, w_exp_down, ln3_g, ln3_b)
    h = x
    for l in range(DEPTH):
        h = _layer(h, mem, *[p[l] for p in params])
    return h
```

```python
import functools
import math

import jax
import jax.numpy as jnp
from jax import lax
from jax.experimental import pallas as pl
from jax.experimental.pallas import tpu as pltpu

D_MODEL = 2048
HEAD_DIM = 128
NSA_HEADS = 16
NSA_KV_GROUPS = 2
NSA_HPG = NSA_HEADS // NSA_KV_GROUPS
CMP_BLOCK = 32
CMP_STRIDE = 16
CMP_HIDDEN = 256
SEL_BLOCK = 64
SEL_TOPK = 16
WINDOW = 512
Q_BLOCK = 128
SGU_WIDTH = 2048
SGU_GROUPS = 8
SGU_CHUNK = 128
MEM_HEADS = 4
N_GROUPS = 4
EXPERTS_PER_GROUP = 16
N_EXPERTS = N_GROUPS * EXPERTS_PER_GROUP
EXPERT_TOPK = 2
EXPERT_FF = 512
MOE_BLOCK = 128
ROPE_THETA = 10000.0
LN_EPS = 1e-5
SEL_BIG = 1e9
DEPTH = 1
DN_ALPHA = (2.0 * DEPTH) ** 0.25

Q_WIDTH = NSA_HEADS * HEAD_DIM
KV_WIDTH = 3 * 2 * NSA_KV_GROUPS * HEAD_DIM
NSA_GATE_WIDTH = 3 * NSA_HEADS
MEM_WIDTH = MEM_HEADS * HEAD_DIM

LANES = 128
VMEM_LIMIT = 56 * 1024 * 1024
MASKED = -1e30
M_INIT = -1e20
BF16 = jnp.bfloat16
F32 = jnp.float32


def _params(*sem):
    return pltpu.CompilerParams(dimension_semantics=sem, vmem_limit_bytes=VMEM_LIMIT)


def _dot(a, b):
    return jnp.dot(a, b, preferred_element_type=F32)


def _dot_nt(a, b):
    return lax.dot_general(a, b, (((1,), (1,)), ((), ())), preferred_element_type=F32)


def _gelu(x):
    return 0.5 * x * (1.0 + jnp.tanh(math.sqrt(2.0 / math.pi) * (x + 0.044715 * (x * x * x))))


def _sigmoid(x):
    return 1.0 / (1.0 + jnp.exp(-x))


def _layer_norm(x, g, b):
    mu = jnp.mean(x, -1, keepdims=True)
    xc = x - mu
    var = jnp.mean(xc * xc, -1, keepdims=True)
    return xc * lax.rsqrt(var + LN_EPS) * g + b


def _rope(t, cos_full, sin_signed):
    return t * cos_full + pltpu.roll(t, HEAD_DIM // 2, axis=1) * sin_signed


def _rope_tables(pos):
    inv = ROPE_THETA ** (-jnp.arange(0, HEAD_DIM, 2, dtype=F32) / HEAD_DIM)
    ang = pos.astype(F32)[:, None] * inv[None, :]
    c, s = jnp.cos(ang), jnp.sin(ang)
    return jnp.concatenate([c, c], -1), jnp.concatenate([-s, s], -1)


def _mm_kernel(a_ref, b_ref, o_ref, *, act):
    y = _dot(a_ref[...], b_ref[...])
    if act == "gelu":
        y = _gelu(y)
    o_ref[...] = y.astype(o_ref.dtype)


def _matmul(a, b, *, tm, tn, out_dtype, act=None):
    m, k = a.shape
    n = b.shape[1]
    return pl.pallas_call(
        functools.partial(_mm_kernel, act=act),
        grid=(m // tm, n // tn),
        in_specs=[pl.BlockSpec((tm, k), lambda i, j: (i, 0)),
                  pl.BlockSpec((k, tn), lambda i, j: (0, j))],
        out_specs=pl.BlockSpec((tm, tn), lambda i, j: (i, j)),
        out_shape=jax.ShapeDtypeStruct((m, n), out_dtype),
        compiler_params=_params("parallel", "parallel"),
        name="matmul_" + (act or "plain"),
    )(a, b)


def _qproj_kernel(a_ref, b_ref, cos_ref, sin_ref, o_ref):
    y = _dot(a_ref[...], b_ref[...])
    cos, sin = cos_ref[...], sin_ref[...]
    scale = HEAD_DIM ** -0.5 * math.log2(math.e)
    for h in range(y.shape[1] // HEAD_DIM):
        sl = slice(h * HEAD_DIM, (h + 1) * HEAD_DIM)
        o_ref[:, sl] = (_rope(y[:, sl], cos, sin) * scale).astype(o_ref.dtype)


def _q_proj(x_bf, w_q, cos, sin, seq, *, tm=512, tn=512):
    m, k = x_bf.shape
    n = w_q.shape[1]
    per_seq = seq // tm
    return pl.pallas_call(
        _qproj_kernel,
        grid=(m // tm, n // tn),
        in_specs=[pl.BlockSpec((tm, k), lambda i, j: (i, 0)),
                  pl.BlockSpec((k, tn), lambda i, j: (0, j)),
                  pl.BlockSpec((tm, HEAD_DIM), lambda i, j: (i % per_seq, 0)),
                  pl.BlockSpec((tm, HEAD_DIM), lambda i, j: (i % per_seq, 0))],
        out_specs=pl.BlockSpec((tm, tn), lambda i, j: (i, j)),
        out_shape=jax.ShapeDtypeStruct((m, n), BF16),
        compiler_params=_params("parallel", "parallel"),
        name="q_proj",
    )(x_bf, w_q, cos, sin)


_N_CMP_COLS = 2 * NSA_KV_GROUPS * HEAD_DIM
_N_REST_CHUNKS = (KV_WIDTH - _N_CMP_COLS) // HEAD_DIM
_N_REST_COLS = (_N_REST_CHUNKS + NSA_KV_GROUPS) * HEAD_DIM
_GATE_COLS = NSA_KV_GROUPS * LANES
_SEL_SHIFT = SEL_BLOCK.bit_length() - 1
assert 1 << _SEL_SHIFT == SEL_BLOCK


def _kvproj_kernel(a_ref, b_ref, cos_ref, sin_ref, cmp_ref, rest_ref, gate_ref, *, per_seq):
    g = NSA_KV_GROUPS
    y = _dot(a_ref[...], b_ref[...])
    cos, sin = cos_ref[...], sin_ref[...]
    cmp_ref[...] = y[:, :_N_CMP_COLS]
    tm = y.shape[0]
    pos = (pl.program_id(0) % per_seq) * tm + lax.broadcasted_iota(jnp.int32, (tm, LANES), 0)
    lane = lax.broadcasted_iota(jnp.int32, (tm, LANES), 1)
    blk_onehot = jnp.where(lane == (pos >> _SEL_SHIFT), 1.0, 0.0).astype(rest_ref.dtype)
    for c in range(_N_REST_CHUNKS):
        t = y[:, _N_CMP_COLS + c * HEAD_DIM:_N_CMP_COLS + (c + 1) * HEAD_DIM]
        if (c % (2 * g)) < g:
            t = _rope(t, cos, sin)
        dst = 2 * c if c < g else g + c
        rest_ref[:, dst * HEAD_DIM:(dst + 1) * HEAD_DIM] = t.astype(rest_ref.dtype)
        if c < g:
            rest_ref[:, (dst + 1) * HEAD_DIM:(dst + 2) * HEAD_DIM] = blk_onehot
    gate_ref[...] = _sigmoid(y[:, KV_WIDTH:])


def _kv_proj(x_bf, w_kvg, cos, sin, seq, *, tm=512):
    m, k = x_bf.shape
    n = w_kvg.shape[1]
    per_seq = seq // tm
    return pl.pallas_call(
        functools.partial(_kvproj_kernel, per_seq=per_seq),
        grid=(m // tm,),
        in_specs=[pl.BlockSpec((tm, k), lambda i: (i, 0)),
                  pl.BlockSpec((k, n), lambda i: (0, 0)),
                  pl.BlockSpec((tm, HEAD_DIM), lambda i: (i % per_seq, 0)),
                  pl.BlockSpec((tm, HEAD_DIM), lambda i: (i % per_seq, 0))],
        out_specs=[pl.BlockSpec((tm, _N_CMP_COLS), lambda i: (i, 0)),
                   pl.BlockSpec((tm, _N_REST_COLS), lambda i: (i, 0)),
                   pl.BlockSpec((tm, _GATE_COLS), lambda i: (i, 0))],
        out_shape=[jax.ShapeDtypeStruct((m, _N_CMP_COLS), F32),
                   jax.ShapeDtypeStruct((m, _N_REST_COLS), BF16),
                   jax.ShapeDtypeStruct((m, _GATE_COLS), F32)],
        compiler_params=_params("parallel"),
        name="kv_gate_proj",
    )(x_bf, w_kvg, cos, sin)


def _compress_kernel(kc_ref, pe_ref, w1_ref, w2_ref, cos_ref, sin_ref, o_ref):
    n_chunks = o_ref.shape[2]
    top = jnp.zeros((n_chunks, CMP_HIDDEN), F32)
    bot = jnp.zeros((n_chunks, CMP_HIDDEN), F32)
    for p in range(CMP_STRIDE):
        xp = kc_ref[0, pl.ds(p, n_chunks, stride=CMP_STRIDE), :]
        q = CMP_STRIDE + p
        top += _dot((xp + pe_ref[0, p:p + 1, :]).astype(BF16), w1_ref[0, p * HEAD_DIM:(p + 1) * HEAD_DIM, :])
        bot += _dot((xp + pe_ref[0, q:q + 1, :]).astype(BF16), w1_ref[0, q * HEAD_DIM:(q + 1) * HEAD_DIM, :])
    h = top + pltpu.roll(bot, n_chunks - 1, axis=0)
    y = _dot(_gelu(h).astype(BF16), w2_ref[0])
    o_ref[0, 0] = _rope(y, cos_ref[0], sin_ref[0]).astype(o_ref.dtype)


def _compress(kv_cmp, pe, w1, w2, cos, sin):
    bsz, seq, width = kv_cmp.shape
    four = width // HEAD_DIM
    n_chunks = seq // CMP_STRIDE
    g = NSA_KV_GROUPS
    return pl.pallas_call(
        _compress_kernel,
        grid=(four, bsz),
        in_specs=[pl.BlockSpec((1, seq, HEAD_DIM), lambda a, b: (b, 0, a)),
                  pl.BlockSpec((1, CMP_BLOCK, HEAD_DIM), lambda a, b: (a // g, 0, 0)),
                  pl.BlockSpec((1, CMP_BLOCK * HEAD_DIM, CMP_HIDDEN), lambda a, b: (a // g, 0, 0)),
                  pl.BlockSpec((1, CMP_HIDDEN, HEAD_DIM), lambda a, b: (a // g, 0, 0)),
                  pl.BlockSpec((1, n_chunks, HEAD_DIM), lambda a, b: (a // g, 0, 0)),
                  pl.BlockSpec((1, n_chunks, HEAD_DIM), lambda a, b: (a // g, 0, 0))],
        out_specs=pl.BlockSpec((1, 1, n_chunks, HEAD_DIM), lambda a, b: (a, b, 0, 0)),
        out_shape=jax.ShapeDtypeStruct((four, bsz, n_chunks, HEAD_DIM), BF16),
        compiler_params=_params("parallel", "parallel"),
        name="compress",
    )(kv_cmp, pe, w1, w2, cos, sin)


SEL_TILE = 512
SUBLANES = 8


def _softmax_parts(s3, bias):
    s3 = s3 + bias[None]
    m = jnp.max(s3, -1, keepdims=True)
    m = jnp.where(m > 0.5 * MASKED, m, 0.0)
    p = jnp.exp2(s3 - m)
    return p, 1.0 / jnp.maximum(jnp.sum(p, -1, keepdims=True), 1e-30)


def _topk_unselected(score_t, n_sel, n_pick):
    tq = score_t.shape[1]
    n_grp = n_sel // SUBLANES
    grp = [score_t[g * SUBLANES:(g + 1) * SUBLANES] for g in range(n_grp)]
    row_in_grp = lax.broadcasted_iota(jnp.int32, (SUBLANES, tq), 0)
    later = [jnp.where(row_in_grp > li, 1.0, 0.0) for li in range(SUBLANES)]
    beaten = [jnp.zeros((SUBLANES, tq), F32) for _ in range(n_grp)]
    for i in range(n_sel):
        gi, li = divmod(i, SUBLANES)
        row = grp[gi][li:li + 1, :]
        for g in range(n_grp):
            if g > gi:
                beaten[g] = beaten[g] + jnp.where(row >= grp[g], 1.0, 0.0)
            elif g < gi:
                beaten[g] = beaten[g] + jnp.where(row > grp[g], 1.0, 0.0)
            else:
                beaten[g] = (beaten[g] + jnp.where(row > grp[g], 1.0, 0.0)
                             + jnp.where(row == grp[g], later[li], 0.0))
    return jnp.concatenate([jnp.where(b < n_pick, 0.0, MASKED) for b in beaten], axis=0)


def _nsa_kernel(q_ref, kc_ref, vc_ref, ks_ref, vs_ref, kw_ref, vw_ref, gate_ref, selmap_t_ref, o_ref,
                *, n_sel, n_pick):
    hpg = NSA_HPG
    tq = Q_BLOCK
    c = pl.program_id(2)
    q_t = q_ref[0]
    q8 = jnp.concatenate([q_t[:, n * HEAD_DIM:(n + 1) * HEAD_DIM] for n in range(hpg)], axis=0)
    t_pos = c * tq + lax.broadcasted_iota(jnp.int32, (tq, 1), 0)
    t_lane = c * tq + lax.broadcasted_iota(jnp.int32, (1, tq), 1)

    kc = kc_ref[0, 0]
    n_cmp_pad = kc.shape[0]
    k_idx = lax.broadcasted_iota(jnp.int32, (1, n_cmp_pad), 1)
    vis_c = (CMP_STRIDE * k_idx + (CMP_BLOCK - 1) <= t_pos) & (k_idx < n_cmp_pad - 1)
    p_c, r_c = _softmax_parts(_dot_nt(q8, kc).reshape(hpg, tq, n_cmp_pad), jnp.where(vis_c, 0.0, MASKED))
    o_c = _dot(p_c.reshape(hpg * tq, n_cmp_pad).astype(BF16), vc_ref[0, 0]).reshape(hpg, tq, HEAD_DIM) * r_c

    p_sum = jnp.sum(p_c * r_c, axis=0)
    p_hi = p_sum.astype(BF16)
    p_lo = (p_sum - p_hi.astype(F32)).astype(BF16)
    selmap_t = selmap_t_ref[...]
    imp_t = _dot_nt(selmap_t, p_hi) + _dot_nt(selmap_t, p_lo)
    blk = lax.broadcasted_iota(jnp.int32, (LANES, tq), 0)
    cur = t_lane >> _SEL_SHIFT
    forced = (blk == 0) | (blk == cur) | (blk == cur - 1)
    score_t = jnp.where(forced, SEL_BIG, jnp.where(blk <= cur, imp_t, -SEL_BIG))
    unsel_t = _topk_unselected(score_t, n_sel, n_pick)
    unsel_t = jnp.concatenate([unsel_t, jnp.zeros((LANES - n_sel, tq), F32)], axis=0)
    unsel = unsel_t.T.astype(BF16)
    q_aug = jnp.concatenate([q8, jnp.concatenate([unsel] * hpg, axis=0)], axis=1)

    def sel_tile(base, carry, bias):
        m_old, l_old, acc = carry
        k = ks_ref[0, pl.ds(base, SEL_TILE), :]
        v = vs_ref[0, pl.ds(base, SEL_TILE), :]
        s = _dot_nt(q_aug, k).reshape(hpg, tq, SEL_TILE)
        if bias is not None:
            s = s + bias[None]
        m_new = jnp.maximum(m_old, jnp.max(s, -1, keepdims=True))
        alpha = jnp.exp2(m_old - m_new)
        p = jnp.exp2(s - m_new)
        l_new = alpha * l_old + jnp.sum(p, -1, keepdims=True)
        pv = _dot(p.reshape(hpg * tq, SEL_TILE).astype(BF16), v).reshape(hpg, tq, HEAD_DIM)
        return m_new, l_new, alpha * acc + pv

    n_full = c // (SEL_TILE // tq)
    init = (jnp.full((hpg, tq, 1), M_INIT, F32), jnp.zeros((hpg, tq, 1), F32),
            jnp.zeros((hpg, tq, HEAD_DIM), F32))
    carry = lax.fori_loop(
        0, n_full, lambda kt, cr: sel_tile(pl.multiple_of(kt * SEL_TILE, SEL_TILE), cr, None), init)
    base_last = pl.multiple_of(n_full * SEL_TILE, SEL_TILE)
    kp_last = base_last + lax.broadcasted_iota(jnp.int32, (1, SEL_TILE), 1)
    _, l_s, acc_s = sel_tile(base_last, carry, jnp.where(kp_last <= t_pos, 0.0, MASKED))
    o_s = acc_s * (1.0 / jnp.maximum(l_s, 1e-30))

    span = WINDOW + tq
    start = pl.multiple_of(jnp.maximum(c * tq - WINDOW, 0), tq)
    kw = kw_ref[0, pl.ds(start, span), :]
    vw = vw_ref[0, pl.ds(start, span), :]
    kp = start + lax.broadcasted_iota(jnp.int32, (1, span), 1)
    vis_w = (kp <= t_pos) & (t_pos - kp < WINDOW)
    p_w, r_w = _softmax_parts(_dot_nt(q8, kw).reshape(hpg, tq, span), jnp.where(vis_w, 0.0, MASKED))
    o_w = _dot(p_w.reshape(hpg * tq, span).astype(BF16), vw).reshape(hpg, tq, HEAD_DIM) * r_w

    gates = gate_ref[0]
    for n in range(hpg):
        out = (gates[:, n:n + 1] * o_c[n]
               + gates[:, hpg + n:hpg + n + 1] * o_s[n]
               + gates[:, 2 * hpg + n:2 * hpg + n + 1] * o_w[n])
        o_ref[0, :, n * HEAD_DIM:(n + 1) * HEAD_DIM] = out.astype(o_ref.dtype)


def _nsa(q, cmp_kv, rest, gates, selmap_t, *, n_sel, n_pick):
    bsz, seq, _ = q.shape
    g = NSA_KV_GROUPS
    n_cmp_pad = cmp_kv.shape[2]
    gw = NSA_HPG * HEAD_DIM
    assert n_sel % SUBLANES == 0 and n_sel <= LANES and seq >= WINDOW + Q_BLOCK
    kv_spec = lambda off: pl.BlockSpec((1, seq, HEAD_DIM), lambda b, gi, c: (b, 0, off + gi))
    return pl.pallas_call(
        functools.partial(_nsa_kernel, n_sel=n_sel, n_pick=n_pick),
        grid=(bsz, g, seq // Q_BLOCK),
        in_specs=[pl.BlockSpec((1, Q_BLOCK, gw), lambda b, gi, c: (b, c, gi)),
                  pl.BlockSpec((1, 1, n_cmp_pad, HEAD_DIM), lambda b, gi, c: (gi, b, 0, 0)),
                  pl.BlockSpec((1, 1, n_cmp_pad, HEAD_DIM), lambda b, gi, c: (g + gi, b, 0, 0)),
                  pl.BlockSpec((1, seq, 2 * HEAD_DIM), lambda b, gi, c: (b, 0, gi)),
                  kv_spec(2 * g), kv_spec(3 * g), kv_spec(4 * g),
                  pl.BlockSpec((1, Q_BLOCK, LANES), lambda b, gi, c: (b, c, gi)),
                  pl.BlockSpec(selmap_t.shape, lambda b, gi, c: (0, 0))],
        out_specs=pl.BlockSpec((1, Q_BLOCK, gw), lambda b, gi, c: (b, c, gi)),
        out_shape=jax.ShapeDtypeStruct(q.shape, BF16),
        compiler_params=_params("parallel", "parallel", "arbitrary"),
        name="nsa_attention",
    )(q, cmp_kv, cmp_kv, rest, rest, rest, rest, gates, selmap_t)


def _sgu_kernel(u_ref, v_ref, g_ref, b_ref, ws_ref, bs_ref, o_ref, *, chunks):
    gd = SGU_WIDTH // SGU_GROUPS
    row = lax.broadcasted_iota(jnp.int32, (SGU_CHUNK, SGU_CHUNK), 0)
    col = lax.broadcasted_iota(jnp.int32, (SGU_CHUNK, SGU_CHUNK), 1)
    causal = col <= row
    bs = bs_ref[...]
    for ch in range(chunks):
        rows = slice(ch * SGU_CHUNK, (ch + 1) * SGU_CHUNK)
        vn = _layer_norm(v_ref[rows, :], g_ref[...], b_ref[...]).astype(BF16)
        for gi in range(SGU_GROUPS):
            cols = slice(gi * gd, (gi + 1) * gd)
            w = jnp.where(causal, ws_ref[gi], 0.0).astype(BF16)
            z = _dot(w, vn[:, cols]) + bs[:, gi:gi + 1]
            o_ref[rows, cols] = (u_ref[rows, cols] * z).astype(o_ref.dtype)


def _sgu(uv, ln_g, ln_b, w_s, b_s_t, *, chunks=2):
    m = uv.shape[0]
    tm = chunks * SGU_CHUNK
    return pl.pallas_call(
        functools.partial(_sgu_kernel, chunks=chunks),
        grid=(m // tm,),
        in_specs=[pl.BlockSpec((tm, SGU_WIDTH), lambda i: (i, 0)),
                  pl.BlockSpec((tm, SGU_WIDTH), lambda i: (i, 1)),
                  pl.BlockSpec((1, SGU_WIDTH), lambda i: (0, 0)),
                  pl.BlockSpec((1, SGU_WIDTH), lambda i: (0, 0)),
                  pl.BlockSpec(w_s.shape, lambda i: (0, 0, 0)),
                  pl.BlockSpec(b_s_t.shape, lambda i: (0, 0))],
        out_specs=pl.BlockSpec((tm, SGU_WIDTH), lambda i: (i, 0)),
        out_shape=jax.ShapeDtypeStruct((m, SGU_WIDTH), BF16),
        compiler_params=_params("parallel"),
        name="sgu",
    )(uv, uv, ln_g, ln_b, w_s, b_s_t)


def _merge_kernel(x_ref, a_ref, s_ref, wga_ref, wgb_ref, wa_ref, wb_ref, o_ref):
    x = x_ref[...]
    ga = _sigmoid(_dot(x, wga_ref[...]))
    gb = _sigmoid(_dot(x, wgb_ref[...]))
    o_ref[...] = (ga * _dot(a_ref[...], wa_ref[...]) + gb * _dot(s_ref[...], wb_ref[...])).astype(o_ref.dtype)


def _merge(x_bf, o_nsa, o_sgu, w_merge, w_a, w_b, *, tm=1024, tn=256):
    m, k = x_bf.shape
    n = w_a.shape[1]
    nb = n // tn
    row = pl.BlockSpec((tm, k), lambda i, j: (i, 0))
    col = pl.BlockSpec((k, tn), lambda i, j: (0, j))
    return pl.pallas_call(
        _merge_kernel,
        grid=(m // tm, nb),
        in_specs=[row, row, row, col, pl.BlockSpec((k, tn), lambda i, j: (0, nb + j)), col, col],
        out_specs=pl.BlockSpec((tm, tn), lambda i, j: (i, j)),
        out_shape=jax.ShapeDtypeStruct((m, n), BF16),
        compiler_params=_params("parallel", "parallel"),
        name="merge",
    )(x_bf, o_nsa, o_sgu, w_merge, w_merge, w_a, w_b)


def _proj_ln_kernel(a_ref, w_ref, x_ref, g_ref, b_ref, o_ref, obf_ref):
    y = DN_ALPHA * x_ref[...] + _dot(a_ref[...], w_ref[...])
    out = _layer_norm(y, g_ref[...], b_ref[...])
    o_ref[...] = out
    obf_ref[...] = out.astype(BF16)


def _proj_ln(a_bf, w, x, ln_g, ln_b, *, tm=256):
    m, k = a_bf.shape
    n = w.shape[1]
    vec = pl.BlockSpec((1, n), lambda i: (0, 0))
    return pl.pallas_call(
        _proj_ln_kernel,
        grid=(m // tm,),
        in_specs=[pl.BlockSpec((tm, k), lambda i: (i, 0)),
                  pl.BlockSpec((k, n), lambda i: (0, 0)),
                  pl.BlockSpec((tm, n), lambda i: (i, 0)), vec, vec],
        out_specs=[pl.BlockSpec((tm, n), lambda i: (i, 0)), pl.BlockSpec((tm, n), lambda i: (i, 0))],
        out_shape=[jax.ShapeDtypeStruct((m, n), F32), jax.ShapeDtypeStruct((m, n), BF16)],
        compiler_params=_params("parallel"),
        name="proj_ln",
    )(a_bf, w, x, ln_g, ln_b)


def _xattn_kernel(xbf_ref, x_ref, wq_ref, kv_ref, wo_ref, g_ref, b_ref, o_ref):
    q = (_dot(xbf_ref[0], wq_ref[...]) * HEAD_DIM ** -0.5).astype(BF16)
    kv = kv_ref[0]
    heads = []
    for h in range(MEM_HEADS):
        cols = slice(h * HEAD_DIM, (h + 1) * HEAD_DIM)
        s = _dot_nt(q[:, cols], kv[:, cols])
        p = jnp.exp(s - jnp.max(s, -1, keepdims=True))
        p = p / jnp.sum(p, -1, keepdims=True)
        heads.append(_dot(p.astype(BF16), kv[:, MEM_WIDTH + h * HEAD_DIM:MEM_WIDTH + (h + 1) * HEAD_DIM]))
    o = jnp.concatenate(heads, axis=1).astype(BF16)
    y = DN_ALPHA * x_ref[0] + _dot(o, wo_ref[...])
    o_ref[0] = _layer_norm(y, g_ref[...], b_ref[...])


def _xattn(x_bf, x, w_xq, mem_kv, w_xo, ln_g, ln_b, *, tm=256):
    bsz, seq, d = x.shape
    mem_len = mem_kv.shape[1]
    vec = pl.BlockSpec((1, d), lambda b, i: (0, 0))
    return pl.pallas_call(
        _xattn_kernel,
        grid=(bsz, seq // tm),
        in_specs=[pl.BlockSpec((1, tm, d), lambda b, i: (b, i, 0)),
                  pl.BlockSpec((1, tm, d), lambda b, i: (b, i, 0)),
                  pl.BlockSpec(w_xq.shape, lambda b, i: (0, 0)),
                  pl.BlockSpec((1, mem_len, 2 * MEM_WIDTH), lambda b, i: (b, 0, 0)),
                  pl.BlockSpec(w_xo.shape, lambda b, i: (0, 0)), vec, vec],
        out_specs=pl.BlockSpec((1, tm, d), lambda b, i: (b, i, 0)),
        out_shape=jax.ShapeDtypeStruct(x.shape, F32),
        compiler_params=_params("parallel", "parallel"),
        name="xattn_ln",
    )(x_bf, x, w_xq, mem_kv, w_xo, ln_g, ln_b)


def _router_kernel(x_ref, whi_ref, wlo_ref, b_ref, gate_ref, exp_ref, rank_ref, cnt_ref, seen_ref):
    @pl.when(pl.program_id(0) == 0)
    def _():
        seen_ref[...] = jnp.zeros_like(seen_ref)

    x = x_ref[...]
    x_hi = x.astype(BF16)
    x_lo = (x - x_hi.astype(F32)).astype(BF16)
    whi = whi_ref[...]
    logits = _dot(x_hi, whi) + _dot(x_hi, wlo_ref[...]) + _dot(x_lo, whi) + b_ref[...]
    rows = logits.shape[0]
    lane = lax.broadcasted_iota(jnp.int32, (rows, LANES), 1)
    lane_f = lane.astype(F32)
    is_grp = lane < N_GROUPS
    lg = jnp.where(is_grp, logits, MASKED)
    eg = jnp.where(is_grp, jnp.exp(lg - jnp.max(lg, -1, keepdims=True)), 0.0)
    p_grp = eg / jnp.sum(eg, -1, keepdims=True)
    g_grp = jnp.max(p_grp, -1, keepdims=True)
    grp = jnp.min(jnp.where((p_grp == g_grp) & is_grp, lane_f, 1e9), -1, keepdims=True).astype(jnp.int32)
    lo = N_GROUPS + grp * EXPERTS_PER_GROUP
    in_grp = (lane >= lo) & (lane < lo + EXPERTS_PER_GROUP)
    le = jnp.where(in_grp, logits, MASKED)
    ee = jnp.where(in_grp, jnp.exp(le - jnp.max(le, -1, keepdims=True)), 0.0)
    p_exp = jnp.where(in_grp, ee / jnp.sum(ee, -1, keepdims=True), -1.0)
    v1 = jnp.max(p_exp, -1, keepdims=True)
    i1 = jnp.min(jnp.where(p_exp == v1, lane_f, 1e9), -1, keepdims=True)
    p_rest = jnp.where(lane_f == i1, -1.0, p_exp)
    v2 = jnp.max(p_rest, -1, keepdims=True)
    i2 = jnp.min(jnp.where(p_rest == v2, lane_f, 1e9), -1, keepdims=True)
    denom = v1 + v2
    gate_ref[...] = jnp.where(lane == 0, g_grp * v1 / denom, jnp.where(lane == 1, g_grp * v2 / denom, 0.0))
    exp_ref[...] = jnp.where(lane == 0, i1, jnp.where(lane == 1, i2, float(N_GROUPS))).astype(jnp.int32) - N_GROUPS

    onehot = jnp.where((lane_f == i1) | (lane_f == i2), 1.0, 0.0)
    tok_r = lax.broadcasted_iota(jnp.int32, (rows, rows), 0)
    tok_c = lax.broadcasted_iota(jnp.int32, (rows, rows), 1)
    earlier = jnp.where(tok_c < tok_r, 1.0, 0.0).astype(BF16)
    before = _dot(earlier, onehot.astype(BF16)) + seen_ref[...]
    r1 = jnp.sum(jnp.where(lane_f == i1, before, 0.0), -1, keepdims=True)
    r2 = jnp.sum(jnp.where(lane_f == i2, before, 0.0), -1, keepdims=True)
    rank_ref[...] = jnp.where(lane == 0, r1, jnp.where(lane == 1, r2, 0.0)).astype(jnp.int32)
    seen_ref[...] += jnp.sum(onehot, axis=0, keepdims=True)
    cnt_ref[...] = seen_ref[...]


def _router(x, w_hi, w_lo, bias, *, tm=512):
    m, d = x.shape
    wspec = pl.BlockSpec((d, LANES), lambda i: (0, 0))
    vec = pl.BlockSpec((1, LANES), lambda i: (0, 0))
    tile = pl.BlockSpec((tm, LANES), lambda i: (i, 0))
    return pl.pallas_call(
        _router_kernel,
        grid=(m // tm,),
        in_specs=[pl.BlockSpec((tm, d), lambda i: (i, 0)), wspec, wspec, vec],
        out_specs=[tile, tile, tile, vec],
        out_shape=[jax.ShapeDtypeStruct((m, LANES), F32), jax.ShapeDtypeStruct((m, LANES), jnp.int32),
                   jax.ShapeDtypeStruct((m, LANES), jnp.int32), jax.ShapeDtypeStruct((1, LANES), F32)],
        scratch_shapes=[pltpu.VMEM((1, LANES), F32)],
        compiler_params=_params("arbitrary"),
        name="router",
    )(x, w_hi, w_lo, bias)


def _ffn_kernel(blk_exp_ref, blk_on_ref, h_ref, wg_ref, wu_ref, wd_ref, o_ref):
    i = pl.program_id(0)

    @pl.when(blk_on_ref[i] != 0)
    def _():
        h = h_ref[...].astype(BF16)
        a = _dot(h, wg_ref[0].astype(BF16))
        a = a * _sigmoid(a) * _dot(h, wu_ref[0].astype(BF16))
        o_ref[...] = _dot(a.astype(BF16), wd_ref[0].astype(BF16))

    @pl.when(blk_on_ref[i] == 0)
    def _():
        o_ref[...] = jnp.zeros_like(o_ref)


def _expert_ffn(blk_exp, blk_on, h_rows, w_gate, w_up, w_down):
    p_rows, d = h_rows.shape
    ff = w_gate.shape[2]
    n_blocks = p_rows // MOE_BLOCK
    grid_spec = pltpu.PrefetchScalarGridSpec(
        num_scalar_prefetch=2,
        grid=(n_blocks,),
        in_specs=[pl.BlockSpec((MOE_BLOCK, d), lambda i, be, on: (i, 0)),
                  pl.BlockSpec((1, d, ff), lambda i, be, on: (be[i], 0, 0)),
                  pl.BlockSpec((1, d, ff), lambda i, be, on: (be[i], 0, 0)),
                  pl.BlockSpec((1, ff, d), lambda i, be, on: (be[i], 0, 0))],
        out_specs=pl.BlockSpec((MOE_BLOCK, d), lambda i, be, on: (i, 0)),
    )
    return pl.pallas_call(
        _ffn_kernel,
        grid_spec=grid_spec,
        out_shape=jax.ShapeDtypeStruct((p_rows, d), F32),
        compiler_params=_params("arbitrary"),
        name="expert_ffn",
    )(blk_exp, blk_on, h_rows, w_gate, w_up, w_down)


def _combine_ln_kernel(x_ref, y0_ref, y1_ref, gate_ref, g_ref, b_ref, o_ref):
    gates = gate_ref[...]
    moe = y0_ref[...] * gates[:, 0:1] + y1_ref[...] * gates[:, 1:2]
    o_ref[...] = _layer_norm(DN_ALPHA * x_ref[...] + moe, g_ref[...], b_ref[...])


def _combine_ln(x, y0, y1, gates, ln_g, ln_b, *, tm=256):
    m, d = x.shape
    row = pl.BlockSpec((tm, d), lambda i: (i, 0))
    vec = pl.BlockSpec((1, d), lambda i: (0, 0))
    return pl.pallas_call(
        _combine_ln_kernel,
        grid=(m // tm,),
        in_specs=[row, row, row, pl.BlockSpec((tm, LANES), lambda i: (i, 0)), vec, vec],
        out_specs=row,
        out_shape=jax.ShapeDtypeStruct((m, d), F32),
        compiler_params=_params("parallel"),
        name="combine_ln",
    )(x, y0, y1, gates, ln_g, ln_b)


def _layer(x, mem, w_in, cmp_pe_k, cmp_w1_k, cmp_w2_k, cmp_pe_v, cmp_w1_v, cmp_w2_v,
           sgu_ln_g, sgu_ln_b, sgu_w_s, sgu_b_s, w_branch_a, w_branch_b, w_o, ln1_g, ln1_b,
           w_xq, w_xk, w_xv, w_xo, ln2_g, ln2_b,
           w_router_grp, b_router_grp, w_router_exp, b_router_exp,
           w_exp_gate, w_exp_up, w_exp_down, ln3_g, ln3_b):
    bsz, seq, d = x.shape
    n_tok = bsz * seq
    g = NSA_KV_GROUPS
    hpg = NSA_HPG
    xf = x.reshape(n_tok, d)
    x_bf = xf.astype(BF16)

    c0 = Q_WIDTH
    c1 = c0 + KV_WIDTH
    c2 = c1 + NSA_GATE_WIDTH
    c3 = c2 + 2 * SGU_WIDTH
    w_q = w_in[:, :c0].astype(BF16)
    w_gate = w_in[:, c1:c2].reshape(d, g, hpg, 3).transpose(0, 1, 3, 2).reshape(d, g, 3 * hpg)
    w_gate = jnp.pad(w_gate, ((0, 0), (0, 0), (0, LANES - 3 * hpg))).reshape(d, g * LANES)
    w_kvg = jnp.concatenate([w_in[:, c0:c1], w_gate], axis=1).astype(BF16)
    w_uv = w_in[:, c2:c3].astype(BF16)
    w_merge = w_in[:, c3:].astype(BF16)

    cos, sin = _rope_tables(jnp.arange(seq, dtype=jnp.int32))
    q = _q_proj(x_bf, w_q, cos, sin, seq).reshape(bsz, seq, Q_WIDTH)
    kv_cmp, kv_rest, nsa_gates = _kv_proj(x_bf, w_kvg, cos, sin, seq)

    n_chunks = seq // CMP_STRIDE
    pe = jnp.stack([cmp_pe_k, cmp_pe_v])
    w1 = jnp.stack([cmp_w1_k, cmp_w1_v]).astype(BF16)
    w2 = jnp.stack([cmp_w2_k, cmp_w2_v]).astype(BF16)
    cmp_end = CMP_STRIDE * jnp.arange(n_chunks, dtype=jnp.int32) + CMP_BLOCK - 1
    ccos, csin = _rope_tables(cmp_end)
    cmp_cos = jnp.stack([ccos, jnp.ones_like(ccos)])
    cmp_sin = jnp.stack([csin, jnp.zeros_like(csin)])
    cmp_kv = _compress(kv_cmp.reshape(bsz, seq, _N_CMP_COLS), pe, w1, w2, cmp_cos, cmp_sin)

    n_sel = seq // SEL_BLOCK
    n_pick = min(SEL_TOPK, n_sel)
    ci = CMP_STRIDE * jnp.arange(n_chunks)
    sj = SEL_BLOCK * jnp.arange(LANES)
    overlap = jnp.clip(jnp.minimum(ci[:, None] + CMP_BLOCK, sj[None, :] + SEL_BLOCK)
                       - jnp.maximum(ci[:, None], sj[None, :]), 0, None)
    selmap_t = (overlap.astype(F32) / CMP_BLOCK).astype(BF16).T
    o_nsa = _nsa(q, cmp_kv, kv_rest.reshape(bsz, seq, _N_REST_COLS),
                 nsa_gates.reshape(bsz, seq, _GATE_COLS), selmap_t, n_sel=n_sel, n_pick=n_pick)

    uv = _matmul(x_bf, w_uv, tm=512, tn=512, out_dtype=F32, act="gelu")
    o_sgu = _sgu(uv, sgu_ln_g.reshape(1, -1), sgu_ln_b.reshape(1, -1), sgu_w_s, sgu_b_s.T)

    merged = _merge(x_bf, o_nsa.reshape(n_tok, Q_WIDTH), o_sgu, w_merge,
                    w_branch_a.astype(BF16), w_branch_b.astype(BF16))
    x1, x1_bf = _proj_ln(merged, w_o.astype(BF16), xf, ln1_g.reshape(1, -1), ln1_b.reshape(1, -1))

    mem_len = mem.shape[1]
    w_mem = jnp.concatenate([w_xk, w_xv], axis=1).astype(BF16)
    mem_kv = _matmul(mem.reshape(bsz * mem_len, d).astype(BF16), w_mem, tm=256, tn=512, out_dtype=BF16)
    x2 = _xattn(x1_bf.reshape(bsz, seq, d), x1.reshape(bsz, seq, d), w_xq.astype(BF16),
                mem_kv.reshape(bsz, mem_len, 2 * MEM_WIDTH), w_xo.astype(BF16),
                ln2_g.reshape(1, -1), ln2_b.reshape(1, -1)).reshape(n_tok, d)

    w_r = jnp.concatenate([w_router_grp, w_router_exp], axis=1)
    w_r = jnp.pad(w_r, ((0, 0), (0, LANES - w_r.shape[1])))
    w_r_hi = w_r.astype(BF16)
    w_r_lo = (w_r - w_r_hi.astype(F32)).astype(BF16)
    b_r = jnp.concatenate([b_router_grp, b_router_exp])
    b_r = jnp.pad(b_r, (0, LANES - b_r.shape[0])).reshape(1, LANES)
    gates, experts, ranks, seen = _router(x2, w_r_hi, w_r_lo, b_r)

    nk = n_tok * EXPERT_TOPK
    counts = seen[0, N_GROUPS:N_GROUPS + N_EXPERTS].astype(jnp.int32)
    pcounts = ((counts + MOE_BLOCK - 1) // MOE_BLOCK) * MOE_BLOCK
    pend = jnp.cumsum(pcounts)
    pstart = pend - pcounts
    e_tok = experts[:, :EXPERT_TOPK]
    seg_start = jnp.sum(jnp.where(e_tok[..., None] == jnp.arange(N_EXPERTS), pstart, 0), -1)
    dest_of = seg_start + ranks[:, :EXPERT_TOPK]
    n_blocks = -(-nk // MOE_BLOCK) + N_EXPERTS
    p_rows = n_blocks * MOE_BLOCK
    tok_flat = jnp.repeat(jnp.arange(n_tok, dtype=jnp.int32), EXPERT_TOPK)
    row_tok = (jnp.arange(p_rows, dtype=jnp.int32) % n_tok).at[dest_of.reshape(nk)].set(tok_flat)
    blk_start = jnp.arange(n_blocks) * MOE_BLOCK
    blk_on = (blk_start < pend[-1]).astype(jnp.int32)
    blk_probe = jnp.minimum(blk_start, pend[-1] - 1)
    blk_exp = jnp.minimum(jnp.sum((pend[None, :] <= blk_probe[:, None]).astype(jnp.int32), -1),
                          N_EXPERTS - 1).astype(jnp.int32)

    h_rows = x2[row_tok]
    y_rows = _expert_ffn(blk_exp, blk_on, h_rows, w_exp_gate, w_exp_up, w_exp_down)
    out = _combine_ln(x2, y_rows[dest_of[:, 0]], y_rows[dest_of[:, 1]], gates,
                      ln3_g.reshape(1, -1), ln3_b.reshape(1, -1))
    return out.reshape(bsz, seq, d)


def kernel(x, mem, w_in, cmp_pe_k, cmp_w1_k, cmp_w2_k, cmp_pe_v, cmp_w1_v, cmp_w2_v, sgu_ln_g, sgu_ln_b, sgu_w_s, sgu_b_s, w_branch_a, w_branch_b, w_o, ln1_g, ln1_b, w_xq, w_xk, w_xv, w_xo, ln2_g, ln2_b, w_router_grp, b_router_grp, w_router_exp, b_router_exp, w_exp_gate, w_exp_up, w_exp_down, ln3_g, ln3_b):
    params = (w_in, cmp_pe_k, cmp_w1_k, cmp_w2_k, cmp_pe_v, cmp_w1_v, cmp_w2_v,
              sgu_ln_g, sgu_ln_b, sgu_w_s, sgu_b_s, w_branch_a, w_branch_b, w_o, ln1_g, ln1_b,
              w_xq, w_xk, w_xv, w_xo, ln2_g, ln2_b,
              w_router_grp, b_router_grp, w_router_exp, b_router_exp,
              w_exp_gate, w_exp_up, w_exp_down, ln3_g, ln3_b)
    h = x
    for l in range(DEPTH):
        h = _layer(h, mem, *[p[l] for p in params])
    return h
```

```python
import functools
import math

import jax
import jax.numpy as jnp
from jax import lax
from jax.experimental import pallas as pl
from jax.experimental.pallas import tpu as pltpu

D_MODEL = 2048
HEAD_DIM = 128
NSA_HEADS = 16
NSA_KV_GROUPS = 2
NSA_HPG = NSA_HEADS // NSA_KV_GROUPS
CMP_BLOCK = 32
CMP_STRIDE = 16
CMP_HIDDEN = 256
SEL_BLOCK = 64
SEL_TOPK = 16
WINDOW = 512
Q_BLOCK = 128
SGU_WIDTH = 2048
SGU_GROUPS = 8
SGU_CHUNK = 128
MEM_HEADS = 4
N_GROUPS = 4
EXPERTS_PER_GROUP = 16
N_EXPERTS = N_GROUPS * EXPERTS_PER_GROUP
EXPERT_TOPK = 2
EXPERT_FF = 512
MOE_BLOCK = 128
ROPE_THETA = 10000.0
LN_EPS = 1e-5
SEL_BIG = 1e9
DEPTH = 1
DN_ALPHA = (2.0 * DEPTH) ** 0.25

Q_WIDTH = NSA_HEADS * HEAD_DIM
KV_WIDTH = 3 * 2 * NSA_KV_GROUPS * HEAD_DIM
NSA_GATE_WIDTH = 3 * NSA_HEADS
MEM_WIDTH = MEM_HEADS * HEAD_DIM

LANES = 128
VMEM_LIMIT = 56 * 1024 * 1024
MASKED = -1e30
M_INIT = -1e20
BF16 = jnp.bfloat16
F32 = jnp.float32


def _params(*sem):
    return pltpu.CompilerParams(dimension_semantics=sem, vmem_limit_bytes=VMEM_LIMIT)


def _dot(a, b):
    return jnp.dot(a, b, preferred_element_type=F32)


def _dot_nt(a, b):
    return lax.dot_general(a, b, (((1,), (1,)), ((), ())), preferred_element_type=F32)


def _gelu(x):
    return 0.5 * x * (1.0 + jnp.tanh(math.sqrt(2.0 / math.pi) * (x + 0.044715 * (x * x * x))))


def _sigmoid(x):
    return 1.0 / (1.0 + jnp.exp(-x))


def _layer_norm(x, g, b):
    mu = jnp.mean(x, -1, keepdims=True)
    xc = x - mu
    var = jnp.mean(xc * xc, -1, keepdims=True)
    return xc * lax.rsqrt(var + LN_EPS) * g + b


def _rope(t, cos_full, sin_signed):
    return t * cos_full + pltpu.roll(t, HEAD_DIM // 2, axis=1) * sin_signed


def _rope_tables(pos):
    inv = ROPE_THETA ** (-jnp.arange(0, HEAD_DIM, 2, dtype=F32) / HEAD_DIM)
    ang = pos.astype(F32)[:, None] * inv[None, :]
    c, s = jnp.cos(ang), jnp.sin(ang)
    return jnp.concatenate([c, c], -1), jnp.concatenate([-s, s], -1)


MXU_COLS = 256


def _mm_kernel(a_ref, b_ref, o_ref, *, act):
    a = a_ref[...]
    for j in range(o_ref.shape[1] // MXU_COLS):
        cols = slice(j * MXU_COLS, (j + 1) * MXU_COLS)
        y = _dot(a, b_ref[:, cols])
        if act == "gelu":
            y = _gelu(y)
        o_ref[:, cols] = y.astype(o_ref.dtype)


def _matmul(a, b, *, tm, tn, out_dtype, act=None):
    m, k = a.shape
    n = b.shape[1]
    return pl.pallas_call(
        functools.partial(_mm_kernel, act=act),
        grid=(m // tm, n // tn),
        in_specs=[pl.BlockSpec((tm, k), lambda i, j: (i, 0)),
                  pl.BlockSpec((k, tn), lambda i, j: (0, j))],
        out_specs=pl.BlockSpec((tm, tn), lambda i, j: (i, j)),
        out_shape=jax.ShapeDtypeStruct((m, n), out_dtype),
        compiler_params=_params("parallel", "parallel"),
        name="matmul_" + (act or "plain"),
    )(a, b)


def _qproj_kernel(a_ref, b_ref, cos_ref, sin_ref, o_ref):
    a = a_ref[...]
    cos, sin = cos_ref[...], sin_ref[...]
    scale = HEAD_DIM ** -0.5 * math.log2(math.e)
    for j in range(o_ref.shape[1] // MXU_COLS):
        y = _dot(a, b_ref[:, j * MXU_COLS:(j + 1) * MXU_COLS])
        for h in range(MXU_COLS // HEAD_DIM):
            sl = slice(j * MXU_COLS + h * HEAD_DIM, j * MXU_COLS + (h + 1) * HEAD_DIM)
            o_ref[:, sl] = (_rope(y[:, h * HEAD_DIM:(h + 1) * HEAD_DIM], cos, sin) * scale).astype(o_ref.dtype)


def _q_proj(x_bf, w_q, cos, sin, seq, *, tm=512, tn=512):
    m, k = x_bf.shape
    n = w_q.shape[1]
    per_seq = seq // tm
    return pl.pallas_call(
        _qproj_kernel,
        grid=(m // tm, n // tn),
        in_specs=[pl.BlockSpec((tm, k), lambda i, j: (i, 0)),
                  pl.BlockSpec((k, tn), lambda i, j: (0, j)),
                  pl.BlockSpec((tm, HEAD_DIM), lambda i, j: (i % per_seq, 0)),
                  pl.BlockSpec((tm, HEAD_DIM), lambda i, j: (i % per_seq, 0))],
        out_specs=pl.BlockSpec((tm, tn), lambda i, j: (i, j)),
        out_shape=jax.ShapeDtypeStruct((m, n), BF16),
        compiler_params=_params("parallel", "parallel"),
        name="q_proj",
    )(x_bf, w_q, cos, sin)


_N_CMP_COLS = 2 * NSA_KV_GROUPS * HEAD_DIM
_N_REST_CHUNKS = (KV_WIDTH - _N_CMP_COLS) // HEAD_DIM
_N_REST_COLS = (_N_REST_CHUNKS + NSA_KV_GROUPS) * HEAD_DIM
_GATE_COLS = NSA_KV_GROUPS * LANES
_SEL_SHIFT = SEL_BLOCK.bit_length() - 1
assert 1 << _SEL_SHIFT == SEL_BLOCK


def _kvproj_kernel(a_ref, b_ref, cos_ref, sin_ref, cmp_ref, rest_ref, gate_ref, *, per_seq):
    g = NSA_KV_GROUPS
    y = _dot(a_ref[...], b_ref[...])
    cos, sin = cos_ref[...], sin_ref[...]
    cmp_ref[...] = y[:, :_N_CMP_COLS]
    tm = y.shape[0]
    pos = (pl.program_id(0) % per_seq) * tm + lax.broadcasted_iota(jnp.int32, (tm, LANES), 0)
    lane = lax.broadcasted_iota(jnp.int32, (tm, LANES), 1)
    blk_onehot = jnp.where(lane == (pos >> _SEL_SHIFT), 1.0, 0.0).astype(rest_ref.dtype)
    for c in range(_N_REST_CHUNKS):
        t = y[:, _N_CMP_COLS + c * HEAD_DIM:_N_CMP_COLS + (c + 1) * HEAD_DIM]
        if (c % (2 * g)) < g:
            t = _rope(t, cos, sin)
        dst = 2 * c if c < g else g + c
        rest_ref[:, dst * HEAD_DIM:(dst + 1) * HEAD_DIM] = t.astype(rest_ref.dtype)
        if c < g:
            rest_ref[:, (dst + 1) * HEAD_DIM:(dst + 2) * HEAD_DIM] = blk_onehot
    gate_ref[...] = _sigmoid(y[:, KV_WIDTH:])


def _kv_proj(x_bf, w_kvg, cos, sin, seq, *, tm=512):
    m, k = x_bf.shape
    n = w_kvg.shape[1]
    per_seq = seq // tm
    return pl.pallas_call(
        functools.partial(_kvproj_kernel, per_seq=per_seq),
        grid=(m // tm,),
        in_specs=[pl.BlockSpec((tm, k), lambda i: (i, 0)),
                  pl.BlockSpec((k, n), lambda i: (0, 0)),
                  pl.BlockSpec((tm, HEAD_DIM), lambda i: (i % per_seq, 0)),
                  pl.BlockSpec((tm, HEAD_DIM), lambda i: (i % per_seq, 0))],
        out_specs=[pl.BlockSpec((tm, _N_CMP_COLS), lambda i: (i, 0)),
                   pl.BlockSpec((tm, _N_REST_COLS), lambda i: (i, 0)),
                   pl.BlockSpec((tm, _GATE_COLS), lambda i: (i, 0))],
        out_shape=[jax.ShapeDtypeStruct((m, _N_CMP_COLS), F32),
                   jax.ShapeDtypeStruct((m, _N_REST_COLS), BF16),
                   jax.ShapeDtypeStruct((m, _GATE_COLS), F32)],
        compiler_params=_params("parallel"),
        name="kv_gate_proj",
    )(x_bf, w_kvg, cos, sin)


def _compress_kernel(kc_ref, pe_ref, w1_ref, w2_ref, cos_ref, sin_ref, o_ref):
    n_chunks = o_ref.shape[2]
    top = jnp.zeros((n_chunks, CMP_HIDDEN), F32)
    bot = jnp.zeros((n_chunks, CMP_HIDDEN), F32)
    for p in range(CMP_STRIDE):
        xp = kc_ref[0, pl.ds(p, n_chunks, stride=CMP_STRIDE), :]
        q = CMP_STRIDE + p
        top += _dot((xp + pe_ref[0, p:p + 1, :]).astype(BF16), w1_ref[0, p * HEAD_DIM:(p + 1) * HEAD_DIM, :])
        bot += _dot((xp + pe_ref[0, q:q + 1, :]).astype(BF16), w1_ref[0, q * HEAD_DIM:(q + 1) * HEAD_DIM, :])
    h = top + pltpu.roll(bot, n_chunks - 1, axis=0)
    y = _dot(_gelu(h).astype(BF16), w2_ref[0])
    o_ref[0, 0] = _rope(y, cos_ref[0], sin_ref[0]).astype(o_ref.dtype)


def _compress(kv_cmp, pe, w1, w2, cos, sin):
    bsz, seq, width = kv_cmp.shape
    four = width // HEAD_DIM
    n_chunks = seq // CMP_STRIDE
    g = NSA_KV_GROUPS
    return pl.pallas_call(
        _compress_kernel,
        grid=(four, bsz),
        in_specs=[pl.BlockSpec((1, seq, HEAD_DIM), lambda a, b: (b, 0, a)),
                  pl.BlockSpec((1, CMP_BLOCK, HEAD_DIM), lambda a, b: (a // g, 0, 0)),
                  pl.BlockSpec((1, CMP_BLOCK * HEAD_DIM, CMP_HIDDEN), lambda a, b: (a // g, 0, 0)),
                  pl.BlockSpec((1, CMP_HIDDEN, HEAD_DIM), lambda a, b: (a // g, 0, 0)),
                  pl.BlockSpec((1, n_chunks, HEAD_DIM), lambda a, b: (a // g, 0, 0)),
                  pl.BlockSpec((1, n_chunks, HEAD_DIM), lambda a, b: (a // g, 0, 0))],
        out_specs=pl.BlockSpec((1, 1, n_chunks, HEAD_DIM), lambda a, b: (a, b, 0, 0)),
        out_shape=jax.ShapeDtypeStruct((four, bsz, n_chunks, HEAD_DIM), BF16),
        compiler_params=_params("parallel", "parallel"),
        name="compress",
    )(kv_cmp, pe, w1, w2, cos, sin)


SEL_TILE = 512
SUBLANES = 8


def _softmax_parts(s3, bias):
    s3 = s3 + bias[None]
    m = jnp.max(s3, -1, keepdims=True)
    m = jnp.where(m > 0.5 * MASKED, m, 0.0)
    p = jnp.exp2(s3 - m)
    return p, 1.0 / jnp.maximum(jnp.sum(p, -1, keepdims=True), 1e-30)


def _topk_unselected(score_t, n_sel, n_pick):
    tq = score_t.shape[1]
    n_grp = n_sel // SUBLANES
    grp = [score_t[g * SUBLANES:(g + 1) * SUBLANES] for g in range(n_grp)]
    row_in_grp = lax.broadcasted_iota(jnp.int32, (SUBLANES, tq), 0)
    later = [jnp.where(row_in_grp > li, 1.0, 0.0) for li in range(SUBLANES)]
    beaten = [jnp.zeros((SUBLANES, tq), F32) for _ in range(n_grp)]
    for i in range(n_sel):
        gi, li = divmod(i, SUBLANES)
        row = grp[gi][li:li + 1, :]
        for g in range(n_grp):
            if g > gi:
                beaten[g] = beaten[g] + jnp.where(row >= grp[g], 1.0, 0.0)
            elif g < gi:
                beaten[g] = beaten[g] + jnp.where(row > grp[g], 1.0, 0.0)
            else:
                beaten[g] = (beaten[g] + jnp.where(row > grp[g], 1.0, 0.0)
                             + jnp.where(row == grp[g], later[li], 0.0))
    return jnp.concatenate([jnp.where(b < n_pick, 0.0, MASKED) for b in beaten], axis=0)


def _nsa_kernel(q_ref, kc_ref, vc_ref, ks_ref, vs_ref, kw_ref, vw_ref, gate_ref, selmap_t_ref, o_ref,
                *, n_sel, n_pick):
    hpg = NSA_HPG
    tq = Q_BLOCK
    c = pl.program_id(2)
    q_t = q_ref[0]
    q8 = jnp.concatenate([q_t[:, n * HEAD_DIM:(n + 1) * HEAD_DIM] for n in range(hpg)], axis=0)
    t_pos = c * tq + lax.broadcasted_iota(jnp.int32, (tq, 1), 0)
    t_lane = c * tq + lax.broadcasted_iota(jnp.int32, (1, tq), 1)

    kc = kc_ref[0, 0]
    n_cmp_pad = kc.shape[0]
    k_idx = lax.broadcasted_iota(jnp.int32, (1, n_cmp_pad), 1)
    vis_c = (CMP_STRIDE * k_idx + (CMP_BLOCK - 1) <= t_pos) & (k_idx < n_cmp_pad - 1)
    p_c, r_c = _softmax_parts(_dot_nt(q8, kc).reshape(hpg, tq, n_cmp_pad), jnp.where(vis_c, 0.0, MASKED))
    o_c = _dot(p_c.reshape(hpg * tq, n_cmp_pad).astype(BF16), vc_ref[0, 0]).reshape(hpg, tq, HEAD_DIM) * r_c

    p_sum = jnp.sum(p_c * r_c, axis=0)
    p_hi = p_sum.astype(BF16)
    p_lo = (p_sum - p_hi.astype(F32)).astype(BF16)
    selmap_t = selmap_t_ref[...]
    imp_t = _dot_nt(selmap_t, p_hi) + _dot_nt(selmap_t, p_lo)
    blk = lax.broadcasted_iota(jnp.int32, (LANES, tq), 0)
    cur = t_lane >> _SEL_SHIFT
    forced = (blk == 0) | (blk == cur) | (blk == cur - 1)
    score_t = jnp.where(forced, SEL_BIG, jnp.where(blk <= cur, imp_t, -SEL_BIG))
    unsel_t = _topk_unselected(score_t, n_sel, n_pick)
    unsel_t = jnp.concatenate([unsel_t, jnp.zeros((LANES - n_sel, tq), F32)], axis=0)
    unsel = unsel_t.T.astype(BF16)
    q_aug = jnp.concatenate([q8, jnp.concatenate([unsel] * hpg, axis=0)], axis=1)

    def sel_tile(base, carry, bias):
        m_old, l_old, acc = carry
        k = ks_ref[0, pl.ds(base, SEL_TILE), :]
        v = vs_ref[0, pl.ds(base, SEL_TILE), :]
        s = _dot_nt(q_aug, k).reshape(hpg, tq, SEL_TILE)
        if bias is not None:
            s = s + bias[None]
        m_new = jnp.maximum(m_old, jnp.max(s, -1, keepdims=True))
        alpha = jnp.exp2(m_old - m_new)
        p = jnp.exp2(s - m_new)
        l_new = alpha * l_old + jnp.sum(p, -1, keepdims=True)
        pv = _dot(p.reshape(hpg * tq, SEL_TILE).astype(BF16), v).reshape(hpg, tq, HEAD_DIM)
        return m_new, l_new, alpha * acc + pv

    n_full = c // (SEL_TILE // tq)
    init = (jnp.full((hpg, tq, 1), M_INIT, F32), jnp.zeros((hpg, tq, 1), F32),
            jnp.zeros((hpg, tq, HEAD_DIM), F32))
    carry = lax.fori_loop(
        0, n_full, lambda kt, cr: sel_tile(pl.multiple_of(kt * SEL_TILE, SEL_TILE), cr, None), init)
    base_last = pl.multiple_of(n_full * SEL_TILE, SEL_TILE)
    kp_last = base_last + lax.broadcasted_iota(jnp.int32, (1, SEL_TILE), 1)
    _, l_s, acc_s = sel_tile(base_last, carry, jnp.where(kp_last <= t_pos, 0.0, MASKED))
    o_s = acc_s * (1.0 / jnp.maximum(l_s, 1e-30))

    span = WINDOW + tq
    start = pl.multiple_of(jnp.maximum(c * tq - WINDOW, 0), tq)
    kw = kw_ref[0, pl.ds(start, span), :]
    vw = vw_ref[0, pl.ds(start, span), :]
    kp = start + lax.broadcasted_iota(jnp.int32, (1, span), 1)
    vis_w = (kp <= t_pos) & (t_pos - kp < WINDOW)
    p_w, r_w = _softmax_parts(_dot_nt(q8, kw).reshape(hpg, tq, span), jnp.where(vis_w, 0.0, MASKED))
    o_w = _dot(p_w.reshape(hpg * tq, span).astype(BF16), vw).reshape(hpg, tq, HEAD_DIM) * r_w

    gates = gate_ref[0]
    for n in range(hpg):
        out = (gates[:, n:n + 1] * o_c[n]
               + gates[:, hpg + n:hpg + n + 1] * o_s[n]
               + gates[:, 2 * hpg + n:2 * hpg + n + 1] * o_w[n])
        o_ref[0, :, n * HEAD_DIM:(n + 1) * HEAD_DIM] = out.astype(o_ref.dtype)


def _nsa(q, cmp_kv, rest, gates, selmap_t, *, n_sel, n_pick):
    bsz, seq, _ = q.shape
    g = NSA_KV_GROUPS
    n_cmp_pad = cmp_kv.shape[2]
    gw = NSA_HPG * HEAD_DIM
    assert n_sel % SUBLANES == 0 and n_sel <= LANES and seq >= WINDOW + Q_BLOCK
    kv_spec = lambda off: pl.BlockSpec((1, seq, HEAD_DIM), lambda b, gi, c: (b, 0, off + gi))
    return pl.pallas_call(
        functools.partial(_nsa_kernel, n_sel=n_sel, n_pick=n_pick),
        grid=(bsz, g, seq // Q_BLOCK),
        in_specs=[pl.BlockSpec((1, Q_BLOCK, gw), lambda b, gi, c: (b, c, gi)),
                  pl.BlockSpec((1, 1, n_cmp_pad, HEAD_DIM), lambda b, gi, c: (gi, b, 0, 0)),
                  pl.BlockSpec((1, 1, n_cmp_pad, HEAD_DIM), lambda b, gi, c: (g + gi, b, 0, 0)),
                  pl.BlockSpec((1, seq, 2 * HEAD_DIM), lambda b, gi, c: (b, 0, gi)),
                  kv_spec(2 * g), kv_spec(3 * g), kv_spec(4 * g),
                  pl.BlockSpec((1, Q_BLOCK, LANES), lambda b, gi, c: (b, c, gi)),
                  pl.BlockSpec(selmap_t.shape, lambda b, gi, c: (0, 0))],
        out_specs=pl.BlockSpec((1, Q_BLOCK, gw), lambda b, gi, c: (b, c, gi)),
        out_shape=jax.ShapeDtypeStruct(q.shape, BF16),
        compiler_params=_params("parallel", "parallel", "arbitrary"),
        name="nsa_attention",
    )(q, cmp_kv, cmp_kv, rest, rest, rest, rest, gates, selmap_t)


def _sgu_kernel(u_ref, v_ref, g_ref, b_ref, ws_ref, bs_ref, o_ref, *, chunks):
    gd = SGU_WIDTH // SGU_GROUPS
    row = lax.broadcasted_iota(jnp.int32, (SGU_CHUNK, SGU_CHUNK), 0)
    col = lax.broadcasted_iota(jnp.int32, (SGU_CHUNK, SGU_CHUNK), 1)
    causal = col <= row
    bs = bs_ref[...]
    for ch in range(chunks):
        rows = slice(ch * SGU_CHUNK, (ch + 1) * SGU_CHUNK)
        vn = _layer_norm(v_ref[rows, :], g_ref[...], b_ref[...]).astype(BF16)
        for gi in range(SGU_GROUPS):
            cols = slice(gi * gd, (gi + 1) * gd)
            w = jnp.where(causal, ws_ref[gi], 0.0).astype(BF16)
            z = _dot(w, vn[:, cols]) + bs[:, gi:gi + 1]
            o_ref[rows, cols] = (u_ref[rows, cols] * z).astype(o_ref.dtype)


def _sgu(uv, ln_g, ln_b, w_s, b_s_t, *, chunks=2):
    m = uv.shape[0]
    tm = chunks * SGU_CHUNK
    return pl.pallas_call(
        functools.partial(_sgu_kernel, chunks=chunks),
        grid=(m // tm,),
        in_specs=[pl.BlockSpec((tm, SGU_WIDTH), lambda i: (i, 0)),
                  pl.BlockSpec((tm, SGU_WIDTH), lambda i: (i, 1)),
                  pl.BlockSpec((1, SGU_WIDTH), lambda i: (0, 0)),
                  pl.BlockSpec((1, SGU_WIDTH), lambda i: (0, 0)),
                  pl.BlockSpec(w_s.shape, lambda i: (0, 0, 0)),
                  pl.BlockSpec(b_s_t.shape, lambda i: (0, 0))],
        out_specs=pl.BlockSpec((tm, SGU_WIDTH), lambda i: (i, 0)),
        out_shape=jax.ShapeDtypeStruct((m, SGU_WIDTH), BF16),
        compiler_params=_params("parallel"),
        name="sgu",
    )(uv, uv, ln_g, ln_b, w_s, b_s_t)


def _merge_kernel(x_ref, a_ref, s_ref, wga_ref, wgb_ref, wa_ref, wb_ref, o_ref):
    x = x_ref[...]
    ga = _sigmoid(_dot(x, wga_ref[...]))
    gb = _sigmoid(_dot(x, wgb_ref[...]))
    o_ref[...] = (ga * _dot(a_ref[...], wa_ref[...]) + gb * _dot(s_ref[...], wb_ref[...])).astype(o_ref.dtype)


def _merge(x_bf, o_nsa, o_sgu, w_merge, w_a, w_b, *, tm=1024, tn=256):
    m, k = x_bf.shape
    n = w_a.shape[1]
    nb = n // tn
    row = pl.BlockSpec((tm, k), lambda i, j: (i, 0))
    col = pl.BlockSpec((k, tn), lambda i, j: (0, j))
    return pl.pallas_call(
        _merge_kernel,
        grid=(m // tm, nb),
        in_specs=[row, row, row, col, pl.BlockSpec((k, tn), lambda i, j: (0, nb + j)), col, col],
        out_specs=pl.BlockSpec((tm, tn), lambda i, j: (i, j)),
        out_shape=jax.ShapeDtypeStruct((m, n), BF16),
        compiler_params=_params("parallel", "parallel"),
        name="merge",
    )(x_bf, o_nsa, o_sgu, w_merge, w_merge, w_a, w_b)


def _proj_ln_kernel(a_ref, w_ref, x_ref, g_ref, b_ref, o_ref, obf_ref):
    y = DN_ALPHA * x_ref[...] + _dot(a_ref[...], w_ref[...])
    out = _layer_norm(y, g_ref[...], b_ref[...])
    o_ref[...] = out
    obf_ref[...] = out.astype(BF16)


def _proj_ln(a_bf, w, x, ln_g, ln_b, *, tm=256):
    m, k = a_bf.shape
    n = w.shape[1]
    vec = pl.BlockSpec((1, n), lambda i: (0, 0))
    return pl.pallas_call(
        _proj_ln_kernel,
        grid=(m // tm,),
        in_specs=[pl.BlockSpec((tm, k), lambda i: (i, 0)),
                  pl.BlockSpec((k, n), lambda i: (0, 0)),
                  pl.BlockSpec((tm, n), lambda i: (i, 0)), vec, vec],
        out_specs=[pl.BlockSpec((tm, n), lambda i: (i, 0)), pl.BlockSpec((tm, n), lambda i: (i, 0))],
        out_shape=[jax.ShapeDtypeStruct((m, n), F32), jax.ShapeDtypeStruct((m, n), BF16)],
        compiler_params=_params("parallel"),
        name="proj_ln",
    )(a_bf, w, x, ln_g, ln_b)


def _xattn_kernel(xbf_ref, x_ref, wq_ref, kv_ref, wo_ref, g_ref, b_ref, o_ref):
    q = (_dot(xbf_ref[0], wq_ref[...]) * HEAD_DIM ** -0.5).astype(BF16)
    kv = kv_ref[0]
    heads = []
    for h in range(MEM_HEADS):
        cols = slice(h * HEAD_DIM, (h + 1) * HEAD_DIM)
        s = _dot_nt(q[:, cols], kv[:, cols])
        p = jnp.exp(s - jnp.max(s, -1, keepdims=True))
        p = p / jnp.sum(p, -1, keepdims=True)
        heads.append(_dot(p.astype(BF16), kv[:, MEM_WIDTH + h * HEAD_DIM:MEM_WIDTH + (h + 1) * HEAD_DIM]))
    o = jnp.concatenate(heads, axis=1).astype(BF16)
    y = DN_ALPHA * x_ref[0] + _dot(o, wo_ref[...])
    o_ref[0] = _layer_norm(y, g_ref[...], b_ref[...])


def _xattn(x_bf, x, w_xq, mem_kv, w_xo, ln_g, ln_b, *, tm=256):
    bsz, seq, d = x.shape
    mem_len = mem_kv.shape[1]
    vec = pl.BlockSpec((1, d), lambda b, i: (0, 0))
    return pl.pallas_call(
        _xattn_kernel,
        grid=(bsz, seq // tm),
        in_specs=[pl.BlockSpec((1, tm, d), lambda b, i: (b, i, 0)),
                  pl.BlockSpec((1, tm, d), lambda b, i: (b, i, 0)),
                  pl.BlockSpec(w_xq.shape, lambda b, i: (0, 0)),
                  pl.BlockSpec((1, mem_len, 2 * MEM_WIDTH), lambda b, i: (b, 0, 0)),
                  pl.BlockSpec(w_xo.shape, lambda b, i: (0, 0)), vec, vec],
        out_specs=pl.BlockSpec((1, tm, d), lambda b, i: (b, i, 0)),
        out_shape=jax.ShapeDtypeStruct(x.shape, F32),
        compiler_params=_params("parallel", "parallel"),
        name="xattn_ln",
    )(x_bf, x, w_xq, mem_kv, w_xo, ln_g, ln_b)


def _router_kernel(x_ref, whi_ref, wlo_ref, b_ref, gate_ref, exp_ref, rank_ref, cnt_ref, seen_ref):
    @pl.when(pl.program_id(0) == 0)
    def _():
        seen_ref[...] = jnp.zeros_like(seen_ref)

    x = x_ref[...]
    x_hi = x.astype(BF16)
    x_lo = (x - x_hi.astype(F32)).astype(BF16)
    whi = whi_ref[...]
    logits = _dot(x_hi, whi) + _dot(x_hi, wlo_ref[...]) + _dot(x_lo, whi) + b_ref[...]
    rows = logits.shape[0]
    lane = lax.broadcasted_iota(jnp.int32, (rows, LANES), 1)
    lane_f = lane.astype(F32)
    is_grp = lane < N_GROUPS
    lg = jnp.where(is_grp, logits, MASKED)
    eg = jnp.where(is_grp, jnp.exp(lg - jnp.max(lg, -1, keepdims=True)), 0.0)
    p_grp = eg / jnp.sum(eg, -1, keepdims=True)
    g_grp = jnp.max(p_grp, -1, keepdims=True)
    grp = jnp.min(jnp.where((p_grp == g_grp) & is_grp, lane_f, 1e9), -1, keepdims=True).astype(jnp.int32)
    lo = N_GROUPS + grp * EXPERTS_PER_GROUP
    in_grp = (lane >= lo) & (lane < lo + EXPERTS_PER_GROUP)
    le = jnp.where(in_grp, logits, MASKED)
    ee = jnp.where(in_grp, jnp.exp(le - jnp.max(le, -1, keepdims=True)), 0.0)
    p_exp = jnp.where(in_grp, ee / jnp.sum(ee, -1, keepdims=True), -1.0)
    v1 = jnp.max(p_exp, -1, keepdims=True)
    i1 = jnp.min(jnp.where(p_exp == v1, lane_f, 1e9), -1, keepdims=True)
    p_rest = jnp.where(lane_f == i1, -1.0, p_exp)
    v2 = jnp.max(p_rest, -1, keepdims=True)
    i2 = jnp.min(jnp.where(p_rest == v2, lane_f, 1e9), -1, keepdims=True)
    denom = v1 + v2
    gate_ref[...] = jnp.where(lane == 0, g_grp * v1 / denom, jnp.where(lane == 1, g_grp * v2 / denom, 0.0))
    exp_ref[...] = jnp.where(lane == 0, i1, jnp.where(lane == 1, i2, float(N_GROUPS))).astype(jnp.int32) - N_GROUPS

    onehot = jnp.where((lane_f == i1) | (lane_f == i2), 1.0, 0.0)
    tok_r = lax.broadcasted_iota(jnp.int32, (rows, rows), 0)
    tok_c = lax.broadcasted_iota(jnp.int32, (rows, rows), 1)
    earlier = jnp.where(tok_c < tok_r, 1.0, 0.0).astype(BF16)
    before = _dot(earlier, onehot.astype(BF16)) + seen_ref[...]
    r1 = jnp.sum(jnp.where(lane_f == i1, before, 0.0), -1, keepdims=True)
    r2 = jnp.sum(jnp.where(lane_f == i2, before, 0.0), -1, keepdims=True)
    rank_ref[...] = jnp.where(lane == 0, r1, jnp.where(lane == 1, r2, 0.0)).astype(jnp.int32)
    seen_ref[...] += jnp.sum(onehot, axis=0, keepdims=True)
    cnt_ref[...] = seen_ref[...]


def _router(x, w_hi, w_lo, bias, *, tm=512):
    m, d = x.shape
    wspec = pl.BlockSpec((d, LANES), lambda i: (0, 0))
    vec = pl.BlockSpec((1, LANES), lambda i: (0, 0))
    tile = pl.BlockSpec((tm, LANES), lambda i: (i, 0))
    return pl.pallas_call(
        _router_kernel,
        grid=(m // tm,),
        in_specs=[pl.BlockSpec((tm, d), lambda i: (i, 0)), wspec, wspec, vec],
        out_specs=[tile, tile, tile, vec],
        out_shape=[jax.ShapeDtypeStruct((m, LANES), F32), jax.ShapeDtypeStruct((m, LANES), jnp.int32),
                   jax.ShapeDtypeStruct((m, LANES), jnp.int32), jax.ShapeDtypeStruct((1, LANES), F32)],
        scratch_shapes=[pltpu.VMEM((1, LANES), F32)],
        compiler_params=_params("arbitrary"),
        name="router",
    )(x, w_hi, w_lo, bias)


def _ffn_kernel(exp_ref, on_ref, first_ref, slot_ref, next_ref, h_ref, wg_hbm, wu_hbm, wd_hbm, o_ref,
                wg_buf, wu_buf, wd_buf, sem):
    i = pl.program_id(0)
    slot = slot_ref[i]

    def weight_copies(e, s):
        return (pltpu.make_async_copy(wg_hbm.at[e], wg_buf.at[s], sem.at[s, 0]),
                pltpu.make_async_copy(wu_hbm.at[e], wu_buf.at[s], sem.at[s, 1]),
                pltpu.make_async_copy(wd_hbm.at[e], wd_buf.at[s], sem.at[s, 2]))

    @pl.when(i == 0)
    def _():
        for cp in weight_copies(exp_ref[0], 0):
            cp.start()

    @pl.when(first_ref[i] != 0)
    def _():
        nxt = next_ref[i]

        @pl.when(nxt >= 0)
        def _():
            for cp in weight_copies(nxt, 1 - slot):
                cp.start()

        for cp in weight_copies(exp_ref[i], slot):
            cp.wait()

    @pl.when(on_ref[i] != 0)
    def _():
        h = h_ref[...].astype(BF16)
        a = _dot(h, wg_buf[slot].astype(BF16))
        a = a * _sigmoid(a) * _dot(h, wu_buf[slot].astype(BF16))
        o_ref[...] = _dot(a.astype(BF16), wd_buf[slot].astype(BF16))

    @pl.when(on_ref[i] == 0)
    def _():
        o_ref[...] = jnp.zeros_like(o_ref)


def _expert_ffn(blk_exp, blk_on, h_rows, w_gate, w_up, w_down):
    p_rows, d = h_rows.shape
    n_exp, _, ff = w_gate.shape
    n_blocks = p_rows // MOE_BLOCK
    prev_exp = jnp.concatenate([jnp.full((1,), -1, jnp.int32), blk_exp[:-1]])
    first = ((blk_on != 0) & (blk_exp != prev_exp)).astype(jnp.int32)
    seg_id = jnp.cumsum(first) - 1
    seg_ids = jnp.arange(n_exp + 1)
    hit = (first[:, None] != 0) & (seg_id[:, None] == seg_ids[None, :])
    seg_exp = jnp.sum(jnp.where(hit, blk_exp[:, None] + 1, 0), 0) - 1
    blk_next = jnp.sum(jnp.where(seg_ids[None, :] == seg_id[:, None] + 1, seg_exp[None, :] + 1, 0), 1) - 1
    blk_slot = jnp.maximum(seg_id, 0) % 2
    grid_spec = pltpu.PrefetchScalarGridSpec(
        num_scalar_prefetch=5,
        grid=(n_blocks,),
        in_specs=[pl.BlockSpec((MOE_BLOCK, d), lambda i, *_: (i, 0)),
                  pl.BlockSpec(memory_space=pl.ANY),
                  pl.BlockSpec(memory_space=pl.ANY),
                  pl.BlockSpec(memory_space=pl.ANY)],
        out_specs=pl.BlockSpec((MOE_BLOCK, d), lambda i, *_: (i, 0)),
        scratch_shapes=[pltpu.VMEM((2, d, ff), F32), pltpu.VMEM((2, d, ff), F32), pltpu.VMEM((2, ff, d), F32),
                        pltpu.SemaphoreType.DMA((2, 3))],
    )
    return pl.pallas_call(
        _ffn_kernel,
        grid_spec=grid_spec,
        out_shape=jax.ShapeDtypeStruct((p_rows, d), F32),
        compiler_params=_params("arbitrary"),
        name="expert_ffn",
    )(blk_exp, blk_on, first, blk_slot.astype(jnp.int32), blk_next.astype(jnp.int32),
      h_rows, w_gate, w_up, w_down)


def _combine_ln_kernel(x_ref, y0_ref, y1_ref, gate_ref, g_ref, b_ref, o_ref):
    gates = gate_ref[...]
    moe = y0_ref[...] * gates[:, 0:1] + y1_ref[...] * gates[:, 1:2]
    o_ref[...] = _layer_norm(DN_ALPHA * x_ref[...] + moe, g_ref[...], b_ref[...])


def _combine_ln(x, y0, y1, gates, ln_g, ln_b, *, tm=256):
    m, d = x.shape
    row = pl.BlockSpec((tm, d), lambda i: (i, 0))
    vec = pl.BlockSpec((1, d), lambda i: (0, 0))
    return pl.pallas_call(
        _combine_ln_kernel,
        grid=(m // tm,),
        in_specs=[row, row, row, pl.BlockSpec((tm, LANES), lambda i: (i, 0)), vec, vec],
        out_specs=row,
        out_shape=jax.ShapeDtypeStruct((m, d), F32),
        compiler_params=_params("parallel"),
        name="combine_ln",
    )(x, y0, y1, gates, ln_g, ln_b)


def _layer(x, mem, w_in, cmp_pe_k, cmp_w1_k, cmp_w2_k, cmp_pe_v, cmp_w1_v, cmp_w2_v,
           sgu_ln_g, sgu_ln_b, sgu_w_s, sgu_b_s, w_branch_a, w_branch_b, w_o, ln1_g, ln1_b,
           w_xq, w_xk, w_xv, w_xo, ln2_g, ln2_b,
           w_router_grp, b_router_grp, w_router_exp, b_router_exp,
           w_exp_gate, w_exp_up, w_exp_down, ln3_g, ln3_b):
    bsz, seq, d = x.shape
    n_tok = bsz * seq
    g = NSA_KV_GROUPS
    hpg = NSA_HPG
    xf = x.reshape(n_tok, d)
    x_bf = xf.astype(BF16)

    c0 = Q_WIDTH
    c1 = c0 + KV_WIDTH
    c2 = c1 + NSA_GATE_WIDTH
    c3 = c2 + 2 * SGU_WIDTH
    w_q = w_in[:, :c0].astype(BF16)
    w_gate = w_in[:, c1:c2].reshape(d, g, hpg, 3).transpose(0, 1, 3, 2).reshape(d, g, 3 * hpg)
    w_gate = jnp.pad(w_gate, ((0, 0), (0, 0), (0, LANES - 3 * hpg))).reshape(d, g * LANES)
    w_kvg = jnp.concatenate([w_in[:, c0:c1], w_gate], axis=1).astype(BF16)
    w_uv = w_in[:, c2:c3].astype(BF16)
    w_merge = w_in[:, c3:].astype(BF16)

    cos, sin = _rope_tables(jnp.arange(seq, dtype=jnp.int32))
    q = _q_proj(x_bf, w_q, cos, sin, seq).reshape(bsz, seq, Q_WIDTH)
    kv_cmp, kv_rest, nsa_gates = _kv_proj(x_bf, w_kvg, cos, sin, seq)

    n_chunks = seq // CMP_STRIDE
    pe = jnp.stack([cmp_pe_k, cmp_pe_v])
    w1 = jnp.stack([cmp_w1_k, cmp_w1_v]).astype(BF16)
    w2 = jnp.stack([cmp_w2_k, cmp_w2_v]).astype(BF16)
    cmp_end = CMP_STRIDE * jnp.arange(n_chunks, dtype=jnp.int32) + CMP_BLOCK - 1
    ccos, csin = _rope_tables(cmp_end)
    cmp_cos = jnp.stack([ccos, jnp.ones_like(ccos)])
    cmp_sin = jnp.stack([csin, jnp.zeros_like(csin)])
    cmp_kv = _compress(kv_cmp.reshape(bsz, seq, _N_CMP_COLS), pe, w1, w2, cmp_cos, cmp_sin)

    n_sel = seq // SEL_BLOCK
    n_pick = min(SEL_TOPK, n_sel)
    ci = CMP_STRIDE * jnp.arange(n_chunks)
    sj = SEL_BLOCK * jnp.arange(LANES)
    overlap = jnp.clip(jnp.minimum(ci[:, None] + CMP_BLOCK, sj[None, :] + SEL_BLOCK)
                       - jnp.maximum(ci[:, None], sj[None, :]), 0, None)
    selmap_t = (overlap.astype(F32) / CMP_BLOCK).astype(BF16).T
    o_nsa = _nsa(q, cmp_kv, kv_rest.reshape(bsz, seq, _N_REST_COLS),
                 nsa_gates.reshape(bsz, seq, _GATE_COLS), selmap_t, n_sel=n_sel, n_pick=n_pick)

    uv = _matmul(x_bf, w_uv, tm=512, tn=512, out_dtype=F32, act="gelu")
    o_sgu = _sgu(uv, sgu_ln_g.reshape(1, -1), sgu_ln_b.reshape(1, -1), sgu_w_s, sgu_b_s.T)

    merged = _merge(x_bf, o_nsa.reshape(n_tok, Q_WIDTH), o_sgu, w_merge,
                    w_branch_a.astype(BF16), w_branch_b.astype(BF16))
    x1, x1_bf = _proj_ln(merged, w_o.astype(BF16), xf, ln1_g.reshape(1, -1), ln1_b.reshape(1, -1))

    mem_len = mem.shape[1]
    w_mem = jnp.concatenate([w_xk, w_xv], axis=1).astype(BF16)
    mem_kv = _matmul(mem.reshape(bsz * mem_len, d).astype(BF16), w_mem, tm=256, tn=512, out_dtype=BF16)
    x2 = _xattn(x1_bf.reshape(bsz, seq, d), x1.reshape(bsz, seq, d), w_xq.astype(BF16),
                mem_kv.reshape(bsz, mem_len, 2 * MEM_WIDTH), w_xo.astype(BF16),
                ln2_g.reshape(1, -1), ln2_b.reshape(1, -1)).reshape(n_tok, d)

    w_r = jnp.concatenate([w_router_grp, w_router_exp], axis=1)
    w_r = jnp.pad(w_r, ((0, 0), (0, LANES - w_r.shape[1])))
    w_r_hi = w_r.astype(BF16)
    w_r_lo = (w_r - w_r_hi.astype(F32)).astype(BF16)
    b_r = jnp.concatenate([b_router_grp, b_router_exp])
    b_r = jnp.pad(b_r, (0, LANES - b_r.shape[0])).reshape(1, LANES)
    gates, experts, ranks, seen = _router(x2, w_r_hi, w_r_lo, b_r)

    nk = n_tok * EXPERT_TOPK
    counts = seen[0, N_GROUPS:N_GROUPS + N_EXPERTS].astype(jnp.int32)
    pcounts = ((counts + MOE_BLOCK - 1) // MOE_BLOCK) * MOE_BLOCK
    pend = jnp.cumsum(pcounts)
    pstart = pend - pcounts
    e_tok = experts[:, :EXPERT_TOPK]
    seg_start = jnp.sum(jnp.where(e_tok[..., None] == jnp.arange(N_EXPERTS), pstart, 0), -1)
    dest_of = seg_start + ranks[:, :EXPERT_TOPK]
    n_blocks = -(-nk // MOE_BLOCK) + N_EXPERTS
    p_rows = n_blocks * MOE_BLOCK
    tok_flat = jnp.repeat(jnp.arange(n_tok, dtype=jnp.int32), EXPERT_TOPK)
    row_tok = (jnp.arange(p_rows, dtype=jnp.int32) % n_tok).at[dest_of.reshape(nk)].set(tok_flat)
    blk_start = jnp.arange(n_blocks) * MOE_BLOCK
    blk_on = (blk_start < pend[-1]).astype(jnp.int32)
    blk_probe = jnp.minimum(blk_start, pend[-1] - 1)
    blk_exp = jnp.minimum(jnp.sum((pend[None, :] <= blk_probe[:, None]).astype(jnp.int32), -1),
                          N_EXPERTS - 1).astype(jnp.int32)

    h_rows = x2[row_tok]
    y_rows = _expert_ffn(blk_exp, blk_on, h_rows, w_exp_gate, w_exp_up, w_exp_down)
    out = _combine_ln(x2, y_rows[dest_of[:, 0]], y_rows[dest_of[:, 1]], gates,
                      ln3_g.reshape(1, -1), ln3_b.reshape(1, -1))
    return out.reshape(bsz, seq, d)


def kernel(x, mem, w_in, cmp_pe_k, cmp_w1_k, cmp_w2_k, cmp_pe_v, cmp_w1_v, cmp_w2_v, sgu_ln_g, sgu_ln_b, sgu_w_s, sgu_b_s, w_branch_a, w_branch_b, w_o, ln1_g, ln1_b, w_xq, w_xk, w_xv, w_xo, ln2_g, ln2_b, w_router_grp, b_router_grp, w_router_exp, b_router_exp, w_exp_gate, w_exp_up, w_exp_down, ln3_g, ln3_b):
    params = (w_in, cmp_pe_k, cmp_w1_k, cmp_w2_k, cmp_pe_v, cmp_w1_v, cmp_w2_v,
              sgu_ln_g, sgu_ln_b, sgu_w_s, sgu_b_s, w_branch_a, w_branch_b, w_o, ln1_g, ln1_b,
              w_xq, w_xk, w_xv, w_xo, ln2_g, ln2_b,
              w_router_grp, b_router_grp, w_router_exp, b_router_exp,
              w_exp_gate, w_exp_up, w_exp_down, ln3_g, ln3_b)
    h = x
    for l in range(DEPTH):
        h = _layer(h, mem, *[p[l] for p in params])
    return h
```

```python
import functools
import math

import jax
import jax.numpy as jnp
from jax import lax
from jax.experimental import pallas as pl
from jax.experimental.pallas import tpu as pltpu

D_MODEL = 2048
HEAD_DIM = 128
NSA_HEADS = 16
NSA_KV_GROUPS = 2
NSA_HPG = NSA_HEADS // NSA_KV_GROUPS
CMP_BLOCK = 32
CMP_STRIDE = 16
CMP_HIDDEN = 256
SEL_BLOCK = 64
SEL_TOPK = 16
WINDOW = 512
Q_BLOCK = 128
SGU_WIDTH = 2048
SGU_GROUPS = 8
SGU_CHUNK = 128
MEM_HEADS = 4
N_GROUPS = 4
EXPERTS_PER_GROUP = 16
N_EXPERTS = N_GROUPS * EXPERTS_PER_GROUP
EXPERT_TOPK = 2
EXPERT_FF = 512
MOE_BLOCK = 128
ROPE_THETA = 10000.0
LN_EPS = 1e-5
SEL_BIG = 1e9
DEPTH = 1
DN_ALPHA = (2.0 * DEPTH) ** 0.25

Q_WIDTH = NSA_HEADS * HEAD_DIM
KV_WIDTH = 3 * 2 * NSA_KV_GROUPS * HEAD_DIM
NSA_GATE_WIDTH = 3 * NSA_HEADS
MEM_WIDTH = MEM_HEADS * HEAD_DIM

LANES = 128
VMEM_LIMIT = 56 * 1024 * 1024
MASKED = -1e30
M_INIT = -1e20
BF16 = jnp.bfloat16
F32 = jnp.float32


def _params(*sem):
    return pltpu.CompilerParams(dimension_semantics=sem, vmem_limit_bytes=VMEM_LIMIT)


def _dot(a, b):
    return jnp.dot(a, b, preferred_element_type=F32)


def _dot_nt(a, b):
    return lax.dot_general(a, b, (((1,), (1,)), ((), ())), preferred_element_type=F32)


def _gelu(x):
    return 0.5 * x * (1.0 + jnp.tanh(math.sqrt(2.0 / math.pi) * (x + 0.044715 * (x * x * x))))


def _sigmoid(x):
    return 1.0 / (1.0 + jnp.exp(-x))


def _layer_norm(x, g, b):
    mu = jnp.mean(x, -1, keepdims=True)
    xc = x - mu
    var = jnp.mean(xc * xc, -1, keepdims=True)
    return xc * lax.rsqrt(var + LN_EPS) * g + b


def _rope(t, cos_full, sin_signed):
    return t * cos_full + pltpu.roll(t, HEAD_DIM // 2, axis=1) * sin_signed


def _rope_tables(pos):
    inv = ROPE_THETA ** (-jnp.arange(0, HEAD_DIM, 2, dtype=F32) / HEAD_DIM)
    ang = pos.astype(F32)[:, None] * inv[None, :]
    c, s = jnp.cos(ang), jnp.sin(ang)
    return jnp.concatenate([c, c], -1), jnp.concatenate([-s, s], -1)


MXU_COLS = 256


def _mm_kernel(a_ref, b_ref, o_ref, *, act):
    a = a_ref[...]
    for j in range(o_ref.shape[1] // MXU_COLS):
        cols = slice(j * MXU_COLS, (j + 1) * MXU_COLS)
        y = _dot(a, b_ref[:, cols])
        if act == "gelu":
            y = _gelu(y)
        o_ref[:, cols] = y.astype(o_ref.dtype)


def _matmul(a, b, *, tm, tn, out_dtype, act=None):
    m, k = a.shape
    n = b.shape[1]
    return pl.pallas_call(
        functools.partial(_mm_kernel, act=act),
        grid=(m // tm, n // tn),
        in_specs=[pl.BlockSpec((tm, k), lambda i, j: (i, 0)),
                  pl.BlockSpec((k, tn), lambda i, j: (0, j))],
        out_specs=pl.BlockSpec((tm, tn), lambda i, j: (i, j)),
        out_shape=jax.ShapeDtypeStruct((m, n), out_dtype),
        compiler_params=_params("parallel", "parallel"),
        name="matmul_" + (act or "plain"),
    )(a, b)


def _qproj_kernel(a_ref, b_ref, cos_ref, sin_ref, o_ref):
    a = a_ref[...]
    cos, sin = cos_ref[...], sin_ref[...]
    scale = HEAD_DIM ** -0.5 * math.log2(math.e)
    for j in range(o_ref.shape[1] // MXU_COLS):
        y = _dot(a, b_ref[:, j * MXU_COLS:(j + 1) * MXU_COLS])
        for h in range(MXU_COLS // HEAD_DIM):
            sl = slice(j * MXU_COLS + h * HEAD_DIM, j * MXU_COLS + (h + 1) * HEAD_DIM)
            o_ref[:, sl] = (_rope(y[:, h * HEAD_DIM:(h + 1) * HEAD_DIM], cos, sin) * scale).astype(o_ref.dtype)


def _q_proj(x_bf, w_q, cos, sin, seq, *, tm=1024, tn=1024):
    m, k = x_bf.shape
    n = w_q.shape[1]
    per_seq = seq // tm
    return pl.pallas_call(
        _qproj_kernel,
        grid=(m // tm, n // tn),
        in_specs=[pl.BlockSpec((tm, k), lambda i, j: (i, 0)),
                  pl.BlockSpec((k, tn), lambda i, j: (0, j)),
                  pl.BlockSpec((tm, HEAD_DIM), lambda i, j: (i % per_seq, 0)),
                  pl.BlockSpec((tm, HEAD_DIM), lambda i, j: (i % per_seq, 0))],
        out_specs=pl.BlockSpec((tm, tn), lambda i, j: (i, j)),
        out_shape=jax.ShapeDtypeStruct((m, n), BF16),
        compiler_params=_params("parallel", "parallel"),
        name="q_proj",
    )(x_bf, w_q, cos, sin)


_N_CMP_COLS = 2 * NSA_KV_GROUPS * HEAD_DIM
_N_REST_CHUNKS = (KV_WIDTH - _N_CMP_COLS) // HEAD_DIM
_N_REST_COLS = (_N_REST_CHUNKS + NSA_KV_GROUPS) * HEAD_DIM
_GATE_COLS = NSA_KV_GROUPS * LANES
_SEL_SHIFT = SEL_BLOCK.bit_length() - 1
assert 1 << _SEL_SHIFT == SEL_BLOCK


def _kvproj_kernel(a_ref, b_ref, cos_ref, sin_ref, cmp_ref, rest_ref, gate_ref, *, per_seq):
    g = NSA_KV_GROUPS
    y = _dot(a_ref[...], b_ref[...])
    cos, sin = cos_ref[...], sin_ref[...]
    cmp_ref[...] = y[:, :_N_CMP_COLS]
    tm = y.shape[0]
    pos = (pl.program_id(0) % per_seq) * tm + lax.broadcasted_iota(jnp.int32, (tm, LANES), 0)
    lane = lax.broadcasted_iota(jnp.int32, (tm, LANES), 1)
    blk_onehot = jnp.where(lane == (pos >> _SEL_SHIFT), 1.0, 0.0).astype(rest_ref.dtype)
    for c in range(_N_REST_CHUNKS):
        t = y[:, _N_CMP_COLS + c * HEAD_DIM:_N_CMP_COLS + (c + 1) * HEAD_DIM]
        if (c % (2 * g)) < g:
            t = _rope(t, cos, sin)
        dst = 2 * c if c < g else g + c
        rest_ref[:, dst * HEAD_DIM:(dst + 1) * HEAD_DIM] = t.astype(rest_ref.dtype)
        if c < g:
            rest_ref[:, (dst + 1) * HEAD_DIM:(dst + 2) * HEAD_DIM] = blk_onehot
    gate_ref[...] = _sigmoid(y[:, KV_WIDTH:])


def _kv_proj(x_bf, w_kvg, cos, sin, seq, *, tm=512):
    m, k = x_bf.shape
    n = w_kvg.shape[1]
    per_seq = seq // tm
    return pl.pallas_call(
        functools.partial(_kvproj_kernel, per_seq=per_seq),
        grid=(m // tm,),
        in_specs=[pl.BlockSpec((tm, k), lambda i: (i, 0)),
                  pl.BlockSpec((k, n), lambda i: (0, 0)),
                  pl.BlockSpec((tm, HEAD_DIM), lambda i: (i % per_seq, 0)),
                  pl.BlockSpec((tm, HEAD_DIM), lambda i: (i % per_seq, 0))],
        out_specs=[pl.BlockSpec((tm, _N_CMP_COLS), lambda i: (i, 0)),
                   pl.BlockSpec((tm, _N_REST_COLS), lambda i: (i, 0)),
                   pl.BlockSpec((tm, _GATE_COLS), lambda i: (i, 0))],
        out_shape=[jax.ShapeDtypeStruct((m, _N_CMP_COLS), F32),
                   jax.ShapeDtypeStruct((m, _N_REST_COLS), BF16),
                   jax.ShapeDtypeStruct((m, _GATE_COLS), F32)],
        compiler_params=_params("parallel"),
        name="kv_gate_proj",
    )(x_bf, w_kvg, cos, sin)


def _compress_kernel(kc_ref, pe_ref, w1_ref, w2_ref, cos_ref, sin_ref, o_ref):
    n_chunks = o_ref.shape[2]
    top = jnp.zeros((n_chunks, CMP_HIDDEN), F32)
    bot = jnp.zeros((n_chunks, CMP_HIDDEN), F32)
    for p in range(CMP_STRIDE):
        xp = kc_ref[0, pl.ds(p, n_chunks, stride=CMP_STRIDE), :]
        q = CMP_STRIDE + p
        top += _dot((xp + pe_ref[0, p:p + 1, :]).astype(BF16), w1_ref[0, p * HEAD_DIM:(p + 1) * HEAD_DIM, :])
        bot += _dot((xp + pe_ref[0, q:q + 1, :]).astype(BF16), w1_ref[0, q * HEAD_DIM:(q + 1) * HEAD_DIM, :])
    h = top + pltpu.roll(bot, n_chunks - 1, axis=0)
    y = _dot(_gelu(h).astype(BF16), w2_ref[0])
    o_ref[0, 0] = _rope(y, cos_ref[0], sin_ref[0]).astype(o_ref.dtype)


def _compress(kv_cmp, pe, w1, w2, cos, sin):
    bsz, seq, width = kv_cmp.shape
    four = width // HEAD_DIM
    n_chunks = seq // CMP_STRIDE
    g = NSA_KV_GROUPS
    return pl.pallas_call(
        _compress_kernel,
        grid=(four, bsz),
        in_specs=[pl.BlockSpec((1, seq, HEAD_DIM), lambda a, b: (b, 0, a)),
                  pl.BlockSpec((1, CMP_BLOCK, HEAD_DIM), lambda a, b: (a // g, 0, 0)),
                  pl.BlockSpec((1, CMP_BLOCK * HEAD_DIM, CMP_HIDDEN), lambda a, b: (a // g, 0, 0)),
                  pl.BlockSpec((1, CMP_HIDDEN, HEAD_DIM), lambda a, b: (a // g, 0, 0)),
                  pl.BlockSpec((1, n_chunks, HEAD_DIM), lambda a, b: (a // g, 0, 0)),
                  pl.BlockSpec((1, n_chunks, HEAD_DIM), lambda a, b: (a // g, 0, 0))],
        out_specs=pl.BlockSpec((1, 1, n_chunks, HEAD_DIM), lambda a, b: (a, b, 0, 0)),
        out_shape=jax.ShapeDtypeStruct((four, bsz, n_chunks, HEAD_DIM), BF16),
        compiler_params=_params("parallel", "parallel"),
        name="compress",
    )(kv_cmp, pe, w1, w2, cos, sin)


SEL_TILE = 512
SUBLANES = 8


def _softmax_parts(s3, bias):
    s3 = s3 + bias[None]
    m = jnp.max(s3, -1, keepdims=True)
    m = jnp.where(m > 0.5 * MASKED, m, 0.0)
    p = jnp.exp2(s3 - m)
    return p, 1.0 / jnp.maximum(jnp.sum(p, -1, keepdims=True), 1e-30)


def _topk_unselected(score_t, n_sel, n_pick):
    tq = score_t.shape[1]
    n_grp = n_sel // SUBLANES
    grp = [score_t[g * SUBLANES:(g + 1) * SUBLANES] for g in range(n_grp)]
    row_in_grp = lax.broadcasted_iota(jnp.int32, (SUBLANES, tq), 0)
    later = [jnp.where(row_in_grp > li, 1.0, 0.0) for li in range(SUBLANES)]
    beaten = [jnp.zeros((SUBLANES, tq), F32) for _ in range(n_grp)]
    for i in range(n_sel):
        gi, li = divmod(i, SUBLANES)
        row = grp[gi][li:li + 1, :]
        for g in range(n_grp):
            if g > gi:
                beaten[g] = beaten[g] + jnp.where(row >= grp[g], 1.0, 0.0)
            elif g < gi:
                beaten[g] = beaten[g] + jnp.where(row > grp[g], 1.0, 0.0)
            else:
                beaten[g] = (beaten[g] + jnp.where(row > grp[g], 1.0, 0.0)
                             + jnp.where(row == grp[g], later[li], 0.0))
    return jnp.concatenate([jnp.where(b < n_pick, 0.0, MASKED) for b in beaten], axis=0)


def _nsa_kernel(q_ref, kc_ref, vc_ref, ks_ref, vs_ref, kw_ref, vw_ref, gate_ref, selmap_t_ref, o_ref,
                *, n_sel, n_pick):
    hpg = NSA_HPG
    tq = Q_BLOCK
    c = pl.program_id(2)
    q_t = q_ref[0]
    q8 = jnp.concatenate([q_t[:, n * HEAD_DIM:(n + 1) * HEAD_DIM] for n in range(hpg)], axis=0)
    t_pos = c * tq + lax.broadcasted_iota(jnp.int32, (tq, 1), 0)
    t_lane = c * tq + lax.broadcasted_iota(jnp.int32, (1, tq), 1)

    kc = kc_ref[0, 0]
    n_cmp_pad = kc.shape[0]
    k_idx = lax.broadcasted_iota(jnp.int32, (1, n_cmp_pad), 1)
    vis_c = (CMP_STRIDE * k_idx + (CMP_BLOCK - 1) <= t_pos) & (k_idx < n_cmp_pad - 1)
    p_c, r_c = _softmax_parts(_dot_nt(q8, kc).reshape(hpg, tq, n_cmp_pad), jnp.where(vis_c, 0.0, MASKED))
    o_c = _dot(p_c.reshape(hpg * tq, n_cmp_pad).astype(BF16), vc_ref[0, 0]).reshape(hpg, tq, HEAD_DIM) * r_c

    p_sum = jnp.sum(p_c * r_c, axis=0)
    p_hi = p_sum.astype(BF16)
    p_lo = (p_sum - p_hi.astype(F32)).astype(BF16)
    selmap_t = selmap_t_ref[...]
    imp_t = _dot_nt(selmap_t, p_hi) + _dot_nt(selmap_t, p_lo)
    blk = lax.broadcasted_iota(jnp.int32, (LANES, tq), 0)
    cur = t_lane >> _SEL_SHIFT
    forced = (blk == 0) | (blk == cur) | (blk == cur - 1)
    score_t = jnp.where(forced, SEL_BIG, jnp.where(blk <= cur, imp_t, -SEL_BIG))
    unsel_t = _topk_unselected(score_t, n_sel, n_pick)
    unsel_t = jnp.concatenate([unsel_t, jnp.zeros((LANES - n_sel, tq), F32)], axis=0)
    unsel = unsel_t.T.astype(BF16)
    q_aug = jnp.concatenate([q8, jnp.concatenate([unsel] * hpg, axis=0)], axis=1)

    def sel_tile(base, carry, bias):
        m_old, l_old, acc = carry
        k = ks_ref[0, pl.ds(base, SEL_TILE), :]
        v = vs_ref[0, pl.ds(base, SEL_TILE), :]
        s = _dot_nt(q_aug, k).reshape(hpg, tq, SEL_TILE)
        if bias is not None:
            s = s + bias[None]
        m_new = jnp.maximum(m_old, jnp.max(s, -1, keepdims=True))
        alpha = jnp.exp2(m_old - m_new)
        p = jnp.exp2(s - m_new)
        l_new = alpha * l_old + jnp.sum(p, -1, keepdims=True)
        pv = _dot(p.reshape(hpg * tq, SEL_TILE).astype(BF16), v).reshape(hpg, tq, HEAD_DIM)
        return m_new, l_new, alpha * acc + pv

    n_full = c // (SEL_TILE // tq)
    init = (jnp.full((hpg, tq, 1), M_INIT, F32), jnp.zeros((hpg, tq, 1), F32),
            jnp.zeros((hpg, tq, HEAD_DIM), F32))
    carry = lax.fori_loop(
        0, n_full, lambda kt, cr: sel_tile(pl.multiple_of(kt * SEL_TILE, SEL_TILE), cr, None), init)
    base_last = pl.multiple_of(n_full * SEL_TILE, SEL_TILE)
    kp_last = base_last + lax.broadcasted_iota(jnp.int32, (1, SEL_TILE), 1)
    _, l_s, acc_s = sel_tile(base_last, carry, jnp.where(kp_last <= t_pos, 0.0, MASKED))
    o_s = acc_s * (1.0 / jnp.maximum(l_s, 1e-30))

    span = WINDOW + tq
    start = pl.multiple_of(jnp.maximum(c * tq - WINDOW, 0), tq)
    kw = kw_ref[0, pl.ds(start, span), :]
    vw = vw_ref[0, pl.ds(start, span), :]
    kp = start + lax.broadcasted_iota(jnp.int32, (1, span), 1)
    vis_w = (kp <= t_pos) & (t_pos - kp < WINDOW)
    p_w, r_w = _softmax_parts(_dot_nt(q8, kw).reshape(hpg, tq, span), jnp.where(vis_w, 0.0, MASKED))
    o_w = _dot(p_w.reshape(hpg * tq, span).astype(BF16), vw).reshape(hpg, tq, HEAD_DIM) * r_w

    gates = gate_ref[0]
    for n in range(hpg):
        out = (gates[:, n:n + 1] * o_c[n]
               + gates[:, hpg + n:hpg + n + 1] * o_s[n]
               + gates[:, 2 * hpg + n:2 * hpg + n + 1] * o_w[n])
        o_ref[0, :, n * HEAD_DIM:(n + 1) * HEAD_DIM] = out.astype(o_ref.dtype)


def _nsa(q, cmp_kv, rest, gates, selmap_t, *, n_sel, n_pick):
    bsz, seq, _ = q.shape
    g = NSA_KV_GROUPS
    n_cmp_pad = cmp_kv.shape[2]
    gw = NSA_HPG * HEAD_DIM
    assert n_sel % SUBLANES == 0 and n_sel <= LANES and seq >= WINDOW + Q_BLOCK
    kv_spec = lambda off: pl.BlockSpec((1, seq, HEAD_DIM), lambda b, gi, c: (b, 0, off + gi))
    return pl.pallas_call(
        functools.partial(_nsa_kernel, n_sel=n_sel, n_pick=n_pick),
        grid=(bsz, g, seq // Q_BLOCK),
        in_specs=[pl.BlockSpec((1, Q_BLOCK, gw), lambda b, gi, c: (b, c, gi)),
                  pl.BlockSpec((1, 1, n_cmp_pad, HEAD_DIM), lambda b, gi, c: (gi, b, 0, 0)),
                  pl.BlockSpec((1, 1, n_cmp_pad, HEAD_DIM), lambda b, gi, c: (g + gi, b, 0, 0)),
                  pl.BlockSpec((1, seq, 2 * HEAD_DIM), lambda b, gi, c: (b, 0, gi)),
                  kv_spec(2 * g), kv_spec(3 * g), kv_spec(4 * g),
                  pl.BlockSpec((1, Q_BLOCK, LANES), lambda b, gi, c: (b, c, gi)),
                  pl.BlockSpec(selmap_t.shape, lambda b, gi, c: (0, 0))],
        out_specs=pl.BlockSpec((1, Q_BLOCK, gw), lambda b, gi, c: (b, c, gi)),
        out_shape=jax.ShapeDtypeStruct(q.shape, BF16),
        compiler_params=_params("parallel", "parallel", "arbitrary"),
        name="nsa_attention",
    )(q, cmp_kv, cmp_kv, rest, rest, rest, rest, gates, selmap_t)


def _sgu_kernel(u_ref, v_ref, g_ref, b_ref, ws_ref, bs_ref, o_ref, *, chunks):
    gd = SGU_WIDTH // SGU_GROUPS
    row = lax.broadcasted_iota(jnp.int32, (SGU_CHUNK, SGU_CHUNK), 0)
    col = lax.broadcasted_iota(jnp.int32, (SGU_CHUNK, SGU_CHUNK), 1)
    causal = col <= row
    bs = bs_ref[...]
    for ch in range(chunks):
        rows = slice(ch * SGU_CHUNK, (ch + 1) * SGU_CHUNK)
        vn = _layer_norm(v_ref[rows, :], g_ref[...], b_ref[...]).astype(BF16)
        for gi in range(SGU_GROUPS):
            cols = slice(gi * gd, (gi + 1) * gd)
            w = jnp.where(causal, ws_ref[gi], 0.0).astype(BF16)
            z = _dot(w, vn[:, cols]) + bs[:, gi:gi + 1]
            o_ref[rows, cols] = (u_ref[rows, cols] * z).astype(o_ref.dtype)


def _sgu(uv, ln_g, ln_b, w_s, b_s_t, *, chunks=2):
    m = uv.shape[0]
    tm = chunks * SGU_CHUNK
    return pl.pallas_call(
        functools.partial(_sgu_kernel, chunks=chunks),
        grid=(m // tm,),
        in_specs=[pl.BlockSpec((tm, SGU_WIDTH), lambda i: (i, 0)),
                  pl.BlockSpec((tm, SGU_WIDTH), lambda i: (i, 1)),
                  pl.BlockSpec((1, SGU_WIDTH), lambda i: (0, 0)),
                  pl.BlockSpec((1, SGU_WIDTH), lambda i: (0, 0)),
                  pl.BlockSpec(w_s.shape, lambda i: (0, 0, 0)),
                  pl.BlockSpec(b_s_t.shape, lambda i: (0, 0))],
        out_specs=pl.BlockSpec((tm, SGU_WIDTH), lambda i: (i, 0)),
        out_shape=jax.ShapeDtypeStruct((m, SGU_WIDTH), BF16),
        compiler_params=_params("parallel"),
        name="sgu",
    )(uv, uv, ln_g, ln_b, w_s, b_s_t)


def _merge_kernel(x_ref, a_ref, s_ref, wga_ref, wgb_ref, wa_ref, wb_ref, o_ref):
    x = x_ref[...]
    ga = _sigmoid(_dot(x, wga_ref[...]))
    gb = _sigmoid(_dot(x, wgb_ref[...]))
    o_ref[...] = (ga * _dot(a_ref[...], wa_ref[...]) + gb * _dot(s_ref[...], wb_ref[...])).astype(o_ref.dtype)


def _merge(x_bf, o_nsa, o_sgu, w_merge, w_a, w_b, *, tm=1024, tn=256):
    m, k = x_bf.shape
    n = w_a.shape[1]
    nb = n // tn
    row = pl.BlockSpec((tm, k), lambda i, j: (i, 0))
    col = pl.BlockSpec((k, tn), lambda i, j: (0, j))
    return pl.pallas_call(
        _merge_kernel,
        grid=(m // tm, nb),
        in_specs=[row, row, row, col, pl.BlockSpec((k, tn), lambda i, j: (0, nb + j)), col, col],
        out_specs=pl.BlockSpec((tm, tn), lambda i, j: (i, j)),
        out_shape=jax.ShapeDtypeStruct((m, n), BF16),
        compiler_params=_params("parallel", "parallel"),
        name="merge",
    )(x_bf, o_nsa, o_sgu, w_merge, w_merge, w_a, w_b)


def _proj_ln_kernel(a_ref, w_ref, x_ref, g_ref, b_ref, o_ref, obf_ref):
    y = DN_ALPHA * x_ref[...] + _dot(a_ref[...], w_ref[...])
    out = _layer_norm(y, g_ref[...], b_ref[...])
    o_ref[...] = out
    obf_ref[...] = out.astype(BF16)


def _proj_ln(a_bf, w, x, ln_g, ln_b, *, tm=512):
    m, k = a_bf.shape
    n = w.shape[1]
    vec = pl.BlockSpec((1, n), lambda i: (0, 0))
    return pl.pallas_call(
        _proj_ln_kernel,
        grid=(m // tm,),
        in_specs=[pl.BlockSpec((tm, k), lambda i: (i, 0)),
                  pl.BlockSpec((k, n), lambda i: (0, 0)),
                  pl.BlockSpec((tm, n), lambda i: (i, 0)), vec, vec],
        out_specs=[pl.BlockSpec((tm, n), lambda i: (i, 0)), pl.BlockSpec((tm, n), lambda i: (i, 0))],
        out_shape=[jax.ShapeDtypeStruct((m, n), F32), jax.ShapeDtypeStruct((m, n), BF16)],
        compiler_params=_params("parallel"),
        name="proj_ln",
    )(a_bf, w, x, ln_g, ln_b)


def _xattn_kernel(xbf_ref, x_ref, wq_ref, kv_ref, wo_ref, g_ref, b_ref, o_ref):
    q = (_dot(xbf_ref[0], wq_ref[...]) * HEAD_DIM ** -0.5).astype(BF16)
    kv = kv_ref[0]
    heads = []
    for h in range(MEM_HEADS):
        cols = slice(h * HEAD_DIM, (h + 1) * HEAD_DIM)
        s = _dot_nt(q[:, cols], kv[:, cols])
        p = jnp.exp(s - jnp.max(s, -1, keepdims=True))
        p = p / jnp.sum(p, -1, keepdims=True)
        heads.append(_dot(p.astype(BF16), kv[:, MEM_WIDTH + h * HEAD_DIM:MEM_WIDTH + (h + 1) * HEAD_DIM]))
    o = jnp.concatenate(heads, axis=1).astype(BF16)
    y = DN_ALPHA * x_ref[0] + _dot(o, wo_ref[...])
    o_ref[0] = _layer_norm(y, g_ref[...], b_ref[...])


def _xattn(x_bf, x, w_xq, mem_kv, w_xo, ln_g, ln_b, *, tm=512):
    bsz, seq, d = x.shape
    mem_len = mem_kv.shape[1]
    vec = pl.BlockSpec((1, d), lambda b, i: (0, 0))
    return pl.pallas_call(
        _xattn_kernel,
        grid=(bsz, seq // tm),
        in_specs=[pl.BlockSpec((1, tm, d), lambda b, i: (b, i, 0)),
                  pl.BlockSpec((1, tm, d), lambda b, i: (b, i, 0)),
                  pl.BlockSpec(w_xq.shape, lambda b, i: (0, 0)),
                  pl.BlockSpec((1, mem_len, 2 * MEM_WIDTH), lambda b, i: (b, 0, 0)),
                  pl.BlockSpec(w_xo.shape, lambda b, i: (0, 0)), vec, vec],
        out_specs=pl.BlockSpec((1, tm, d), lambda b, i: (b, i, 0)),
        out_shape=jax.ShapeDtypeStruct(x.shape, F32),
        compiler_params=_params("parallel", "parallel"),
        name="xattn_ln",
    )(x_bf, x, w_xq, mem_kv, w_xo, ln_g, ln_b)


def _router_kernel(x_ref, whi_ref, wlo_ref, b_ref, gate_ref, exp_ref, rank_ref, cnt_ref, seen_ref):
    @pl.when(pl.program_id(0) == 0)
    def _():
        seen_ref[...] = jnp.zeros_like(seen_ref)

    x = x_ref[...]
    x_hi = x.astype(BF16)
    x_lo = (x - x_hi.astype(F32)).astype(BF16)
    whi = whi_ref[...]
    logits = _dot(x_hi, whi) + _dot(x_hi, wlo_ref[...]) + _dot(x_lo, whi) + b_ref[...]
    rows = logits.shape[0]
    lane = lax.broadcasted_iota(jnp.int32, (rows, LANES), 1)
    lane_f = lane.astype(F32)
    is_grp = lane < N_GROUPS
    lg = jnp.where(is_grp, logits, MASKED)
    eg = jnp.where(is_grp, jnp.exp(lg - jnp.max(lg, -1, keepdims=True)), 0.0)
    p_grp = eg / jnp.sum(eg, -1, keepdims=True)
    g_grp = jnp.max(p_grp, -1, keepdims=True)
    grp = jnp.min(jnp.where((p_grp == g_grp) & is_grp, lane_f, 1e9), -1, keepdims=True).astype(jnp.int32)
    lo = N_GROUPS + grp * EXPERTS_PER_GROUP
    in_grp = (lane >= lo) & (lane < lo + EXPERTS_PER_GROUP)
    le = jnp.where(in_grp, logits, MASKED)
    ee = jnp.where(in_grp, jnp.exp(le - jnp.max(le, -1, keepdims=True)), 0.0)
    p_exp = jnp.where(in_grp, ee / jnp.sum(ee, -1, keepdims=True), -1.0)
    v1 = jnp.max(p_exp, -1, keepdims=True)
    i1 = jnp.min(jnp.where(p_exp == v1, lane_f, 1e9), -1, keepdims=True)
    p_rest = jnp.where(lane_f == i1, -1.0, p_exp)
    v2 = jnp.max(p_rest, -1, keepdims=True)
    i2 = jnp.min(jnp.where(p_rest == v2, lane_f, 1e9), -1, keepdims=True)
    denom = v1 + v2
    gate_ref[...] = jnp.where(lane == 0, g_grp * v1 / denom, jnp.where(lane == 1, g_grp * v2 / denom, 0.0))
    exp_ref[...] = jnp.where(lane == 0, i1, jnp.where(lane == 1, i2, float(N_GROUPS))).astype(jnp.int32) - N_GROUPS

    onehot = jnp.where((lane_f == i1) | (lane_f == i2), 1.0, 0.0)
    tok_r = lax.broadcasted_iota(jnp.int32, (rows, rows), 0)
    tok_c = lax.broadcasted_iota(jnp.int32, (rows, rows), 1)
    earlier = jnp.where(tok_c < tok_r, 1.0, 0.0).astype(BF16)
    before = _dot(earlier, onehot.astype(BF16)) + seen_ref[...]
    r1 = jnp.sum(jnp.where(lane_f == i1, before, 0.0), -1, keepdims=True)
    r2 = jnp.sum(jnp.where(lane_f == i2, before, 0.0), -1, keepdims=True)
    rank_ref[...] = jnp.where(lane == 0, r1, jnp.where(lane == 1, r2, 0.0)).astype(jnp.int32)
    seen_ref[...] += jnp.sum(onehot, axis=0, keepdims=True)
    cnt_ref[...] = seen_ref[...]


def _router(x, w_hi, w_lo, bias, *, tm=512):
    m, d = x.shape
    wspec = pl.BlockSpec((d, LANES), lambda i: (0, 0))
    vec = pl.BlockSpec((1, LANES), lambda i: (0, 0))
    tile = pl.BlockSpec((tm, LANES), lambda i: (i, 0))
    return pl.pallas_call(
        _router_kernel,
        grid=(m // tm,),
        in_specs=[pl.BlockSpec((tm, d), lambda i: (i, 0)), wspec, wspec, vec],
        out_specs=[tile, tile, tile, vec],
        out_shape=[jax.ShapeDtypeStruct((m, LANES), F32), jax.ShapeDtypeStruct((m, LANES), jnp.int32),
                   jax.ShapeDtypeStruct((m, LANES), jnp.int32), jax.ShapeDtypeStruct((1, LANES), F32)],
        scratch_shapes=[pltpu.VMEM((1, LANES), F32)],
        compiler_params=_params("arbitrary"),
        name="router",
    )(x, w_hi, w_lo, bias)


def _ffn_kernel(exp_ref, on_ref, first_ref, slot_ref, next_ref, h_ref, wg_hbm, wu_hbm, wd_hbm, o_ref,
                wg_buf, wu_buf, wd_buf, sem):
    i = pl.program_id(0)
    slot = slot_ref[i]

    def weight_copies(e, s):
        return (pltpu.make_async_copy(wg_hbm.at[e], wg_buf.at[s], sem.at[s, 0]),
                pltpu.make_async_copy(wu_hbm.at[e], wu_buf.at[s], sem.at[s, 1]),
                pltpu.make_async_copy(wd_hbm.at[e], wd_buf.at[s], sem.at[s, 2]))

    @pl.when(i == 0)
    def _():
        for cp in weight_copies(exp_ref[0], 0):
            cp.start()

    @pl.when(first_ref[i] != 0)
    def _():
        nxt = next_ref[i]

        @pl.when(nxt >= 0)
        def _():
            for cp in weight_copies(nxt, 1 - slot):
                cp.start()

        for cp in weight_copies(exp_ref[i], slot):
            cp.wait()

    @pl.when(on_ref[i] != 0)
    def _():
        h = h_ref[...].astype(BF16)
        a = _dot(h, wg_buf[slot].astype(BF16))
        a = a * _sigmoid(a) * _dot(h, wu_buf[slot].astype(BF16))
        o_ref[...] = _dot(a.astype(BF16), wd_buf[slot].astype(BF16))

    @pl.when(on_ref[i] == 0)
    def _():
        o_ref[...] = jnp.zeros_like(o_ref)


def _expert_ffn(blk_exp, blk_on, h_rows, w_gate, w_up, w_down):
    p_rows, d = h_rows.shape
    n_exp, _, ff = w_gate.shape
    n_blocks = p_rows // MOE_BLOCK
    prev_exp = jnp.concatenate([jnp.full((1,), -1, jnp.int32), blk_exp[:-1]])
    first = ((blk_on != 0) & (blk_exp != prev_exp)).astype(jnp.int32)
    seg_id = jnp.cumsum(first) - 1
    seg_ids = jnp.arange(n_exp + 1)
    hit = (first[:, None] != 0) & (seg_id[:, None] == seg_ids[None, :])
    seg_exp = jnp.sum(jnp.where(hit, blk_exp[:, None] + 1, 0), 0) - 1
    blk_next = jnp.sum(jnp.where(seg_ids[None, :] == seg_id[:, None] + 1, seg_exp[None, :] + 1, 0), 1) - 1
    blk_slot = jnp.maximum(seg_id, 0) % 2
    grid_spec = pltpu.PrefetchScalarGridSpec(
        num_scalar_prefetch=5,
        grid=(n_blocks,),
        in_specs=[pl.BlockSpec((MOE_BLOCK, d), lambda i, *_: (i, 0)),
                  pl.BlockSpec(memory_space=pl.ANY),
                  pl.BlockSpec(memory_space=pl.ANY),
                  pl.BlockSpec(memory_space=pl.ANY)],
        out_specs=pl.BlockSpec((MOE_BLOCK, d), lambda i, *_: (i, 0)),
        scratch_shapes=[pltpu.VMEM((2, d, ff), F32), pltpu.VMEM((2, d, ff), F32), pltpu.VMEM((2, ff, d), F32),
                        pltpu.SemaphoreType.DMA((2, 3))],
    )
    return pl.pallas_call(
        _ffn_kernel,
        grid_spec=grid_spec,
        out_shape=jax.ShapeDtypeStruct((p_rows, d), F32),
        compiler_params=_params("arbitrary"),
        name="expert_ffn",
    )(blk_exp, blk_on, first, blk_slot.astype(jnp.int32), blk_next.astype(jnp.int32),
      h_rows, w_gate, w_up, w_down)


def _combine_ln_kernel(x_ref, y0_ref, y1_ref, gate_ref, g_ref, b_ref, o_ref):
    gates = gate_ref[...]
    moe = y0_ref[...] * gates[:, 0:1] + y1_ref[...] * gates[:, 1:2]
    o_ref[...] = _layer_norm(DN_ALPHA * x_ref[...] + moe, g_ref[...], b_ref[...])


def _combine_ln(x, y0, y1, gates, ln_g, ln_b, *, tm=256):
    m, d = x.shape
    row = pl.BlockSpec((tm, d), lambda i: (i, 0))
    vec = pl.BlockSpec((1, d), lambda i: (0, 0))
    return pl.pallas_call(
        _combine_ln_kernel,
        grid=(m // tm,),
        in_specs=[row, row, row, pl.BlockSpec((tm, LANES), lambda i: (i, 0)), vec, vec],
        out_specs=row,
        out_shape=jax.ShapeDtypeStruct((m, d), F32),
        compiler_params=_params("parallel"),
        name="combine_ln",
    )(x, y0, y1, gates, ln_g, ln_b)


def _layer(x, mem, w_in, cmp_pe_k, cmp_w1_k, cmp_w2_k, cmp_pe_v, cmp_w1_v, cmp_w2_v,
           sgu_ln_g, sgu_ln_b, sgu_w_s, sgu_b_s, w_branch_a, w_branch_b, w_o, ln1_g, ln1_b,
           w_xq, w_xk, w_xv, w_xo, ln2_g, ln2_b,
           w_router_grp, b_router_grp, w_router_exp, b_router_exp,
           w_exp_gate, w_exp_up, w_exp_down, ln3_g, ln3_b):
    bsz, seq, d = x.shape
    n_tok = bsz * seq
    g = NSA_KV_GROUPS
    hpg = NSA_HPG
    xf = x.reshape(n_tok, d)
    x_bf = xf.astype(BF16)

    c0 = Q_WIDTH
    c1 = c0 + KV_WIDTH
    c2 = c1 + NSA_GATE_WIDTH
    c3 = c2 + 2 * SGU_WIDTH
    w_q = w_in[:, :c0].astype(BF16)
    w_gate = w_in[:, c1:c2].reshape(d, g, hpg, 3).transpose(0, 1, 3, 2).reshape(d, g, 3 * hpg)
    w_gate = jnp.pad(w_gate, ((0, 0), (0, 0), (0, LANES - 3 * hpg))).reshape(d, g * LANES)
    w_kvg = jnp.concatenate([w_in[:, c0:c1], w_gate], axis=1).astype(BF16)
    w_uv = w_in[:, c2:c3].astype(BF16)
    w_merge = w_in[:, c3:].astype(BF16)

    cos, sin = _rope_tables(jnp.arange(seq, dtype=jnp.int32))
    q = _q_proj(x_bf, w_q, cos, sin, seq).reshape(bsz, seq, Q_WIDTH)
    kv_cmp, kv_rest, nsa_gates = _kv_proj(x_bf, w_kvg, cos, sin, seq)

    n_chunks = seq // CMP_STRIDE
    pe = jnp.stack([cmp_pe_k, cmp_pe_v])
    w1 = jnp.stack([cmp_w1_k, cmp_w1_v]).astype(BF16)
    w2 = jnp.stack([cmp_w2_k, cmp_w2_v]).astype(BF16)
    cmp_end = CMP_STRIDE * jnp.arange(n_chunks, dtype=jnp.int32) + CMP_BLOCK - 1
    ccos, csin = _rope_tables(cmp_end)
    cmp_cos = jnp.stack([ccos, jnp.ones_like(ccos)])
    cmp_sin = jnp.stack([csin, jnp.zeros_like(csin)])
    cmp_kv = _compress(kv_cmp.reshape(bsz, seq, _N_CMP_COLS), pe, w1, w2, cmp_cos, cmp_sin)

    n_sel = seq // SEL_BLOCK
    n_pick = min(SEL_TOPK, n_sel)
    ci = CMP_STRIDE * jnp.arange(n_chunks)
    sj = SEL_BLOCK * jnp.arange(LANES)
    overlap = jnp.clip(jnp.minimum(ci[:, None] + CMP_BLOCK, sj[None, :] + SEL_BLOCK)
                       - jnp.maximum(ci[:, None], sj[None, :]), 0, None)
    selmap_t = (overlap.astype(F32) / CMP_BLOCK).astype(BF16).T
    o_nsa = _nsa(q, cmp_kv, kv_rest.reshape(bsz, seq, _N_REST_COLS),
                 nsa_gates.reshape(bsz, seq, _GATE_COLS), selmap_t, n_sel=n_sel, n_pick=n_pick)

    uv = _matmul(x_bf, w_uv, tm=1024, tn=1024, out_dtype=F32, act="gelu")
    o_sgu = _sgu(uv, sgu_ln_g.reshape(1, -1), sgu_ln_b.reshape(1, -1), sgu_w_s, sgu_b_s.T)

    merged = _merge(x_bf, o_nsa.reshape(n_tok, Q_WIDTH), o_sgu, w_merge,
                    w_branch_a.astype(BF16), w_branch_b.astype(BF16))
    x1, x1_bf = _proj_ln(merged, w_o.astype(BF16), xf, ln1_g.reshape(1, -1), ln1_b.reshape(1, -1))

    mem_len = mem.shape[1]
    w_mem = jnp.concatenate([w_xk, w_xv], axis=1).astype(BF16)
    mem_kv = _matmul(mem.reshape(bsz * mem_len, d).astype(BF16), w_mem, tm=256, tn=512, out_dtype=BF16)
    x2 = _xattn(x1_bf.reshape(bsz, seq, d), x1.reshape(bsz, seq, d), w_xq.astype(BF16),
                mem_kv.reshape(bsz, mem_len, 2 * MEM_WIDTH), w_xo.astype(BF16),
                ln2_g.reshape(1, -1), ln2_b.reshape(1, -1)).reshape(n_tok, d)

    w_r = jnp.concatenate([w_router_grp, w_router_exp], axis=1)
    w_r = jnp.pad(w_r, ((0, 0), (0, LANES - w_r.shape[1])))
    w_r_hi = w_r.astype(BF16)
    w_r_lo = (w_r - w_r_hi.astype(F32)).astype(BF16)
    b_r = jnp.concatenate([b_router_grp, b_router_exp])
    b_r = jnp.pad(b_r, (0, LANES - b_r.shape[0])).reshape(1, LANES)
    gates, experts, ranks, seen = _router(x2, w_r_hi, w_r_lo, b_r)

    nk = n_tok * EXPERT_TOPK
    counts = seen[0, N_GROUPS:N_GROUPS + N_EXPERTS].astype(jnp.int32)
    pcounts = ((counts + MOE_BLOCK - 1) // MOE_BLOCK) * MOE_BLOCK
    pend = jnp.cumsum(pcounts)
    pstart = pend - pcounts
    e_tok = experts[:, :EXPERT_TOPK]
    seg_start = jnp.sum(jnp.where(e_tok[..., None] == jnp.arange(N_EXPERTS), pstart, 0), -1)
    dest_of = seg_start + ranks[:, :EXPERT_TOPK]
    n_blocks = -(-nk // MOE_BLOCK) + N_EXPERTS
    p_rows = n_blocks * MOE_BLOCK
    tok_flat = jnp.repeat(jnp.arange(n_tok, dtype=jnp.int32), EXPERT_TOPK)
    row_tok = (jnp.arange(p_rows, dtype=jnp.int32) % n_tok).at[dest_of.reshape(nk)].set(tok_flat)
    blk_start = jnp.arange(n_blocks) * MOE_BLOCK
    blk_on = (blk_start < pend[-1]).astype(jnp.int32)
    blk_probe = jnp.minimum(blk_start, pend[-1] - 1)
    blk_exp = jnp.minimum(jnp.sum((pend[None, :] <= blk_probe[:, None]).astype(jnp.int32), -1),
                          N_EXPERTS - 1).astype(jnp.int32)

    h_rows = x2[row_tok]
    y_rows = _expert_ffn(blk_exp, blk_on, h_rows, w_exp_gate, w_exp_up, w_exp_down)
    out = _combine_ln(x2, y_rows[dest_of[:, 0]], y_rows[dest_of[:, 1]], gates,
                      ln3_g.reshape(1, -1), ln3_b.reshape(1, -1))
    return out.reshape(bsz, seq, d)


def kernel(x, mem, w_in, cmp_pe_k, cmp_w1_k, cmp_w2_k, cmp_pe_v, cmp_w1_v, cmp_w2_v, sgu_ln_g, sgu_ln_b, sgu_w_s, sgu_b_s, w_branch_a, w_branch_b, w_o, ln1_g, ln1_b, w_xq, w_xk, w_xv, w_xo, ln2_g, ln2_b, w_router_grp, b_router_grp, w_router_exp, b_router_exp, w_exp_gate, w_exp_up, w_exp_down, ln3_g, ln3_b):
    params = (w_in, cmp_pe_k, cmp_w1_k, cmp_w2_k, cmp_pe_v, cmp_w1_v, cmp_w2_v,
              sgu_ln_g, sgu_ln_b, sgu_w_s, sgu_b_s, w_branch_a, w_branch_b, w_o, ln1_g, ln1_b,
              w_xq, w_xk, w_xv, w_xo, ln2_g, ln2_b,
              w_router_grp, b_router_grp, w_router_exp, b_router_exp,
              w_exp_gate, w_exp_up, w_exp_down, ln3_g, ln3_b)
    h = x
    for l in range(DEPTH):
        h = _layer(h, mem, *[p[l] for p in params])
    return h
```

```python
import functools
import math

import jax
import jax.numpy as jnp
from jax import lax
from jax.experimental import pallas as pl
from jax.experimental.pallas import tpu as pltpu

D_MODEL = 2048
HEAD_DIM = 128
NSA_HEADS = 16
NSA_KV_GROUPS = 2
NSA_HPG = NSA_HEADS // NSA_KV_GROUPS
CMP_BLOCK = 32
CMP_STRIDE = 16
CMP_HIDDEN = 256
SEL_BLOCK = 64
SEL_TOPK = 16
WINDOW = 512
Q_BLOCK = 128
SGU_WIDTH = 2048
SGU_GROUPS = 8
SGU_CHUNK = 128
MEM_HEADS = 4
N_GROUPS = 4
EXPERTS_PER_GROUP = 16
N_EXPERTS = N_GROUPS * EXPERTS_PER_GROUP
EXPERT_TOPK = 2
EXPERT_FF = 512
MOE_BLOCK = 128
ROPE_THETA = 10000.0
LN_EPS = 1e-5
SEL_BIG = 1e9
DEPTH = 1
DN_ALPHA = (2.0 * DEPTH) ** 0.25

Q_WIDTH = NSA_HEADS * HEAD_DIM
KV_WIDTH = 3 * 2 * NSA_KV_GROUPS * HEAD_DIM
NSA_GATE_WIDTH = 3 * NSA_HEADS
MEM_WIDTH = MEM_HEADS * HEAD_DIM

LANES = 128
VMEM_LIMIT = 56 * 1024 * 1024
MASKED = -1e30
M_INIT = -1e20
BF16 = jnp.bfloat16
F32 = jnp.float32


def _params(*sem):
    return pltpu.CompilerParams(dimension_semantics=sem, vmem_limit_bytes=VMEM_LIMIT)


def _dot(a, b):
    return jnp.dot(a, b, preferred_element_type=F32)


def _dot_nt(a, b):
    return lax.dot_general(a, b, (((1,), (1,)), ((), ())), preferred_element_type=F32)


def _gelu(x):
    return 0.5 * x * (1.0 + jnp.tanh(math.sqrt(2.0 / math.pi) * (x + 0.044715 * (x * x * x))))


def _sigmoid(x):
    return 1.0 / (1.0 + jnp.exp(-x))


def _layer_norm(x, g, b):
    mu = jnp.mean(x, -1, keepdims=True)
    xc = x - mu
    var = jnp.mean(xc * xc, -1, keepdims=True)
    return xc * lax.rsqrt(var + LN_EPS) * g + b


def _rope(t, cos_full, sin_signed):
    return t * cos_full + pltpu.roll(t, HEAD_DIM // 2, axis=1) * sin_signed


def _rope_tables(pos):
    inv = ROPE_THETA ** (-jnp.arange(0, HEAD_DIM, 2, dtype=F32) / HEAD_DIM)
    ang = pos.astype(F32)[:, None] * inv[None, :]
    c, s = jnp.cos(ang), jnp.sin(ang)
    return jnp.concatenate([c, c], -1), jnp.concatenate([-s, s], -1)


MXU_COLS = 256


def _mm_kernel(a_ref, b_ref, o_ref, *, act):
    a = a_ref[...]
    for j in range(o_ref.shape[1] // MXU_COLS):
        cols = slice(j * MXU_COLS, (j + 1) * MXU_COLS)
        y = _dot(a, b_ref[:, cols])
        if act == "gelu":
            y = _gelu(y)
        o_ref[:, cols] = y.astype(o_ref.dtype)


def _matmul(a, b, *, tm, tn, out_dtype, act=None):
    m, k = a.shape
    n = b.shape[1]
    return pl.pallas_call(
        functools.partial(_mm_kernel, act=act),
        grid=(m // tm, n // tn),
        in_specs=[pl.BlockSpec((tm, k), lambda i, j: (i, 0)),
                  pl.BlockSpec((k, tn), lambda i, j: (0, j))],
        out_specs=pl.BlockSpec((tm, tn), lambda i, j: (i, j)),
        out_shape=jax.ShapeDtypeStruct((m, n), out_dtype),
        compiler_params=_params("parallel", "parallel"),
        name="matmul_" + (act or "plain"),
    )(a, b)


def _qproj_kernel(a_ref, b_ref, cos_ref, sin_ref, o_ref):
    a = a_ref[...]
    cos, sin = cos_ref[...], sin_ref[...]
    scale = HEAD_DIM ** -0.5 * math.log2(math.e)
    for j in range(o_ref.shape[1] // MXU_COLS):
        y = _dot(a, b_ref[:, j * MXU_COLS:(j + 1) * MXU_COLS])
        for h in range(MXU_COLS // HEAD_DIM):
            sl = slice(j * MXU_COLS + h * HEAD_DIM, j * MXU_COLS + (h + 1) * HEAD_DIM)
            o_ref[:, sl] = (_rope(y[:, h * HEAD_DIM:(h + 1) * HEAD_DIM], cos, sin) * scale).astype(o_ref.dtype)


def _q_proj(x_bf, w_q, cos, sin, seq, *, tm=1024, tn=1024):
    m, k = x_bf.shape
    n = w_q.shape[1]
    per_seq = seq // tm
    return pl.pallas_call(
        _qproj_kernel,
        grid=(m // tm, n // tn),
        in_specs=[pl.BlockSpec((tm, k), lambda i, j: (i, 0)),
                  pl.BlockSpec((k, tn), lambda i, j: (0, j)),
                  pl.BlockSpec((tm, HEAD_DIM), lambda i, j: (i % per_seq, 0)),
                  pl.BlockSpec((tm, HEAD_DIM), lambda i, j: (i % per_seq, 0))],
        out_specs=pl.BlockSpec((tm, tn), lambda i, j: (i, j)),
        out_shape=jax.ShapeDtypeStruct((m, n), BF16),
        compiler_params=_params("parallel", "parallel"),
        name="q_proj",
    )(x_bf, w_q, cos, sin)


_N_CMP_COLS = 2 * NSA_KV_GROUPS * HEAD_DIM
_N_REST_CHUNKS = (KV_WIDTH - _N_CMP_COLS) // HEAD_DIM
_N_REST_COLS = (_N_REST_CHUNKS + NSA_KV_GROUPS) * HEAD_DIM
_GATE_COLS = NSA_KV_GROUPS * LANES
_SEL_SHIFT = SEL_BLOCK.bit_length() - 1
assert 1 << _SEL_SHIFT == SEL_BLOCK


def _kvproj_kernel(a_ref, b_ref, cos_ref, sin_ref, cmp_ref, rest_ref, gate_ref, *, per_seq):
    g = NSA_KV_GROUPS
    y = _dot(a_ref[...], b_ref[...])
    cos, sin = cos_ref[...], sin_ref[...]
    cmp_ref[...] = y[:, :_N_CMP_COLS]
    tm = y.shape[0]
    pos = (pl.program_id(0) % per_seq) * tm + lax.broadcasted_iota(jnp.int32, (tm, LANES), 0)
    lane = lax.broadcasted_iota(jnp.int32, (tm, LANES), 1)
    blk_onehot = jnp.where(lane == (pos >> _SEL_SHIFT), 1.0, 0.0).astype(rest_ref.dtype)
    for c in range(_N_REST_CHUNKS):
        t = y[:, _N_CMP_COLS + c * HEAD_DIM:_N_CMP_COLS + (c + 1) * HEAD_DIM]
        if (c % (2 * g)) < g:
            t = _rope(t, cos, sin)
        dst = 2 * c if c < g else g + c
        rest_ref[:, dst * HEAD_DIM:(dst + 1) * HEAD_DIM] = t.astype(rest_ref.dtype)
        if c < g:
            rest_ref[:, (dst + 1) * HEAD_DIM:(dst + 2) * HEAD_DIM] = blk_onehot
    gate_ref[...] = _sigmoid(y[:, KV_WIDTH:])


def _kv_proj(x_bf, w_kvg, cos, sin, seq, *, tm=512):
    m, k = x_bf.shape
    n = w_kvg.shape[1]
    per_seq = seq // tm
    return pl.pallas_call(
        functools.partial(_kvproj_kernel, per_seq=per_seq),
        grid=(m // tm,),
        in_specs=[pl.BlockSpec((tm, k), lambda i: (i, 0)),
                  pl.BlockSpec((k, n), lambda i: (0, 0)),
                  pl.BlockSpec((tm, HEAD_DIM), lambda i: (i % per_seq, 0)),
                  pl.BlockSpec((tm, HEAD_DIM), lambda i: (i % per_seq, 0))],
        out_specs=[pl.BlockSpec((tm, _N_CMP_COLS), lambda i: (i, 0)),
                   pl.BlockSpec((tm, _N_REST_COLS), lambda i: (i, 0)),
                   pl.BlockSpec((tm, _GATE_COLS), lambda i: (i, 0))],
        out_shape=[jax.ShapeDtypeStruct((m, _N_CMP_COLS), F32),
                   jax.ShapeDtypeStruct((m, _N_REST_COLS), BF16),
                   jax.ShapeDtypeStruct((m, _GATE_COLS), F32)],
        compiler_params=_params("parallel"),
        name="kv_gate_proj",
    )(x_bf, w_kvg, cos, sin)


def _compress_kernel(kc_ref, pe_ref, w1_ref, w2_ref, cos_ref, sin_ref, o_ref):
    n_chunks = o_ref.shape[2]
    top = jnp.zeros((n_chunks, CMP_HIDDEN), F32)
    bot = jnp.zeros((n_chunks, CMP_HIDDEN), F32)
    for p in range(CMP_STRIDE):
        xp = kc_ref[0, pl.ds(p, n_chunks, stride=CMP_STRIDE), :]
        q = CMP_STRIDE + p
        top += _dot((xp + pe_ref[0, p:p + 1, :]).astype(BF16), w1_ref[0, p * HEAD_DIM:(p + 1) * HEAD_DIM, :])
        bot += _dot((xp + pe_ref[0, q:q + 1, :]).astype(BF16), w1_ref[0, q * HEAD_DIM:(q + 1) * HEAD_DIM, :])
    h = top + pltpu.roll(bot, n_chunks - 1, axis=0)
    y = _dot(_gelu(h).astype(BF16), w2_ref[0])
    o_ref[0, 0] = _rope(y, cos_ref[0], sin_ref[0]).astype(o_ref.dtype)


def _compress(kv_cmp, pe, w1, w2, cos, sin):
    bsz, seq, width = kv_cmp.shape
    four = width // HEAD_DIM
    n_chunks = seq // CMP_STRIDE
    g = NSA_KV_GROUPS
    return pl.pallas_call(
        _compress_kernel,
        grid=(four, bsz),
        in_specs=[pl.BlockSpec((1, seq, HEAD_DIM), lambda a, b: (b, 0, a)),
                  pl.BlockSpec((1, CMP_BLOCK, HEAD_DIM), lambda a, b: (a // g, 0, 0)),
                  pl.BlockSpec((1, CMP_BLOCK * HEAD_DIM, CMP_HIDDEN), lambda a, b: (a // g, 0, 0)),
                  pl.BlockSpec((1, CMP_HIDDEN, HEAD_DIM), lambda a, b: (a // g, 0, 0)),
                  pl.BlockSpec((1, n_chunks, HEAD_DIM), lambda a, b: (a // g, 0, 0)),
                  pl.BlockSpec((1, n_chunks, HEAD_DIM), lambda a, b: (a // g, 0, 0))],
        out_specs=pl.BlockSpec((1, 1, n_chunks, HEAD_DIM), lambda a, b: (a, b, 0, 0)),
        out_shape=jax.ShapeDtypeStruct((four, bsz, n_chunks, HEAD_DIM), BF16),
        compiler_params=_params("parallel", "parallel"),
        name="compress",
    )(kv_cmp, pe, w1, w2, cos, sin)


SEL_TILE = 512
SUBLANES = 8


def _softmax_parts(s3, bias):
    s3 = s3 + bias[None]
    m = jnp.max(s3, -1, keepdims=True)
    m = jnp.where(m > 0.5 * MASKED, m, 0.0)
    p = jnp.exp2(s3 - m)
    return p, 1.0 / jnp.maximum(jnp.sum(p, -1, keepdims=True), 1e-30)


def _topk_unselected(score_t, n_sel, n_pick):
    tq = score_t.shape[1]
    n_grp = n_sel // SUBLANES
    grp = [score_t[g * SUBLANES:(g + 1) * SUBLANES] for g in range(n_grp)]
    row_in_grp = lax.broadcasted_iota(jnp.int32, (SUBLANES, tq), 0)
    later = [jnp.where(row_in_grp > li, 1.0, 0.0) for li in range(SUBLANES)]
    beaten = [jnp.zeros((SUBLANES, tq), F32) for _ in range(n_grp)]
    for i in range(n_sel):
        gi, li = divmod(i, SUBLANES)
        row = grp[gi][li:li + 1, :]
        for g in range(n_grp):
            if g > gi:
                beaten[g] = beaten[g] + jnp.where(row >= grp[g], 1.0, 0.0)
            elif g < gi:
                beaten[g] = beaten[g] + jnp.where(row > grp[g], 1.0, 0.0)
            else:
                beaten[g] = (beaten[g] + jnp.where(row > grp[g], 1.0, 0.0)
                             + jnp.where(row == grp[g], later[li], 0.0))
    return jnp.concatenate([jnp.where(b < n_pick, 0.0, MASKED) for b in beaten], axis=0)


def _nsa_kernel(q_ref, kc_ref, vc_ref, ks_ref, vs_ref, kw_ref, vw_ref, gate_ref, selmap_t_ref, o_ref,
                *, n_sel, n_pick):
    hpg = NSA_HPG
    tq = Q_BLOCK
    c = pl.program_id(2)
    q_t = q_ref[0]
    q8 = jnp.concatenate([q_t[:, n * HEAD_DIM:(n + 1) * HEAD_DIM] for n in range(hpg)], axis=0)
    t_pos = c * tq + lax.broadcasted_iota(jnp.int32, (tq, 1), 0)
    t_lane = c * tq + lax.broadcasted_iota(jnp.int32, (1, tq), 1)

    kc = kc_ref[0, 0]
    n_cmp_pad = kc.shape[0]
    k_idx = lax.broadcasted_iota(jnp.int32, (1, n_cmp_pad), 1)
    vis_c = (CMP_STRIDE * k_idx + (CMP_BLOCK - 1) <= t_pos) & (k_idx < n_cmp_pad - 1)
    p_c, r_c = _softmax_parts(_dot_nt(q8, kc).reshape(hpg, tq, n_cmp_pad), jnp.where(vis_c, 0.0, MASKED))
    o_c = _dot(p_c.reshape(hpg * tq, n_cmp_pad).astype(BF16), vc_ref[0, 0]).reshape(hpg, tq, HEAD_DIM) * r_c

    p_sum = jnp.sum(p_c * r_c, axis=0)
    p_hi = p_sum.astype(BF16)
    p_lo = (p_sum - p_hi.astype(F32)).astype(BF16)
    selmap_t = selmap_t_ref[...]
    imp_t = _dot_nt(selmap_t, p_hi) + _dot_nt(selmap_t, p_lo)
    blk = lax.broadcasted_iota(jnp.int32, (LANES, tq), 0)
    cur = t_lane >> _SEL_SHIFT
    forced = (blk == 0) | (blk == cur) | (blk == cur - 1)
    score_t = jnp.where(forced, SEL_BIG, jnp.where(blk <= cur, imp_t, -SEL_BIG))
    unsel_t = _topk_unselected(score_t, n_sel, n_pick)
    unsel_t = jnp.concatenate([unsel_t, jnp.zeros((LANES - n_sel, tq), F32)], axis=0)
    unsel = unsel_t.T.astype(BF16)
    q_aug = jnp.concatenate([q8, jnp.concatenate([unsel] * hpg, axis=0)], axis=1)

    def sel_tile(base, carry, bias):
        m_old, l_old, acc = carry
        k = ks_ref[0, pl.ds(base, SEL_TILE), :]
        v = vs_ref[0, pl.ds(base, SEL_TILE), :]
        s = _dot_nt(q_aug, k).reshape(hpg, tq, SEL_TILE)
        if bias is not None:
            s = s + bias[None]
        m_new = jnp.maximum(m_old, jnp.max(s, -1, keepdims=True))
        alpha = jnp.exp2(m_old - m_new)
        p = jnp.exp2(s - m_new)
        l_new = alpha * l_old + jnp.sum(p, -1, keepdims=True)
        pv = _dot(p.reshape(hpg * tq, SEL_TILE).astype(BF16), v).reshape(hpg, tq, HEAD_DIM)
        return m_new, l_new, alpha * acc + pv

    n_full = c // (SEL_TILE // tq)
    init = (jnp.full((hpg, tq, 1), M_INIT, F32), jnp.zeros((hpg, tq, 1), F32),
            jnp.zeros((hpg, tq, HEAD_DIM), F32))
    carry = lax.fori_loop(
        0, n_full, lambda kt, cr: sel_tile(pl.multiple_of(kt * SEL_TILE, SEL_TILE), cr, None), init)
    base_last = pl.multiple_of(n_full * SEL_TILE, SEL_TILE)
    kp_last = base_last + lax.broadcasted_iota(jnp.int32, (1, SEL_TILE), 1)
    _, l_s, acc_s = sel_tile(base_last, carry, jnp.where(kp_last <= t_pos, 0.0, MASKED))
    o_s = acc_s * (1.0 / jnp.maximum(l_s, 1e-30))

    span = WINDOW + tq
    start = pl.multiple_of(jnp.maximum(c * tq - WINDOW, 0), tq)
    kw = kw_ref[0, pl.ds(start, span), :]
    vw = vw_ref[0, pl.ds(start, span), :]
    kp = start + lax.broadcasted_iota(jnp.int32, (1, span), 1)
    vis_w = (kp <= t_pos) & (t_pos - kp < WINDOW)
    p_w, r_w = _softmax_parts(_dot_nt(q8, kw).reshape(hpg, tq, span), jnp.where(vis_w, 0.0, MASKED))
    o_w = _dot(p_w.reshape(hpg * tq, span).astype(BF16), vw).reshape(hpg, tq, HEAD_DIM) * r_w

    gates = gate_ref[0]
    for n in range(hpg):
        out = (gates[:, n:n + 1] * o_c[n]
               + gates[:, hpg + n:hpg + n + 1] * o_s[n]
               + gates[:, 2 * hpg + n:2 * hpg + n + 1] * o_w[n])
        o_ref[0, :, n * HEAD_DIM:(n + 1) * HEAD_DIM] = out.astype(o_ref.dtype)


def _nsa(q, cmp_kv, rest, gates, selmap_t, *, n_sel, n_pick):
    bsz, seq, _ = q.shape
    g = NSA_KV_GROUPS
    n_cmp_pad = cmp_kv.shape[2]
    gw = NSA_HPG * HEAD_DIM
    assert n_sel % SUBLANES == 0 and n_sel <= LANES and seq >= WINDOW + Q_BLOCK
    kv_spec = lambda off: pl.BlockSpec((1, seq, HEAD_DIM), lambda b, gi, c: (b, 0, off + gi))
    return pl.pallas_call(
        functools.partial(_nsa_kernel, n_sel=n_sel, n_pick=n_pick),
        grid=(bsz, g, seq // Q_BLOCK),
        in_specs=[pl.BlockSpec((1, Q_BLOCK, gw), lambda b, gi, c: (b, c, gi)),
                  pl.BlockSpec((1, 1, n_cmp_pad, HEAD_DIM), lambda b, gi, c: (gi, b, 0, 0)),
                  pl.BlockSpec((1, 1, n_cmp_pad, HEAD_DIM), lambda b, gi, c: (g + gi, b, 0, 0)),
                  pl.BlockSpec((1, seq, 2 * HEAD_DIM), lambda b, gi, c: (b, 0, gi)),
                  kv_spec(2 * g), kv_spec(3 * g), kv_spec(4 * g),
                  pl.BlockSpec((1, Q_BLOCK, LANES), lambda b, gi, c: (b, c, gi)),
                  pl.BlockSpec(selmap_t.shape, lambda b, gi, c: (0, 0))],
        out_specs=pl.BlockSpec((1, Q_BLOCK, gw), lambda b, gi, c: (b, c, gi)),
        out_shape=jax.ShapeDtypeStruct(q.shape, BF16),
        compiler_params=_params("parallel", "parallel", "arbitrary"),
        name="nsa_attention",
    )(q, cmp_kv, cmp_kv, rest, rest, rest, rest, gates, selmap_t)


def _sgu_kernel(u_ref, v_ref, g_ref, b_ref, ws_ref, bs_ref, o_ref, *, chunks):
    gd = SGU_WIDTH // SGU_GROUPS
    row = lax.broadcasted_iota(jnp.int32, (SGU_CHUNK, SGU_CHUNK), 0)
    col = lax.broadcasted_iota(jnp.int32, (SGU_CHUNK, SGU_CHUNK), 1)
    causal = col <= row
    bs = bs_ref[...]
    for ch in range(chunks):
        rows = slice(ch * SGU_CHUNK, (ch + 1) * SGU_CHUNK)
        vn = _layer_norm(v_ref[rows, :], g_ref[...], b_ref[...]).astype(BF16)
        for gi in range(SGU_GROUPS):
            cols = slice(gi * gd, (gi + 1) * gd)
            w = jnp.where(causal, ws_ref[gi], 0.0).astype(BF16)
            z = _dot(w, vn[:, cols]) + bs[:, gi:gi + 1]
            o_ref[rows, cols] = (u_ref[rows, cols] * z).astype(o_ref.dtype)


def _sgu(uv, ln_g, ln_b, w_s, b_s_t, *, chunks=2):
    m = uv.shape[0]
    tm = chunks * SGU_CHUNK
    return pl.pallas_call(
        functools.partial(_sgu_kernel, chunks=chunks),
        grid=(m // tm,),
        in_specs=[pl.BlockSpec((tm, SGU_WIDTH), lambda i: (i, 0)),
                  pl.BlockSpec((tm, SGU_WIDTH), lambda i: (i, 1)),
                  pl.BlockSpec((1, SGU_WIDTH), lambda i: (0, 0)),
                  pl.BlockSpec((1, SGU_WIDTH), lambda i: (0, 0)),
                  pl.BlockSpec(w_s.shape, lambda i: (0, 0, 0)),
                  pl.BlockSpec(b_s_t.shape, lambda i: (0, 0))],
        out_specs=pl.BlockSpec((tm, SGU_WIDTH), lambda i: (i, 0)),
        out_shape=jax.ShapeDtypeStruct((m, SGU_WIDTH), BF16),
        compiler_params=_params("parallel"),
        name="sgu",
    )(uv, uv, ln_g, ln_b, w_s, b_s_t)


def _merge_kernel(x_ref, a_ref, s_ref, wga_ref, wgb_ref, wa_ref, wb_ref, o_ref):
    x = x_ref[...]
    ga = _sigmoid(_dot(x, wga_ref[...]))
    gb = _sigmoid(_dot(x, wgb_ref[...]))
    o_ref[...] = (ga * _dot(a_ref[...], wa_ref[...]) + gb * _dot(s_ref[...], wb_ref[...])).astype(o_ref.dtype)


def _merge(x_bf, o_nsa, o_sgu, w_merge, w_a, w_b, *, tm=1024, tn=256):
    m, k = x_bf.shape
    n = w_a.shape[1]
    nb = n // tn
    row = pl.BlockSpec((tm, k), lambda i, j: (i, 0))
    col = pl.BlockSpec((k, tn), lambda i, j: (0, j))
    return pl.pallas_call(
        _merge_kernel,
        grid=(m // tm, nb),
        in_specs=[row, row, row, col, pl.BlockSpec((k, tn), lambda i, j: (0, nb + j)), col, col],
        out_specs=pl.BlockSpec((tm, tn), lambda i, j: (i, j)),
        out_shape=jax.ShapeDtypeStruct((m, n), BF16),
        compiler_params=_params("parallel", "parallel"),
        name="merge",
    )(x_bf, o_nsa, o_sgu, w_merge, w_merge, w_a, w_b)


def _proj_ln_kernel(a_ref, w_ref, x_ref, g_ref, b_ref, o_ref, obf_ref):
    y = DN_ALPHA * x_ref[...] + _dot(a_ref[...], w_ref[...])
    out = _layer_norm(y, g_ref[...], b_ref[...])
    o_ref[...] = out
    obf_ref[...] = out.astype(BF16)


def _proj_ln(a_bf, w, x, ln_g, ln_b, *, tm=512):
    m, k = a_bf.shape
    n = w.shape[1]
    vec = pl.BlockSpec((1, n), lambda i: (0, 0))
    return pl.pallas_call(
        _proj_ln_kernel,
        grid=(m // tm,),
        in_specs=[pl.BlockSpec((tm, k), lambda i: (i, 0)),
                  pl.BlockSpec((k, n), lambda i: (0, 0)),
                  pl.BlockSpec((tm, n), lambda i: (i, 0)), vec, vec],
        out_specs=[pl.BlockSpec((tm, n), lambda i: (i, 0)), pl.BlockSpec((tm, n), lambda i: (i, 0))],
        out_shape=[jax.ShapeDtypeStruct((m, n), F32), jax.ShapeDtypeStruct((m, n), BF16)],
        compiler_params=_params("parallel"),
        name="proj_ln",
    )(a_bf, w, x, ln_g, ln_b)


def _xattn_kernel(xbf_ref, x_ref, wq_ref, kv_ref, wo_ref, g_ref, b_ref, o_ref):
    q = (_dot(xbf_ref[0], wq_ref[...]) * HEAD_DIM ** -0.5).astype(BF16)
    kv = kv_ref[0]
    heads = []
    for h in range(MEM_HEADS):
        cols = slice(h * HEAD_DIM, (h + 1) * HEAD_DIM)
        s = _dot_nt(q[:, cols], kv[:, cols])
        p = jnp.exp(s - jnp.max(s, -1, keepdims=True))
        p = p / jnp.sum(p, -1, keepdims=True)
        heads.append(_dot(p.astype(BF16), kv[:, MEM_WIDTH + h * HEAD_DIM:MEM_WIDTH + (h + 1) * HEAD_DIM]))
    o = jnp.concatenate(heads, axis=1).astype(BF16)
    y = DN_ALPHA * x_ref[0] + _dot(o, wo_ref[...])
    o_ref[0] = _layer_norm(y, g_ref[...], b_ref[...])


def _xattn(x_bf, x, w_xq, mem_kv, w_xo, ln_g, ln_b, *, tm=512):
    bsz, seq, d = x.shape
    mem_len = mem_kv.shape[1]
    vec = pl.BlockSpec((1, d), lambda b, i: (0, 0))
    return pl.pallas_call(
        _xattn_kernel,
        grid=(bsz, seq // tm),
        in_specs=[pl.BlockSpec((1, tm, d), lambda b, i: (b, i, 0)),
                  pl.BlockSpec((1, tm, d), lambda b, i: (b, i, 0)),
                  pl.BlockSpec(w_xq.shape, lambda b, i: (0, 0)),
                  pl.BlockSpec((1, mem_len, 2 * MEM_WIDTH), lambda b, i: (b, 0, 0)),
                  pl.BlockSpec(w_xo.shape, lambda b, i: (0, 0)), vec, vec],
        out_specs=pl.BlockSpec((1, tm, d), lambda b, i: (b, i, 0)),
        out_shape=jax.ShapeDtypeStruct(x.shape, F32),
        compiler_params=_params("parallel", "parallel"),
        name="xattn_ln",
    )(x_bf, x, w_xq, mem_kv, w_xo, ln_g, ln_b)


def _router_kernel(x_ref, whi_ref, wlo_ref, b_ref, gate_ref, exp_ref, rank_ref, cnt_ref, seen_ref):
    @pl.when(pl.program_id(0) == 0)
    def _():
        seen_ref[...] = jnp.zeros_like(seen_ref)

    x = x_ref[...]
    x_hi = x.astype(BF16)
    x_lo = (x - x_hi.astype(F32)).astype(BF16)
    whi = whi_ref[...]
    logits = _dot(x_hi, whi) + _dot(x_hi, wlo_ref[...]) + _dot(x_lo, whi) + b_ref[...]
    rows = logits.shape[0]
    lane = lax.broadcasted_iota(jnp.int32, (rows, LANES), 1)
    lane_f = lane.astype(F32)
    is_grp = lane < N_GROUPS
    lg = jnp.where(is_grp, logits, MASKED)
    eg = jnp.where(is_grp, jnp.exp(lg - jnp.max(lg, -1, keepdims=True)), 0.0)
    p_grp = eg / jnp.sum(eg, -1, keepdims=True)
    g_grp = jnp.max(p_grp, -1, keepdims=True)
    grp = jnp.min(jnp.where((p_grp == g_grp) & is_grp, lane_f, 1e9), -1, keepdims=True).astype(jnp.int32)
    lo = N_GROUPS + grp * EXPERTS_PER_GROUP
    in_grp = (lane >= lo) & (lane < lo + EXPERTS_PER_GROUP)
    le = jnp.where(in_grp, logits, MASKED)
    ee = jnp.where(in_grp, jnp.exp(le - jnp.max(le, -1, keepdims=True)), 0.0)
    p_exp = jnp.where(in_grp, ee / jnp.sum(ee, -1, keepdims=True), -1.0)
    v1 = jnp.max(p_exp, -1, keepdims=True)
    i1 = jnp.min(jnp.where(p_exp == v1, lane_f, 1e9), -1, keepdims=True)
    p_rest = jnp.where(lane_f == i1, -1.0, p_exp)
    v2 = jnp.max(p_rest, -1, keepdims=True)
    i2 = jnp.min(jnp.where(p_rest == v2, lane_f, 1e9), -1, keepdims=True)
    denom = v1 + v2
    gate_ref[...] = jnp.where(lane == 0, g_grp * v1 / denom, jnp.where(lane == 1, g_grp * v2 / denom, 0.0))
    exp_ref[...] = jnp.where(lane == 0, i1, jnp.where(lane == 1, i2, float(N_GROUPS))).astype(jnp.int32) - N_GROUPS

    onehot = jnp.where((lane_f == i1) | (lane_f == i2), 1.0, 0.0)
    tok_r = lax.broadcasted_iota(jnp.int32, (rows, rows), 0)
    tok_c = lax.broadcasted_iota(jnp.int32, (rows, rows), 1)
    earlier = jnp.where(tok_c < tok_r, 1.0, 0.0).astype(BF16)
    before = _dot(earlier, onehot.astype(BF16)) + seen_ref[...]
    r1 = jnp.sum(jnp.where(lane_f == i1, before, 0.0), -1, keepdims=True)
    r2 = jnp.sum(jnp.where(lane_f == i2, before, 0.0), -1, keepdims=True)
    rank_ref[...] = jnp.where(lane == 0, r1, jnp.where(lane == 1, r2, 0.0)).astype(jnp.int32)
    seen_ref[...] += jnp.sum(onehot, axis=0, keepdims=True)
    cnt_ref[...] = seen_ref[...]


def _router(x, w_hi, w_lo, bias, *, tm=512):
    m, d = x.shape
    wspec = pl.BlockSpec((d, LANES), lambda i: (0, 0))
    vec = pl.BlockSpec((1, LANES), lambda i: (0, 0))
    tile = pl.BlockSpec((tm, LANES), lambda i: (i, 0))
    return pl.pallas_call(
        _router_kernel,
        grid=(m // tm,),
        in_specs=[pl.BlockSpec((tm, d), lambda i: (i, 0)), wspec, wspec, vec],
        out_specs=[tile, tile, tile, vec],
        out_shape=[jax.ShapeDtypeStruct((m, LANES), F32), jax.ShapeDtypeStruct((m, LANES), jnp.int32),
                   jax.ShapeDtypeStruct((m, LANES), jnp.int32), jax.ShapeDtypeStruct((1, LANES), F32)],
        scratch_shapes=[pltpu.VMEM((1, LANES), F32)],
        compiler_params=_params("arbitrary"),
        name="router",
    )(x, w_hi, w_lo, bias)


GATHER_UNROLL = 8


def _ffn_kernel(exp_ref, on_ref, first_ref, slot_ref, next_ref, tok_ref, x_hbm, wg_hbm, wu_hbm, wd_hbm, o_ref,
                wg_buf, wu_buf, wd_buf, h_buf, sem, row_sem):
    i = pl.program_id(0)
    n_blk = pl.num_programs(0)
    slot = slot_ref[i]
    rows = h_buf.shape[1]

    def weight_copies(e, s):
        return (pltpu.make_async_copy(wg_hbm.at[e], wg_buf.at[s], sem.at[s, 0]),
                pltpu.make_async_copy(wu_hbm.at[e], wu_buf.at[s], sem.at[s, 1]),
                pltpu.make_async_copy(wd_hbm.at[e], wd_buf.at[s], sem.at[s, 2]))

    def row_copy(blk, r):
        tok = tok_ref[blk * rows + r]
        return pltpu.make_async_copy(x_hbm.at[pl.ds(tok, 1), :], h_buf.at[blk % 2, pl.ds(r, 1), :],
                                     row_sem.at[blk % 2])

    def for_rows(fn):
        def body(r, carry):
            fn(r)
            return carry
        lax.fori_loop(0, rows, body, 0, unroll=GATHER_UNROLL)

    @pl.when(i == 0)
    def _():
        for cp in weight_copies(exp_ref[0], 0):
            cp.start()
        for_rows(lambda r: row_copy(0, r).start())

    nxt_blk = jnp.minimum(i + 1, n_blk - 1)

    @pl.when((i + 1 < n_blk) & (on_ref[nxt_blk] != 0))
    def _():
        for_rows(lambda r: row_copy(i + 1, r).start())

    @pl.when(first_ref[i] != 0)
    def _():
        nxt = next_ref[i]

        @pl.when(nxt >= 0)
        def _():
            for cp in weight_copies(nxt, 1 - slot):
                cp.start()

        for cp in weight_copies(exp_ref[i], slot):
            cp.wait()

    @pl.when(on_ref[i] != 0)
    def _():
        for_rows(lambda r: row_copy(i, r).wait())
        h = h_buf[i % 2].astype(BF16)
        a = _dot(h, wg_buf[slot].astype(BF16))
        a = a * _sigmoid(a) * _dot(h, wu_buf[slot].astype(BF16))
        o_ref[...] = _dot(a.astype(BF16), wd_buf[slot].astype(BF16))

    @pl.when(on_ref[i] == 0)
    def _():
        o_ref[...] = jnp.zeros_like(o_ref)


def _expert_ffn(blk_exp, blk_on, row_tok, x, w_gate, w_up, w_down):
    p_rows = row_tok.shape[0]
    d = x.shape[1]
    n_exp, _, ff = w_gate.shape
    n_blocks = p_rows // MOE_BLOCK
    prev_exp = jnp.concatenate([jnp.full((1,), -1, jnp.int32), blk_exp[:-1]])
    first = ((blk_on != 0) & (blk_exp != prev_exp)).astype(jnp.int32)
    seg_id = jnp.cumsum(first) - 1
    seg_ids = jnp.arange(n_exp + 1)
    hit = (first[:, None] != 0) & (seg_id[:, None] == seg_ids[None, :])
    seg_exp = jnp.sum(jnp.where(hit, blk_exp[:, None] + 1, 0), 0) - 1
    blk_next = jnp.sum(jnp.where(seg_ids[None, :] == seg_id[:, None] + 1, seg_exp[None, :] + 1, 0), 1) - 1
    blk_slot = jnp.maximum(seg_id, 0) % 2
    grid_spec = pltpu.PrefetchScalarGridSpec(
        num_scalar_prefetch=6,
        grid=(n_blocks,),
        in_specs=[pl.BlockSpec(memory_space=pl.ANY)] * 4,
        out_specs=pl.BlockSpec((MOE_BLOCK, d), lambda i, *_: (i, 0)),
        scratch_shapes=[pltpu.VMEM((2, d, ff), F32), pltpu.VMEM((2, d, ff), F32), pltpu.VMEM((2, ff, d), F32),
                        pltpu.VMEM((2, MOE_BLOCK, d), F32),
                        pltpu.SemaphoreType.DMA((2, 3)), pltpu.SemaphoreType.DMA((2,))],
    )
    return pl.pallas_call(
        _ffn_kernel,
        grid_spec=grid_spec,
        out_shape=jax.ShapeDtypeStruct((p_rows, d), F32),
        compiler_params=_params("arbitrary"),
        name="expert_ffn",
    )(blk_exp, blk_on, first, blk_slot.astype(jnp.int32), blk_next.astype(jnp.int32), row_tok,
      x, w_gate, w_up, w_down)


def _combine_ln_kernel(x_ref, y0_ref, y1_ref, gate_ref, g_ref, b_ref, o_ref):
    gates = gate_ref[...]
    moe = y0_ref[...] * gates[:, 0:1] + y1_ref[...] * gates[:, 1:2]
    o_ref[...] = _layer_norm(DN_ALPHA * x_ref[...] + moe, g_ref[...], b_ref[...])


def _combine_ln(x, y0, y1, gates, ln_g, ln_b, *, tm=256):
    m, d = x.shape
    row = pl.BlockSpec((tm, d), lambda i: (i, 0))
    vec = pl.BlockSpec((1, d), lambda i: (0, 0))
    return pl.pallas_call(
        _combine_ln_kernel,
        grid=(m // tm,),
        in_specs=[row, row, row, pl.BlockSpec((tm, LANES), lambda i: (i, 0)), vec, vec],
        out_specs=row,
        out_shape=jax.ShapeDtypeStruct((m, d), F32),
        compiler_params=_params("parallel"),
        name="combine_ln",
    )(x, y0, y1, gates, ln_g, ln_b)


def _layer(x, mem, w_in, cmp_pe_k, cmp_w1_k, cmp_w2_k, cmp_pe_v, cmp_w1_v, cmp_w2_v,
           sgu_ln_g, sgu_ln_b, sgu_w_s, sgu_b_s, w_branch_a, w_branch_b, w_o, ln1_g, ln1_b,
           w_xq, w_xk, w_xv, w_xo, ln2_g, ln2_b,
           w_router_grp, b_router_grp, w_router_exp, b_router_exp,
           w_exp_gate, w_exp_up, w_exp_down, ln3_g, ln3_b):
    bsz, seq, d = x.shape
    n_tok = bsz * seq
    g = NSA_KV_GROUPS
    hpg = NSA_HPG
    xf = x.reshape(n_tok, d)
    x_bf = xf.astype(BF16)

    c0 = Q_WIDTH
    c1 = c0 + KV_WIDTH
    c2 = c1 + NSA_GATE_WIDTH
    c3 = c2 + 2 * SGU_WIDTH
    w_q = w_in[:, :c0].astype(BF16)
    w_gate = w_in[:, c1:c2].reshape(d, g, hpg, 3).transpose(0, 1, 3, 2).reshape(d, g, 3 * hpg)
    w_gate = jnp.pad(w_gate, ((0, 0), (0, 0), (0, LANES - 3 * hpg))).reshape(d, g * LANES)
    w_kvg = jnp.concatenate([w_in[:, c0:c1], w_gate], axis=1).astype(BF16)
    w_uv = w_in[:, c2:c3].astype(BF16)
    w_merge = w_in[:, c3:].astype(BF16)

    cos, sin = _rope_tables(jnp.arange(seq, dtype=jnp.int32))
    q = _q_proj(x_bf, w_q, cos, sin, seq).reshape(bsz, seq, Q_WIDTH)
    kv_cmp, kv_rest, nsa_gates = _kv_proj(x_bf, w_kvg, cos, sin, seq)

    n_chunks = seq // CMP_STRIDE
    pe = jnp.stack([cmp_pe_k, cmp_pe_v])
    w1 = jnp.stack([cmp_w1_k, cmp_w1_v]).astype(BF16)
    w2 = jnp.stack([cmp_w2_k, cmp_w2_v]).astype(BF16)
    cmp_end = CMP_STRIDE * jnp.arange(n_chunks, dtype=jnp.int32) + CMP_BLOCK - 1
    ccos, csin = _rope_tables(cmp_end)
    cmp_cos = jnp.stack([ccos, jnp.ones_like(ccos)])
    cmp_sin = jnp.stack([csin, jnp.zeros_like(csin)])
    cmp_kv = _compress(kv_cmp.reshape(bsz, seq, _N_CMP_COLS), pe, w1, w2, cmp_cos, cmp_sin)

    n_sel = seq // SEL_BLOCK
    n_pick = min(SEL_TOPK, n_sel)
    ci = CMP_STRIDE * jnp.arange(n_chunks)
    sj = SEL_BLOCK * jnp.arange(LANES)
    overlap = jnp.clip(jnp.minimum(ci[:, None] + CMP_BLOCK, sj[None, :] + SEL_BLOCK)
                       - jnp.maximum(ci[:, None], sj[None, :]), 0, None)
    selmap_t = (overlap.astype(F32) / CMP_BLOCK).astype(BF16).T
    o_nsa = _nsa(q, cmp_kv, kv_rest.reshape(bsz, seq, _N_REST_COLS),
                 nsa_gates.reshape(bsz, seq, _GATE_COLS), selmap_t, n_sel=n_sel, n_pick=n_pick)

    uv = _matmul(x_bf, w_uv, tm=1024, tn=1024, out_dtype=F32, act="gelu")
    o_sgu = _sgu(uv, sgu_ln_g.reshape(1, -1), sgu_ln_b.reshape(1, -1), sgu_w_s, sgu_b_s.T)

    merged = _merge(x_bf, o_nsa.reshape(n_tok, Q_WIDTH), o_sgu, w_merge,
                    w_branch_a.astype(BF16), w_branch_b.astype(BF16))
    x1, x1_bf = _proj_ln(merged, w_o.astype(BF16), xf, ln1_g.reshape(1, -1), ln1_b.reshape(1, -1))

    mem_len = mem.shape[1]
    w_mem = jnp.concatenate([w_xk, w_xv], axis=1).astype(BF16)
    mem_kv = _matmul(mem.reshape(bsz * mem_len, d).astype(BF16), w_mem, tm=256, tn=512, out_dtype=BF16)
    x2 = _xattn(x1_bf.reshape(bsz, seq, d), x1.reshape(bsz, seq, d), w_xq.astype(BF16),
                mem_kv.reshape(bsz, mem_len, 2 * MEM_WIDTH), w_xo.astype(BF16),
                ln2_g.reshape(1, -1), ln2_b.reshape(1, -1)).reshape(n_tok, d)

    w_r = jnp.concatenate([w_router_grp, w_router_exp], axis=1)
    w_r = jnp.pad(w_r, ((0, 0), (0, LANES - w_r.shape[1])))
    w_r_hi = w_r.astype(BF16)
    w_r_lo = (w_r - w_r_hi.astype(F32)).astype(BF16)
    b_r = jnp.concatenate([b_router_grp, b_router_exp])
    b_r = jnp.pad(b_r, (0, LANES - b_r.shape[0])).reshape(1, LANES)
    gates, experts, ranks, seen = _router(x2, w_r_hi, w_r_lo, b_r)

    nk = n_tok * EXPERT_TOPK
    counts = seen[0, N_GROUPS:N_GROUPS + N_EXPERTS].astype(jnp.int32)
    pcounts = ((counts + MOE_BLOCK - 1) // MOE_BLOCK) * MOE_BLOCK
    pend = jnp.cumsum(pcounts)
    pstart = pend - pcounts
    e_tok = experts[:, :EXPERT_TOPK]
    seg_start = jnp.sum(jnp.where(e_tok[..., None] == jnp.arange(N_EXPERTS), pstart, 0), -1)
    dest_of = seg_start + ranks[:, :EXPERT_TOPK]
    n_blocks = -(-nk // MOE_BLOCK) + N_EXPERTS
    p_rows = n_blocks * MOE_BLOCK
    tok_flat = jnp.repeat(jnp.arange(n_tok, dtype=jnp.int32), EXPERT_TOPK)
    row_tok = (jnp.arange(p_rows, dtype=jnp.int32) % n_tok).at[dest_of.reshape(nk)].set(tok_flat)
    blk_start = jnp.arange(n_blocks) * MOE_BLOCK
    blk_on = (blk_start < pend[-1]).astype(jnp.int32)
    blk_probe = jnp.minimum(blk_start, pend[-1] - 1)
    blk_exp = jnp.minimum(jnp.sum((pend[None, :] <= blk_probe[:, None]).astype(jnp.int32), -1),
                          N_EXPERTS - 1).astype(jnp.int32)

    y_rows = _expert_ffn(blk_exp, blk_on, row_tok, x2, w_exp_gate, w_exp_up, w_exp_down)
    out = _combine_ln(x2, y_rows[dest_of[:, 0]], y_rows[dest_of[:, 1]], gates,
                      ln3_g.reshape(1, -1), ln3_b.reshape(1, -1))
    return out.reshape(bsz, seq, d)


def kernel(x, mem, w_in, cmp_pe_k, cmp_w1_k, cmp_w2_k, cmp_pe_v, cmp_w1_v, cmp_w2_v, sgu_ln_g, sgu_ln_b, sgu_w_s, sgu_b_s, w_branch_a, w_branch_b, w_o, ln1_g, ln1_b, w_xq, w_xk, w_xv, w_xo, ln2_g, ln2_b, w_router_grp, b_router_grp, w_router_exp, b_router_exp, w_exp_gate, w_exp_up, w_exp_down, ln3_g, ln3_b):
    params = (w_in, cmp_pe_k, cmp_w1_k, cmp_w2_k, cmp_pe_v, cmp_w1_v, cmp_w2_v,
              sgu_ln_g, sgu_ln_b, sgu_w_s, sgu_b_s, w_branch_a, w_branch_b, w_o, ln1_g, ln1_b,
              w_xq, w_xk, w_xv, w_xo, ln2_g, ln2_b,
              w_router_grp, b_router_grp, w_router_exp, b_router_exp,
              w_exp_gate, w_exp_up, w_exp_down, ln3_g, ln3_b)
    h = x
    for l in range(DEPTH):
        h = _layer(h, mem, *[p[l] for p in params])
    return h
```

```python
import functools
import math

import jax
import jax.numpy as jnp
from jax import lax
from jax.experimental import pallas as pl
from jax.experimental.pallas import tpu as pltpu

D_MODEL = 2048
HEAD_DIM = 128
NSA_HEADS = 16
NSA_KV_GROUPS = 2
NSA_HPG = NSA_HEADS // NSA_KV_GROUPS
CMP_BLOCK = 32
CMP_STRIDE = 16
CMP_HIDDEN = 256
SEL_BLOCK = 64
SEL_TOPK = 16
WINDOW = 512
Q_BLOCK = 128
SGU_WIDTH = 2048
SGU_GROUPS = 8
SGU_CHUNK = 128
MEM_HEADS = 4
N_GROUPS = 4
EXPERTS_PER_GROUP = 16
N_EXPERTS = N_GROUPS * EXPERTS_PER_GROUP
EXPERT_TOPK = 2
EXPERT_FF = 512
MOE_BLOCK = 128
ROPE_THETA = 10000.0
LN_EPS = 1e-5
SEL_BIG = 1e9
DEPTH = 1
DN_ALPHA = (2.0 * DEPTH) ** 0.25

Q_WIDTH = NSA_HEADS * HEAD_DIM
KV_WIDTH = 3 * 2 * NSA_KV_GROUPS * HEAD_DIM
NSA_GATE_WIDTH = 3 * NSA_HEADS
MEM_WIDTH = MEM_HEADS * HEAD_DIM

LANES = 128
VMEM_LIMIT = 56 * 1024 * 1024
MASKED = -1e30
M_INIT = -1e20
BF16 = jnp.bfloat16
F32 = jnp.float32


def _params(*sem):
    return pltpu.CompilerParams(dimension_semantics=sem, vmem_limit_bytes=VMEM_LIMIT)


def _dot(a, b):
    return jnp.dot(a, b, preferred_element_type=F32)


def _dot_nt(a, b):
    return lax.dot_general(a, b, (((1,), (1,)), ((), ())), preferred_element_type=F32)


def _gelu(x):
    return 0.5 * x * (1.0 + jnp.tanh(math.sqrt(2.0 / math.pi) * (x + 0.044715 * (x * x * x))))


def _sigmoid(x):
    return 1.0 / (1.0 + jnp.exp(-x))


def _layer_norm(x, g, b):
    mu = jnp.mean(x, -1, keepdims=True)
    xc = x - mu
    var = jnp.mean(xc * xc, -1, keepdims=True)
    return xc * lax.rsqrt(var + LN_EPS) * g + b


def _rope(t, cos_full, sin_signed):
    return t * cos_full + pltpu.roll(t, HEAD_DIM // 2, axis=1) * sin_signed


def _rope_tables(pos):
    inv = ROPE_THETA ** (-jnp.arange(0, HEAD_DIM, 2, dtype=F32) / HEAD_DIM)
    ang = pos.astype(F32)[:, None] * inv[None, :]
    c, s = jnp.cos(ang), jnp.sin(ang)
    return jnp.concatenate([c, c], -1), jnp.concatenate([-s, s], -1)


MXU_COLS = 256


def _mm_kernel(a_ref, b_ref, o_ref, *, act):
    a = a_ref[...]
    for j in range(o_ref.shape[1] // MXU_COLS):
        cols = slice(j * MXU_COLS, (j + 1) * MXU_COLS)
        y = _dot(a, b_ref[:, cols])
        if act == "gelu":
            y = _gelu(y)
        o_ref[:, cols] = y.astype(o_ref.dtype)


def _matmul(a, b, *, tm, tn, out_dtype, act=None):
    m, k = a.shape
    n = b.shape[1]
    return pl.pallas_call(
        functools.partial(_mm_kernel, act=act),
        grid=(m // tm, n // tn),
        in_specs=[pl.BlockSpec((tm, k), lambda i, j: (i, 0)),
                  pl.BlockSpec((k, tn), lambda i, j: (0, j))],
        out_specs=pl.BlockSpec((tm, tn), lambda i, j: (i, j)),
        out_shape=jax.ShapeDtypeStruct((m, n), out_dtype),
        compiler_params=_params("parallel", "parallel"),
        name="matmul_" + (act or "plain"),
    )(a, b)


def _qproj_kernel(a_ref, b_ref, cos_ref, sin_ref, o_ref):
    a = a_ref[...]
    cos, sin = cos_ref[...], sin_ref[...]
    scale = HEAD_DIM ** -0.5 * math.log2(math.e)
    for j in range(o_ref.shape[1] // MXU_COLS):
        y = _dot(a, b_ref[:, j * MXU_COLS:(j + 1) * MXU_COLS])
        for h in range(MXU_COLS // HEAD_DIM):
            sl = slice(j * MXU_COLS + h * HEAD_DIM, j * MXU_COLS + (h + 1) * HEAD_DIM)
            o_ref[:, sl] = (_rope(y[:, h * HEAD_DIM:(h + 1) * HEAD_DIM], cos, sin) * scale).astype(o_ref.dtype)


def _q_proj(x_bf, w_q, cos, sin, seq, *, tm=1024, tn=1024):
    m, k = x_bf.shape
    n = w_q.shape[1]
    per_seq = seq // tm
    return pl.pallas_call(
        _qproj_kernel,
        grid=(m // tm, n // tn),
        in_specs=[pl.BlockSpec((tm, k), lambda i, j: (i, 0)),
                  pl.BlockSpec((k, tn), lambda i, j: (0, j)),
                  pl.BlockSpec((tm, HEAD_DIM), lambda i, j: (i % per_seq, 0)),
                  pl.BlockSpec((tm, HEAD_DIM), lambda i, j: (i % per_seq, 0))],
        out_specs=pl.BlockSpec((tm, tn), lambda i, j: (i, j)),
        out_shape=jax.ShapeDtypeStruct((m, n), BF16),
        compiler_params=_params("parallel", "parallel"),
        name="q_proj",
    )(x_bf, w_q, cos, sin)


_N_CMP_COLS = 2 * NSA_KV_GROUPS * HEAD_DIM
_N_REST_CHUNKS = (KV_WIDTH - _N_CMP_COLS) // HEAD_DIM
_N_REST_COLS = (_N_REST_CHUNKS + NSA_KV_GROUPS) * HEAD_DIM
_GATE_COLS = NSA_KV_GROUPS * LANES
_SEL_SHIFT = SEL_BLOCK.bit_length() - 1
assert 1 << _SEL_SHIFT == SEL_BLOCK


def _kvproj_kernel(a_ref, b_ref, cos_ref, sin_ref, cmp_ref, rest_ref, gate_ref, *, per_seq):
    g = NSA_KV_GROUPS
    y = _dot(a_ref[...], b_ref[...])
    cos, sin = cos_ref[...], sin_ref[...]
    cmp_ref[...] = y[:, :_N_CMP_COLS]
    tm = y.shape[0]
    pos = (pl.program_id(0) % per_seq) * tm + lax.broadcasted_iota(jnp.int32, (tm, LANES), 0)
    lane = lax.broadcasted_iota(jnp.int32, (tm, LANES), 1)
    blk_onehot = jnp.where(lane == (pos >> _SEL_SHIFT), 1.0, 0.0).astype(rest_ref.dtype)
    for c in range(_N_REST_CHUNKS):
        t = y[:, _N_CMP_COLS + c * HEAD_DIM:_N_CMP_COLS + (c + 1) * HEAD_DIM]
        if (c % (2 * g)) < g:
            t = _rope(t, cos, sin)
        dst = 2 * c if c < g else g + c
        rest_ref[:, dst * HEAD_DIM:(dst + 1) * HEAD_DIM] = t.astype(rest_ref.dtype)
        if c < g:
            rest_ref[:, (dst + 1) * HEAD_DIM:(dst + 2) * HEAD_DIM] = blk_onehot
    gate_ref[...] = _sigmoid(y[:, KV_WIDTH:])


def _kv_proj(x_bf, w_kvg, cos, sin, seq, *, tm=512):
    m, k = x_bf.shape
    n = w_kvg.shape[1]
    per_seq = seq // tm
    return pl.pallas_call(
        functools.partial(_kvproj_kernel, per_seq=per_seq),
        grid=(m // tm,),
        in_specs=[pl.BlockSpec((tm, k), lambda i: (i, 0)),
                  pl.BlockSpec((k, n), lambda i: (0, 0)),
                  pl.BlockSpec((tm, HEAD_DIM), lambda i: (i % per_seq, 0)),
                  pl.BlockSpec((tm, HEAD_DIM), lambda i: (i % per_seq, 0))],
        out_specs=[pl.BlockSpec((tm, _N_CMP_COLS), lambda i: (i, 0)),
                   pl.BlockSpec((tm, _N_REST_COLS), lambda i: (i, 0)),
                   pl.BlockSpec((tm, _GATE_COLS), lambda i: (i, 0))],
        out_shape=[jax.ShapeDtypeStruct((m, _N_CMP_COLS), F32),
                   jax.ShapeDtypeStruct((m, _N_REST_COLS), BF16),
                   jax.ShapeDtypeStruct((m, _GATE_COLS), F32)],
        compiler_params=_params("parallel"),
        name="kv_gate_proj",
    )(x_bf, w_kvg, cos, sin)


def _compress_kernel(kc_ref, pe_ref, w1_ref, w2_ref, cos_ref, sin_ref, o_ref):
    n_chunks = o_ref.shape[2]
    top = jnp.zeros((n_chunks, CMP_HIDDEN), F32)
    bot = jnp.zeros((n_chunks, CMP_HIDDEN), F32)
    for p in range(CMP_STRIDE):
        xp = kc_ref[0, pl.ds(p, n_chunks, stride=CMP_STRIDE), :]
        q = CMP_STRIDE + p
        top += _dot((xp + pe_ref[0, p:p + 1, :]).astype(BF16), w1_ref[0, p * HEAD_DIM:(p + 1) * HEAD_DIM, :])
        bot += _dot((xp + pe_ref[0, q:q + 1, :]).astype(BF16), w1_ref[0, q * HEAD_DIM:(q + 1) * HEAD_DIM, :])
    h = top + pltpu.roll(bot, n_chunks - 1, axis=0)
    y = _dot(_gelu(h).astype(BF16), w2_ref[0])
    o_ref[0, 0] = _rope(y, cos_ref[0], sin_ref[0]).astype(o_ref.dtype)


def _compress(kv_cmp, pe, w1, w2, cos, sin):
    bsz, seq, width = kv_cmp.shape
    four = width // HEAD_DIM
    n_chunks = seq // CMP_STRIDE
    g = NSA_KV_GROUPS
    return pl.pallas_call(
        _compress_kernel,
        grid=(four, bsz),
        in_specs=[pl.BlockSpec((1, seq, HEAD_DIM), lambda a, b: (b, 0, a)),
                  pl.BlockSpec((1, CMP_BLOCK, HEAD_DIM), lambda a, b: (a // g, 0, 0)),
                  pl.BlockSpec((1, CMP_BLOCK * HEAD_DIM, CMP_HIDDEN), lambda a, b: (a // g, 0, 0)),
                  pl.BlockSpec((1, CMP_HIDDEN, HEAD_DIM), lambda a, b: (a // g, 0, 0)),
                  pl.BlockSpec((1, n_chunks, HEAD_DIM), lambda a, b: (a // g, 0, 0)),
                  pl.BlockSpec((1, n_chunks, HEAD_DIM), lambda a, b: (a // g, 0, 0))],
        out_specs=pl.BlockSpec((1, 1, n_chunks, HEAD_DIM), lambda a, b: (a, b, 0, 0)),
        out_shape=jax.ShapeDtypeStruct((four, bsz, n_chunks, HEAD_DIM), BF16),
        compiler_params=_params("parallel", "parallel"),
        name="compress",
    )(kv_cmp, pe, w1, w2, cos, sin)


SEL_TILE = 512
NSA_TQ = 256
SUBLANES = 8


def _softmax_parts(s3, bias):
    s3 = s3 + bias[None]
    m = jnp.max(s3, -1, keepdims=True)
    m = jnp.where(m > 0.5 * MASKED, m, 0.0)
    p = jnp.exp2(s3 - m)
    return p, 1.0 / jnp.maximum(jnp.sum(p, -1, keepdims=True), 1e-30)


def _topk_unselected(score_t, n_sel, n_pick):
    tq = score_t.shape[1]
    n_grp = n_sel // SUBLANES
    grp = [score_t[g * SUBLANES:(g + 1) * SUBLANES] for g in range(n_grp)]
    row_in_grp = lax.broadcasted_iota(jnp.int32, (SUBLANES, tq), 0)
    later = [jnp.where(row_in_grp > li, 1.0, 0.0) for li in range(SUBLANES)]
    beaten = [jnp.zeros((SUBLANES, tq), F32) for _ in range(n_grp)]
    for i in range(n_sel):
        gi, li = divmod(i, SUBLANES)
        row = grp[gi][li:li + 1, :]
        for g in range(n_grp):
            if g > gi:
                beaten[g] = beaten[g] + jnp.where(row >= grp[g], 1.0, 0.0)
            elif g < gi:
                beaten[g] = beaten[g] + jnp.where(row > grp[g], 1.0, 0.0)
            else:
                beaten[g] = (beaten[g] + jnp.where(row > grp[g], 1.0, 0.0)
                             + jnp.where(row == grp[g], later[li], 0.0))
    return jnp.concatenate([jnp.where(b < n_pick, 0.0, MASKED) for b in beaten], axis=0)


def _nsa_kernel(q_ref, kc_ref, vc_ref, ks_ref, vs_ref, kw_ref, vw_ref, gate_ref, selmap_t_ref, o_ref,
                *, n_sel, n_pick):
    hpg = NSA_HPG
    tq = NSA_TQ
    c = pl.program_id(2)
    q_t = q_ref[0]
    q8 = jnp.concatenate([q_t[:, n * HEAD_DIM:(n + 1) * HEAD_DIM] for n in range(hpg)], axis=0)
    t_pos = c * tq + lax.broadcasted_iota(jnp.int32, (tq, 1), 0)
    t_lane = c * tq + lax.broadcasted_iota(jnp.int32, (1, tq), 1)

    kc = kc_ref[0, 0]
    n_cmp_pad = kc.shape[0]
    k_idx = lax.broadcasted_iota(jnp.int32, (1, n_cmp_pad), 1)
    vis_c = (CMP_STRIDE * k_idx + (CMP_BLOCK - 1) <= t_pos) & (k_idx < n_cmp_pad - 1)
    p_c, r_c = _softmax_parts(_dot_nt(q8, kc).reshape(hpg, tq, n_cmp_pad), jnp.where(vis_c, 0.0, MASKED))
    o_c = _dot(p_c.reshape(hpg * tq, n_cmp_pad).astype(BF16), vc_ref[0, 0]).reshape(hpg, tq, HEAD_DIM) * r_c

    p_sum = jnp.sum(p_c * r_c, axis=0)
    p_hi = p_sum.astype(BF16)
    p_lo = (p_sum - p_hi.astype(F32)).astype(BF16)
    selmap_t = selmap_t_ref[...]
    imp_t = _dot_nt(selmap_t, p_hi) + _dot_nt(selmap_t, p_lo)
    blk = lax.broadcasted_iota(jnp.int32, (LANES, tq), 0)
    cur = t_lane >> _SEL_SHIFT
    forced = (blk == 0) | (blk == cur) | (blk == cur - 1)
    score_t = jnp.where(forced, SEL_BIG, jnp.where(blk <= cur, imp_t, -SEL_BIG))
    unsel_t = _topk_unselected(score_t, n_sel, n_pick)
    unsel_t = jnp.concatenate([unsel_t, jnp.zeros((LANES - n_sel, tq), F32)], axis=0)
    unsel = unsel_t.T.astype(BF16)
    q_aug = jnp.concatenate([q8, jnp.concatenate([unsel] * hpg, axis=0)], axis=1)

    def sel_tile(base, carry, bias):
        m_old, l_old, acc = carry
        k = ks_ref[0, pl.ds(base, SEL_TILE), :]
        v = vs_ref[0, pl.ds(base, SEL_TILE), :]
        s = _dot_nt(q_aug, k).reshape(hpg, tq, SEL_TILE)
        if bias is not None:
            s = s + bias[None]
        m_new = jnp.maximum(m_old, jnp.max(s, -1, keepdims=True))
        alpha = jnp.exp2(m_old - m_new)
        p = jnp.exp2(s - m_new)
        l_new = alpha * l_old + jnp.sum(p, -1, keepdims=True)
        pv = _dot(p.reshape(hpg * tq, SEL_TILE).astype(BF16), v).reshape(hpg, tq, HEAD_DIM)
        return m_new, l_new, alpha * acc + pv

    n_full = c // (SEL_TILE // tq)
    init = (jnp.full((hpg, tq, 1), M_INIT, F32), jnp.zeros((hpg, tq, 1), F32),
            jnp.zeros((hpg, tq, HEAD_DIM), F32))
    carry = lax.fori_loop(
        0, n_full, lambda kt, cr: sel_tile(pl.multiple_of(kt * SEL_TILE, SEL_TILE), cr, None), init)
    base_last = pl.multiple_of(n_full * SEL_TILE, SEL_TILE)
    kp_last = base_last + lax.broadcasted_iota(jnp.int32, (1, SEL_TILE), 1)
    _, l_s, acc_s = sel_tile(base_last, carry, jnp.where(kp_last <= t_pos, 0.0, MASKED))
    o_s = acc_s * (1.0 / jnp.maximum(l_s, 1e-30))

    span = WINDOW + tq
    start = pl.multiple_of(jnp.maximum(c * tq - WINDOW, 0), tq)
    kw = kw_ref[0, pl.ds(start, span), :]
    vw = vw_ref[0, pl.ds(start, span), :]
    kp = start + lax.broadcasted_iota(jnp.int32, (1, span), 1)
    vis_w = (kp <= t_pos) & (t_pos - kp < WINDOW)
    p_w, r_w = _softmax_parts(_dot_nt(q8, kw).reshape(hpg, tq, span), jnp.where(vis_w, 0.0, MASKED))
    o_w = _dot(p_w.reshape(hpg * tq, span).astype(BF16), vw).reshape(hpg, tq, HEAD_DIM) * r_w

    gates = gate_ref[0]
    for n in range(hpg):
        out = (gates[:, n:n + 1] * o_c[n]
               + gates[:, hpg + n:hpg + n + 1] * o_s[n]
               + gates[:, 2 * hpg + n:2 * hpg + n + 1] * o_w[n])
        o_ref[0, :, n * HEAD_DIM:(n + 1) * HEAD_DIM] = out.astype(o_ref.dtype)


def _nsa(q, cmp_kv, rest, gates, selmap_t, *, n_sel, n_pick):
    bsz, seq, _ = q.shape
    g = NSA_KV_GROUPS
    n_cmp_pad = cmp_kv.shape[2]
    gw = NSA_HPG * HEAD_DIM
    assert n_sel % SUBLANES == 0 and n_sel <= LANES and seq >= WINDOW + NSA_TQ
    assert SEL_TILE % NSA_TQ == 0 and seq % SEL_TILE == 0
    kv_spec = lambda off: pl.BlockSpec((1, seq, HEAD_DIM), lambda b, gi, c: (b, 0, off + gi))
    return pl.pallas_call(
        functools.partial(_nsa_kernel, n_sel=n_sel, n_pick=n_pick),
        grid=(bsz, g, seq // NSA_TQ),
        in_specs=[pl.BlockSpec((1, NSA_TQ, gw), lambda b, gi, c: (b, c, gi)),
                  pl.BlockSpec((1, 1, n_cmp_pad, HEAD_DIM), lambda b, gi, c: (gi, b, 0, 0)),
                  pl.BlockSpec((1, 1, n_cmp_pad, HEAD_DIM), lambda b, gi, c: (g + gi, b, 0, 0)),
                  pl.BlockSpec((1, seq, 2 * HEAD_DIM), lambda b, gi, c: (b, 0, gi)),
                  kv_spec(2 * g), kv_spec(3 * g), kv_spec(4 * g),
                  pl.BlockSpec((1, NSA_TQ, LANES), lambda b, gi, c: (b, c, gi)),
                  pl.BlockSpec(selmap_t.shape, lambda b, gi, c: (0, 0))],
        out_specs=pl.BlockSpec((1, NSA_TQ, gw), lambda b, gi, c: (b, c, gi)),
        out_shape=jax.ShapeDtypeStruct(q.shape, BF16),
        compiler_params=_params("parallel", "parallel", "arbitrary"),
        name="nsa_attention",
    )(q, cmp_kv, cmp_kv, rest, rest, rest, rest, gates, selmap_t)


def _sgu_kernel(u_ref, v_ref, g_ref, b_ref, ws_ref, bs_ref, o_ref, *, chunks):
    gd = SGU_WIDTH // SGU_GROUPS
    row = lax.broadcasted_iota(jnp.int32, (SGU_CHUNK, SGU_CHUNK), 0)
    col = lax.broadcasted_iota(jnp.int32, (SGU_CHUNK, SGU_CHUNK), 1)
    causal = col <= row
    bs = bs_ref[...]
    for ch in range(chunks):
        rows = slice(ch * SGU_CHUNK, (ch + 1) * SGU_CHUNK)
        vn = _layer_norm(v_ref[rows, :], g_ref[...], b_ref[...]).astype(BF16)
        for gi in range(SGU_GROUPS):
            cols = slice(gi * gd, (gi + 1) * gd)
            w = jnp.where(causal, ws_ref[gi], 0.0).astype(BF16)
            z = _dot(w, vn[:, cols]) + bs[:, gi:gi + 1]
            o_ref[rows, cols] = (u_ref[rows, cols] * z).astype(o_ref.dtype)


def _sgu(uv, ln_g, ln_b, w_s, b_s_t, *, chunks=2):
    m = uv.shape[0]
    tm = chunks * SGU_CHUNK
    return pl.pallas_call(
        functools.partial(_sgu_kernel, chunks=chunks),
        grid=(m // tm,),
        in_specs=[pl.BlockSpec((tm, SGU_WIDTH), lambda i: (i, 0)),
                  pl.BlockSpec((tm, SGU_WIDTH), lambda i: (i, 1)),
                  pl.BlockSpec((1, SGU_WIDTH), lambda i: (0, 0)),
                  pl.BlockSpec((1, SGU_WIDTH), lambda i: (0, 0)),
                  pl.BlockSpec(w_s.shape, lambda i: (0, 0, 0)),
                  pl.BlockSpec(b_s_t.shape, lambda i: (0, 0))],
        out_specs=pl.BlockSpec((tm, SGU_WIDTH), lambda i: (i, 0)),
        out_shape=jax.ShapeDtypeStruct((m, SGU_WIDTH), BF16),
        compiler_params=_params("parallel"),
        name="sgu",
    )(uv, uv, ln_g, ln_b, w_s, b_s_t)


def _merge_kernel(x_ref, a_ref, s_ref, wga_ref, wgb_ref, wa_ref, wb_ref, o_ref):
    x = x_ref[...]
    ga = _sigmoid(_dot(x, wga_ref[...]))
    gb = _sigmoid(_dot(x, wgb_ref[...]))
    o_ref[...] = (ga * _dot(a_ref[...], wa_ref[...]) + gb * _dot(s_ref[...], wb_ref[...])).astype(o_ref.dtype)


def _merge(x_bf, o_nsa, o_sgu, w_merge, w_a, w_b, *, tm=1024, tn=256):
    m, k = x_bf.shape
    n = w_a.shape[1]
    nb = n // tn
    row = pl.BlockSpec((tm, k), lambda i, j: (i, 0))
    col = pl.BlockSpec((k, tn), lambda i, j: (0, j))
    return pl.pallas_call(
        _merge_kernel,
        grid=(m // tm, nb),
        in_specs=[row, row, row, col, pl.BlockSpec((k, tn), lambda i, j: (0, nb + j)), col, col],
        out_specs=pl.BlockSpec((tm, tn), lambda i, j: (i, j)),
        out_shape=jax.ShapeDtypeStruct((m, n), BF16),
        compiler_params=_params("parallel", "parallel"),
        name="merge",
    )(x_bf, o_nsa, o_sgu, w_merge, w_merge, w_a, w_b)


def _proj_ln_kernel(a_ref, w_ref, x_ref, g_ref, b_ref, o_ref, obf_ref):
    y = DN_ALPHA * x_ref[...] + _dot(a_ref[...], w_ref[...])
    out = _layer_norm(y, g_ref[...], b_ref[...])
    o_ref[...] = out
    obf_ref[...] = out.astype(BF16)


def _proj_ln(a_bf, w, x, ln_g, ln_b, *, tm=512):
    m, k = a_bf.shape
    n = w.shape[1]
    vec = pl.BlockSpec((1, n), lambda i: (0, 0))
    return pl.pallas_call(
        _proj_ln_kernel,
        grid=(m // tm,),
        in_specs=[pl.BlockSpec((tm, k), lambda i: (i, 0)),
                  pl.BlockSpec((k, n), lambda i: (0, 0)),
                  pl.BlockSpec((tm, n), lambda i: (i, 0)), vec, vec],
        out_specs=[pl.BlockSpec((tm, n), lambda i: (i, 0)), pl.BlockSpec((tm, n), lambda i: (i, 0))],
        out_shape=[jax.ShapeDtypeStruct((m, n), F32), jax.ShapeDtypeStruct((m, n), BF16)],
        compiler_params=_params("parallel"),
        name="proj_ln",
    )(a_bf, w, x, ln_g, ln_b)


def _xattn_kernel(xbf_ref, x_ref, wq_ref, kv_ref, wo_ref, g_ref, b_ref, o_ref):
    q = (_dot(xbf_ref[0], wq_ref[...]) * HEAD_DIM ** -0.5).astype(BF16)
    kv = kv_ref[0]
    heads = []
    for h in range(MEM_HEADS):
        cols = slice(h * HEAD_DIM, (h + 1) * HEAD_DIM)
        s = _dot_nt(q[:, cols], kv[:, cols])
        p = jnp.exp(s - jnp.max(s, -1, keepdims=True))
        p = p / jnp.sum(p, -1, keepdims=True)
        heads.append(_dot(p.astype(BF16), kv[:, MEM_WIDTH + h * HEAD_DIM:MEM_WIDTH + (h + 1) * HEAD_DIM]))
    o = jnp.concatenate(heads, axis=1).astype(BF16)
    y = DN_ALPHA * x_ref[0] + _dot(o, wo_ref[...])
    o_ref[0] = _layer_norm(y, g_ref[...], b_ref[...])


def _xattn(x_bf, x, w_xq, mem_kv, w_xo, ln_g, ln_b, *, tm=512):
    bsz, seq, d = x.shape
    mem_len = mem_kv.shape[1]
    vec = pl.BlockSpec((1, d), lambda b, i: (0, 0))
    return pl.pallas_call(
        _xattn_kernel,
        grid=(bsz, seq // tm),
        in_specs=[pl.BlockSpec((1, tm, d), lambda b, i: (b, i, 0)),
                  pl.BlockSpec((1, tm, d), lambda b, i: (b, i, 0)),
                  pl.BlockSpec(w_xq.shape, lambda b, i: (0, 0)),
                  pl.BlockSpec((1, mem_len, 2 * MEM_WIDTH), lambda b, i: (b, 0, 0)),
                  pl.BlockSpec(w_xo.shape, lambda b, i: (0, 0)), vec, vec],
        out_specs=pl.BlockSpec((1, tm, d), lambda b, i: (b, i, 0)),
        out_shape=jax.ShapeDtypeStruct(x.shape, F32),
        compiler_params=_params("parallel", "parallel"),
        name="xattn_ln",
    )(x_bf, x, w_xq, mem_kv, w_xo, ln_g, ln_b)


def _router_kernel(x_ref, whi_ref, wlo_ref, b_ref, gate_ref, exp_ref, rank_ref, cnt_ref, seen_ref):
    @pl.when(pl.program_id(0) == 0)
    def _():
        seen_ref[...] = jnp.zeros_like(seen_ref)

    x = x_ref[...]
    x_hi = x.astype(BF16)
    x_lo = (x - x_hi.astype(F32)).astype(BF16)
    whi = whi_ref[...]
    logits = _dot(x_hi, whi) + _dot(x_hi, wlo_ref[...]) + _dot(x_lo, whi) + b_ref[...]
    rows = logits.shape[0]
    lane = lax.broadcasted_iota(jnp.int32, (rows, LANES), 1)
    lane_f = lane.astype(F32)
    is_grp = lane < N_GROUPS
    lg = jnp.where(is_grp, logits, MASKED)
    eg = jnp.where(is_grp, jnp.exp(lg - jnp.max(lg, -1, keepdims=True)), 0.0)
    p_grp = eg / jnp.sum(eg, -1, keepdims=True)
    g_grp = jnp.max(p_grp, -1, keepdims=True)
    grp = jnp.min(jnp.where((p_grp == g_grp) & is_grp, lane_f, 1e9), -1, keepdims=True).astype(jnp.int32)
    lo = N_GROUPS + grp * EXPERTS_PER_GROUP
    in_grp = (lane >= lo) & (lane < lo + EXPERTS_PER_GROUP)
    le = jnp.where(in_grp, logits, MASKED)
    ee = jnp.where(in_grp, jnp.exp(le - jnp.max(le, -1, keepdims=True)), 0.0)
    p_exp = jnp.where(in_grp, ee / jnp.sum(ee, -1, keepdims=True), -1.0)
    v1 = jnp.max(p_exp, -1, keepdims=True)
    i1 = jnp.min(jnp.where(p_exp == v1, lane_f, 1e9), -1, keepdims=True)
    p_rest = jnp.where(lane_f == i1, -1.0, p_exp)
    v2 = jnp.max(p_rest, -1, keepdims=True)
    i2 = jnp.min(jnp.where(p_rest == v2, lane_f, 1e9), -1, keepdims=True)
    denom = v1 + v2
    gate_ref[...] = jnp.where(lane == 0, g_grp * v1 / denom, jnp.where(lane == 1, g_grp * v2 / denom, 0.0))
    exp_ref[...] = jnp.where(lane == 0, i1, jnp.where(lane == 1, i2, float(N_GROUPS))).astype(jnp.int32) - N_GROUPS

    onehot = jnp.where((lane_f == i1) | (lane_f == i2), 1.0, 0.0)
    tok_r = lax.broadcasted_iota(jnp.int32, (rows, rows), 0)
    tok_c = lax.broadcasted_iota(jnp.int32, (rows, rows), 1)
    earlier = jnp.where(tok_c < tok_r, 1.0, 0.0).astype(BF16)
    before = _dot(earlier, onehot.astype(BF16)) + seen_ref[...]
    r1 = jnp.sum(jnp.where(lane_f == i1, before, 0.0), -1, keepdims=True)
    r2 = jnp.sum(jnp.where(lane_f == i2, before, 0.0), -1, keepdims=True)
    rank_ref[...] = jnp.where(lane == 0, r1, jnp.where(lane == 1, r2, 0.0)).astype(jnp.int32)
    seen_ref[...] += jnp.sum(onehot, axis=0, keepdims=True)
    cnt_ref[...] = seen_ref[...]


def _router(x, w_hi, w_lo, bias, *, tm=512):
    m, d = x.shape
    wspec = pl.BlockSpec((d, LANES), lambda i: (0, 0))
    vec = pl.BlockSpec((1, LANES), lambda i: (0, 0))
    tile = pl.BlockSpec((tm, LANES), lambda i: (i, 0))
    return pl.pallas_call(
        _router_kernel,
        grid=(m // tm,),
        in_specs=[pl.BlockSpec((tm, d), lambda i: (i, 0)), wspec, wspec, vec],
        out_specs=[tile, tile, tile, vec],
        out_shape=[jax.ShapeDtypeStruct((m, LANES), F32), jax.ShapeDtypeStruct((m, LANES), jnp.int32),
                   jax.ShapeDtypeStruct((m, LANES), jnp.int32), jax.ShapeDtypeStruct((1, LANES), F32)],
        scratch_shapes=[pltpu.VMEM((1, LANES), F32)],
        compiler_params=_params("arbitrary"),
        name="router",
    )(x, w_hi, w_lo, bias)


GATHER_UNROLL = 8


def _ffn_kernel(exp_ref, on_ref, first_ref, slot_ref, next_ref, tok_ref, dst_ref, x_hbm, wg_hbm, wu_hbm, wd_hbm,
                y_hbm, wg_buf, wu_buf, wd_buf, h_buf, y_buf, sem, row_sem, out_sem):
    i = pl.program_id(0)
    n_blk = pl.num_programs(0)
    slot = slot_ref[i]
    rows = h_buf.shape[1]

    def out_copy(blk, r):
        dst = dst_ref[blk * rows + r]
        return pltpu.make_async_copy(y_buf.at[blk % 2, pl.ds(r, 1), :], y_hbm.at[pl.ds(dst, 1), :],
                                     out_sem.at[blk % 2])

    def weight_copies(e, s):
        return (pltpu.make_async_copy(wg_hbm.at[e], wg_buf.at[s], sem.at[s, 0]),
                pltpu.make_async_copy(wu_hbm.at[e], wu_buf.at[s], sem.at[s, 1]),
                pltpu.make_async_copy(wd_hbm.at[e], wd_buf.at[s], sem.at[s, 2]))

    def row_copy(blk, r):
        tok = tok_ref[blk * rows + r]
        return pltpu.make_async_copy(x_hbm.at[pl.ds(tok, 1), :], h_buf.at[blk % 2, pl.ds(r, 1), :],
                                     row_sem.at[blk % 2])

    def for_rows(fn):
        def body(r, carry):
            fn(r)
            return carry
        lax.fori_loop(0, rows, body, 0, unroll=GATHER_UNROLL)

    @pl.when(i == 0)
    def _():
        for cp in weight_copies(exp_ref[0], 0):
            cp.start()
        for_rows(lambda r: row_copy(0, r).start())
        y_buf[...] = jnp.zeros_like(y_buf)
        spare = y_hbm.shape[0] - 2 * rows
        for s in range(2):
            cp = pltpu.make_async_copy(y_buf.at[s], y_hbm.at[pl.ds(spare + s * rows, rows), :], out_sem.at[s])
            cp.start()
            cp.wait()

    prev2 = jnp.maximum(i - 2, 0)

    @pl.when((i >= 2) & (on_ref[prev2] != 0))
    def _():
        for_rows(lambda r: out_copy(i - 2, r).wait())

    nxt_blk = jnp.minimum(i + 1, n_blk - 1)

    @pl.when((i + 1 < n_blk) & (on_ref[nxt_blk] != 0))
    def _():
        for_rows(lambda r: row_copy(i + 1, r).start())

    @pl.when(first_ref[i] != 0)
    def _():
        nxt = next_ref[i]

        @pl.when(nxt >= 0)
        def _():
            for cp in weight_copies(nxt, 1 - slot):
                cp.start()

        for cp in weight_copies(exp_ref[i], slot):
            cp.wait()

    @pl.when(on_ref[i] != 0)
    def _():
        for_rows(lambda r: row_copy(i, r).wait())
        h = h_buf[i % 2].astype(BF16)
        a = _dot(h, wg_buf[slot].astype(BF16))
        a = a * _sigmoid(a) * _dot(h, wu_buf[slot].astype(BF16))
        y_buf[i % 2] = _dot(a.astype(BF16), wd_buf[slot].astype(BF16))
        for_rows(lambda r: out_copy(i, r).start())

    @pl.when(i == n_blk - 1)
    def _():
        prev1 = jnp.maximum(i - 1, 0)

        @pl.when((i >= 1) & (on_ref[prev1] != 0))
        def _():
            for_rows(lambda r: out_copy(i - 1, r).wait())

        @pl.when(on_ref[i] != 0)
        def _():
            for_rows(lambda r: out_copy(i, r).wait())


def _expert_ffn(blk_exp, blk_on, row_tok, row_dst, x, w_gate, w_up, w_down):
    p_rows = row_tok.shape[0]
    n_tok, d = x.shape
    n_exp, _, ff = w_gate.shape
    n_blocks = p_rows // MOE_BLOCK
    prev_exp = jnp.concatenate([jnp.full((1,), -1, jnp.int32), blk_exp[:-1]])
    first = ((blk_on != 0) & (blk_exp != prev_exp)).astype(jnp.int32)
    seg_id = jnp.cumsum(first) - 1
    seg_ids = jnp.arange(n_exp + 1)
    hit = (first[:, None] != 0) & (seg_id[:, None] == seg_ids[None, :])
    seg_exp = jnp.sum(jnp.where(hit, blk_exp[:, None] + 1, 0), 0) - 1
    blk_next = jnp.sum(jnp.where(seg_ids[None, :] == seg_id[:, None] + 1, seg_exp[None, :] + 1, 0), 1) - 1
    blk_slot = jnp.maximum(seg_id, 0) % 2
    grid_spec = pltpu.PrefetchScalarGridSpec(
        num_scalar_prefetch=7,
        grid=(n_blocks,),
        in_specs=[pl.BlockSpec(memory_space=pl.ANY)] * 4,
        out_specs=pl.BlockSpec(memory_space=pl.ANY),
        scratch_shapes=[pltpu.VMEM((2, d, ff), F32), pltpu.VMEM((2, d, ff), F32), pltpu.VMEM((2, ff, d), F32),
                        pltpu.VMEM((2, MOE_BLOCK, d), F32), pltpu.VMEM((2, MOE_BLOCK, d), F32),
                        pltpu.SemaphoreType.DMA((2, 3)), pltpu.SemaphoreType.DMA((2,)),
                        pltpu.SemaphoreType.DMA((2,))],
    )
    return pl.pallas_call(
        _ffn_kernel,
        grid_spec=grid_spec,
        out_shape=jax.ShapeDtypeStruct((EXPERT_TOPK * n_tok + 2 * MOE_BLOCK, d), F32),
        compiler_params=_params("arbitrary"),
        name="expert_ffn",
    )(blk_exp, blk_on, first, blk_slot.astype(jnp.int32), blk_next.astype(jnp.int32), row_tok, row_dst,
      x, w_gate, w_up, w_down)


def _combine_ln_kernel(x_ref, y0_ref, y1_ref, gate_ref, g_ref, b_ref, o_ref):
    gates = gate_ref[...]
    moe = y0_ref[...] * gates[:, 0:1] + y1_ref[...] * gates[:, 1:2]
    o_ref[...] = _layer_norm(DN_ALPHA * x_ref[...] + moe, g_ref[...], b_ref[...])


def _combine_ln(x, y_tok, gates, ln_g, ln_b, *, tm=256):
    m, d = x.shape
    row = pl.BlockSpec((tm, d), lambda i: (i, 0))
    vec = pl.BlockSpec((1, d), lambda i: (0, 0))
    return pl.pallas_call(
        _combine_ln_kernel,
        grid=(m // tm,),
        in_specs=[row, row, pl.BlockSpec((tm, d), lambda i: (m // tm + i, 0)),
                  pl.BlockSpec((tm, LANES), lambda i: (i, 0)), vec, vec],
        out_specs=row,
        out_shape=jax.ShapeDtypeStruct((m, d), F32),
        compiler_params=_params("parallel"),
        name="combine_ln",
    )(x, y_tok, y_tok, gates, ln_g, ln_b)


def _layer(x, mem, w_in, cmp_pe_k, cmp_w1_k, cmp_w2_k, cmp_pe_v, cmp_w1_v, cmp_w2_v,
           sgu_ln_g, sgu_ln_b, sgu_w_s, sgu_b_s, w_branch_a, w_branch_b, w_o, ln1_g, ln1_b,
           w_xq, w_xk, w_xv, w_xo, ln2_g, ln2_b,
           w_router_grp, b_router_grp, w_router_exp, b_router_exp,
           w_exp_gate, w_exp_up, w_exp_down, ln3_g, ln3_b):
    bsz, seq, d = x.shape
    n_tok = bsz * seq
    g = NSA_KV_GROUPS
    hpg = NSA_HPG
    xf = x.reshape(n_tok, d)
    x_bf = xf.astype(BF16)

    c0 = Q_WIDTH
    c1 = c0 + KV_WIDTH
    c2 = c1 + NSA_GATE_WIDTH
    c3 = c2 + 2 * SGU_WIDTH
    w_q = w_in[:, :c0].astype(BF16)
    w_gate = w_in[:, c1:c2].reshape(d, g, hpg, 3).transpose(0, 1, 3, 2).reshape(d, g, 3 * hpg)
    w_gate = jnp.pad(w_gate, ((0, 0), (0, 0), (0, LANES - 3 * hpg))).reshape(d, g * LANES)
    w_kvg = jnp.concatenate([w_in[:, c0:c1], w_gate], axis=1).astype(BF16)
    w_uv = w_in[:, c2:c3].astype(BF16)
    w_merge = w_in[:, c3:].astype(BF16)

    cos, sin = _rope_tables(jnp.arange(seq, dtype=jnp.int32))
    q = _q_proj(x_bf, w_q, cos, sin, seq).reshape(bsz, seq, Q_WIDTH)
    kv_cmp, kv_rest, nsa_gates = _kv_proj(x_bf, w_kvg, cos, sin, seq)

    n_chunks = seq // CMP_STRIDE
    pe = jnp.stack([cmp_pe_k, cmp_pe_v])
    w1 = jnp.stack([cmp_w1_k, cmp_w1_v]).astype(BF16)
    w2 = jnp.stack([cmp_w2_k, cmp_w2_v]).astype(BF16)
    cmp_end = CMP_STRIDE * jnp.arange(n_chunks, dtype=jnp.int32) + CMP_BLOCK - 1
    ccos, csin = _rope_tables(cmp_end)
    cmp_cos = jnp.stack([ccos, jnp.ones_like(ccos)])
    cmp_sin = jnp.stack([csin, jnp.zeros_like(csin)])
    cmp_kv = _compress(kv_cmp.reshape(bsz, seq, _N_CMP_COLS), pe, w1, w2, cmp_cos, cmp_sin)

    n_sel = seq // SEL_BLOCK
    n_pick = min(SEL_TOPK, n_sel)
    ci = CMP_STRIDE * jnp.arange(n_chunks)
    sj = SEL_BLOCK * jnp.arange(LANES)
    overlap = jnp.clip(jnp.minimum(ci[:, None] + CMP_BLOCK, sj[None, :] + SEL_BLOCK)
                       - jnp.maximum(ci[:, None], sj[None, :]), 0, None)
    selmap_t = (overlap.astype(F32) / CMP_BLOCK).astype(BF16).T
    o_nsa = _nsa(q, cmp_kv, kv_rest.reshape(bsz, seq, _N_REST_COLS),
                 nsa_gates.reshape(bsz, seq, _GATE_COLS), selmap_t, n_sel=n_sel, n_pick=n_pick)

    uv = _matmul(x_bf, w_uv, tm=1024, tn=1024, out_dtype=F32, act="gelu")
    o_sgu = _sgu(uv, sgu_ln_g.reshape(1, -1), sgu_ln_b.reshape(1, -1), sgu_w_s, sgu_b_s.T)

    merged = _merge(x_bf, o_nsa.reshape(n_tok, Q_WIDTH), o_sgu, w_merge,
                    w_branch_a.astype(BF16), w_branch_b.astype(BF16))
    x1, x1_bf = _proj_ln(merged, w_o.astype(BF16), xf, ln1_g.reshape(1, -1), ln1_b.reshape(1, -1))

    mem_len = mem.shape[1]
    w_mem = jnp.concatenate([w_xk, w_xv], axis=1).astype(BF16)
    mem_kv = _matmul(mem.reshape(bsz * mem_len, d).astype(BF16), w_mem, tm=256, tn=512, out_dtype=BF16)
    x2 = _xattn(x1_bf.reshape(bsz, seq, d), x1.reshape(bsz, seq, d), w_xq.astype(BF16),
                mem_kv.reshape(bsz, mem_len, 2 * MEM_WIDTH), w_xo.astype(BF16),
                ln2_g.reshape(1, -1), ln2_b.reshape(1, -1)).reshape(n_tok, d)

    w_r = jnp.concatenate([w_router_grp, w_router_exp], axis=1)
    w_r = jnp.pad(w_r, ((0, 0), (0, LANES - w_r.shape[1])))
    w_r_hi = w_r.astype(BF16)
    w_r_lo = (w_r - w_r_hi.astype(F32)).astype(BF16)
    b_r = jnp.concatenate([b_router_grp, b_router_exp])
    b_r = jnp.pad(b_r, (0, LANES - b_r.shape[0])).reshape(1, LANES)
    gates, experts, ranks, seen = _router(x2, w_r_hi, w_r_lo, b_r)

    nk = n_tok * EXPERT_TOPK
    counts = seen[0, N_GROUPS:N_GROUPS + N_EXPERTS].astype(jnp.int32)
    pcounts = ((counts + MOE_BLOCK - 1) // MOE_BLOCK) * MOE_BLOCK
    pend = jnp.cumsum(pcounts)
    pstart = pend - pcounts
    e_tok = experts[:, :EXPERT_TOPK]
    seg_start = jnp.sum(jnp.where(e_tok[..., None] == jnp.arange(N_EXPERTS), pstart, 0), -1)
    dest_of = seg_start + ranks[:, :EXPERT_TOPK]
    n_blocks = -(-nk // MOE_BLOCK) + N_EXPERTS
    p_rows = n_blocks * MOE_BLOCK
    row_ids = jnp.arange(p_rows, dtype=jnp.int32)
    row_asg = jnp.full((p_rows,), -1, jnp.int32).at[dest_of.reshape(nk)].set(
        jnp.arange(nk, dtype=jnp.int32), unique_indices=True)
    is_real = row_asg >= 0
    row_tok = jnp.where(is_real, row_asg // EXPERT_TOPK, row_ids % n_tok)
    spare = EXPERT_TOPK * n_tok + ((row_ids // MOE_BLOCK) % 2) * MOE_BLOCK + row_ids % MOE_BLOCK
    row_dst = jnp.where(is_real, (row_asg % EXPERT_TOPK) * n_tok + row_asg // EXPERT_TOPK, spare)
    blk_start = jnp.arange(n_blocks) * MOE_BLOCK
    blk_on = (blk_start < pend[-1]).astype(jnp.int32)
    blk_probe = jnp.minimum(blk_start, pend[-1] - 1)
    blk_exp = jnp.minimum(jnp.sum((pend[None, :] <= blk_probe[:, None]).astype(jnp.int32), -1),
                          N_EXPERTS - 1).astype(jnp.int32)

    y_tok = _expert_ffn(blk_exp, blk_on, row_tok, row_dst, x2, w_exp_gate, w_exp_up, w_exp_down)
    out = _combine_ln(x2, y_tok, gates, ln3_g.reshape(1, -1), ln3_b.reshape(1, -1))
    return out.reshape(bsz, seq, d)


def kernel(x, mem, w_in, cmp_pe_k, cmp_w1_k, cmp_w2_k, cmp_pe_v, cmp_w1_v, cmp_w2_v, sgu_ln_g, sgu_ln_b, sgu_w_s, sgu_b_s, w_branch_a, w_branch_b, w_o, ln1_g, ln1_b, w_xq, w_xk, w_xv, w_xo, ln2_g, ln2_b, w_router_grp, b_router_grp, w_router_exp, b_router_exp, w_exp_gate, w_exp_up, w_exp_down, ln3_g, ln3_b):
    params = (w_in, cmp_pe_k, cmp_w1_k, cmp_w2_k, cmp_pe_v, cmp_w1_v, cmp_w2_v,
              sgu_ln_g, sgu_ln_b, sgu_w_s, sgu_b_s, w_branch_a, w_branch_b, w_o, ln1_g, ln1_b,
              w_xq, w_xk, w_xv, w_xo, ln2_g, ln2_b,
              w_router_grp, b_router_grp, w_router_exp, b_router_exp,
              w_exp_gate, w_exp_up, w_exp_down, ln3_g, ln3_b)
    h = x
    for l in range(DEPTH):
        h = _layer(h, mem, *[p[l] for p in params])
    return h
```

```python
import functools
import math

import jax
import jax.numpy as jnp
from jax import lax
from jax.experimental import pallas as pl
from jax.experimental.pallas import tpu as pltpu

D_MODEL = 2048
HEAD_DIM = 128
NSA_HEADS = 16
NSA_KV_GROUPS = 2
NSA_HPG = NSA_HEADS // NSA_KV_GROUPS
CMP_BLOCK = 32
CMP_STRIDE = 16
CMP_HIDDEN = 256
SEL_BLOCK = 64
SEL_TOPK = 16
WINDOW = 512
Q_BLOCK = 128
SGU_WIDTH = 2048
SGU_GROUPS = 8
SGU_CHUNK = 128
MEM_HEADS = 4
N_GROUPS = 4
EXPERTS_PER_GROUP = 16
N_EXPERTS = N_GROUPS * EXPERTS_PER_GROUP
EXPERT_TOPK = 2
EXPERT_FF = 512
MOE_BLOCK = 128
ROPE_THETA = 10000.0
LN_EPS = 1e-5
SEL_BIG = 1e9
DEPTH = 1
DN_ALPHA = (2.0 * DEPTH) ** 0.25

Q_WIDTH = NSA_HEADS * HEAD_DIM
KV_WIDTH = 3 * 2 * NSA_KV_GROUPS * HEAD_DIM
NSA_GATE_WIDTH = 3 * NSA_HEADS
MEM_WIDTH = MEM_HEADS * HEAD_DIM

LANES = 128
VMEM_LIMIT = 56 * 1024 * 1024
MASKED = -1e30
M_INIT = -1e20
BF16 = jnp.bfloat16
F32 = jnp.float32


def _params(*sem):
    return pltpu.CompilerParams(dimension_semantics=sem, vmem_limit_bytes=VMEM_LIMIT)


def _dot(a, b):
    return jnp.dot(a, b, preferred_element_type=F32)


def _dot_nt(a, b):
    return lax.dot_general(a, b, (((1,), (1,)), ((), ())), preferred_element_type=F32)


def _gelu(x):
    return 0.5 * x * (1.0 + jnp.tanh(math.sqrt(2.0 / math.pi) * (x + 0.044715 * (x * x * x))))


def _sigmoid(x):
    return 1.0 / (1.0 + jnp.exp(-x))


def _layer_norm(x, g, b):
    mu = jnp.mean(x, -1, keepdims=True)
    xc = x - mu
    var = jnp.mean(xc * xc, -1, keepdims=True)
    return xc * lax.rsqrt(var + LN_EPS) * g + b


def _rope(t, cos_full, sin_signed):
    return t * cos_full + pltpu.roll(t, HEAD_DIM // 2, axis=1) * sin_signed


def _rope_tables(pos):
    inv = ROPE_THETA ** (-jnp.arange(0, HEAD_DIM, 2, dtype=F32) / HEAD_DIM)
    ang = pos.astype(F32)[:, None] * inv[None, :]
    c, s = jnp.cos(ang), jnp.sin(ang)
    return jnp.concatenate([c, c], -1), jnp.concatenate([-s, s], -1)


MXU_COLS = 256


def _mm_kernel(a_ref, b_ref, o_ref, *, act):
    a = a_ref[...]
    for j in range(o_ref.shape[1] // MXU_COLS):
        cols = slice(j * MXU_COLS, (j + 1) * MXU_COLS)
        y = _dot(a, b_ref[:, cols])
        if act == "gelu":
            y = _gelu(y)
        o_ref[:, cols] = y.astype(o_ref.dtype)


def _matmul(a, b, *, tm, tn, out_dtype, act=None):
    m, k = a.shape
    n = b.shape[1]
    return pl.pallas_call(
        functools.partial(_mm_kernel, act=act),
        grid=(m // tm, n // tn),
        in_specs=[pl.BlockSpec((tm, k), lambda i, j: (i, 0)),
                  pl.BlockSpec((k, tn), lambda i, j: (0, j))],
        out_specs=pl.BlockSpec((tm, tn), lambda i, j: (i, j)),
        out_shape=jax.ShapeDtypeStruct((m, n), out_dtype),
        compiler_params=_params("parallel", "parallel"),
        name="matmul_" + (act or "plain"),
    )(a, b)


def _qproj_kernel(a_ref, b_ref, cos_ref, sin_ref, o_ref):
    a = a_ref[...]
    cos, sin = cos_ref[...], sin_ref[...]
    scale = HEAD_DIM ** -0.5 * math.log2(math.e)
    for j in range(o_ref.shape[1] // MXU_COLS):
        y = _dot(a, b_ref[:, j * MXU_COLS:(j + 1) * MXU_COLS])
        for h in range(MXU_COLS // HEAD_DIM):
            sl = slice(j * MXU_COLS + h * HEAD_DIM, j * MXU_COLS + (h + 1) * HEAD_DIM)
            o_ref[:, sl] = (_rope(y[:, h * HEAD_DIM:(h + 1) * HEAD_DIM], cos, sin) * scale).astype(o_ref.dtype)


def _q_proj(x_bf, w_q, cos, sin, seq, *, tm=1024, tn=1024):
    m, k = x_bf.shape
    n = w_q.shape[1]
    per_seq = seq // tm
    return pl.pallas_call(
        _qproj_kernel,
        grid=(m // tm, n // tn),
        in_specs=[pl.BlockSpec((tm, k), lambda i, j: (i, 0)),
                  pl.BlockSpec((k, tn), lambda i, j: (0, j)),
                  pl.BlockSpec((tm, HEAD_DIM), lambda i, j: (i % per_seq, 0)),
                  pl.BlockSpec((tm, HEAD_DIM), lambda i, j: (i % per_seq, 0))],
        out_specs=pl.BlockSpec((tm, tn), lambda i, j: (i, j)),
        out_shape=jax.ShapeDtypeStruct((m, n), BF16),
        compiler_params=_params("parallel", "parallel"),
        name="q_proj",
    )(x_bf, w_q, cos, sin)


_N_CMP_COLS = 2 * NSA_KV_GROUPS * HEAD_DIM
_N_REST_CHUNKS = (KV_WIDTH - _N_CMP_COLS) // HEAD_DIM
_N_REST_COLS = (_N_REST_CHUNKS + NSA_KV_GROUPS) * HEAD_DIM
_GATE_COLS = NSA_KV_GROUPS * LANES
_SEL_SHIFT = SEL_BLOCK.bit_length() - 1
assert 1 << _SEL_SHIFT == SEL_BLOCK


def _kvproj_kernel(a_ref, b_ref, cos_ref, sin_ref, cmp_ref, rest_ref, gate_ref, *, per_seq):
    g = NSA_KV_GROUPS
    y = _dot(a_ref[...], b_ref[...])
    cos, sin = cos_ref[...], sin_ref[...]
    cmp_ref[...] = y[:, :_N_CMP_COLS]
    tm = y.shape[0]
    pos = (pl.program_id(0) % per_seq) * tm + lax.broadcasted_iota(jnp.int32, (tm, LANES), 0)
    lane = lax.broadcasted_iota(jnp.int32, (tm, LANES), 1)
    blk_onehot = jnp.where(lane == (pos >> _SEL_SHIFT), 1.0, 0.0).astype(rest_ref.dtype)
    for c in range(_N_REST_CHUNKS):
        t = y[:, _N_CMP_COLS + c * HEAD_DIM:_N_CMP_COLS + (c + 1) * HEAD_DIM]
        if (c % (2 * g)) < g:
            t = _rope(t, cos, sin)
        dst = 2 * c if c < g else g + c
        rest_ref[:, dst * HEAD_DIM:(dst + 1) * HEAD_DIM] = t.astype(rest_ref.dtype)
        if c < g:
            rest_ref[:, (dst + 1) * HEAD_DIM:(dst + 2) * HEAD_DIM] = blk_onehot
    gate_ref[...] = _sigmoid(y[:, KV_WIDTH:])


def _kv_proj(x_bf, w_kvg, cos, sin, seq, *, tm=512):
    m, k = x_bf.shape
    n = w_kvg.shape[1]
    per_seq = seq // tm
    return pl.pallas_call(
        functools.partial(_kvproj_kernel, per_seq=per_seq),
        grid=(m // tm,),
        in_specs=[pl.BlockSpec((tm, k), lambda i: (i, 0)),
                  pl.BlockSpec((k, n), lambda i: (0, 0)),
                  pl.BlockSpec((tm, HEAD_DIM), lambda i: (i % per_seq, 0)),
                  pl.BlockSpec((tm, HEAD_DIM), lambda i: (i % per_seq, 0))],
        out_specs=[pl.BlockSpec((tm, _N_CMP_COLS), lambda i: (i, 0)),
                   pl.BlockSpec((tm, _N_REST_COLS), lambda i: (i, 0)),
                   pl.BlockSpec((tm, _GATE_COLS), lambda i: (i, 0))],
        out_shape=[jax.ShapeDtypeStruct((m, _N_CMP_COLS), F32),
                   jax.ShapeDtypeStruct((m, _N_REST_COLS), BF16),
                   jax.ShapeDtypeStruct((m, _GATE_COLS), F32)],
        compiler_params=_params("parallel"),
        name="kv_gate_proj",
    )(x_bf, w_kvg, cos, sin)


def _compress_kernel(kc_ref, pe_ref, w1_ref, w2_ref, cos_ref, sin_ref, o_ref):
    n_chunks = o_ref.shape[2]
    top = jnp.zeros((n_chunks, CMP_HIDDEN), F32)
    bot = jnp.zeros((n_chunks, CMP_HIDDEN), F32)
    for p in range(CMP_STRIDE):
        xp = kc_ref[0, pl.ds(p, n_chunks, stride=CMP_STRIDE), :]
        q = CMP_STRIDE + p
        top += _dot((xp + pe_ref[0, p:p + 1, :]).astype(BF16), w1_ref[0, p * HEAD_DIM:(p + 1) * HEAD_DIM, :])
        bot += _dot((xp + pe_ref[0, q:q + 1, :]).astype(BF16), w1_ref[0, q * HEAD_DIM:(q + 1) * HEAD_DIM, :])
    h = top + pltpu.roll(bot, n_chunks - 1, axis=0)
    y = _dot(_gelu(h).astype(BF16), w2_ref[0])
    o_ref[0, 0] = _rope(y, cos_ref[0], sin_ref[0]).astype(o_ref.dtype)


def _compress(kv_cmp, pe, w1, w2, cos, sin):
    bsz, seq, width = kv_cmp.shape
    four = width // HEAD_DIM
    n_chunks = seq // CMP_STRIDE
    g = NSA_KV_GROUPS
    return pl.pallas_call(
        _compress_kernel,
        grid=(four, bsz),
        in_specs=[pl.BlockSpec((1, seq, HEAD_DIM), lambda a, b: (b, 0, a)),
                  pl.BlockSpec((1, CMP_BLOCK, HEAD_DIM), lambda a, b: (a // g, 0, 0)),
                  pl.BlockSpec((1, CMP_BLOCK * HEAD_DIM, CMP_HIDDEN), lambda a, b: (a // g, 0, 0)),
                  pl.BlockSpec((1, CMP_HIDDEN, HEAD_DIM), lambda a, b: (a // g, 0, 0)),
                  pl.BlockSpec((1, n_chunks, HEAD_DIM), lambda a, b: (a // g, 0, 0)),
                  pl.BlockSpec((1, n_chunks, HEAD_DIM), lambda a, b: (a // g, 0, 0))],
        out_specs=pl.BlockSpec((1, 1, n_chunks, HEAD_DIM), lambda a, b: (a, b, 0, 0)),
        out_shape=jax.ShapeDtypeStruct((four, bsz, n_chunks, HEAD_DIM), BF16),
        compiler_params=_params("parallel", "parallel"),
        name="compress",
    )(kv_cmp, pe, w1, w2, cos, sin)


SEL_TILE = 512
NSA_TQ = 256
SUBLANES = 8


def _softmax_parts(s3, bias):
    s3 = s3 + bias[None]
    m = jnp.max(s3, -1, keepdims=True)
    m = jnp.where(m > 0.5 * MASKED, m, 0.0)
    p = jnp.exp2(s3 - m)
    return p, 1.0 / jnp.maximum(jnp.sum(p, -1, keepdims=True), 1e-30)


def _topk_unselected(score_t, n_sel, n_pick):
    tq = score_t.shape[1]
    n_grp = n_sel // SUBLANES
    grp = [score_t[g * SUBLANES:(g + 1) * SUBLANES] for g in range(n_grp)]
    row_in_grp = lax.broadcasted_iota(jnp.int32, (SUBLANES, tq), 0)
    later = [jnp.where(row_in_grp > li, 1.0, 0.0) for li in range(SUBLANES)]
    beaten = [jnp.zeros((SUBLANES, tq), F32) for _ in range(n_grp)]
    for i in range(n_sel):
        gi, li = divmod(i, SUBLANES)
        row = grp[gi][li:li + 1, :]
        for g in range(n_grp):
            if g > gi:
                beaten[g] = beaten[g] + jnp.where(row >= grp[g], 1.0, 0.0)
            elif g < gi:
                beaten[g] = beaten[g] + jnp.where(row > grp[g], 1.0, 0.0)
            else:
                beaten[g] = (beaten[g] + jnp.where(row > grp[g], 1.0, 0.0)
                             + jnp.where(row == grp[g], later[li], 0.0))
    return jnp.concatenate([jnp.where(b < n_pick, 0.0, MASKED) for b in beaten], axis=0)


def _nsa_kernel(q_ref, kc_ref, vc_ref, ks_ref, vs_ref, kw_ref, vw_ref, gate_ref, selmap_t_ref, o_ref,
                *, n_sel, n_pick):
    hpg = NSA_HPG
    tq = NSA_TQ
    c = pl.program_id(2)
    q_t = q_ref[0]
    q8 = jnp.concatenate([q_t[:, n * HEAD_DIM:(n + 1) * HEAD_DIM] for n in range(hpg)], axis=0)
    t_pos = c * tq + lax.broadcasted_iota(jnp.int32, (tq, 1), 0)
    t_lane = c * tq + lax.broadcasted_iota(jnp.int32, (1, tq), 1)

    kc = kc_ref[0, 0]
    n_cmp_pad = kc.shape[0]
    k_idx = lax.broadcasted_iota(jnp.int32, (1, n_cmp_pad), 1)
    vis_c = (CMP_STRIDE * k_idx + (CMP_BLOCK - 1) <= t_pos) & (k_idx < n_cmp_pad - 1)
    p_c, r_c = _softmax_parts(_dot_nt(q8, kc).reshape(hpg, tq, n_cmp_pad), jnp.where(vis_c, 0.0, MASKED))
    o_c = _dot(p_c.reshape(hpg * tq, n_cmp_pad).astype(BF16), vc_ref[0, 0]).reshape(hpg, tq, HEAD_DIM) * r_c

    p_sum = jnp.sum(p_c * r_c, axis=0)
    p_hi = p_sum.astype(BF16)
    p_lo = (p_sum - p_hi.astype(F32)).astype(BF16)
    selmap_t = selmap_t_ref[...]
    imp_t = _dot_nt(selmap_t, p_hi) + _dot_nt(selmap_t, p_lo)
    blk = lax.broadcasted_iota(jnp.int32, (LANES, tq), 0)
    cur = t_lane >> _SEL_SHIFT
    forced = (blk == 0) | (blk == cur) | (blk == cur - 1)
    score_t = jnp.where(forced, SEL_BIG, jnp.where(blk <= cur, imp_t, -SEL_BIG))
    unsel_t = _topk_unselected(score_t, n_sel, n_pick)
    unsel_t = jnp.concatenate([unsel_t, jnp.zeros((LANES - n_sel, tq), F32)], axis=0)
    unsel = unsel_t.T.astype(BF16)
    q_aug = jnp.concatenate([q8, jnp.concatenate([unsel] * hpg, axis=0)], axis=1)

    def sel_tile(base, carry, bias):
        m_old, l_old, acc = carry
        k = ks_ref[0, pl.ds(base, SEL_TILE), :]
        v = vs_ref[0, pl.ds(base, SEL_TILE), :]
        s = _dot_nt(q_aug, k).reshape(hpg, tq, SEL_TILE)
        if bias is not None:
            s = s + bias[None]
        m_new = jnp.maximum(m_old, jnp.max(s, -1, keepdims=True))
        alpha = jnp.exp2(m_old - m_new)
        p = jnp.exp2(s - m_new)
        l_new = alpha * l_old + jnp.sum(p, -1, keepdims=True)
        pv = _dot(p.reshape(hpg * tq, SEL_TILE).astype(BF16), v).reshape(hpg, tq, HEAD_DIM)
        return m_new, l_new, alpha * acc + pv

    n_full = c // (SEL_TILE // tq)
    init = (jnp.full((hpg, tq, 1), M_INIT, F32), jnp.zeros((hpg, tq, 1), F32),
            jnp.zeros((hpg, tq, HEAD_DIM), F32))
    carry = lax.fori_loop(
        0, n_full, lambda kt, cr: sel_tile(pl.multiple_of(kt * SEL_TILE, SEL_TILE), cr, None), init)
    base_last = pl.multiple_of(n_full * SEL_TILE, SEL_TILE)
    kp_last = base_last + lax.broadcasted_iota(jnp.int32, (1, SEL_TILE), 1)
    _, l_s, acc_s = sel_tile(base_last, carry, jnp.where(kp_last <= t_pos, 0.0, MASKED))
    o_s = acc_s * (1.0 / jnp.maximum(l_s, 1e-30))

    span = WINDOW + tq
    start = pl.multiple_of(jnp.maximum(c * tq - WINDOW, 0), tq)
    kw = kw_ref[0, pl.ds(start, span), :]
    vw = vw_ref[0, pl.ds(start, span), :]
    kp = start + lax.broadcasted_iota(jnp.int32, (1, span), 1)
    vis_w = (kp <= t_pos) & (t_pos - kp < WINDOW)
    p_w, r_w = _softmax_parts(_dot_nt(q8, kw).reshape(hpg, tq, span), jnp.where(vis_w, 0.0, MASKED))
    o_w = _dot(p_w.reshape(hpg * tq, span).astype(BF16), vw).reshape(hpg, tq, HEAD_DIM) * r_w

    gates = gate_ref[0]
    for n in range(hpg):
        out = (gates[:, n:n + 1] * o_c[n]
               + gates[:, hpg + n:hpg + n + 1] * o_s[n]
               + gates[:, 2 * hpg + n:2 * hpg + n + 1] * o_w[n])
        o_ref[0, :, n * HEAD_DIM:(n + 1) * HEAD_DIM] = out.astype(o_ref.dtype)


def _nsa(q, cmp_kv, rest, gates, selmap_t, *, n_sel, n_pick):
    bsz, seq, _ = q.shape
    g = NSA_KV_GROUPS
    n_cmp_pad = cmp_kv.shape[2]
    gw = NSA_HPG * HEAD_DIM
    assert n_sel % SUBLANES == 0 and n_sel <= LANES and seq >= WINDOW + NSA_TQ
    assert SEL_TILE % NSA_TQ == 0 and seq % SEL_TILE == 0
    kv_spec = lambda off: pl.BlockSpec((1, seq, HEAD_DIM), lambda b, gi, c: (b, 0, off + gi))
    return pl.pallas_call(
        functools.partial(_nsa_kernel, n_sel=n_sel, n_pick=n_pick),
        grid=(bsz, g, seq // NSA_TQ),
        in_specs=[pl.BlockSpec((1, NSA_TQ, gw), lambda b, gi, c: (b, c, gi)),
                  pl.BlockSpec((1, 1, n_cmp_pad, HEAD_DIM), lambda b, gi, c: (gi, b, 0, 0)),
                  pl.BlockSpec((1, 1, n_cmp_pad, HEAD_DIM), lambda b, gi, c: (g + gi, b, 0, 0)),
                  pl.BlockSpec((1, seq, 2 * HEAD_DIM), lambda b, gi, c: (b, 0, gi)),
                  kv_spec(2 * g), kv_spec(3 * g), kv_spec(4 * g),
                  pl.BlockSpec((1, NSA_TQ, LANES), lambda b, gi, c: (b, c, gi)),
                  pl.BlockSpec(selmap_t.shape, lambda b, gi, c: (0, 0))],
        out_specs=pl.BlockSpec((1, NSA_TQ, gw), lambda b, gi, c: (b, c, gi)),
        out_shape=jax.ShapeDtypeStruct(q.shape, BF16),
        compiler_params=_params("parallel", "parallel", "arbitrary"),
        name="nsa_attention",
    )(q, cmp_kv, cmp_kv, rest, rest, rest, rest, gates, selmap_t)


def _sgu_kernel(u_ref, v_ref, g_ref, b_ref, ws_ref, bs_ref, o_ref, *, chunks):
    gd = SGU_WIDTH // SGU_GROUPS
    row = lax.broadcasted_iota(jnp.int32, (SGU_CHUNK, SGU_CHUNK), 0)
    col = lax.broadcasted_iota(jnp.int32, (SGU_CHUNK, SGU_CHUNK), 1)
    causal = col <= row
    bs = bs_ref[...]
    for ch in range(chunks):
        rows = slice(ch * SGU_CHUNK, (ch + 1) * SGU_CHUNK)
        vn = _layer_norm(v_ref[rows, :], g_ref[...], b_ref[...]).astype(BF16)
        for gi in range(SGU_GROUPS):
            cols = slice(gi * gd, (gi + 1) * gd)
            w = jnp.where(causal, ws_ref[gi], 0.0).astype(BF16)
            z = _dot(w, vn[:, cols]) + bs[:, gi:gi + 1]
            o_ref[rows, cols] = (u_ref[rows, cols] * z).astype(o_ref.dtype)


def _sgu(uv, ln_g, ln_b, w_s, b_s_t, *, chunks=2):
    m = uv.shape[0]
    tm = chunks * SGU_CHUNK
    return pl.pallas_call(
        functools.partial(_sgu_kernel, chunks=chunks),
        grid=(m // tm,),
        in_specs=[pl.BlockSpec((tm, SGU_WIDTH), lambda i: (i, 0)),
                  pl.BlockSpec((tm, SGU_WIDTH), lambda i: (i, 1)),
                  pl.BlockSpec((1, SGU_WIDTH), lambda i: (0, 0)),
                  pl.BlockSpec((1, SGU_WIDTH), lambda i: (0, 0)),
                  pl.BlockSpec(w_s.shape, lambda i: (0, 0, 0)),
                  pl.BlockSpec(b_s_t.shape, lambda i: (0, 0))],
        out_specs=pl.BlockSpec((tm, SGU_WIDTH), lambda i: (i, 0)),
        out_shape=jax.ShapeDtypeStruct((m, SGU_WIDTH), BF16),
        compiler_params=_params("parallel"),
        name="sgu",
    )(uv, uv, ln_g, ln_b, w_s, b_s_t)


def _merge_kernel(x_ref, a_ref, s_ref, wga_ref, wgb_ref, wa_ref, wb_ref, o_ref):
    x = x_ref[...]
    ga = _sigmoid(_dot(x, wga_ref[...]))
    gb = _sigmoid(_dot(x, wgb_ref[...]))
    o_ref[...] = (ga * _dot(a_ref[...], wa_ref[...]) + gb * _dot(s_ref[...], wb_ref[...])).astype(o_ref.dtype)


def _merge(x_bf, o_nsa, o_sgu, w_merge, w_a, w_b, *, tm=1024, tn=256):
    m, k = x_bf.shape
    n = w_a.shape[1]
    nb = n // tn
    row = pl.BlockSpec((tm, k), lambda i, j: (i, 0))
    col = pl.BlockSpec((k, tn), lambda i, j: (0, j))
    return pl.pallas_call(
        _merge_kernel,
        grid=(m // tm, nb),
        in_specs=[row, row, row, col, pl.BlockSpec((k, tn), lambda i, j: (0, nb + j)), col, col],
        out_specs=pl.BlockSpec((tm, tn), lambda i, j: (i, j)),
        out_shape=jax.ShapeDtypeStruct((m, n), BF16),
        compiler_params=_params("parallel", "parallel"),
        name="merge",
    )(x_bf, o_nsa, o_sgu, w_merge, w_merge, w_a, w_b)


def _proj_ln_kernel(a_ref, w_ref, x_ref, g_ref, b_ref, o_ref, obf_ref):
    y = DN_ALPHA * x_ref[...] + _dot(a_ref[...], w_ref[...])
    out = _layer_norm(y, g_ref[...], b_ref[...])
    o_ref[...] = out
    obf_ref[...] = out.astype(BF16)


def _proj_ln(a_bf, w, x, ln_g, ln_b, *, tm=512):
    m, k = a_bf.shape
    n = w.shape[1]
    vec = pl.BlockSpec((1, n), lambda i: (0, 0))
    return pl.pallas_call(
        _proj_ln_kernel,
        grid=(m // tm,),
        in_specs=[pl.BlockSpec((tm, k), lambda i: (i, 0)),
                  pl.BlockSpec((k, n), lambda i: (0, 0)),
                  pl.BlockSpec((tm, n), lambda i: (i, 0)), vec, vec],
        out_specs=[pl.BlockSpec((tm, n), lambda i: (i, 0)), pl.BlockSpec((tm, n), lambda i: (i, 0))],
        out_shape=[jax.ShapeDtypeStruct((m, n), F32), jax.ShapeDtypeStruct((m, n), BF16)],
        compiler_params=_params("parallel"),
        name="proj_ln",
    )(a_bf, w, x, ln_g, ln_b)


def _xattn_kernel(xbf_ref, x_ref, wq_ref, kv_ref, wo_ref, g_ref, b_ref, o_ref):
    q = (_dot(xbf_ref[0], wq_ref[...]) * HEAD_DIM ** -0.5).astype(BF16)
    kv = kv_ref[0]
    heads = []
    for h in range(MEM_HEADS):
        cols = slice(h * HEAD_DIM, (h + 1) * HEAD_DIM)
        s = _dot_nt(q[:, cols], kv[:, cols])
        p = jnp.exp(s - jnp.max(s, -1, keepdims=True))
        p = p / jnp.sum(p, -1, keepdims=True)
        heads.append(_dot(p.astype(BF16), kv[:, MEM_WIDTH + h * HEAD_DIM:MEM_WIDTH + (h + 1) * HEAD_DIM]))
    o = jnp.concatenate(heads, axis=1).astype(BF16)
    y = DN_ALPHA * x_ref[0] + _dot(o, wo_ref[...])
    o_ref[0] = _layer_norm(y, g_ref[...], b_ref[...])


def _xattn(x_bf, x, w_xq, mem_kv, w_xo, ln_g, ln_b, *, tm=512):
    bsz, seq, d = x.shape
    mem_len = mem_kv.shape[1]
    vec = pl.BlockSpec((1, d), lambda b, i: (0, 0))
    return pl.pallas_call(
        _xattn_kernel,
        grid=(bsz, seq // tm),
        in_specs=[pl.BlockSpec((1, tm, d), lambda b, i: (b, i, 0)),
                  pl.BlockSpec((1, tm, d), lambda b, i: (b, i, 0)),
                  pl.BlockSpec(w_xq.shape, lambda b, i: (0, 0)),
                  pl.BlockSpec((1, mem_len, 2 * MEM_WIDTH), lambda b, i: (b, 0, 0)),
                  pl.BlockSpec(w_xo.shape, lambda b, i: (0, 0)), vec, vec],
        out_specs=pl.BlockSpec((1, tm, d), lambda b, i: (b, i, 0)),
        out_shape=jax.ShapeDtypeStruct(x.shape, F32),
        compiler_params=_params("parallel", "parallel"),
        name="xattn_ln",
    )(x_bf, x, w_xq, mem_kv, w_xo, ln_g, ln_b)


def _router_kernel(x_ref, whi_ref, wlo_ref, b_ref, gate_ref, exp_ref, rank_ref, cnt_ref, seen_ref):
    @pl.when(pl.program_id(0) == 0)
    def _():
        seen_ref[...] = jnp.zeros_like(seen_ref)

    x = x_ref[...]
    x_hi = x.astype(BF16)
    x_lo = (x - x_hi.astype(F32)).astype(BF16)
    whi = whi_ref[...]
    logits = _dot(x_hi, whi) + _dot(x_hi, wlo_ref[...]) + _dot(x_lo, whi) + b_ref[...]
    rows = logits.shape[0]
    lane = lax.broadcasted_iota(jnp.int32, (rows, LANES), 1)
    lane_f = lane.astype(F32)
    is_grp = lane < N_GROUPS
    lg = jnp.where(is_grp, logits, MASKED)
    eg = jnp.where(is_grp, jnp.exp(lg - jnp.max(lg, -1, keepdims=True)), 0.0)
    p_grp = eg / jnp.sum(eg, -1, keepdims=True)
    g_grp = jnp.max(p_grp, -1, keepdims=True)
    grp = jnp.min(jnp.where((p_grp == g_grp) & is_grp, lane_f, 1e9), -1, keepdims=True).astype(jnp.int32)
    lo = N_GROUPS + grp * EXPERTS_PER_GROUP
    in_grp = (lane >= lo) & (lane < lo + EXPERTS_PER_GROUP)
    le = jnp.where(in_grp, logits, MASKED)
    ee = jnp.where(in_grp, jnp.exp(le - jnp.max(le, -1, keepdims=True)), 0.0)
    p_exp = jnp.where(in_grp, ee / jnp.sum(ee, -1, keepdims=True), -1.0)
    v1 = jnp.max(p_exp, -1, keepdims=True)
    i1 = jnp.min(jnp.where(p_exp == v1, lane_f, 1e9), -1, keepdims=True)
    p_rest = jnp.where(lane_f == i1, -1.0, p_exp)
    v2 = jnp.max(p_rest, -1, keepdims=True)
    i2 = jnp.min(jnp.where(p_rest == v2, lane_f, 1e9), -1, keepdims=True)
    denom = v1 + v2
    gate_ref[...] = jnp.where(lane == 0, g_grp * v1 / denom, jnp.where(lane == 1, g_grp * v2 / denom, 0.0))
    exp_ref[...] = jnp.where(lane == 0, i1, jnp.where(lane == 1, i2, float(N_GROUPS))).astype(jnp.int32) - N_GROUPS

    onehot = jnp.where((lane_f == i1) | (lane_f == i2), 1.0, 0.0)
    tok_r = lax.broadcasted_iota(jnp.int32, (rows, rows), 0)
    tok_c = lax.broadcasted_iota(jnp.int32, (rows, rows), 1)
    earlier = jnp.where(tok_c < tok_r, 1.0, 0.0).astype(BF16)
    before = _dot(earlier, onehot.astype(BF16)) + seen_ref[...]
    r1 = jnp.sum(jnp.where(lane_f == i1, before, 0.0), -1, keepdims=True)
    r2 = jnp.sum(jnp.where(lane_f == i2, before, 0.0), -1, keepdims=True)
    rank_ref[...] = jnp.where(lane == 0, r1, jnp.where(lane == 1, r2, 0.0)).astype(jnp.int32)
    seen_ref[...] += jnp.sum(onehot, axis=0, keepdims=True)
    cnt_ref[...] = seen_ref[...]


def _router(x, w_hi, w_lo, bias, *, tm=512):
    m, d = x.shape
    wspec = pl.BlockSpec((d, LANES), lambda i: (0, 0))
    vec = pl.BlockSpec((1, LANES), lambda i: (0, 0))
    tile = pl.BlockSpec((tm, LANES), lambda i: (i, 0))
    return pl.pallas_call(
        _router_kernel,
        grid=(m // tm,),
        in_specs=[pl.BlockSpec((tm, d), lambda i: (i, 0)), wspec, wspec, vec],
        out_specs=[tile, tile, tile, vec],
        out_shape=[jax.ShapeDtypeStruct((m, LANES), F32), jax.ShapeDtypeStruct((m, LANES), jnp.int32),
                   jax.ShapeDtypeStruct((m, LANES), jnp.int32), jax.ShapeDtypeStruct((1, LANES), F32)],
        scratch_shapes=[pltpu.VMEM((1, LANES), F32)],
        compiler_params=_params("arbitrary"),
        name="router",
    )(x, w_hi, w_lo, bias)


GATHER_UNROLL = 8


def _ffn_kernel(exp_ref, on_ref, first_ref, slot_ref, next_ref, tok_ref, dst_ref, x_hbm, wg_hbm, wu_hbm, wd_hbm,
                y_hbm, wg_buf, wu_buf, wd_buf, h_buf, y_buf, sem, row_sem, out_sem):
    i = pl.program_id(0)
    n_blk = pl.num_programs(0)
    slot = slot_ref[i]
    rows = h_buf.shape[1]

    def out_copy(blk, r):
        dst = dst_ref[blk * rows + r]
        return pltpu.make_async_copy(y_buf.at[blk % 2, pl.ds(r, 1), :], y_hbm.at[pl.ds(dst, 1), :],
                                     out_sem.at[blk % 2])

    def weight_copies(e, s):
        return (pltpu.make_async_copy(wg_hbm.at[e], wg_buf.at[s], sem.at[s, 0]),
                pltpu.make_async_copy(wu_hbm.at[e], wu_buf.at[s], sem.at[s, 1]),
                pltpu.make_async_copy(wd_hbm.at[e], wd_buf.at[s], sem.at[s, 2]))

    def row_copy(blk, r):
        tok = tok_ref[blk * rows + r]
        return pltpu.make_async_copy(x_hbm.at[pl.ds(tok, 1), :], h_buf.at[blk % 2, pl.ds(r, 1), :],
                                     row_sem.at[blk % 2])

    def for_rows(fn):
        def body(r, carry):
            fn(r)
            return carry
        lax.fori_loop(0, rows, body, 0, unroll=GATHER_UNROLL)

    @pl.when(i == 0)
    def _():
        for cp in weight_copies(exp_ref[0], 0):
            cp.start()
        for_rows(lambda r: row_copy(0, r).start())
        y_buf[...] = jnp.zeros_like(y_buf)
        spare = y_hbm.shape[0] - 2 * rows
        for s in range(2):
            cp = pltpu.make_async_copy(y_buf.at[s], y_hbm.at[pl.ds(spare + s * rows, rows), :], out_sem.at[s])
            cp.start()
            cp.wait()

    @pl.when(first_ref[i] != 0)
    def _():
        nxt = next_ref[i]

        @pl.when(nxt >= 0)
        def _():
            for cp in weight_copies(nxt, 1 - slot):
                cp.start()

        for cp in weight_copies(exp_ref[i], slot):
            cp.wait()

    def ffn(h):
        a = _dot(h, wg_buf[slot].astype(BF16))
        a = a * _sigmoid(a) * _dot(h, wu_buf[slot].astype(BF16))
        return _dot(a.astype(BF16), wd_buf[slot].astype(BF16))

    on = on_ref[i] != 0
    prev_on = on_ref[jnp.maximum(i - 1, 0)] != 0
    prev2_on = on_ref[jnp.maximum(i - 2, 0)] != 0

    @pl.when(on & (i >= 2))
    def _():
        for_rows(lambda r: row_copy(i, r).wait())
        h = h_buf[i % 2].astype(BF16)
        for r in range(rows):
            row_copy(i + 1, r).start()
        for r in range(rows):
            out_copy(i - 1, r).start()
        y = ffn(h)
        for_rows(lambda r: out_copy(i - 2, r).wait())
        y_buf[i % 2] = y

    @pl.when(on & (i < 2))
    def _():
        for_rows(lambda r: row_copy(i, r).wait())
        h = h_buf[i % 2].astype(BF16)
        for_rows(lambda r: row_copy(i + 1, r).start())

        @pl.when(i == 1)
        def _():
            for_rows(lambda r: out_copy(0, r).start())

        y_buf[i % 2] = ffn(h)

    @pl.when(jnp.logical_not(on))
    def _():
        @pl.when((i >= 1) & prev_on)
        def _():
            for_rows(lambda r: row_copy(i, r).wait())
            for_rows(lambda r: out_copy(i - 1, r).start())

        @pl.when((i >= 2) & prev2_on)
        def _():
            for_rows(lambda r: out_copy(i - 2, r).wait())

        @pl.when((i == n_blk - 1) & (i >= 1) & prev_on)
        def _():
            for_rows(lambda r: out_copy(i - 1, r).wait())


def _expert_ffn(blk_exp, blk_on, row_tok, row_dst, x, w_gate, w_up, w_down):
    p_rows = row_tok.shape[0]
    n_tok, d = x.shape
    n_exp, _, ff = w_gate.shape
    n_blocks = p_rows // MOE_BLOCK
    prev_exp = jnp.concatenate([jnp.full((1,), -1, jnp.int32), blk_exp[:-1]])
    first = ((blk_on != 0) & (blk_exp != prev_exp)).astype(jnp.int32)
    seg_id = jnp.cumsum(first) - 1
    seg_ids = jnp.arange(n_exp + 1)
    hit = (first[:, None] != 0) & (seg_id[:, None] == seg_ids[None, :])
    seg_exp = jnp.sum(jnp.where(hit, blk_exp[:, None] + 1, 0), 0) - 1
    blk_next = jnp.sum(jnp.where(seg_ids[None, :] == seg_id[:, None] + 1, seg_exp[None, :] + 1, 0), 1) - 1
    blk_slot = jnp.maximum(seg_id, 0) % 2
    grid_spec = pltpu.PrefetchScalarGridSpec(
        num_scalar_prefetch=7,
        grid=(n_blocks,),
        in_specs=[pl.BlockSpec(memory_space=pl.ANY)] * 4,
        out_specs=pl.BlockSpec(memory_space=pl.ANY),
        scratch_shapes=[pltpu.VMEM((2, d, ff), F32), pltpu.VMEM((2, d, ff), F32), pltpu.VMEM((2, ff, d), F32),
                        pltpu.VMEM((2, MOE_BLOCK, d), F32), pltpu.VMEM((2, MOE_BLOCK, d), F32),
                        pltpu.SemaphoreType.DMA((2, 3)), pltpu.SemaphoreType.DMA((2,)),
                        pltpu.SemaphoreType.DMA((2,))],
    )
    return pl.pallas_call(
        _ffn_kernel,
        grid_spec=grid_spec,
        out_shape=jax.ShapeDtypeStruct((EXPERT_TOPK * n_tok + 2 * MOE_BLOCK, d), F32),
        compiler_params=_params("arbitrary"),
        name="expert_ffn",
    )(blk_exp, blk_on, first, blk_slot.astype(jnp.int32), blk_next.astype(jnp.int32), row_tok, row_dst,
      x, w_gate, w_up, w_down)


def _combine_ln_kernel(x_ref, y0_ref, y1_ref, gate_ref, g_ref, b_ref, o_ref):
    gates = gate_ref[...]
    moe = y0_ref[...] * gates[:, 0:1] + y1_ref[...] * gates[:, 1:2]
    o_ref[...] = _layer_norm(DN_ALPHA * x_ref[...] + moe, g_ref[...], b_ref[...])


def _combine_ln(x, y_tok, gates, ln_g, ln_b, *, tm=256):
    m, d = x.shape
    row = pl.BlockSpec((tm, d), lambda i: (i, 0))
    vec = pl.BlockSpec((1, d), lambda i: (0, 0))
    return pl.pallas_call(
        _combine_ln_kernel,
        grid=(m // tm,),
        in_specs=[row, row, pl.BlockSpec((tm, d), lambda i: (m // tm + i, 0)),
                  pl.BlockSpec((tm, LANES), lambda i: (i, 0)), vec, vec],
        out_specs=row,
        out_shape=jax.ShapeDtypeStruct((m, d), F32),
        compiler_params=_params("parallel"),
        name="combine_ln",
    )(x, y_tok, y_tok, gates, ln_g, ln_b)


def _layer(x, mem, w_in, cmp_pe_k, cmp_w1_k, cmp_w2_k, cmp_pe_v, cmp_w1_v, cmp_w2_v,
           sgu_ln_g, sgu_ln_b, sgu_w_s, sgu_b_s, w_branch_a, w_branch_b, w_o, ln1_g, ln1_b,
           w_xq, w_xk, w_xv, w_xo, ln2_g, ln2_b,
           w_router_grp, b_router_grp, w_router_exp, b_router_exp,
           w_exp_gate, w_exp_up, w_exp_down, ln3_g, ln3_b):
    bsz, seq, d = x.shape
    n_tok = bsz * seq
    g = NSA_KV_GROUPS
    hpg = NSA_HPG
    xf = x.reshape(n_tok, d)
    x_bf = xf.astype(BF16)

    c0 = Q_WIDTH
    c1 = c0 + KV_WIDTH
    c2 = c1 + NSA_GATE_WIDTH
    c3 = c2 + 2 * SGU_WIDTH
    w_q = w_in[:, :c0].astype(BF16)
    w_gate = w_in[:, c1:c2].reshape(d, g, hpg, 3).transpose(0, 1, 3, 2).reshape(d, g, 3 * hpg)
    w_gate = jnp.pad(w_gate, ((0, 0), (0, 0), (0, LANES - 3 * hpg))).reshape(d, g * LANES)
    w_kvg = jnp.concatenate([w_in[:, c0:c1], w_gate], axis=1).astype(BF16)
    w_uv = w_in[:, c2:c3].astype(BF16)
    w_merge = w_in[:, c3:].astype(BF16)

    cos, sin = _rope_tables(jnp.arange(seq, dtype=jnp.int32))
    q = _q_proj(x_bf, w_q, cos, sin, seq).reshape(bsz, seq, Q_WIDTH)
    kv_cmp, kv_rest, nsa_gates = _kv_proj(x_bf, w_kvg, cos, sin, seq)

    n_chunks = seq // CMP_STRIDE
    pe = jnp.stack([cmp_pe_k, cmp_pe_v])
    w1 = jnp.stack([cmp_w1_k, cmp_w1_v]).astype(BF16)
    w2 = jnp.stack([cmp_w2_k, cmp_w2_v]).astype(BF16)
    cmp_end = CMP_STRIDE * jnp.arange(n_chunks, dtype=jnp.int32) + CMP_BLOCK - 1
    ccos, csin = _rope_tables(cmp_end)
    cmp_cos = jnp.stack([ccos, jnp.ones_like(ccos)])
    cmp_sin = jnp.stack([csin, jnp.zeros_like(csin)])
    cmp_kv = _compress(kv_cmp.reshape(bsz, seq, _N_CMP_COLS), pe, w1, w2, cmp_cos, cmp_sin)

    n_sel = seq // SEL_BLOCK
    n_pick = min(SEL_TOPK, n_sel)
    ci = CMP_STRIDE * jnp.arange(n_chunks)
    sj = SEL_BLOCK * jnp.arange(LANES)
    overlap = jnp.clip(jnp.minimum(ci[:, None] + CMP_BLOCK, sj[None, :] + SEL_BLOCK)
                       - jnp.maximum(ci[:, None], sj[None, :]), 0, None)
    selmap_t = (overlap.astype(F32) / CMP_BLOCK).astype(BF16).T
    o_nsa = _nsa(q, cmp_kv, kv_rest.reshape(bsz, seq, _N_REST_COLS),
                 nsa_gates.reshape(bsz, seq, _GATE_COLS), selmap_t, n_sel=n_sel, n_pick=n_pick)

    uv = _matmul(x_bf, w_uv, tm=1024, tn=1024, out_dtype=F32, act="gelu")
    o_sgu = _sgu(uv, sgu_ln_g.reshape(1, -1), sgu_ln_b.reshape(1, -1), sgu_w_s, sgu_b_s.T)

    merged = _merge(x_bf, o_nsa.reshape(n_tok, Q_WIDTH), o_sgu, w_merge,
                    w_branch_a.astype(BF16), w_branch_b.astype(BF16))
    x1, x1_bf = _proj_ln(merged, w_o.astype(BF16), xf, ln1_g.reshape(1, -1), ln1_b.reshape(1, -1))

    mem_len = mem.shape[1]
    w_mem = jnp.concatenate([w_xk, w_xv], axis=1).astype(BF16)
    mem_kv = _matmul(mem.reshape(bsz * mem_len, d).astype(BF16), w_mem, tm=256, tn=512, out_dtype=BF16)
    x2 = _xattn(x1_bf.reshape(bsz, seq, d), x1.reshape(bsz, seq, d), w_xq.astype(BF16),
                mem_kv.reshape(bsz, mem_len, 2 * MEM_WIDTH), w_xo.astype(BF16),
                ln2_g.reshape(1, -1), ln2_b.reshape(1, -1)).reshape(n_tok, d)

    w_r = jnp.concatenate([w_router_grp, w_router_exp], axis=1)
    w_r = jnp.pad(w_r, ((0, 0), (0, LANES - w_r.shape[1])))
    w_r_hi = w_r.astype(BF16)
    w_r_lo = (w_r - w_r_hi.astype(F32)).astype(BF16)
    b_r = jnp.concatenate([b_router_grp, b_router_exp])
    b_r = jnp.pad(b_r, (0, LANES - b_r.shape[0])).reshape(1, LANES)
    gates, experts, ranks, seen = _router(x2, w_r_hi, w_r_lo, b_r)

    nk = n_tok * EXPERT_TOPK
    counts = seen[0, N_GROUPS:N_GROUPS + N_EXPERTS].astype(jnp.int32)
    pcounts = ((counts + MOE_BLOCK - 1) // MOE_BLOCK) * MOE_BLOCK
    pend = jnp.cumsum(pcounts)
    pstart = pend - pcounts
    e_tok = experts[:, :EXPERT_TOPK]
    seg_start = jnp.sum(jnp.where(e_tok[..., None] == jnp.arange(N_EXPERTS), pstart, 0), -1)
    dest_of = seg_start + ranks[:, :EXPERT_TOPK]
    n_blocks = -(-nk // MOE_BLOCK) + N_EXPERTS
    p_rows = n_blocks * MOE_BLOCK
    row_ids = jnp.arange(p_rows, dtype=jnp.int32)
    row_asg = jnp.full((p_rows,), -1, jnp.int32).at[dest_of.reshape(nk)].set(
        jnp.arange(nk, dtype=jnp.int32), unique_indices=True)
    is_real = row_asg >= 0
    row_tok = jnp.where(is_real, row_asg // EXPERT_TOPK, row_ids % n_tok)
    spare = EXPERT_TOPK * n_tok + ((row_ids // MOE_BLOCK) % 2) * MOE_BLOCK + row_ids % MOE_BLOCK
    row_dst = jnp.where(is_real, (row_asg % EXPERT_TOPK) * n_tok + row_asg // EXPERT_TOPK, spare)
    blk_start = jnp.arange(n_blocks) * MOE_BLOCK
    blk_on = (blk_start < pend[-1]).astype(jnp.int32)
    blk_probe = jnp.minimum(blk_start, pend[-1] - 1)
    blk_exp = jnp.minimum(jnp.sum((pend[None, :] <= blk_probe[:, None]).astype(jnp.int32), -1),
                          N_EXPERTS - 1).astype(jnp.int32)

    y_tok = _expert_ffn(blk_exp, blk_on, row_tok, row_dst, x2, w_exp_gate, w_exp_up, w_exp_down)
    out = _combine_ln(x2, y_tok, gates, ln3_g.reshape(1, -1), ln3_b.reshape(1, -1))
    return out.reshape(bsz, seq, d)


def kernel(x, mem, w_in, cmp_pe_k, cmp_w1_k, cmp_w2_k, cmp_pe_v, cmp_w1_v, cmp_w2_v, sgu_ln_g, sgu_ln_b, sgu_w_s, sgu_b_s, w_branch_a, w_branch_b, w_o, ln1_g, ln1_b, w_xq, w_xk, w_xv, w_xo, ln2_g, ln2_b, w_router_grp, b_router_grp, w_router_exp, b_router_exp, w_exp_gate, w_exp_up, w_exp_down, ln3_g, ln3_b):
    params = (w_in, cmp_pe_k, cmp_w1_k, cmp_w2_k, cmp_pe_v, cmp_w1_v, cmp_w2_v,
              sgu_ln_g, sgu_ln_b, sgu_w_s, sgu_b_s, w_branch_a, w_branch_b, w_o, ln1_g, ln1_b,
              w_xq, w_xk, w_xv, w_xo, ln2_g, ln2_b,
              w_router_grp, b_router_grp, w_router_exp, b_router_exp,
              w_exp_gate, w_exp_up, w_exp_down, ln3_g, ln3_b)
    h = x
    for l in range(DEPTH):
        h = _layer(h, mem, *[p[l] for p in params])
    return h
```

```python
import functools
import math

import jax
import jax.numpy as jnp
from jax import lax
from jax.experimental import pallas as pl
from jax.experimental.pallas import tpu as pltpu

D_MODEL = 2048
HEAD_DIM = 128
NSA_HEADS = 16
NSA_KV_GROUPS = 2
NSA_HPG = NSA_HEADS // NSA_KV_GROUPS
CMP_BLOCK = 32
CMP_STRIDE = 16
CMP_HIDDEN = 256
SEL_BLOCK = 64
SEL_TOPK = 16
WINDOW = 512
Q_BLOCK = 128
SGU_WIDTH = 2048
SGU_GROUPS = 8
SGU_CHUNK = 128
MEM_HEADS = 4
N_GROUPS = 4
EXPERTS_PER_GROUP = 16
N_EXPERTS = N_GROUPS * EXPERTS_PER_GROUP
EXPERT_TOPK = 2
EXPERT_FF = 512
MOE_BLOCK = 128
ROPE_THETA = 10000.0
LN_EPS = 1e-5
SEL_BIG = 1e9
DEPTH = 1
DN_ALPHA = (2.0 * DEPTH) ** 0.25

Q_WIDTH = NSA_HEADS * HEAD_DIM
KV_WIDTH = 3 * 2 * NSA_KV_GROUPS * HEAD_DIM
NSA_GATE_WIDTH = 3 * NSA_HEADS
MEM_WIDTH = MEM_HEADS * HEAD_DIM

LANES = 128
VMEM_LIMIT = 56 * 1024 * 1024
MASKED = -1e30
M_INIT = -1e20
BF16 = jnp.bfloat16
F32 = jnp.float32


def _params(*sem):
    return pltpu.CompilerParams(dimension_semantics=sem, vmem_limit_bytes=VMEM_LIMIT)


def _dot(a, b):
    return jnp.dot(a, b, preferred_element_type=F32)


def _dot_nt(a, b):
    return lax.dot_general(a, b, (((1,), (1,)), ((), ())), preferred_element_type=F32)


def _gelu(x):
    return 0.5 * x * (1.0 + jnp.tanh(math.sqrt(2.0 / math.pi) * (x + 0.044715 * (x * x * x))))


def _sigmoid(x):
    return 1.0 / (1.0 + jnp.exp(-x))


def _layer_norm(x, g, b):
    mu = jnp.mean(x, -1, keepdims=True)
    xc = x - mu
    var = jnp.mean(xc * xc, -1, keepdims=True)
    return xc * lax.rsqrt(var + LN_EPS) * g + b


def _rope(t, cos_full, sin_signed):
    return t * cos_full + pltpu.roll(t, HEAD_DIM // 2, axis=1) * sin_signed


def _rope_tables(pos):
    inv = ROPE_THETA ** (-jnp.arange(0, HEAD_DIM, 2, dtype=F32) / HEAD_DIM)
    ang = pos.astype(F32)[:, None] * inv[None, :]
    c, s = jnp.cos(ang), jnp.sin(ang)
    return jnp.concatenate([c, c], -1), jnp.concatenate([-s, s], -1)


MXU_COLS = 256


def _mm_kernel(a_ref, b_ref, o_ref, *, act):
    a = a_ref[...]
    for j in range(o_ref.shape[1] // MXU_COLS):
        cols = slice(j * MXU_COLS, (j + 1) * MXU_COLS)
        y = _dot(a, b_ref[:, cols])
        if act == "gelu":
            y = _gelu(y)
        o_ref[:, cols] = y.astype(o_ref.dtype)


def _matmul(a, b, *, tm, tn, out_dtype, act=None):
    m, k = a.shape
    n = b.shape[1]
    return pl.pallas_call(
        functools.partial(_mm_kernel, act=act),
        grid=(m // tm, n // tn),
        in_specs=[pl.BlockSpec((tm, k), lambda i, j: (i, 0)),
                  pl.BlockSpec((k, tn), lambda i, j: (0, j))],
        out_specs=pl.BlockSpec((tm, tn), lambda i, j: (i, j)),
        out_shape=jax.ShapeDtypeStruct((m, n), out_dtype),
        compiler_params=_params("parallel", "parallel"),
        name="matmul_" + (act or "plain"),
    )(a, b)


def _qproj_kernel(a_ref, b_ref, cos_ref, sin_ref, o_ref):
    a = a_ref[...]
    cos, sin = cos_ref[...], sin_ref[...]
    scale = HEAD_DIM ** -0.5 * math.log2(math.e)
    for j in range(o_ref.shape[1] // MXU_COLS):
        y = _dot(a, b_ref[:, j * MXU_COLS:(j + 1) * MXU_COLS])
        for h in range(MXU_COLS // HEAD_DIM):
            sl = slice(j * MXU_COLS + h * HEAD_DIM, j * MXU_COLS + (h + 1) * HEAD_DIM)
            o_ref[:, sl] = (_rope(y[:, h * HEAD_DIM:(h + 1) * HEAD_DIM], cos, sin) * scale).astype(o_ref.dtype)


def _q_proj(x_bf, w_q, cos, sin, seq, *, tm=1024, tn=1024):
    m, k = x_bf.shape
    n = w_q.shape[1]
    per_seq = seq // tm
    return pl.pallas_call(
        _qproj_kernel,
        grid=(m // tm, n // tn),
        in_specs=[pl.BlockSpec((tm, k), lambda i, j: (i, 0)),
                  pl.BlockSpec((k, tn), lambda i, j: (0, j)),
                  pl.BlockSpec((tm, HEAD_DIM), lambda i, j: (i % per_seq, 0)),
                  pl.BlockSpec((tm, HEAD_DIM), lambda i, j: (i % per_seq, 0))],
        out_specs=pl.BlockSpec((tm, tn), lambda i, j: (i, j)),
        out_shape=jax.ShapeDtypeStruct((m, n), BF16),
        compiler_params=_params("parallel", "parallel"),
        name="q_proj",
    )(x_bf, w_q, cos, sin)


_N_CMP_COLS = 2 * NSA_KV_GROUPS * HEAD_DIM
_N_REST_CHUNKS = (KV_WIDTH - _N_CMP_COLS) // HEAD_DIM
_N_REST_COLS = (_N_REST_CHUNKS + NSA_KV_GROUPS) * HEAD_DIM
_GATE_COLS = NSA_KV_GROUPS * LANES
_SEL_SHIFT = SEL_BLOCK.bit_length() - 1
assert 1 << _SEL_SHIFT == SEL_BLOCK


def _kvproj_kernel(a_ref, b_ref, cos_ref, sin_ref, cmp_ref, rest_ref, gate_ref, *, per_seq):
    g = NSA_KV_GROUPS
    y = _dot(a_ref[...], b_ref[...])
    cos, sin = cos_ref[...], sin_ref[...]
    cmp_ref[...] = y[:, :_N_CMP_COLS]
    tm = y.shape[0]
    pos = (pl.program_id(0) % per_seq) * tm + lax.broadcasted_iota(jnp.int32, (tm, LANES), 0)
    lane = lax.broadcasted_iota(jnp.int32, (tm, LANES), 1)
    blk_onehot = jnp.where(lane == (pos >> _SEL_SHIFT), 1.0, 0.0).astype(rest_ref.dtype)
    for c in range(_N_REST_CHUNKS):
        t = y[:, _N_CMP_COLS + c * HEAD_DIM:_N_CMP_COLS + (c + 1) * HEAD_DIM]
        if (c % (2 * g)) < g:
            t = _rope(t, cos, sin)
        dst = 2 * c if c < g else g + c
        rest_ref[:, dst * HEAD_DIM:(dst + 1) * HEAD_DIM] = t.astype(rest_ref.dtype)
        if c < g:
            rest_ref[:, (dst + 1) * HEAD_DIM:(dst + 2) * HEAD_DIM] = blk_onehot
    gate_ref[...] = _sigmoid(y[:, KV_WIDTH:])


def _kv_proj(x_bf, w_kvg, cos, sin, seq, *, tm=512):
    m, k = x_bf.shape
    n = w_kvg.shape[1]
    per_seq = seq // tm
    return pl.pallas_call(
        functools.partial(_kvproj_kernel, per_seq=per_seq),
        grid=(m // tm,),
        in_specs=[pl.BlockSpec((tm, k), lambda i: (i, 0)),
                  pl.BlockSpec((k, n), lambda i: (0, 0)),
                  pl.BlockSpec((tm, HEAD_DIM), lambda i: (i % per_seq, 0)),
                  pl.BlockSpec((tm, HEAD_DIM), lambda i: (i % per_seq, 0))],
        out_specs=[pl.BlockSpec((tm, _N_CMP_COLS), lambda i: (i, 0)),
                   pl.BlockSpec((tm, _N_REST_COLS), lambda i: (i, 0)),
                   pl.BlockSpec((tm, _GATE_COLS), lambda i: (i, 0))],
        out_shape=[jax.ShapeDtypeStruct((m, _N_CMP_COLS), F32),
                   jax.ShapeDtypeStruct((m, _N_REST_COLS), BF16),
                   jax.ShapeDtypeStruct((m, _GATE_COLS), F32)],
        compiler_params=_params("parallel"),
        name="kv_gate_proj",
    )(x_bf, w_kvg, cos, sin)


def _compress_kernel(kc_ref, pe_ref, w1_ref, w2_ref, cos_ref, sin_ref, o_ref):
    n_chunks = o_ref.shape[2]
    top = jnp.zeros((n_chunks, CMP_HIDDEN), F32)
    bot = jnp.zeros((n_chunks, CMP_HIDDEN), F32)
    for p in range(CMP_STRIDE):
        xp = kc_ref[0, pl.ds(p, n_chunks, stride=CMP_STRIDE), :]
        q = CMP_STRIDE + p
        top += _dot((xp + pe_ref[0, p:p + 1, :]).astype(BF16), w1_ref[0, p * HEAD_DIM:(p + 1) * HEAD_DIM, :])
        bot += _dot((xp + pe_ref[0, q:q + 1, :]).astype(BF16), w1_ref[0, q * HEAD_DIM:(q + 1) * HEAD_DIM, :])
    h = top + pltpu.roll(bot, n_chunks - 1, axis=0)
    y = _dot(_gelu(h).astype(BF16), w2_ref[0])
    o_ref[0, 0] = _rope(y, cos_ref[0], sin_ref[0]).astype(o_ref.dtype)


def _compress(kv_cmp, pe, w1, w2, cos, sin):
    bsz, seq, width = kv_cmp.shape
    four = width // HEAD_DIM
    n_chunks = seq // CMP_STRIDE
    g = NSA_KV_GROUPS
    return pl.pallas_call(
        _compress_kernel,
        grid=(four, bsz),
        in_specs=[pl.BlockSpec((1, seq, HEAD_DIM), lambda a, b: (b, 0, a)),
                  pl.BlockSpec((1, CMP_BLOCK, HEAD_DIM), lambda a, b: (a // g, 0, 0)),
                  pl.BlockSpec((1, CMP_BLOCK * HEAD_DIM, CMP_HIDDEN), lambda a, b: (a // g, 0, 0)),
                  pl.BlockSpec((1, CMP_HIDDEN, HEAD_DIM), lambda a, b: (a // g, 0, 0)),
                  pl.BlockSpec((1, n_chunks, HEAD_DIM), lambda a, b: (a // g, 0, 0)),
                  pl.BlockSpec((1, n_chunks, HEAD_DIM), lambda a, b: (a // g, 0, 0))],
        out_specs=pl.BlockSpec((1, 1, n_chunks, HEAD_DIM), lambda a, b: (a, b, 0, 0)),
        out_shape=jax.ShapeDtypeStruct((four, bsz, n_chunks, HEAD_DIM), BF16),
        compiler_params=_params("parallel", "parallel"),
        name="compress",
    )(kv_cmp, pe, w1, w2, cos, sin)


SEL_TILE = 512
SEL_WIDE = 2
NSA_TQ = 256
SUBLANES = 8


def _softmax_parts(s3, bias):
    s3 = s3 + bias[None]
    m = jnp.max(s3, -1, keepdims=True)
    m = jnp.where(m > 0.5 * MASKED, m, 0.0)
    p = jnp.exp2(s3 - m)
    return p, 1.0 / jnp.maximum(jnp.sum(p, -1, keepdims=True), 1e-30)


def _topk_unselected(score_t, n_sel, n_pick):
    tq = score_t.shape[1]
    n_grp = n_sel // SUBLANES
    grp = [score_t[g * SUBLANES:(g + 1) * SUBLANES] for g in range(n_grp)]
    row_in_grp = lax.broadcasted_iota(jnp.int32, (SUBLANES, tq), 0)
    later = [jnp.where(row_in_grp > li, 1.0, 0.0) for li in range(SUBLANES)]
    beaten = [jnp.zeros((SUBLANES, tq), F32) for _ in range(n_grp)]
    for i in range(n_sel):
        gi, li = divmod(i, SUBLANES)
        row = grp[gi][li:li + 1, :]
        for g in range(n_grp):
            if g > gi:
                beaten[g] = beaten[g] + jnp.where(row >= grp[g], 1.0, 0.0)
            elif g < gi:
                beaten[g] = beaten[g] + jnp.where(row > grp[g], 1.0, 0.0)
            else:
                beaten[g] = (beaten[g] + jnp.where(row > grp[g], 1.0, 0.0)
                             + jnp.where(row == grp[g], later[li], 0.0))
    return jnp.concatenate([jnp.where(b < n_pick, 0.0, MASKED) for b in beaten], axis=0)


def _nsa_kernel(q_ref, kc_ref, vc_ref, ks_ref, vs_ref, kw_ref, vw_ref, gate_ref, selmap_t_ref, o_ref,
                *, n_sel, n_pick):
    hpg = NSA_HPG
    tq = NSA_TQ
    c = pl.program_id(2)
    q_t = q_ref[0]
    q8 = jnp.concatenate([q_t[:, n * HEAD_DIM:(n + 1) * HEAD_DIM] for n in range(hpg)], axis=0)
    t_pos = c * tq + lax.broadcasted_iota(jnp.int32, (tq, 1), 0)
    t_lane = c * tq + lax.broadcasted_iota(jnp.int32, (1, tq), 1)

    kc = kc_ref[0, 0]
    n_cmp_pad = kc.shape[0]
    k_idx = lax.broadcasted_iota(jnp.int32, (1, n_cmp_pad), 1)
    vis_c = (CMP_STRIDE * k_idx + (CMP_BLOCK - 1) <= t_pos) & (k_idx < n_cmp_pad - 1)
    p_c, r_c = _softmax_parts(_dot_nt(q8, kc).reshape(hpg, tq, n_cmp_pad), jnp.where(vis_c, 0.0, MASKED))
    o_c = _dot(p_c.reshape(hpg * tq, n_cmp_pad).astype(BF16), vc_ref[0, 0]).reshape(hpg, tq, HEAD_DIM) * r_c

    p_sum = jnp.sum(p_c * r_c, axis=0)
    p_hi = p_sum.astype(BF16)
    p_lo = (p_sum - p_hi.astype(F32)).astype(BF16)
    selmap_t = selmap_t_ref[...]
    imp_t = _dot_nt(selmap_t, p_hi) + _dot_nt(selmap_t, p_lo)
    blk = lax.broadcasted_iota(jnp.int32, (LANES, tq), 0)
    cur = t_lane >> _SEL_SHIFT
    forced = (blk == 0) | (blk == cur) | (blk == cur - 1)
    score_t = jnp.where(forced, SEL_BIG, jnp.where(blk <= cur, imp_t, -SEL_BIG))
    unsel_t = _topk_unselected(score_t, n_sel, n_pick)
    unsel_t = jnp.concatenate([unsel_t, jnp.zeros((LANES - n_sel, tq), F32)], axis=0)
    unsel = unsel_t.T.astype(BF16)
    q_aug = jnp.concatenate([q8, jnp.concatenate([unsel] * hpg, axis=0)], axis=1)

    def sel_tile(kt, size, carry, bias):
        m_old, l_old, acc = carry
        base = pl.multiple_of(kt * size, size)
        k = ks_ref[0, pl.ds(base, size), :]
        v = vs_ref[0, pl.ds(base, size), :]
        s = _dot_nt(q_aug, k).reshape(hpg, tq, size)
        if bias is not None:
            s = s + bias[None]
        m_new = jnp.maximum(m_old, jnp.max(s, -1, keepdims=True))
        alpha = jnp.exp2(m_old - m_new)
        p = jnp.exp2(s - m_new)
        l_new = alpha * l_old + jnp.sum(p, -1, keepdims=True)
        pv = _dot(p.reshape(hpg * tq, size).astype(BF16), v).reshape(hpg, tq, HEAD_DIM)
        return m_new, l_new, alpha * acc + pv

    n_full = c // (SEL_TILE // tq)
    n_wide = n_full // SEL_WIDE
    carry = (jnp.full((hpg, tq, 1), M_INIT, F32), jnp.zeros((hpg, tq, 1), F32),
             jnp.zeros((hpg, tq, HEAD_DIM), F32))
    carry = lax.fori_loop(0, n_wide, lambda kt, cr: sel_tile(kt, SEL_WIDE * SEL_TILE, cr, None), carry)
    carry = lax.fori_loop(n_wide * SEL_WIDE, n_full, lambda kt, cr: sel_tile(kt, SEL_TILE, cr, None), carry)
    kp_last = n_full * SEL_TILE + lax.broadcasted_iota(jnp.int32, (1, SEL_TILE), 1)
    _, l_s, acc_s = sel_tile(n_full, SEL_TILE, carry, jnp.where(kp_last <= t_pos, 0.0, MASKED))
    o_s = acc_s * (1.0 / jnp.maximum(l_s, 1e-30))

    span = WINDOW + tq
    start = pl.multiple_of(jnp.maximum(c * tq - WINDOW, 0), tq)
    kw = kw_ref[0, pl.ds(start, span), :]
    vw = vw_ref[0, pl.ds(start, span), :]
    kp = start + lax.broadcasted_iota(jnp.int32, (1, span), 1)
    vis_w = (kp <= t_pos) & (t_pos - kp < WINDOW)
    p_w, r_w = _softmax_parts(_dot_nt(q8, kw).reshape(hpg, tq, span), jnp.where(vis_w, 0.0, MASKED))
    o_w = _dot(p_w.reshape(hpg * tq, span).astype(BF16), vw).reshape(hpg, tq, HEAD_DIM) * r_w

    gates = gate_ref[0]
    for n in range(hpg):
        out = (gates[:, n:n + 1] * o_c[n]
               + gates[:, hpg + n:hpg + n + 1] * o_s[n]
               + gates[:, 2 * hpg + n:2 * hpg + n + 1] * o_w[n])
        o_ref[0, :, n * HEAD_DIM:(n + 1) * HEAD_DIM] = out.astype(o_ref.dtype)


def _nsa(q, cmp_kv, rest, gates, selmap_t, *, n_sel, n_pick):
    bsz, seq, _ = q.shape
    g = NSA_KV_GROUPS
    n_cmp_pad = cmp_kv.shape[2]
    gw = NSA_HPG * HEAD_DIM
    assert n_sel % SUBLANES == 0 and n_sel <= LANES and seq >= WINDOW + NSA_TQ
    assert SEL_TILE % NSA_TQ == 0 and seq % SEL_TILE == 0
    kv_spec = lambda off: pl.BlockSpec((1, seq, HEAD_DIM), lambda b, gi, c: (b, 0, off + gi))
    return pl.pallas_call(
        functools.partial(_nsa_kernel, n_sel=n_sel, n_pick=n_pick),
        grid=(bsz, g, seq // NSA_TQ),
        in_specs=[pl.BlockSpec((1, NSA_TQ, gw), lambda b, gi, c: (b, c, gi)),
                  pl.BlockSpec((1, 1, n_cmp_pad, HEAD_DIM), lambda b, gi, c: (gi, b, 0, 0)),
                  pl.BlockSpec((1, 1, n_cmp_pad, HEAD_DIM), lambda b, gi, c: (g + gi, b, 0, 0)),
                  pl.BlockSpec((1, seq, 2 * HEAD_DIM), lambda b, gi, c: (b, 0, gi)),
                  kv_spec(2 * g), kv_spec(3 * g), kv_spec(4 * g),
                  pl.BlockSpec((1, NSA_TQ, LANES), lambda b, gi, c: (b, c, gi)),
                  pl.BlockSpec(selmap_t.shape, lambda b, gi, c: (0, 0))],
        out_specs=pl.BlockSpec((1, NSA_TQ, gw), lambda b, gi, c: (b, c, gi)),
        out_shape=jax.ShapeDtypeStruct(q.shape, BF16),
        compiler_params=_params("parallel", "parallel", "arbitrary"),
        name="nsa_attention",
    )(q, cmp_kv, cmp_kv, rest, rest, rest, rest, gates, selmap_t)


def _sgu_kernel(u_ref, v_ref, g_ref, b_ref, ws_ref, bs_ref, o_ref, *, chunks):
    gd = SGU_WIDTH // SGU_GROUPS
    row = lax.broadcasted_iota(jnp.int32, (SGU_CHUNK, SGU_CHUNK), 0)
    col = lax.broadcasted_iota(jnp.int32, (SGU_CHUNK, SGU_CHUNK), 1)
    causal = col <= row
    bs = bs_ref[...]
    for ch in range(chunks):
        rows = slice(ch * SGU_CHUNK, (ch + 1) * SGU_CHUNK)
        vn = _layer_norm(v_ref[rows, :], g_ref[...], b_ref[...]).astype(BF16)
        for gi in range(SGU_GROUPS):
            cols = slice(gi * gd, (gi + 1) * gd)
            w = jnp.where(causal, ws_ref[gi], 0.0).astype(BF16)
            z = _dot(w, vn[:, cols]) + bs[:, gi:gi + 1]
            o_ref[rows, cols] = (u_ref[rows, cols] * z).astype(o_ref.dtype)


def _sgu(uv, ln_g, ln_b, w_s, b_s_t, *, chunks=2):
    m = uv.shape[0]
    tm = chunks * SGU_CHUNK
    return pl.pallas_call(
        functools.partial(_sgu_kernel, chunks=chunks),
        grid=(m // tm,),
        in_specs=[pl.BlockSpec((tm, SGU_WIDTH), lambda i: (i, 0)),
                  pl.BlockSpec((tm, SGU_WIDTH), lambda i: (i, 1)),
                  pl.BlockSpec((1, SGU_WIDTH), lambda i: (0, 0)),
                  pl.BlockSpec((1, SGU_WIDTH), lambda i: (0, 0)),
                  pl.BlockSpec(w_s.shape, lambda i: (0, 0, 0)),
                  pl.BlockSpec(b_s_t.shape, lambda i: (0, 0))],
        out_specs=pl.BlockSpec((tm, SGU_WIDTH), lambda i: (i, 0)),
        out_shape=jax.ShapeDtypeStruct((m, SGU_WIDTH), BF16),
        compiler_params=_params("parallel"),
        name="sgu",
    )(uv, uv, ln_g, ln_b, w_s, b_s_t)


def _merge_kernel(x_ref, a_ref, s_ref, wga_ref, wgb_ref, wa_ref, wb_ref, o_ref):
    x = x_ref[...]
    ga = _sigmoid(_dot(x, wga_ref[...]))
    gb = _sigmoid(_dot(x, wgb_ref[...]))
    o_ref[...] = (ga * _dot(a_ref[...], wa_ref[...]) + gb * _dot(s_ref[...], wb_ref[...])).astype(o_ref.dtype)


def _merge(x_bf, o_nsa, o_sgu, w_merge, w_a, w_b, *, tm=1024, tn=256):
    m, k = x_bf.shape
    n = w_a.shape[1]
    nb = n // tn
    row = pl.BlockSpec((tm, k), lambda i, j: (i, 0))
    col = pl.BlockSpec((k, tn), lambda i, j: (0, j))
    return pl.pallas_call(
        _merge_kernel,
        grid=(m // tm, nb),
        in_specs=[row, row, row, col, pl.BlockSpec((k, tn), lambda i, j: (0, nb + j)), col, col],
        out_specs=pl.BlockSpec((tm, tn), lambda i, j: (i, j)),
        out_shape=jax.ShapeDtypeStruct((m, n), BF16),
        compiler_params=_params("parallel", "parallel"),
        name="merge",
    )(x_bf, o_nsa, o_sgu, w_merge, w_merge, w_a, w_b)


def _proj_ln_kernel(a_ref, w_ref, x_ref, g_ref, b_ref, o_ref, obf_ref):
    y = DN_ALPHA * x_ref[...] + _dot(a_ref[...], w_ref[...])
    out = _layer_norm(y, g_ref[...], b_ref[...])
    o_ref[...] = out
    obf_ref[...] = out.astype(BF16)


def _proj_ln(a_bf, w, x, ln_g, ln_b, *, tm=512):
    m, k = a_bf.shape
    n = w.shape[1]
    vec = pl.BlockSpec((1, n), lambda i: (0, 0))
    return pl.pallas_call(
        _proj_ln_kernel,
        grid=(m // tm,),
        in_specs=[pl.BlockSpec((tm, k), lambda i: (i, 0)),
                  pl.BlockSpec((k, n), lambda i: (0, 0)),
                  pl.BlockSpec((tm, n), lambda i: (i, 0)), vec, vec],
        out_specs=[pl.BlockSpec((tm, n), lambda i: (i, 0)), pl.BlockSpec((tm, n), lambda i: (i, 0))],
        out_shape=[jax.ShapeDtypeStruct((m, n), F32), jax.ShapeDtypeStruct((m, n), BF16)],
        compiler_params=_params("parallel"),
        name="proj_ln",
    )(a_bf, w, x, ln_g, ln_b)


def _xattn_kernel(xbf_ref, x_ref, wq_ref, kv_ref, wo_ref, g_ref, b_ref, o_ref):
    q = (_dot(xbf_ref[0], wq_ref[...]) * HEAD_DIM ** -0.5).astype(BF16)
    kv = kv_ref[0]
    heads = []
    for h in range(MEM_HEADS):
        cols = slice(h * HEAD_DIM, (h + 1) * HEAD_DIM)
        s = _dot_nt(q[:, cols], kv[:, cols])
        p = jnp.exp(s - jnp.max(s, -1, keepdims=True))
        p = p / jnp.sum(p, -1, keepdims=True)
        heads.append(_dot(p.astype(BF16), kv[:, MEM_WIDTH + h * HEAD_DIM:MEM_WIDTH + (h + 1) * HEAD_DIM]))
    o = jnp.concatenate(heads, axis=1).astype(BF16)
    y = DN_ALPHA * x_ref[0] + _dot(o, wo_ref[...])
    o_ref[0] = _layer_norm(y, g_ref[...], b_ref[...])


def _xattn(x_bf, x, w_xq, mem_kv, w_xo, ln_g, ln_b, *, tm=512):
    bsz, seq, d = x.shape
    mem_len = mem_kv.shape[1]
    vec = pl.BlockSpec((1, d), lambda b, i: (0, 0))
    return pl.pallas_call(
        _xattn_kernel,
        grid=(bsz, seq // tm),
        in_specs=[pl.BlockSpec((1, tm, d), lambda b, i: (b, i, 0)),
                  pl.BlockSpec((1, tm, d), lambda b, i: (b, i, 0)),
                  pl.BlockSpec(w_xq.shape, lambda b, i: (0, 0)),
                  pl.BlockSpec((1, mem_len, 2 * MEM_WIDTH), lambda b, i: (b, 0, 0)),
                  pl.BlockSpec(w_xo.shape, lambda b, i: (0, 0)), vec, vec],
        out_specs=pl.BlockSpec((1, tm, d), lambda b, i: (b, i, 0)),
        out_shape=jax.ShapeDtypeStruct(x.shape, F32),
        compiler_params=_params("parallel", "parallel"),
        name="xattn_ln",
    )(x_bf, x, w_xq, mem_kv, w_xo, ln_g, ln_b)


def _router_kernel(x_ref, whi_ref, wlo_ref, b_ref, gate_ref, exp_ref, rank_ref, cnt_ref, seen_ref):
    @pl.when(pl.program_id(0) == 0)
    def _():
        seen_ref[...] = jnp.zeros_like(seen_ref)

    x = x_ref[...]
    x_hi = x.astype(BF16)
    x_lo = (x - x_hi.astype(F32)).astype(BF16)
    whi = whi_ref[...]
    logits = _dot(x_hi, whi) + _dot(x_hi, wlo_ref[...]) + _dot(x_lo, whi) + b_ref[...]
    rows = logits.shape[0]
    lane = lax.broadcasted_iota(jnp.int32, (rows, LANES), 1)
    lane_f = lane.astype(F32)
    is_grp = lane < N_GROUPS
    lg = jnp.where(is_grp, logits, MASKED)
    eg = jnp.where(is_grp, jnp.exp(lg - jnp.max(lg, -1, keepdims=True)), 0.0)
    p_grp = eg / jnp.sum(eg, -1, keepdims=True)
    g_grp = jnp.max(p_grp, -1, keepdims=True)
    grp = jnp.min(jnp.where((p_grp == g_grp) & is_grp, lane_f, 1e9), -1, keepdims=True).astype(jnp.int32)
    lo = N_GROUPS + grp * EXPERTS_PER_GROUP
    in_grp = (lane >= lo) & (lane < lo + EXPERTS_PER_GROUP)
    le = jnp.where(in_grp, logits, MASKED)
    ee = jnp.where(in_grp, jnp.exp(le - jnp.max(le, -1, keepdims=True)), 0.0)
    p_exp = jnp.where(in_grp, ee / jnp.sum(ee, -1, keepdims=True), -1.0)
    v1 = jnp.max(p_exp, -1, keepdims=True)
    i1 = jnp.min(jnp.where(p_exp == v1, lane_f, 1e9), -1, keepdims=True)
    p_rest = jnp.where(lane_f == i1, -1.0, p_exp)
    v2 = jnp.max(p_rest, -1, keepdims=True)
    i2 = jnp.min(jnp.where(p_rest == v2, lane_f, 1e9), -1, keepdims=True)
    denom = v1 + v2
    gate_ref[...] = jnp.where(lane == 0, g_grp * v1 / denom, jnp.where(lane == 1, g_grp * v2 / denom, 0.0))
    exp_ref[...] = jnp.where(lane == 0, i1, jnp.where(lane == 1, i2, float(N_GROUPS))).astype(jnp.int32) - N_GROUPS

    onehot = jnp.where((lane_f == i1) | (lane_f == i2), 1.0, 0.0)
    tok_r = lax.broadcasted_iota(jnp.int32, (rows, rows), 0)
    tok_c = lax.broadcasted_iota(jnp.int32, (rows, rows), 1)
    earlier = jnp.where(tok_c < tok_r, 1.0, 0.0).astype(BF16)
    before = _dot(earlier, onehot.astype(BF16)) + seen_ref[...]
    r1 = jnp.sum(jnp.where(lane_f == i1, before, 0.0), -1, keepdims=True)
    r2 = jnp.sum(jnp.where(lane_f == i2, before, 0.0), -1, keepdims=True)
    rank_ref[...] = jnp.where(lane == 0, r1, jnp.where(lane == 1, r2, 0.0)).astype(jnp.int32)
    seen_ref[...] += jnp.sum(onehot, axis=0, keepdims=True)
    cnt_ref[...] = seen_ref[...]


def _router(x, w_hi, w_lo, bias, *, tm=512):
    m, d = x.shape
    wspec = pl.BlockSpec((d, LANES), lambda i: (0, 0))
    vec = pl.BlockSpec((1, LANES), lambda i: (0, 0))
    tile = pl.BlockSpec((tm, LANES), lambda i: (i, 0))
    return pl.pallas_call(
        _router_kernel,
        grid=(m // tm,),
        in_specs=[pl.BlockSpec((tm, d), lambda i: (i, 0)), wspec, wspec, vec],
        out_specs=[tile, tile, tile, vec],
        out_shape=[jax.ShapeDtypeStruct((m, LANES), F32), jax.ShapeDtypeStruct((m, LANES), jnp.int32),
                   jax.ShapeDtypeStruct((m, LANES), jnp.int32), jax.ShapeDtypeStruct((1, LANES), F32)],
        scratch_shapes=[pltpu.VMEM((1, LANES), F32)],
        compiler_params=_params("arbitrary"),
        name="router",
    )(x, w_hi, w_lo, bias)


GATHER_UNROLL = 8


def _ffn_kernel(exp_ref, on_ref, first_ref, slot_ref, next_ref, tok_ref, dst_ref, x_hbm, wg_hbm, wu_hbm, wd_hbm,
                y_hbm, wg_buf, wu_buf, wd_buf, h_buf, y_buf, sem, row_sem, out_sem):
    i = pl.program_id(0)
    n_blk = pl.num_programs(0)
    slot = slot_ref[i]
    rows = h_buf.shape[1]

    def out_copy(blk, r):
        dst = dst_ref[blk * rows + r]
        return pltpu.make_async_copy(y_buf.at[blk % 2, pl.ds(r, 1), :], y_hbm.at[pl.ds(dst, 1), :],
                                     out_sem.at[blk % 2])

    def weight_copies(e, s):
        return (pltpu.make_async_copy(wg_hbm.at[e], wg_buf.at[s], sem.at[s, 0]),
                pltpu.make_async_copy(wu_hbm.at[e], wu_buf.at[s], sem.at[s, 1]),
                pltpu.make_async_copy(wd_hbm.at[e], wd_buf.at[s], sem.at[s, 2]))

    def row_copy(blk, r):
        tok = tok_ref[blk * rows + r]
        return pltpu.make_async_copy(x_hbm.at[pl.ds(tok, 1), :], h_buf.at[blk % 2, pl.ds(r, 1), :],
                                     row_sem.at[blk % 2])

    def for_rows(blk, fn):
        def body(g, carry):
            for u in range(GATHER_UNROLL):
                fn(blk, g * GATHER_UNROLL + u)
            return carry
        lax.fori_loop(0, on_ref[blk], body, 0)

    @pl.when(i == 0)
    def _():
        for cp in weight_copies(exp_ref[0], 0):
            cp.start()
        h_buf[...] = jnp.zeros_like(h_buf)
        for_rows(0, lambda b, r: row_copy(b, r).start())
        y_buf[...] = jnp.zeros_like(y_buf)
        spare = y_hbm.shape[0] - 2 * rows
        for s in range(2):
            cp = pltpu.make_async_copy(y_buf.at[s], y_hbm.at[pl.ds(spare + s * rows, rows), :], out_sem.at[s])
            cp.start()
            cp.wait()

    prev2 = jnp.maximum(i - 2, 0)

    @pl.when((i >= 2) & (on_ref[prev2] != 0))
    def _():
        for_rows(i - 2, lambda b, r: out_copy(b, r).wait())

    nxt_blk = jnp.minimum(i + 1, n_blk - 1)

    @pl.when((i + 1 < n_blk) & (on_ref[nxt_blk] != 0))
    def _():
        for_rows(i + 1, lambda b, r: row_copy(b, r).start())

    @pl.when(first_ref[i] != 0)
    def _():
        nxt = next_ref[i]

        @pl.when(nxt >= 0)
        def _():
            for cp in weight_copies(nxt, 1 - slot):
                cp.start()

        for cp in weight_copies(exp_ref[i], slot):
            cp.wait()

    @pl.when(on_ref[i] != 0)
    def _():
        for_rows(i, lambda b, r: row_copy(b, r).wait())
        h = h_buf[i % 2].astype(BF16)
        a = _dot(h, wg_buf[slot].astype(BF16))
        a = a * _sigmoid(a) * _dot(h, wu_buf[slot].astype(BF16))
        y_buf[i % 2] = _dot(a.astype(BF16), wd_buf[slot].astype(BF16))
        for_rows(i, lambda b, r: out_copy(b, r).start())

    @pl.when(i == n_blk - 1)
    def _():
        prev1 = jnp.maximum(i - 1, 0)

        @pl.when((i >= 1) & (on_ref[prev1] != 0))
        def _():
            for_rows(i - 1, lambda b, r: out_copy(b, r).wait())

        @pl.when(on_ref[i] != 0)
        def _():
            for_rows(i, lambda b, r: out_copy(b, r).wait())


def _expert_ffn(blk_exp, blk_on, row_tok, row_dst, x, w_gate, w_up, w_down):
    p_rows = row_tok.shape[0]
    n_tok, d = x.shape
    n_exp, _, ff = w_gate.shape
    n_blocks = p_rows // MOE_BLOCK
    prev_exp = jnp.concatenate([jnp.full((1,), -1, jnp.int32), blk_exp[:-1]])
    first = ((blk_on != 0) & (blk_exp != prev_exp)).astype(jnp.int32)
    seg_id = jnp.cumsum(first) - 1
    seg_ids = jnp.arange(n_exp + 1)
    hit = (first[:, None] != 0) & (seg_id[:, None] == seg_ids[None, :])
    seg_exp = jnp.sum(jnp.where(hit, blk_exp[:, None] + 1, 0), 0) - 1
    blk_next = jnp.sum(jnp.where(seg_ids[None, :] == seg_id[:, None] + 1, seg_exp[None, :] + 1, 0), 1) - 1
    blk_slot = jnp.maximum(seg_id, 0) % 2
    grid_spec = pltpu.PrefetchScalarGridSpec(
        num_scalar_prefetch=7,
        grid=(n_blocks,),
        in_specs=[pl.BlockSpec(memory_space=pl.ANY)] * 4,
        out_specs=pl.BlockSpec(memory_space=pl.ANY),
        scratch_shapes=[pltpu.VMEM((2, d, ff), F32), pltpu.VMEM((2, d, ff), F32), pltpu.VMEM((2, ff, d), F32),
                        pltpu.VMEM((2, MOE_BLOCK, d), F32), pltpu.VMEM((2, MOE_BLOCK, d), F32),
                        pltpu.SemaphoreType.DMA((2, 3)), pltpu.SemaphoreType.DMA((2,)),
                        pltpu.SemaphoreType.DMA((2,))],
    )
    return pl.pallas_call(
        _ffn_kernel,
        grid_spec=grid_spec,
        out_shape=jax.ShapeDtypeStruct((EXPERT_TOPK * n_tok + 2 * MOE_BLOCK, d), F32),
        compiler_params=_params("arbitrary"),
        name="expert_ffn",
    )(blk_exp, blk_on, first, blk_slot.astype(jnp.int32), blk_next.astype(jnp.int32), row_tok, row_dst,
      x, w_gate, w_up, w_down)


def _combine_ln_kernel(x_ref, y0_ref, y1_ref, gate_ref, g_ref, b_ref, o_ref):
    gates = gate_ref[...]
    moe = y0_ref[...] * gates[:, 0:1] + y1_ref[...] * gates[:, 1:2]
    o_ref[...] = _layer_norm(DN_ALPHA * x_ref[...] + moe, g_ref[...], b_ref[...])


def _combine_ln(x, y_tok, gates, ln_g, ln_b, *, tm=256):
    m, d = x.shape
    row = pl.BlockSpec((tm, d), lambda i: (i, 0))
    vec = pl.BlockSpec((1, d), lambda i: (0, 0))
    return pl.pallas_call(
        _combine_ln_kernel,
        grid=(m // tm,),
        in_specs=[row, row, pl.BlockSpec((tm, d), lambda i: (m // tm + i, 0)),
                  pl.BlockSpec((tm, LANES), lambda i: (i, 0)), vec, vec],
        out_specs=row,
        out_shape=jax.ShapeDtypeStruct((m, d), F32),
        compiler_params=_params("parallel"),
        name="combine_ln",
    )(x, y_tok, y_tok, gates, ln_g, ln_b)


def _layer(x, mem, w_in, cmp_pe_k, cmp_w1_k, cmp_w2_k, cmp_pe_v, cmp_w1_v, cmp_w2_v,
           sgu_ln_g, sgu_ln_b, sgu_w_s, sgu_b_s, w_branch_a, w_branch_b, w_o, ln1_g, ln1_b,
           w_xq, w_xk, w_xv, w_xo, ln2_g, ln2_b,
           w_router_grp, b_router_grp, w_router_exp, b_router_exp,
           w_exp_gate, w_exp_up, w_exp_down, ln3_g, ln3_b):
    bsz, seq, d = x.shape
    n_tok = bsz * seq
    g = NSA_KV_GROUPS
    hpg = NSA_HPG
    xf = x.reshape(n_tok, d)
    x_bf = xf.astype(BF16)

    c0 = Q_WIDTH
    c1 = c0 + KV_WIDTH
    c2 = c1 + NSA_GATE_WIDTH
    c3 = c2 + 2 * SGU_WIDTH
    w_q = w_in[:, :c0].astype(BF16)
    w_gate = w_in[:, c1:c2].reshape(d, g, hpg, 3).transpose(0, 1, 3, 2).reshape(d, g, 3 * hpg)
    w_gate = jnp.pad(w_gate, ((0, 0), (0, 0), (0, LANES - 3 * hpg))).reshape(d, g * LANES)
    w_kvg = jnp.concatenate([w_in[:, c0:c1], w_gate], axis=1).astype(BF16)
    w_uv = w_in[:, c2:c3].astype(BF16)
    w_merge = w_in[:, c3:].astype(BF16)

    cos, sin = _rope_tables(jnp.arange(seq, dtype=jnp.int32))
    q = _q_proj(x_bf, w_q, cos, sin, seq).reshape(bsz, seq, Q_WIDTH)
    kv_cmp, kv_rest, nsa_gates = _kv_proj(x_bf, w_kvg, cos, sin, seq)

    n_chunks = seq // CMP_STRIDE
    pe = jnp.stack([cmp_pe_k, cmp_pe_v])
    w1 = jnp.stack([cmp_w1_k, cmp_w1_v]).astype(BF16)
    w2 = jnp.stack([cmp_w2_k, cmp_w2_v]).astype(BF16)
    cmp_end = CMP_STRIDE * jnp.arange(n_chunks, dtype=jnp.int32) + CMP_BLOCK - 1
    ccos, csin = _rope_tables(cmp_end)
    cmp_cos = jnp.stack([ccos, jnp.ones_like(ccos)])
    cmp_sin = jnp.stack([csin, jnp.zeros_like(csin)])
    cmp_kv = _compress(kv_cmp.reshape(bsz, seq, _N_CMP_COLS), pe, w1, w2, cmp_cos, cmp_sin)

    n_sel = seq // SEL_BLOCK
    n_pick = min(SEL_TOPK, n_sel)
    ci = CMP_STRIDE * jnp.arange(n_chunks)
    sj = SEL_BLOCK * jnp.arange(LANES)
    overlap = jnp.clip(jnp.minimum(ci[:, None] + CMP_BLOCK, sj[None, :] + SEL_BLOCK)
                       - jnp.maximum(ci[:, None], sj[None, :]), 0, None)
    selmap_t = (overlap.astype(F32) / CMP_BLOCK).astype(BF16).T
    o_nsa = _nsa(q, cmp_kv, kv_rest.reshape(bsz, seq, _N_REST_COLS),
                 nsa_gates.reshape(bsz, seq, _GATE_COLS), selmap_t, n_sel=n_sel, n_pick=n_pick)

    uv = _matmul(x_bf, w_uv, tm=1024, tn=1024, out_dtype=F32, act="gelu")
    o_sgu = _sgu(uv, sgu_ln_g.reshape(1, -1), sgu_ln_b.reshape(1, -1), sgu_w_s, sgu_b_s.T)

    merged = _merge(x_bf, o_nsa.reshape(n_tok, Q_WIDTH), o_sgu, w_merge,
                    w_branch_a.astype(BF16), w_branch_b.astype(BF16))
    x1, x1_bf = _proj_ln(merged, w_o.astype(BF16), xf, ln1_g.reshape(1, -1), ln1_b.reshape(1, -1))

    mem_len = mem.shape[1]
    w_mem = jnp.concatenate([w_xk, w_xv], axis=1).astype(BF16)
    mem_kv = _matmul(mem.reshape(bsz * mem_len, d).astype(BF16), w_mem, tm=256, tn=512, out_dtype=BF16)
    x2 = _xattn(x1_bf.reshape(bsz, seq, d), x1.reshape(bsz, seq, d), w_xq.astype(BF16),
                mem_kv.reshape(bsz, mem_len, 2 * MEM_WIDTH), w_xo.astype(BF16),
                ln2_g.reshape(1, -1), ln2_b.reshape(1, -1)).reshape(n_tok, d)

    w_r = jnp.concatenate([w_router_grp, w_router_exp], axis=1)
    w_r = jnp.pad(w_r, ((0, 0), (0, LANES - w_r.shape[1])))
    w_r_hi = w_r.astype(BF16)
    w_r_lo = (w_r - w_r_hi.astype(F32)).astype(BF16)
    b_r = jnp.concatenate([b_router_grp, b_router_exp])
    b_r = jnp.pad(b_r, (0, LANES - b_r.shape[0])).reshape(1, LANES)
    gates, experts, ranks, seen = _router(x2, w_r_hi, w_r_lo, b_r)

    nk = n_tok * EXPERT_TOPK
    counts = seen[0, N_GROUPS:N_GROUPS + N_EXPERTS].astype(jnp.int32)
    pcounts = ((counts + MOE_BLOCK - 1) // MOE_BLOCK) * MOE_BLOCK
    pend = jnp.cumsum(pcounts)
    pstart = pend - pcounts
    e_tok = experts[:, :EXPERT_TOPK]
    seg_start = jnp.sum(jnp.where(e_tok[..., None] == jnp.arange(N_EXPERTS), pstart, 0), -1)
    dest_of = seg_start + ranks[:, :EXPERT_TOPK]
    n_blocks = -(-nk // MOE_BLOCK) + N_EXPERTS
    p_rows = n_blocks * MOE_BLOCK
    row_ids = jnp.arange(p_rows, dtype=jnp.int32)
    row_asg = jnp.full((p_rows,), -1, jnp.int32).at[dest_of.reshape(nk)].set(
        jnp.arange(nk, dtype=jnp.int32), unique_indices=True)
    is_real = row_asg >= 0
    row_tok = jnp.where(is_real, row_asg // EXPERT_TOPK, row_ids % n_tok)
    spare = EXPERT_TOPK * n_tok + ((row_ids // MOE_BLOCK) % 2) * MOE_BLOCK + row_ids % MOE_BLOCK
    row_dst = jnp.where(is_real, (row_asg % EXPERT_TOPK) * n_tok + row_asg // EXPERT_TOPK, spare)
    blk_start = jnp.arange(n_blocks) * MOE_BLOCK
    blk_on = (blk_start < pend[-1]).astype(jnp.int32)
    blk_probe = jnp.minimum(blk_start, pend[-1] - 1)
    blk_exp = jnp.minimum(jnp.sum((pend[None, :] <= blk_probe[:, None]).astype(jnp.int32), -1),
                          N_EXPERTS - 1).astype(jnp.int32)
    real_end = jnp.sum(jnp.where(blk_exp[:, None] == jnp.arange(N_EXPERTS), (pstart + counts)[None, :], 0), -1)
    blk_real = jnp.clip(real_end - blk_start, 0, MOE_BLOCK) * blk_on
    blk_groups = ((blk_real + GATHER_UNROLL - 1) // GATHER_UNROLL).astype(jnp.int32)

    y_tok = _expert_ffn(blk_exp, blk_groups, row_tok, row_dst, x2, w_exp_gate, w_exp_up, w_exp_down)
    out = _combine_ln(x2, y_tok, gates, ln3_g.reshape(1, -1), ln3_b.reshape(1, -1))
    return out.reshape(bsz, seq, d)


def kernel(x, mem, w_in, cmp_pe_k, cmp_w1_k, cmp_w2_k, cmp_pe_v, cmp_w1_v, cmp_w2_v, sgu_ln_g, sgu_ln_b, sgu_w_s, sgu_b_s, w_branch_a, w_branch_b, w_o, ln1_g, ln1_b, w_xq, w_xk, w_xv, w_xo, ln2_g, ln2_b, w_router_grp, b_router_grp, w_router_exp, b_router_exp, w_exp_gate, w_exp_up, w_exp_down, ln3_g, ln3_b):
    params = (w_in, cmp_pe_k, cmp_w1_k, cmp_w2_k, cmp_pe_v, cmp_w1_v, cmp_w2_v,
              sgu_ln_g, sgu_ln_b, sgu_w_s, sgu_b_s, w_branch_a, w_branch_b, w_o, ln1_g, ln1_b,
              w_xq, w_xk, w_xv, w_xo, ln2_g, ln2_b,
              w_router_grp, b_router_grp, w_router_exp, b_router_exp,
              w_exp_gate, w_exp_up, w_exp_down, ln3_g, ln3_b)
    h = x
    for l in range(DEPTH):
        h = _layer(h, mem, *[p[l] for p in params])
    return h
```

```python
import functools
import math

import jax
import jax.numpy as jnp
from jax import lax
from jax.experimental import pallas as pl
from jax.experimental.pallas import tpu as pltpu

D_MODEL = 2048
HEAD_DIM = 128
NSA_HEADS = 16
NSA_KV_GROUPS = 2
NSA_HPG = NSA_HEADS // NSA_KV_GROUPS
CMP_BLOCK = 32
CMP_STRIDE = 16
CMP_HIDDEN = 256
SEL_BLOCK = 64
SEL_TOPK = 16
WINDOW = 512
Q_BLOCK = 128
SGU_WIDTH = 2048
SGU_GROUPS = 8
SGU_CHUNK = 128
MEM_HEADS = 4
N_GROUPS = 4
EXPERTS_PER_GROUP = 16
N_EXPERTS = N_GROUPS * EXPERTS_PER_GROUP
EXPERT_TOPK = 2
EXPERT_FF = 512
MOE_BLOCK = 128
ROPE_THETA = 10000.0
LN_EPS = 1e-5
SEL_BIG = 1e9
DEPTH = 1
DN_ALPHA = (2.0 * DEPTH) ** 0.25

Q_WIDTH = NSA_HEADS * HEAD_DIM
KV_WIDTH = 3 * 2 * NSA_KV_GROUPS * HEAD_DIM
NSA_GATE_WIDTH = 3 * NSA_HEADS
MEM_WIDTH = MEM_HEADS * HEAD_DIM

LANES = 128
VMEM_LIMIT = 56 * 1024 * 1024
MASKED = -1e30
M_INIT = -1e20
BF16 = jnp.bfloat16
F32 = jnp.float32


def _params(*sem):
    return pltpu.CompilerParams(dimension_semantics=sem, vmem_limit_bytes=VMEM_LIMIT)


def _dot(a, b):
    return jnp.dot(a, b, preferred_element_type=F32)


def _dot_nt(a, b):
    return lax.dot_general(a, b, (((1,), (1,)), ((), ())), preferred_element_type=F32)


def _gelu(x):
    return 0.5 * x * (1.0 + jnp.tanh(math.sqrt(2.0 / math.pi) * (x + 0.044715 * (x * x * x))))


def _sigmoid(x):
    return 1.0 / (1.0 + jnp.exp(-x))


def _layer_norm(x, g, b):
    mu = jnp.mean(x, -1, keepdims=True)
    xc = x - mu
    var = jnp.mean(xc * xc, -1, keepdims=True)
    return xc * lax.rsqrt(var + LN_EPS) * g + b


def _rope(t, cos_full, sin_signed):
    return t * cos_full + pltpu.roll(t, HEAD_DIM // 2, axis=1) * sin_signed


def _rope_tables(pos):
    inv = ROPE_THETA ** (-jnp.arange(0, HEAD_DIM, 2, dtype=F32) / HEAD_DIM)
    ang = pos.astype(F32)[:, None] * inv[None, :]
    c, s = jnp.cos(ang), jnp.sin(ang)
    return jnp.concatenate([c, c], -1), jnp.concatenate([-s, s], -1)


MXU_COLS = 256


def _mm_kernel(a_ref, b_ref, o_ref):
    o_ref[...] = _dot(a_ref[...], b_ref[...]).astype(o_ref.dtype)


def _matmul(a, b, *, tm, tn, out_dtype):
    m, k = a.shape
    n = b.shape[1]
    return pl.pallas_call(
        _mm_kernel,
        grid=(m // tm, n // tn),
        in_specs=[pl.BlockSpec((tm, k), lambda i, j: (i, 0)),
                  pl.BlockSpec((k, tn), lambda i, j: (0, j))],
        out_specs=pl.BlockSpec((tm, tn), lambda i, j: (i, j)),
        out_shape=jax.ShapeDtypeStruct((m, n), out_dtype),
        compiler_params=_params("parallel", "parallel"),
        name="matmul",
    )(a, b)


def _qproj_kernel(a_ref, b_ref, cos_ref, sin_ref, o_ref):
    a = a_ref[...]
    cos, sin = cos_ref[...], sin_ref[...]
    scale = HEAD_DIM ** -0.5 * math.log2(math.e)
    for j in range(o_ref.shape[1] // MXU_COLS):
        y = _dot(a, b_ref[:, j * MXU_COLS:(j + 1) * MXU_COLS])
        for h in range(MXU_COLS // HEAD_DIM):
            sl = slice(j * MXU_COLS + h * HEAD_DIM, j * MXU_COLS + (h + 1) * HEAD_DIM)
            o_ref[:, sl] = (_rope(y[:, h * HEAD_DIM:(h + 1) * HEAD_DIM], cos, sin) * scale).astype(o_ref.dtype)


def _q_proj(x_bf, w_q, cos, sin, seq, *, tm=1024, tn=1024):
    m, k = x_bf.shape
    n = w_q.shape[1]
    per_seq = seq // tm
    return pl.pallas_call(
        _qproj_kernel,
        grid=(m // tm, n // tn),
        in_specs=[pl.BlockSpec((tm, k), lambda i, j: (i, 0)),
                  pl.BlockSpec((k, tn), lambda i, j: (0, j)),
                  pl.BlockSpec((tm, HEAD_DIM), lambda i, j: (i % per_seq, 0)),
                  pl.BlockSpec((tm, HEAD_DIM), lambda i, j: (i % per_seq, 0))],
        out_specs=pl.BlockSpec((tm, tn), lambda i, j: (i, j)),
        out_shape=jax.ShapeDtypeStruct((m, n), BF16),
        compiler_params=_params("parallel", "parallel"),
        name="q_proj",
    )(x_bf, w_q, cos, sin)


_N_CMP_COLS = 2 * NSA_KV_GROUPS * HEAD_DIM
_N_REST_CHUNKS = (KV_WIDTH - _N_CMP_COLS) // HEAD_DIM
_N_REST_COLS = (_N_REST_CHUNKS + NSA_KV_GROUPS) * HEAD_DIM
_GATE_COLS = NSA_KV_GROUPS * LANES
_SEL_SHIFT = SEL_BLOCK.bit_length() - 1
assert 1 << _SEL_SHIFT == SEL_BLOCK


def _kvproj_kernel(a_ref, b_ref, cos_ref, sin_ref, cmp_ref, rest_ref, gate_ref, *, per_seq):
    g = NSA_KV_GROUPS
    y = _dot(a_ref[...], b_ref[...])
    cos, sin = cos_ref[...], sin_ref[...]
    cmp_ref[...] = y[:, :_N_CMP_COLS]
    tm = y.shape[0]
    pos = (pl.program_id(0) % per_seq) * tm + lax.broadcasted_iota(jnp.int32, (tm, LANES), 0)
    lane = lax.broadcasted_iota(jnp.int32, (tm, LANES), 1)
    blk_onehot = jnp.where(lane == (pos >> _SEL_SHIFT), 1.0, 0.0).astype(rest_ref.dtype)
    for c in range(_N_REST_CHUNKS):
        t = y[:, _N_CMP_COLS + c * HEAD_DIM:_N_CMP_COLS + (c + 1) * HEAD_DIM]
        if (c % (2 * g)) < g:
            t = _rope(t, cos, sin)
        dst = 2 * c if c < g else g + c
        rest_ref[:, dst * HEAD_DIM:(dst + 1) * HEAD_DIM] = t.astype(rest_ref.dtype)
        if c < g:
            rest_ref[:, (dst + 1) * HEAD_DIM:(dst + 2) * HEAD_DIM] = blk_onehot
    gate_ref[...] = _sigmoid(y[:, KV_WIDTH:])


def _kv_proj(x_bf, w_kvg, cos, sin, seq, *, tm=512):
    m, k = x_bf.shape
    n = w_kvg.shape[1]
    per_seq = seq // tm
    return pl.pallas_call(
        functools.partial(_kvproj_kernel, per_seq=per_seq),
        grid=(m // tm,),
        in_specs=[pl.BlockSpec((tm, k), lambda i: (i, 0)),
                  pl.BlockSpec((k, n), lambda i: (0, 0)),
                  pl.BlockSpec((tm, HEAD_DIM), lambda i: (i % per_seq, 0)),
                  pl.BlockSpec((tm, HEAD_DIM), lambda i: (i % per_seq, 0))],
        out_specs=[pl.BlockSpec((tm, _N_CMP_COLS), lambda i: (i, 0)),
                   pl.BlockSpec((tm, _N_REST_COLS), lambda i: (i, 0)),
                   pl.BlockSpec((tm, _GATE_COLS), lambda i: (i, 0))],
        out_shape=[jax.ShapeDtypeStruct((m, _N_CMP_COLS), F32),
                   jax.ShapeDtypeStruct((m, _N_REST_COLS), BF16),
                   jax.ShapeDtypeStruct((m, _GATE_COLS), F32)],
        compiler_params=_params("parallel"),
        name="kv_gate_proj",
    )(x_bf, w_kvg, cos, sin)


def _compress_kernel(kc_ref, pe_ref, w1_ref, w2_ref, cos_ref, sin_ref, o_ref):
    n_chunks = o_ref.shape[2]
    top = jnp.zeros((n_chunks, CMP_HIDDEN), F32)
    bot = jnp.zeros((n_chunks, CMP_HIDDEN), F32)
    for p in range(CMP_STRIDE):
        xp = kc_ref[0, pl.ds(p, n_chunks, stride=CMP_STRIDE), :]
        q = CMP_STRIDE + p
        top += _dot((xp + pe_ref[0, p:p + 1, :]).astype(BF16), w1_ref[0, p * HEAD_DIM:(p + 1) * HEAD_DIM, :])
        bot += _dot((xp + pe_ref[0, q:q + 1, :]).astype(BF16), w1_ref[0, q * HEAD_DIM:(q + 1) * HEAD_DIM, :])
    h = top + pltpu.roll(bot, n_chunks - 1, axis=0)
    y = _dot(_gelu(h).astype(BF16), w2_ref[0])
    o_ref[0, 0] = _rope(y, cos_ref[0], sin_ref[0]).astype(o_ref.dtype)


def _compress(kv_cmp, pe, w1, w2, cos, sin):
    bsz, seq, width = kv_cmp.shape
    four = width // HEAD_DIM
    n_chunks = seq // CMP_STRIDE
    g = NSA_KV_GROUPS
    return pl.pallas_call(
        _compress_kernel,
        grid=(four, bsz),
        in_specs=[pl.BlockSpec((1, seq, HEAD_DIM), lambda a, b: (b, 0, a)),
                  pl.BlockSpec((1, CMP_BLOCK, HEAD_DIM), lambda a, b: (a // g, 0, 0)),
                  pl.BlockSpec((1, CMP_BLOCK * HEAD_DIM, CMP_HIDDEN), lambda a, b: (a // g, 0, 0)),
                  pl.BlockSpec((1, CMP_HIDDEN, HEAD_DIM), lambda a, b: (a // g, 0, 0)),
                  pl.BlockSpec((1, n_chunks, HEAD_DIM), lambda a, b: (a // g, 0, 0)),
                  pl.BlockSpec((1, n_chunks, HEAD_DIM), lambda a, b: (a // g, 0, 0))],
        out_specs=pl.BlockSpec((1, 1, n_chunks, HEAD_DIM), lambda a, b: (a, b, 0, 0)),
        out_shape=jax.ShapeDtypeStruct((four, bsz, n_chunks, HEAD_DIM), BF16),
        compiler_params=_params("parallel", "parallel"),
        name="compress",
    )(kv_cmp, pe, w1, w2, cos, sin)


SEL_TILE = 512
SEL_WIDE = 2
NSA_TQ = 256
SUBLANES = 8


def _softmax_parts(s3, bias):
    s3 = s3 + bias[None]
    m = jnp.max(s3, -1, keepdims=True)
    m = jnp.where(m > 0.5 * MASKED, m, 0.0)
    p = jnp.exp2(s3 - m)
    return p, 1.0 / jnp.maximum(jnp.sum(p, -1, keepdims=True), 1e-30)


def _topk_unselected(score_t, n_sel, n_pick):
    tq = score_t.shape[1]
    n_grp = n_sel // SUBLANES
    grp = [score_t[g * SUBLANES:(g + 1) * SUBLANES] for g in range(n_grp)]
    row_in_grp = lax.broadcasted_iota(jnp.int32, (SUBLANES, tq), 0)
    later = [jnp.where(row_in_grp > li, 1.0, 0.0) for li in range(SUBLANES)]
    beaten = [jnp.zeros((SUBLANES, tq), F32) for _ in range(n_grp)]
    for i in range(n_sel):
        gi, li = divmod(i, SUBLANES)
        row = grp[gi][li:li + 1, :]
        for g in range(n_grp):
            if g > gi:
                beaten[g] = beaten[g] + jnp.where(row >= grp[g], 1.0, 0.0)
            elif g < gi:
                beaten[g] = beaten[g] + jnp.where(row > grp[g], 1.0, 0.0)
            else:
                beaten[g] = (beaten[g] + jnp.where(row > grp[g], 1.0, 0.0)
                             + jnp.where(row == grp[g], later[li], 0.0))
    return jnp.concatenate([jnp.where(b < n_pick, 0.0, MASKED) for b in beaten], axis=0)


def _nsa_kernel(q_ref, kc_ref, vc_ref, ks_ref, vs_ref, kw_ref, vw_ref, gate_ref, selmap_t_ref, o_ref,
                *, n_sel, n_pick):
    hpg = NSA_HPG
    tq = NSA_TQ
    c = pl.program_id(2)
    q_t = q_ref[0]
    q8 = jnp.concatenate([q_t[:, n * HEAD_DIM:(n + 1) * HEAD_DIM] for n in range(hpg)], axis=0)
    t_pos = c * tq + lax.broadcasted_iota(jnp.int32, (tq, 1), 0)
    t_lane = c * tq + lax.broadcasted_iota(jnp.int32, (1, tq), 1)

    kc = kc_ref[0, 0]
    n_cmp_pad = kc.shape[0]
    k_idx = lax.broadcasted_iota(jnp.int32, (1, n_cmp_pad), 1)
    vis_c = (CMP_STRIDE * k_idx + (CMP_BLOCK - 1) <= t_pos) & (k_idx < n_cmp_pad - 1)
    p_c, r_c = _softmax_parts(_dot_nt(q8, kc).reshape(hpg, tq, n_cmp_pad), jnp.where(vis_c, 0.0, MASKED))
    o_c = _dot(p_c.reshape(hpg * tq, n_cmp_pad).astype(BF16), vc_ref[0, 0]).reshape(hpg, tq, HEAD_DIM) * r_c

    p_sum = jnp.sum(p_c * r_c, axis=0)
    p_hi = p_sum.astype(BF16)
    p_lo = (p_sum - p_hi.astype(F32)).astype(BF16)
    selmap_t = selmap_t_ref[...]
    imp_t = _dot_nt(selmap_t, p_hi) + _dot_nt(selmap_t, p_lo)
    blk = lax.broadcasted_iota(jnp.int32, (LANES, tq), 0)
    cur = t_lane >> _SEL_SHIFT
    forced = (blk == 0) | (blk == cur) | (blk == cur - 1)
    score_t = jnp.where(forced, SEL_BIG, jnp.where(blk <= cur, imp_t, -SEL_BIG))
    unsel_t = _topk_unselected(score_t, n_sel, n_pick)
    unsel_t = jnp.concatenate([unsel_t, jnp.zeros((LANES - n_sel, tq), F32)], axis=0)
    unsel = unsel_t.T.astype(BF16)
    q_aug = jnp.concatenate([q8, jnp.concatenate([unsel] * hpg, axis=0)], axis=1)

    def sel_tile(kt, size, carry, bias):
        m_old, l_old, acc = carry
        base = pl.multiple_of(kt * size, size)
        k = ks_ref[0, pl.ds(base, size), :]
        v = vs_ref[0, pl.ds(base, size), :]
        s = _dot_nt(q_aug, k).reshape(hpg, tq, size)
        if bias is not None:
            s = s + bias[None]
        m_new = jnp.maximum(m_old, jnp.max(s, -1, keepdims=True))
        alpha = jnp.exp2(m_old - m_new)
        p = jnp.exp2(s - m_new)
        l_new = alpha * l_old + jnp.sum(p, -1, keepdims=True)
        pv = _dot(p.reshape(hpg * tq, size).astype(BF16), v).reshape(hpg, tq, HEAD_DIM)
        return m_new, l_new, alpha * acc + pv

    n_full = c // (SEL_TILE // tq)
    n_wide = n_full // SEL_WIDE
    carry = (jnp.full((hpg, tq, 1), M_INIT, F32), jnp.zeros((hpg, tq, 1), F32),
             jnp.zeros((hpg, tq, HEAD_DIM), F32))
    carry = lax.fori_loop(0, n_wide, lambda kt, cr: sel_tile(kt, SEL_WIDE * SEL_TILE, cr, None), carry)
    carry = lax.fori_loop(n_wide * SEL_WIDE, n_full, lambda kt, cr: sel_tile(kt, SEL_TILE, cr, None), carry)
    kp_last = n_full * SEL_TILE + lax.broadcasted_iota(jnp.int32, (1, SEL_TILE), 1)
    _, l_s, acc_s = sel_tile(n_full, SEL_TILE, carry, jnp.where(kp_last <= t_pos, 0.0, MASKED))
    o_s = acc_s * (1.0 / jnp.maximum(l_s, 1e-30))

    span = WINDOW + tq
    start = pl.multiple_of(jnp.maximum(c * tq - WINDOW, 0), tq)
    kw = kw_ref[0, pl.ds(start, span), :]
    vw = vw_ref[0, pl.ds(start, span), :]
    kp = start + lax.broadcasted_iota(jnp.int32, (1, span), 1)
    vis_w = (kp <= t_pos) & (t_pos - kp < WINDOW)
    p_w, r_w = _softmax_parts(_dot_nt(q8, kw).reshape(hpg, tq, span), jnp.where(vis_w, 0.0, MASKED))
    o_w = _dot(p_w.reshape(hpg * tq, span).astype(BF16), vw).reshape(hpg, tq, HEAD_DIM) * r_w

    gates = gate_ref[0]
    for n in range(hpg):
        out = (gates[:, n:n + 1] * o_c[n]
               + gates[:, hpg + n:hpg + n + 1] * o_s[n]
               + gates[:, 2 * hpg + n:2 * hpg + n + 1] * o_w[n])
        o_ref[0, :, n * HEAD_DIM:(n + 1) * HEAD_DIM] = out.astype(o_ref.dtype)


def _nsa(q, cmp_kv, rest, gates, selmap_t, *, n_sel, n_pick):
    bsz, seq, _ = q.shape
    g = NSA_KV_GROUPS
    n_cmp_pad = cmp_kv.shape[2]
    gw = NSA_HPG * HEAD_DIM
    assert n_sel % SUBLANES == 0 and n_sel <= LANES and seq >= WINDOW + NSA_TQ
    assert SEL_TILE % NSA_TQ == 0 and seq % SEL_TILE == 0
    kv_spec = lambda off: pl.BlockSpec((1, seq, HEAD_DIM), lambda b, gi, c: (b, 0, off + gi))
    return pl.pallas_call(
        functools.partial(_nsa_kernel, n_sel=n_sel, n_pick=n_pick),
        grid=(bsz, g, seq // NSA_TQ),
        in_specs=[pl.BlockSpec((1, NSA_TQ, gw), lambda b, gi, c: (b, c, gi)),
                  pl.BlockSpec((1, 1, n_cmp_pad, HEAD_DIM), lambda b, gi, c: (gi, b, 0, 0)),
                  pl.BlockSpec((1, 1, n_cmp_pad, HEAD_DIM), lambda b, gi, c: (g + gi, b, 0, 0)),
                  pl.BlockSpec((1, seq, 2 * HEAD_DIM), lambda b, gi, c: (b, 0, gi)),
                  kv_spec(2 * g), kv_spec(3 * g), kv_spec(4 * g),
                  pl.BlockSpec((1, NSA_TQ, LANES), lambda b, gi, c: (b, c, gi)),
                  pl.BlockSpec(selmap_t.shape, lambda b, gi, c: (0, 0))],
        out_specs=pl.BlockSpec((1, NSA_TQ, gw), lambda b, gi, c: (b, c, gi)),
        out_shape=jax.ShapeDtypeStruct(q.shape, BF16),
        compiler_params=_params("parallel", "parallel", "arbitrary"),
        name="nsa_attention",
    )(q, cmp_kv, cmp_kv, rest, rest, rest, rest, gates, selmap_t)


def _sgu_kernel(x_ref, w_ref, g_ref, b_ref, ws_ref, bs_ref, o_ref, *, chunks):
    gd = SGU_WIDTH // SGU_GROUPS
    row = lax.broadcasted_iota(jnp.int32, (SGU_CHUNK, SGU_CHUNK), 0)
    col = lax.broadcasted_iota(jnp.int32, (SGU_CHUNK, SGU_CHUNK), 1)
    causal = col <= row
    bs = bs_ref[...]
    x = x_ref[...]
    v = jnp.concatenate(
        [_gelu(_dot(x, w_ref[:, SGU_WIDTH + j * MXU_COLS:SGU_WIDTH + (j + 1) * MXU_COLS]))
         for j in range(SGU_WIDTH // MXU_COLS)], axis=1)
    vn = _layer_norm(v, g_ref[...], b_ref[...]).astype(BF16)
    for gi in range(SGU_GROUPS):
        cols = slice(gi * gd, (gi + 1) * gd)
        u = _gelu(_dot(x, w_ref[:, cols]))
        w = jnp.where(causal, ws_ref[gi], 0.0).astype(BF16)
        for ch in range(chunks):
            rows = slice(ch * SGU_CHUNK, (ch + 1) * SGU_CHUNK)
            z = _dot(w, vn[rows, cols]) + bs[:, gi:gi + 1]
            o_ref[rows, cols] = (u[rows] * z).astype(o_ref.dtype)


def _sgu(x_bf, w_uv, ln_g, ln_b, w_s, b_s_t, *, chunks=4):
    m, d = x_bf.shape
    tm = chunks * SGU_CHUNK
    return pl.pallas_call(
        functools.partial(_sgu_kernel, chunks=chunks),
        grid=(m // tm,),
        in_specs=[pl.BlockSpec((tm, d), lambda i: (i, 0)),
                  pl.BlockSpec(w_uv.shape, lambda i: (0, 0), pipeline_mode=pl.Buffered(1)),
                  pl.BlockSpec((1, SGU_WIDTH), lambda i: (0, 0)),
                  pl.BlockSpec((1, SGU_WIDTH), lambda i: (0, 0)),
                  pl.BlockSpec(w_s.shape, lambda i: (0, 0, 0)),
                  pl.BlockSpec(b_s_t.shape, lambda i: (0, 0))],
        out_specs=pl.BlockSpec((tm, SGU_WIDTH), lambda i: (i, 0)),
        out_shape=jax.ShapeDtypeStruct((m, SGU_WIDTH), BF16),
        compiler_params=_params("parallel"),
        name="sgu",
    )(x_bf, w_uv, ln_g, ln_b, w_s, b_s_t)


def _merge_kernel(x_ref, a_ref, s_ref, wga_ref, wgb_ref, wa_ref, wb_ref, o_ref):
    x = x_ref[...]
    ga = _sigmoid(_dot(x, wga_ref[...]))
    gb = _sigmoid(_dot(x, wgb_ref[...]))
    o_ref[...] = (ga * _dot(a_ref[...], wa_ref[...]) + gb * _dot(s_ref[...], wb_ref[...])).astype(o_ref.dtype)


def _merge(x_bf, o_nsa, o_sgu, w_merge, w_a, w_b, *, tm=1024, tn=256):
    m, k = x_bf.shape
    n = w_a.shape[1]
    nb = n // tn
    row = pl.BlockSpec((tm, k), lambda i, j: (i, 0))
    col = pl.BlockSpec((k, tn), lambda i, j: (0, j))
    return pl.pallas_call(
        _merge_kernel,
        grid=(m // tm, nb),
        in_specs=[row, row, row, col, pl.BlockSpec((k, tn), lambda i, j: (0, nb + j)), col, col],
        out_specs=pl.BlockSpec((tm, tn), lambda i, j: (i, j)),
        out_shape=jax.ShapeDtypeStruct((m, n), BF16),
        compiler_params=_params("parallel", "parallel"),
        name="merge",
    )(x_bf, o_nsa, o_sgu, w_merge, w_merge, w_a, w_b)


def _proj_ln_kernel(a_ref, w_ref, x_ref, g_ref, b_ref, o_ref, obf_ref):
    y = DN_ALPHA * x_ref[...] + _dot(a_ref[...], w_ref[...])
    out = _layer_norm(y, g_ref[...], b_ref[...])
    o_ref[...] = out
    obf_ref[...] = out.astype(BF16)


def _proj_ln(a_bf, w, x, ln_g, ln_b, *, tm=512):
    m, k = a_bf.shape
    n = w.shape[1]
    vec = pl.BlockSpec((1, n), lambda i: (0, 0))
    return pl.pallas_call(
        _proj_ln_kernel,
        grid=(m // tm,),
        in_specs=[pl.BlockSpec((tm, k), lambda i: (i, 0)),
                  pl.BlockSpec((k, n), lambda i: (0, 0)),
                  pl.BlockSpec((tm, n), lambda i: (i, 0)), vec, vec],
        out_specs=[pl.BlockSpec((tm, n), lambda i: (i, 0)), pl.BlockSpec((tm, n), lambda i: (i, 0))],
        out_shape=[jax.ShapeDtypeStruct((m, n), F32), jax.ShapeDtypeStruct((m, n), BF16)],
        compiler_params=_params("parallel"),
        name="proj_ln",
    )(a_bf, w, x, ln_g, ln_b)


def _xattn_kernel(xbf_ref, x_ref, wq_ref, kv_ref, wo_ref, g_ref, b_ref, o_ref):
    q = (_dot(xbf_ref[0], wq_ref[...]) * HEAD_DIM ** -0.5).astype(BF16)
    kv = kv_ref[0]
    heads = []
    for h in range(MEM_HEADS):
        cols = slice(h * HEAD_DIM, (h + 1) * HEAD_DIM)
        s = _dot_nt(q[:, cols], kv[:, cols])
        p = jnp.exp(s - jnp.max(s, -1, keepdims=True))
        p = p / jnp.sum(p, -1, keepdims=True)
        heads.append(_dot(p.astype(BF16), kv[:, MEM_WIDTH + h * HEAD_DIM:MEM_WIDTH + (h + 1) * HEAD_DIM]))
    o = jnp.concatenate(heads, axis=1).astype(BF16)
    y = DN_ALPHA * x_ref[0] + _dot(o, wo_ref[...])
    o_ref[0] = _layer_norm(y, g_ref[...], b_ref[...])


def _xattn(x_bf, x, w_xq, mem_kv, w_xo, ln_g, ln_b, *, tm=512):
    bsz, seq, d = x.shape
    mem_len = mem_kv.shape[1]
    vec = pl.BlockSpec((1, d), lambda b, i: (0, 0))
    return pl.pallas_call(
        _xattn_kernel,
        grid=(bsz, seq // tm),
        in_specs=[pl.BlockSpec((1, tm, d), lambda b, i: (b, i, 0)),
                  pl.BlockSpec((1, tm, d), lambda b, i: (b, i, 0)),
                  pl.BlockSpec(w_xq.shape, lambda b, i: (0, 0)),
                  pl.BlockSpec((1, mem_len, 2 * MEM_WIDTH), lambda b, i: (b, 0, 0)),
                  pl.BlockSpec(w_xo.shape, lambda b, i: (0, 0)), vec, vec],
        out_specs=pl.BlockSpec((1, tm, d), lambda b, i: (b, i, 0)),
        out_shape=jax.ShapeDtypeStruct(x.shape, F32),
        compiler_params=_params("parallel", "parallel"),
        name="xattn_ln",
    )(x_bf, x, w_xq, mem_kv, w_xo, ln_g, ln_b)


def _router_kernel(x_ref, whi_ref, wlo_ref, b_ref, gate_ref, exp_ref, rank_ref, cnt_ref, seen_ref):
    @pl.when(pl.program_id(0) == 0)
    def _():
        seen_ref[...] = jnp.zeros_like(seen_ref)

    x = x_ref[...]
    x_hi = x.astype(BF16)
    x_lo = (x - x_hi.astype(F32)).astype(BF16)
    whi = whi_ref[...]
    logits = _dot(x_hi, whi) + _dot(x_hi, wlo_ref[...]) + _dot(x_lo, whi) + b_ref[...]
    rows = logits.shape[0]
    lane = lax.broadcasted_iota(jnp.int32, (rows, LANES), 1)
    lane_f = lane.astype(F32)
    is_grp = lane < N_GROUPS
    lg = jnp.where(is_grp, logits, MASKED)
    eg = jnp.where(is_grp, jnp.exp(lg - jnp.max(lg, -1, keepdims=True)), 0.0)
    p_grp = eg / jnp.sum(eg, -1, keepdims=True)
    g_grp = jnp.max(p_grp, -1, keepdims=True)
    grp = jnp.min(jnp.where((p_grp == g_grp) & is_grp, lane_f, 1e9), -1, keepdims=True).astype(jnp.int32)
    lo = N_GROUPS + grp * EXPERTS_PER_GROUP
    in_grp = (lane >= lo) & (lane < lo + EXPERTS_PER_GROUP)
    le = jnp.where(in_grp, logits, MASKED)
    ee = jnp.where(in_grp, jnp.exp(le - jnp.max(le, -1, keepdims=True)), 0.0)
    p_exp = jnp.where(in_grp, ee / jnp.sum(ee, -1, keepdims=True), -1.0)
    v1 = jnp.max(p_exp, -1, keepdims=True)
    i1 = jnp.min(jnp.where(p_exp == v1, lane_f, 1e9), -1, keepdims=True)
    p_rest = jnp.where(lane_f == i1, -1.0, p_exp)
    v2 = jnp.max(p_rest, -1, keepdims=True)
    i2 = jnp.min(jnp.where(p_rest == v2, lane_f, 1e9), -1, keepdims=True)
    denom = v1 + v2
    gate_ref[...] = jnp.where(lane == 0, g_grp * v1 / denom, jnp.where(lane == 1, g_grp * v2 / denom, 0.0))
    exp_ref[...] = jnp.where(lane == 0, i1, jnp.where(lane == 1, i2, float(N_GROUPS))).astype(jnp.int32) - N_GROUPS

    onehot = jnp.where((lane_f == i1) | (lane_f == i2), 1.0, 0.0)
    tok_r = lax.broadcasted_iota(jnp.int32, (rows, rows), 0)
    tok_c = lax.broadcasted_iota(jnp.int32, (rows, rows), 1)
    earlier = jnp.where(tok_c < tok_r, 1.0, 0.0).astype(BF16)
    before = _dot(earlier, onehot.astype(BF16)) + seen_ref[...]
    r1 = jnp.sum(jnp.where(lane_f == i1, before, 0.0), -1, keepdims=True)
    r2 = jnp.sum(jnp.where(lane_f == i2, before, 0.0), -1, keepdims=True)
    rank_ref[...] = jnp.where(lane == 0, r1, jnp.where(lane == 1, r2, 0.0)).astype(jnp.int32)
    seen_ref[...] += jnp.sum(onehot, axis=0, keepdims=True)
    cnt_ref[...] = seen_ref[...]


def _router(x, w_hi, w_lo, bias, *, tm=512):
    m, d = x.shape
    wspec = pl.BlockSpec((d, LANES), lambda i: (0, 0))
    vec = pl.BlockSpec((1, LANES), lambda i: (0, 0))
    tile = pl.BlockSpec((tm, LANES), lambda i: (i, 0))
    return pl.pallas_call(
        _router_kernel,
        grid=(m // tm,),
        in_specs=[pl.BlockSpec((tm, d), lambda i: (i, 0)), wspec, wspec, vec],
        out_specs=[tile, tile, tile, vec],
        out_shape=[jax.ShapeDtypeStruct((m, LANES), F32), jax.ShapeDtypeStruct((m, LANES), jnp.int32),
                   jax.ShapeDtypeStruct((m, LANES), jnp.int32), jax.ShapeDtypeStruct((1, LANES), F32)],
        scratch_shapes=[pltpu.VMEM((1, LANES), F32)],
        compiler_params=_params("arbitrary"),
        name="router",
    )(x, w_hi, w_lo, bias)


GATHER_UNROLL = 8


def _ffn_kernel(exp_ref, on_ref, first_ref, slot_ref, next_ref, tok_ref, dst_ref, x_hbm, wg_hbm, wu_hbm, wd_hbm,
                y_hbm, wg_buf, wu_buf, wd_buf, h_buf, y_buf, sem, row_sem, out_sem):
    i = pl.program_id(0)
    n_blk = pl.num_programs(0)
    slot = slot_ref[i]
    rows = h_buf.shape[1]

    def out_copy(blk, r):
        dst = dst_ref[blk * rows + r]
        return pltpu.make_async_copy(y_buf.at[blk % 2, pl.ds(r, 1), :], y_hbm.at[pl.ds(dst, 1), :],
                                     out_sem.at[blk % 2])

    def weight_copies(e, s):
        return (pltpu.make_async_copy(wg_hbm.at[e], wg_buf.at[s], sem.at[s, 0]),
                pltpu.make_async_copy(wu_hbm.at[e], wu_buf.at[s], sem.at[s, 1]),
                pltpu.make_async_copy(wd_hbm.at[e], wd_buf.at[s], sem.at[s, 2]))

    def row_copy(blk, r):
        tok = tok_ref[blk * rows + r]
        return pltpu.make_async_copy(x_hbm.at[pl.ds(tok, 1), :], h_buf.at[blk % 2, pl.ds(r, 1), :],
                                     row_sem.at[blk % 2])

    def for_rows(blk, fn):
        def body(g, carry):
            for u in range(GATHER_UNROLL):
                fn(blk, g * GATHER_UNROLL + u)
            return carry
        lax.fori_loop(0, on_ref[blk], body, 0)

    @pl.when(i == 0)
    def _():
        for cp in weight_copies(exp_ref[0], 0):
            cp.start()
        h_buf[...] = jnp.zeros_like(h_buf)
        for_rows(0, lambda b, r: row_copy(b, r).start())
        y_buf[...] = jnp.zeros_like(y_buf)
        spare = y_hbm.shape[0] - 2 * rows
        for s in range(2):
            cp = pltpu.make_async_copy(y_buf.at[s], y_hbm.at[pl.ds(spare + s * rows, rows), :], out_sem.at[s])
            cp.start()
            cp.wait()

    prev2 = jnp.maximum(i - 2, 0)

    @pl.when((i >= 2) & (on_ref[prev2] != 0))
    def _():
        for_rows(i - 2, lambda b, r: out_copy(b, r).wait())

    nxt_blk = jnp.minimum(i + 1, n_blk - 1)

    @pl.when((i + 1 < n_blk) & (on_ref[nxt_blk] != 0))
    def _():
        for_rows(i + 1, lambda b, r: row_copy(b, r).start())

    @pl.when(first_ref[i] != 0)
    def _():
        nxt = next_ref[i]

        @pl.when(nxt >= 0)
        def _():
            for cp in weight_copies(nxt, 1 - slot):
                cp.start()

        for cp in weight_copies(exp_ref[i], slot):
            cp.wait()

    @pl.when(on_ref[i] != 0)
    def _():
        for_rows(i, lambda b, r: row_copy(b, r).wait())
        h = h_buf[i % 2].astype(BF16)
        a = _dot(h, wg_buf[slot].astype(BF16))
        a = a * _sigmoid(a) * _dot(h, wu_buf[slot].astype(BF16))
        y_buf[i % 2] = _dot(a.astype(BF16), wd_buf[slot].astype(BF16))
        for_rows(i, lambda b, r: out_copy(b, r).start())

    @pl.when(i == n_blk - 1)
    def _():
        prev1 = jnp.maximum(i - 1, 0)

        @pl.when((i >= 1) & (on_ref[prev1] != 0))
        def _():
            for_rows(i - 1, lambda b, r: out_copy(b, r).wait())

        @pl.when(on_ref[i] != 0)
        def _():
            for_rows(i, lambda b, r: out_copy(b, r).wait())


def _expert_ffn(blk_exp, blk_on, row_tok, row_dst, x, w_gate, w_up, w_down):
    p_rows = row_tok.shape[0]
    n_tok, d = x.shape
    n_exp, _, ff = w_gate.shape
    n_blocks = p_rows // MOE_BLOCK
    prev_exp = jnp.concatenate([jnp.full((1,), -1, jnp.int32), blk_exp[:-1]])
    first = ((blk_on != 0) & (blk_exp != prev_exp)).astype(jnp.int32)
    seg_id = jnp.cumsum(first) - 1
    seg_ids = jnp.arange(n_exp + 1)
    hit = (first[:, None] != 0) & (seg_id[:, None] == seg_ids[None, :])
    seg_exp = jnp.sum(jnp.where(hit, blk_exp[:, None] + 1, 0), 0) - 1
    blk_next = jnp.sum(jnp.where(seg_ids[None, :] == seg_id[:, None] + 1, seg_exp[None, :] + 1, 0), 1) - 1
    blk_slot = jnp.maximum(seg_id, 0) % 2
    grid_spec = pltpu.PrefetchScalarGridSpec(
        num_scalar_prefetch=7,
        grid=(n_blocks,),
        in_specs=[pl.BlockSpec(memory_space=pl.ANY)] * 4,
        out_specs=pl.BlockSpec(memory_space=pl.ANY),
        scratch_shapes=[pltpu.VMEM((2, d, ff), F32), pltpu.VMEM((2, d, ff), F32), pltpu.VMEM((2, ff, d), F32),
                        pltpu.VMEM((2, MOE_BLOCK, d), F32), pltpu.VMEM((2, MOE_BLOCK, d), F32),
                        pltpu.SemaphoreType.DMA((2, 3)), pltpu.SemaphoreType.DMA((2,)),
                        pltpu.SemaphoreType.DMA((2,))],
    )
    return pl.pallas_call(
        _ffn_kernel,
        grid_spec=grid_spec,
        out_shape=jax.ShapeDtypeStruct((EXPERT_TOPK * n_tok + 2 * MOE_BLOCK, d), F32),
        compiler_params=_params("arbitrary"),
        name="expert_ffn",
    )(blk_exp, blk_on, first, blk_slot.astype(jnp.int32), blk_next.astype(jnp.int32), row_tok, row_dst,
      x, w_gate, w_up, w_down)


def _combine_ln_kernel(x_ref, y0_ref, y1_ref, gate_ref, g_ref, b_ref, o_ref):
    gates = gate_ref[...]
    moe = y0_ref[...] * gates[:, 0:1] + y1_ref[...] * gates[:, 1:2]
    o_ref[...] = _layer_norm(DN_ALPHA * x_ref[...] + moe, g_ref[...], b_ref[...])


def _combine_ln(x, y_tok, gates, ln_g, ln_b, *, tm=256):
    m, d = x.shape
    row = pl.BlockSpec((tm, d), lambda i: (i, 0))
    vec = pl.BlockSpec((1, d), lambda i: (0, 0))
    return pl.pallas_call(
        _combine_ln_kernel,
        grid=(m // tm,),
        in_specs=[row, row, pl.BlockSpec((tm, d), lambda i: (m // tm + i, 0)),
                  pl.BlockSpec((tm, LANES), lambda i: (i, 0)), vec, vec],
        out_specs=row,
        out_shape=jax.ShapeDtypeStruct((m, d), F32),
        compiler_params=_params("parallel"),
        name="combine_ln",
    )(x, y_tok, y_tok, gates, ln_g, ln_b)


def _layer(x, mem, w_in, cmp_pe_k, cmp_w1_k, cmp_w2_k, cmp_pe_v, cmp_w1_v, cmp_w2_v,
           sgu_ln_g, sgu_ln_b, sgu_w_s, sgu_b_s, w_branch_a, w_branch_b, w_o, ln1_g, ln1_b,
           w_xq, w_xk, w_xv, w_xo, ln2_g, ln2_b,
           w_router_grp, b_router_grp, w_router_exp, b_router_exp,
           w_exp_gate, w_exp_up, w_exp_down, ln3_g, ln3_b):
    bsz, seq, d = x.shape
    n_tok = bsz * seq
    g = NSA_KV_GROUPS
    hpg = NSA_HPG
    xf = x.reshape(n_tok, d)
    x_bf = xf.astype(BF16)

    c0 = Q_WIDTH
    c1 = c0 + KV_WIDTH
    c2 = c1 + NSA_GATE_WIDTH
    c3 = c2 + 2 * SGU_WIDTH
    w_q = w_in[:, :c0].astype(BF16)
    w_gate = w_in[:, c1:c2].reshape(d, g, hpg, 3).transpose(0, 1, 3, 2).reshape(d, g, 3 * hpg)
    w_gate = jnp.pad(w_gate, ((0, 0), (0, 0), (0, LANES - 3 * hpg))).reshape(d, g * LANES)
    w_kvg = jnp.concatenate([w_in[:, c0:c1], w_gate], axis=1).astype(BF16)
    w_uv = w_in[:, c2:c3].astype(BF16)
    w_merge = w_in[:, c3:].astype(BF16)

    cos, sin = _rope_tables(jnp.arange(seq, dtype=jnp.int32))
    q = _q_proj(x_bf, w_q, cos, sin, seq).reshape(bsz, seq, Q_WIDTH)
    kv_cmp, kv_rest, nsa_gates = _kv_proj(x_bf, w_kvg, cos, sin, seq)

    n_chunks = seq // CMP_STRIDE
    pe = jnp.stack([cmp_pe_k, cmp_pe_v])
    w1 = jnp.stack([cmp_w1_k, cmp_w1_v]).astype(BF16)
    w2 = jnp.stack([cmp_w2_k, cmp_w2_v]).astype(BF16)
    cmp_end = CMP_STRIDE * jnp.arange(n_chunks, dtype=jnp.int32) + CMP_BLOCK - 1
    ccos, csin = _rope_tables(cmp_end)
    cmp_cos = jnp.stack([ccos, jnp.ones_like(ccos)])
    cmp_sin = jnp.stack([csin, jnp.zeros_like(csin)])
    cmp_kv = _compress(kv_cmp.reshape(bsz, seq, _N_CMP_COLS), pe, w1, w2, cmp_cos, cmp_sin)

    n_sel = seq // SEL_BLOCK
    n_pick = min(SEL_TOPK, n_sel)
    ci = CMP_STRIDE * jnp.arange(n_chunks)
    sj = SEL_BLOCK * jnp.arange(LANES)
    overlap = jnp.clip(jnp.minimum(ci[:, None] + CMP_BLOCK, sj[None, :] + SEL_BLOCK)
                       - jnp.maximum(ci[:, None], sj[None, :]), 0, None)
    selmap_t = (overlap.astype(F32) / CMP_BLOCK).astype(BF16).T
    o_nsa = _nsa(q, cmp_kv, kv_rest.reshape(bsz, seq, _N_REST_COLS),
                 nsa_gates.reshape(bsz, seq, _GATE_COLS), selmap_t, n_sel=n_sel, n_pick=n_pick)

    o_sgu = _sgu(x_bf, w_uv, sgu_ln_g.reshape(1, -1), sgu_ln_b.reshape(1, -1), sgu_w_s, sgu_b_s.T)

    merged = _merge(x_bf, o_nsa.reshape(n_tok, Q_WIDTH), o_sgu, w_merge,
                    w_branch_a.astype(BF16), w_branch_b.astype(BF16))
    x1, x1_bf = _proj_ln(merged, w_o.astype(BF16), xf, ln1_g.reshape(1, -1), ln1_b.reshape(1, -1))

    mem_len = mem.shape[1]
    w_mem = jnp.concatenate([w_xk, w_xv], axis=1).astype(BF16)
    mem_kv = _matmul(mem.reshape(bsz * mem_len, d).astype(BF16), w_mem, tm=256, tn=512, out_dtype=BF16)
    x2 = _xattn(x1_bf.reshape(bsz, seq, d), x1.reshape(bsz, seq, d), w_xq.astype(BF16),
                mem_kv.reshape(bsz, mem_len, 2 * MEM_WIDTH), w_xo.astype(BF16),
                ln2_g.reshape(1, -1), ln2_b.reshape(1, -1)).reshape(n_tok, d)

    w_r = jnp.concatenate([w_router_grp, w_router_exp], axis=1)
    w_r = jnp.pad(w_r, ((0, 0), (0, LANES - w_r.shape[1])))
    w_r_hi = w_r.astype(BF16)
    w_r_lo = (w_r - w_r_hi.astype(F32)).astype(BF16)
    b_r = jnp.concatenate([b_router_grp, b_router_exp])
    b_r = jnp.pad(b_r, (0, LANES - b_r.shape[0])).reshape(1, LANES)
    gates, experts, ranks, seen = _router(x2, w_r_hi, w_r_lo, b_r)

    nk = n_tok * EXPERT_TOPK
    counts = seen[0, N_GROUPS:N_GROUPS + N_EXPERTS].astype(jnp.int32)
    pcounts = ((counts + MOE_BLOCK - 1) // MOE_BLOCK) * MOE_BLOCK
    pend = jnp.cumsum(pcounts)
    pstart = pend - pcounts
    e_tok = experts[:, :EXPERT_TOPK]
    seg_start = jnp.sum(jnp.where(e_tok[..., None] == jnp.arange(N_EXPERTS), pstart, 0), -1)
    dest_of = seg_start + ranks[:, :EXPERT_TOPK]
    n_blocks = -(-nk // MOE_BLOCK) + N_EXPERTS
    p_rows = n_blocks * MOE_BLOCK
    row_ids = jnp.arange(p_rows, dtype=jnp.int32)
    row_asg = jnp.full((p_rows,), -1, jnp.int32).at[dest_of.reshape(nk)].set(
        jnp.arange(nk, dtype=jnp.int32), unique_indices=True)
    is_real = row_asg >= 0
    row_tok = jnp.where(is_real, row_asg // EXPERT_TOPK, row_ids % n_tok)
    spare = EXPERT_TOPK * n_tok + ((row_ids // MOE_BLOCK) % 2) * MOE_BLOCK + row_ids % MOE_BLOCK
    row_dst = jnp.where(is_real, (row_asg % EXPERT_TOPK) * n_tok + row_asg // EXPERT_TOPK, spare)
    blk_start = jnp.arange(n_blocks) * MOE_BLOCK
    blk_on = (blk_start < pend[-1]).astype(jnp.int32)
    blk_probe = jnp.minimum(blk_start, pend[-1] - 1)
    blk_exp = jnp.minimum(jnp.sum((pend[None, :] <= blk_probe[:, None]).astype(jnp.int32), -1),
                          N_EXPERTS - 1).astype(jnp.int32)
    real_end = jnp.sum(jnp.where(blk_exp[:, None] == jnp.arange(N_EXPERTS), (pstart + counts)[None, :], 0), -1)
    blk_real = jnp.clip(real_end - blk_start, 0, MOE_BLOCK) * blk_on
    blk_groups = ((blk_real + GATHER_UNROLL - 1) // GATHER_UNROLL).astype(jnp.int32)

    y_tok = _expert_ffn(blk_exp, blk_groups, row_tok, row_dst, x2, w_exp_gate, w_exp_up, w_exp_down)
    out = _combine_ln(x2, y_tok, gates, ln3_g.reshape(1, -1), ln3_b.reshape(1, -1))
    return out.reshape(bsz, seq, d)


def kernel(x, mem, w_in, cmp_pe_k, cmp_w1_k, cmp_w2_k, cmp_pe_v, cmp_w1_v, cmp_w2_v, sgu_ln_g, sgu_ln_b, sgu_w_s, sgu_b_s, w_branch_a, w_branch_b, w_o, ln1_g, ln1_b, w_xq, w_xk, w_xv, w_xo, ln2_g, ln2_b, w_router_grp, b_router_grp, w_router_exp, b_router_exp, w_exp_gate, w_exp_up, w_exp_down, ln3_g, ln3_b):
    params = (w_in, cmp_pe_k, cmp_w1_k, cmp_w2_k, cmp_pe_v, cmp_w1_v, cmp_w2_v,
              sgu_ln_g, sgu_ln_b, sgu_w_s, sgu_b_s, w_branch_a, w_branch_b, w_o, ln1_g, ln1_b,
              w_xq, w_xk, w_xv, w_xo, ln2_g, ln2_b,
              w_router_grp, b_router_grp, w_router_exp, b_router_exp,
              w_exp_gate, w_exp_up, w_exp_down, ln3_g, ln3_b)
    h = x
    for l in range(DEPTH):
        h = _layer(h, mem, *[p[l] for p in params])
    return h
```

```python
import functools
import math

import jax
import jax.numpy as jnp
from jax import lax
from jax.experimental import pallas as pl
from jax.experimental.pallas import tpu as pltpu

D_MODEL = 2048
HEAD_DIM = 128
NSA_HEADS = 16
NSA_KV_GROUPS = 2
NSA_HPG = NSA_HEADS // NSA_KV_GROUPS
CMP_BLOCK = 32
CMP_STRIDE = 16
CMP_HIDDEN = 256
SEL_BLOCK = 64
SEL_TOPK = 16
WINDOW = 512
Q_BLOCK = 128
SGU_WIDTH = 2048
SGU_GROUPS = 8
SGU_CHUNK = 128
MEM_HEADS = 4
N_GROUPS = 4
EXPERTS_PER_GROUP = 16
N_EXPERTS = N_GROUPS * EXPERTS_PER_GROUP
EXPERT_TOPK = 2
EXPERT_FF = 512
MOE_BLOCK = 128
ROPE_THETA = 10000.0
LN_EPS = 1e-5
SEL_BIG = 1e9
DEPTH = 1
DN_ALPHA = (2.0 * DEPTH) ** 0.25

Q_WIDTH = NSA_HEADS * HEAD_DIM
KV_WIDTH = 3 * 2 * NSA_KV_GROUPS * HEAD_DIM
NSA_GATE_WIDTH = 3 * NSA_HEADS
MEM_WIDTH = MEM_HEADS * HEAD_DIM

LANES = 128
VMEM_LIMIT = 56 * 1024 * 1024
MASKED = -1e30
M_INIT = -1e20
BF16 = jnp.bfloat16
F32 = jnp.float32


def _params(*sem):
    return pltpu.CompilerParams(dimension_semantics=sem, vmem_limit_bytes=VMEM_LIMIT)


def _dot(a, b):
    return jnp.dot(a, b, preferred_element_type=F32)


def _dot_nt(a, b):
    return lax.dot_general(a, b, (((1,), (1,)), ((), ())), preferred_element_type=F32)


def _gelu(x):
    return 0.5 * x * (1.0 + jnp.tanh(math.sqrt(2.0 / math.pi) * (x + 0.044715 * (x * x * x))))


def _sigmoid(x):
    return 1.0 / (1.0 + jnp.exp(-x))


def _layer_norm(x, g, b):
    mu = jnp.mean(x, -1, keepdims=True)
    xc = x - mu
    var = jnp.mean(xc * xc, -1, keepdims=True)
    return xc * lax.rsqrt(var + LN_EPS) * g + b


def _rope(t, cos_full, sin_signed):
    return t * cos_full + pltpu.roll(t, HEAD_DIM // 2, axis=1) * sin_signed


def _rope_tables(pos):
    inv = ROPE_THETA ** (-jnp.arange(0, HEAD_DIM, 2, dtype=F32) / HEAD_DIM)
    ang = pos.astype(F32)[:, None] * inv[None, :]
    c, s = jnp.cos(ang), jnp.sin(ang)
    return jnp.concatenate([c, c], -1), jnp.concatenate([-s, s], -1)


MXU_COLS = 256


def _mm_kernel(a_ref, b_ref, o_ref):
    o_ref[...] = _dot(a_ref[...], b_ref[...]).astype(o_ref.dtype)


def _matmul(a, b, *, tm, tn, out_dtype):
    m, k = a.shape
    n = b.shape[1]
    return pl.pallas_call(
        _mm_kernel,
        grid=(m // tm, n // tn),
        in_specs=[pl.BlockSpec((tm, k), lambda i, j: (i, 0)),
                  pl.BlockSpec((k, tn), lambda i, j: (0, j))],
        out_specs=pl.BlockSpec((tm, tn), lambda i, j: (i, j)),
        out_shape=jax.ShapeDtypeStruct((m, n), out_dtype),
        compiler_params=_params("parallel", "parallel"),
        name="matmul",
    )(a, b)


def _qproj_kernel(a_ref, b_ref, cos_ref, sin_ref, o_ref):
    a = a_ref[...]
    cos, sin = cos_ref[...], sin_ref[...]
    scale = HEAD_DIM ** -0.5 * math.log2(math.e)
    for j in range(o_ref.shape[1] // MXU_COLS):
        y = _dot(a, b_ref[:, j * MXU_COLS:(j + 1) * MXU_COLS])
        for h in range(MXU_COLS // HEAD_DIM):
            sl = slice(j * MXU_COLS + h * HEAD_DIM, j * MXU_COLS + (h + 1) * HEAD_DIM)
            o_ref[:, sl] = (_rope(y[:, h * HEAD_DIM:(h + 1) * HEAD_DIM], cos, sin) * scale).astype(o_ref.dtype)


def _q_proj(x_bf, w_q, cos, sin, seq, *, tm=1024, tn=1024):
    m, k = x_bf.shape
    n = w_q.shape[1]
    per_seq = seq // tm
    return pl.pallas_call(
        _qproj_kernel,
        grid=(m // tm, n // tn),
        in_specs=[pl.BlockSpec((tm, k), lambda i, j: (i, 0)),
                  pl.BlockSpec((k, tn), lambda i, j: (0, j)),
                  pl.BlockSpec((tm, HEAD_DIM), lambda i, j: (i % per_seq, 0)),
                  pl.BlockSpec((tm, HEAD_DIM), lambda i, j: (i % per_seq, 0))],
        out_specs=pl.BlockSpec((tm, tn), lambda i, j: (i, j)),
        out_shape=jax.ShapeDtypeStruct((m, n), BF16),
        compiler_params=_params("parallel", "parallel"),
        name="q_proj",
    )(x_bf, w_q, cos, sin)


_N_CMP_COLS = 2 * NSA_KV_GROUPS * HEAD_DIM
_N_REST_CHUNKS = (KV_WIDTH - _N_CMP_COLS) // HEAD_DIM
_N_REST_COLS = (_N_REST_CHUNKS + NSA_KV_GROUPS) * HEAD_DIM
_GATE_COLS = NSA_KV_GROUPS * LANES
_SEL_SHIFT = SEL_BLOCK.bit_length() - 1
assert 1 << _SEL_SHIFT == SEL_BLOCK


def _kvproj_kernel(a_ref, b_ref, cos_ref, sin_ref, cmp_ref, rest_ref, gate_ref, *, per_seq):
    g = NSA_KV_GROUPS
    y = _dot(a_ref[...], b_ref[...])
    cos, sin = cos_ref[...], sin_ref[...]
    cmp_ref[...] = y[:, :_N_CMP_COLS]
    tm = y.shape[0]
    pos = (pl.program_id(0) % per_seq) * tm + lax.broadcasted_iota(jnp.int32, (tm, LANES), 0)
    lane = lax.broadcasted_iota(jnp.int32, (tm, LANES), 1)
    blk_onehot = jnp.where(lane == (pos >> _SEL_SHIFT), 1.0, 0.0).astype(rest_ref.dtype)
    for c in range(_N_REST_CHUNKS):
        t = y[:, _N_CMP_COLS + c * HEAD_DIM:_N_CMP_COLS + (c + 1) * HEAD_DIM]
        if (c % (2 * g)) < g:
            t = _rope(t, cos, sin)
        dst = 2 * c if c < g else g + c
        rest_ref[:, dst * HEAD_DIM:(dst + 1) * HEAD_DIM] = t.astype(rest_ref.dtype)
        if c < g:
            rest_ref[:, (dst + 1) * HEAD_DIM:(dst + 2) * HEAD_DIM] = blk_onehot
    gate_ref[...] = _sigmoid(y[:, KV_WIDTH:])


def _kv_proj(x_bf, w_kvg, cos, sin, seq, *, tm=512):
    m, k = x_bf.shape
    n = w_kvg.shape[1]
    per_seq = seq // tm
    return pl.pallas_call(
        functools.partial(_kvproj_kernel, per_seq=per_seq),
        grid=(m // tm,),
        in_specs=[pl.BlockSpec((tm, k), lambda i: (i, 0)),
                  pl.BlockSpec((k, n), lambda i: (0, 0)),
                  pl.BlockSpec((tm, HEAD_DIM), lambda i: (i % per_seq, 0)),
                  pl.BlockSpec((tm, HEAD_DIM), lambda i: (i % per_seq, 0))],
        out_specs=[pl.BlockSpec((tm, _N_CMP_COLS), lambda i: (i, 0)),
                   pl.BlockSpec((tm, _N_REST_COLS), lambda i: (i, 0)),
                   pl.BlockSpec((tm, _GATE_COLS), lambda i: (i, 0))],
        out_shape=[jax.ShapeDtypeStruct((m, _N_CMP_COLS), F32),
                   jax.ShapeDtypeStruct((m, _N_REST_COLS), BF16),
                   jax.ShapeDtypeStruct((m, _GATE_COLS), F32)],
        compiler_params=_params("parallel"),
        name="kv_gate_proj",
    )(x_bf, w_kvg, cos, sin)


def _compress_kernel(kc_ref, pe_ref, w1_ref, w2_ref, cos_ref, sin_ref, o_ref):
    n_chunks = o_ref.shape[2]
    top = jnp.zeros((n_chunks, CMP_HIDDEN), F32)
    bot = jnp.zeros((n_chunks, CMP_HIDDEN), F32)
    for p in range(CMP_STRIDE):
        xp = kc_ref[0, pl.ds(p, n_chunks, stride=CMP_STRIDE), :]
        q = CMP_STRIDE + p
        top += _dot((xp + pe_ref[0, p:p + 1, :]).astype(BF16), w1_ref[0, p * HEAD_DIM:(p + 1) * HEAD_DIM, :])
        bot += _dot((xp + pe_ref[0, q:q + 1, :]).astype(BF16), w1_ref[0, q * HEAD_DIM:(q + 1) * HEAD_DIM, :])
    h = top + pltpu.roll(bot, n_chunks - 1, axis=0)
    y = _dot(_gelu(h).astype(BF16), w2_ref[0])
    o_ref[0, 0] = _rope(y, cos_ref[0], sin_ref[0]).astype(o_ref.dtype)


def _compress(kv_cmp, pe, w1, w2, cos, sin):
    bsz, seq, width = kv_cmp.shape
    four = width // HEAD_DIM
    n_chunks = seq // CMP_STRIDE
    g = NSA_KV_GROUPS
    return pl.pallas_call(
        _compress_kernel,
        grid=(four, bsz),
        in_specs=[pl.BlockSpec((1, seq, HEAD_DIM), lambda a, b: (b, 0, a)),
                  pl.BlockSpec((1, CMP_BLOCK, HEAD_DIM), lambda a, b: (a // g, 0, 0)),
                  pl.BlockSpec((1, CMP_BLOCK * HEAD_DIM, CMP_HIDDEN), lambda a, b: (a // g, 0, 0)),
                  pl.BlockSpec((1, CMP_HIDDEN, HEAD_DIM), lambda a, b: (a // g, 0, 0)),
                  pl.BlockSpec((1, n_chunks, HEAD_DIM), lambda a, b: (a // g, 0, 0)),
                  pl.BlockSpec((1, n_chunks, HEAD_DIM), lambda a, b: (a // g, 0, 0))],
        out_specs=pl.BlockSpec((1, 1, n_chunks, HEAD_DIM), lambda a, b: (a, b, 0, 0)),
        out_shape=jax.ShapeDtypeStruct((four, bsz, n_chunks, HEAD_DIM), BF16),
        compiler_params=_params("parallel", "parallel"),
        name="compress",
    )(kv_cmp, pe, w1, w2, cos, sin)


SEL_TILE = 512
SEL_WIDE = 2
NSA_TQ = 256
SUBLANES = 8


def _softmax_parts(s3, bias):
    s3 = s3 + bias[None]
    m = jnp.max(s3, -1, keepdims=True)
    m = jnp.where(m > 0.5 * MASKED, m, 0.0)
    p = jnp.exp2(s3 - m)
    return p, 1.0 / jnp.maximum(jnp.sum(p, -1, keepdims=True), 1e-30)


def _topk_unselected(score_t, n_sel, n_pick):
    tq = score_t.shape[1]
    n_grp = n_sel // SUBLANES
    grp = [score_t[g * SUBLANES:(g + 1) * SUBLANES] for g in range(n_grp)]
    row_in_grp = lax.broadcasted_iota(jnp.int32, (SUBLANES, tq), 0)
    later = [jnp.where(row_in_grp > li, 1.0, 0.0) for li in range(SUBLANES)]
    beaten = [jnp.zeros((SUBLANES, tq), F32) for _ in range(n_grp)]
    for i in range(n_sel):
        gi, li = divmod(i, SUBLANES)
        row = grp[gi][li:li + 1, :]
        for g in range(n_grp):
            if g > gi:
                beaten[g] = beaten[g] + jnp.where(row >= grp[g], 1.0, 0.0)
            elif g < gi:
                beaten[g] = beaten[g] + jnp.where(row > grp[g], 1.0, 0.0)
            else:
                beaten[g] = (beaten[g] + jnp.where(row > grp[g], 1.0, 0.0)
                             + jnp.where(row == grp[g], later[li], 0.0))
    return jnp.concatenate([jnp.where(b < n_pick, 0.0, MASKED) for b in beaten], axis=0)


def _nsa_kernel(q_ref, kc_ref, vc_ref, ks_ref, vs_ref, kw_ref, vw_ref, gate_ref, selmap_t_ref, o_ref,
                *, n_sel, n_pick):
    hpg = NSA_HPG
    tq = NSA_TQ
    c = pl.program_id(2)
    q_t = q_ref[0]
    q8 = jnp.concatenate([q_t[:, n * HEAD_DIM:(n + 1) * HEAD_DIM] for n in range(hpg)], axis=0)
    t_pos = c * tq + lax.broadcasted_iota(jnp.int32, (tq, 1), 0)
    t_lane = c * tq + lax.broadcasted_iota(jnp.int32, (1, tq), 1)

    kc = kc_ref[0, 0]
    n_cmp_pad = kc.shape[0]
    k_idx = lax.broadcasted_iota(jnp.int32, (1, n_cmp_pad), 1)
    vis_c = (CMP_STRIDE * k_idx + (CMP_BLOCK - 1) <= t_pos) & (k_idx < n_cmp_pad - 1)
    p_c, r_c = _softmax_parts(_dot_nt(q8, kc).reshape(hpg, tq, n_cmp_pad), jnp.where(vis_c, 0.0, MASKED))
    o_c = _dot(p_c.reshape(hpg * tq, n_cmp_pad).astype(BF16), vc_ref[0, 0]).reshape(hpg, tq, HEAD_DIM) * r_c

    p_sum = jnp.sum(p_c * r_c, axis=0)
    p_hi = p_sum.astype(BF16)
    p_lo = (p_sum - p_hi.astype(F32)).astype(BF16)
    selmap_t = selmap_t_ref[...]
    imp_t = _dot_nt(selmap_t, p_hi) + _dot_nt(selmap_t, p_lo)
    blk = lax.broadcasted_iota(jnp.int32, (LANES, tq), 0)
    cur = t_lane >> _SEL_SHIFT
    forced = (blk == 0) | (blk == cur) | (blk == cur - 1)
    score_t = jnp.where(forced, SEL_BIG, jnp.where(blk <= cur, imp_t, -SEL_BIG))
    unsel_t = _topk_unselected(score_t, n_sel, n_pick)
    unsel_t = jnp.concatenate([unsel_t, jnp.zeros((LANES - n_sel, tq), F32)], axis=0)
    unsel = unsel_t.T.astype(BF16)
    q_aug = jnp.concatenate([q8, jnp.concatenate([unsel] * hpg, axis=0)], axis=1)

    def sel_tile(kt, size, carry, bias):
        m_old, l_old, acc = carry
        base = pl.multiple_of(kt * size, size)
        k = ks_ref[0, pl.ds(base, size), :]
        v = vs_ref[0, pl.ds(base, size), :]
        s = _dot_nt(q_aug, k).reshape(hpg, tq, size)
        if bias is not None:
            s = s + bias[None]
        m_new = jnp.maximum(m_old, jnp.max(s, -1, keepdims=True))
        alpha = jnp.exp2(m_old - m_new)
        p = jnp.exp2(s - m_new)
        l_new = alpha * l_old + jnp.sum(p, -1, keepdims=True)
        pv = _dot(p.reshape(hpg * tq, size).astype(BF16), v).reshape(hpg, tq, HEAD_DIM)
        return m_new, l_new, alpha * acc + pv

    n_full = c // (SEL_TILE // tq)
    n_wide = n_full // SEL_WIDE
    carry = (jnp.full((hpg, tq, 1), M_INIT, F32), jnp.zeros((hpg, tq, 1), F32),
             jnp.zeros((hpg, tq, HEAD_DIM), F32))
    carry = lax.fori_loop(0, n_wide, lambda kt, cr: sel_tile(kt, SEL_WIDE * SEL_TILE, cr, None), carry)
    carry = lax.fori_loop(n_wide * SEL_WIDE, n_full, lambda kt, cr: sel_tile(kt, SEL_TILE, cr, None), carry)
    kp_last = n_full * SEL_TILE + lax.broadcasted_iota(jnp.int32, (1, SEL_TILE), 1)
    _, l_s, acc_s = sel_tile(n_full, SEL_TILE, carry, jnp.where(kp_last <= t_pos, 0.0, MASKED))
    o_s = acc_s * (1.0 / jnp.maximum(l_s, 1e-30))

    span = WINDOW + tq
    start = pl.multiple_of(jnp.maximum(c * tq - WINDOW, 0), tq)
    kw = kw_ref[0, pl.ds(start, span), :]
    vw = vw_ref[0, pl.ds(start, span), :]
    kp = start + lax.broadcasted_iota(jnp.int32, (1, span), 1)
    vis_w = (kp <= t_pos) & (t_pos - kp < WINDOW)
    p_w, r_w = _softmax_parts(_dot_nt(q8, kw).reshape(hpg, tq, span), jnp.where(vis_w, 0.0, MASKED))
    o_w = _dot(p_w.reshape(hpg * tq, span).astype(BF16), vw).reshape(hpg, tq, HEAD_DIM) * r_w

    gates = gate_ref[0]
    for n in range(hpg):
        out = (gates[:, n:n + 1] * o_c[n]
               + gates[:, hpg + n:hpg + n + 1] * o_s[n]
               + gates[:, 2 * hpg + n:2 * hpg + n + 1] * o_w[n])
        o_ref[0, :, n * HEAD_DIM:(n + 1) * HEAD_DIM] = out.astype(o_ref.dtype)


def _nsa(q, cmp_kv, rest, gates, selmap_t, *, n_sel, n_pick):
    bsz, seq, _ = q.shape
    g = NSA_KV_GROUPS
    n_cmp_pad = cmp_kv.shape[2]
    gw = NSA_HPG * HEAD_DIM
    assert n_sel % SUBLANES == 0 and n_sel <= LANES and seq >= WINDOW + NSA_TQ
    assert SEL_TILE % NSA_TQ == 0 and seq % SEL_TILE == 0
    kv_spec = lambda off: pl.BlockSpec((1, seq, HEAD_DIM), lambda b, gi, c: (b, 0, off + gi))
    return pl.pallas_call(
        functools.partial(_nsa_kernel, n_sel=n_sel, n_pick=n_pick),
        grid=(bsz, g, seq // NSA_TQ),
        in_specs=[pl.BlockSpec((1, NSA_TQ, gw), lambda b, gi, c: (b, c, gi)),
                  pl.BlockSpec((1, 1, n_cmp_pad, HEAD_DIM), lambda b, gi, c: (gi, b, 0, 0)),
                  pl.BlockSpec((1, 1, n_cmp_pad, HEAD_DIM), lambda b, gi, c: (g + gi, b, 0, 0)),
                  pl.BlockSpec((1, seq, 2 * HEAD_DIM), lambda b, gi, c: (b, 0, gi)),
                  kv_spec(2 * g), kv_spec(3 * g), kv_spec(4 * g),
                  pl.BlockSpec((1, NSA_TQ, LANES), lambda b, gi, c: (b, c, gi)),
                  pl.BlockSpec(selmap_t.shape, lambda b, gi, c: (0, 0))],
        out_specs=pl.BlockSpec((1, NSA_TQ, gw), lambda b, gi, c: (b, c, gi)),
        out_shape=jax.ShapeDtypeStruct(q.shape, BF16),
        compiler_params=_params("parallel", "parallel", "arbitrary"),
        name="nsa_attention",
    )(q, cmp_kv, cmp_kv, rest, rest, rest, rest, gates, selmap_t)


def _sgu_kernel(x_ref, w_ref, g_ref, b_ref, ws_ref, bs_ref, o_ref, *, chunks):
    gd = SGU_WIDTH // SGU_GROUPS
    row = lax.broadcasted_iota(jnp.int32, (SGU_CHUNK, SGU_CHUNK), 0)
    col = lax.broadcasted_iota(jnp.int32, (SGU_CHUNK, SGU_CHUNK), 1)
    causal = col <= row
    bs = bs_ref[...]
    x = x_ref[...]
    v = jnp.concatenate(
        [_gelu(_dot(x, w_ref[:, SGU_WIDTH + j * MXU_COLS:SGU_WIDTH + (j + 1) * MXU_COLS]))
         for j in range(SGU_WIDTH // MXU_COLS)], axis=1)
    vn = _layer_norm(v, g_ref[...], b_ref[...]).astype(BF16)
    for gi in range(SGU_GROUPS):
        cols = slice(gi * gd, (gi + 1) * gd)
        u = _gelu(_dot(x, w_ref[:, cols]))
        w = jnp.where(causal, ws_ref[gi], 0.0).astype(BF16)
        for ch in range(chunks):
            rows = slice(ch * SGU_CHUNK, (ch + 1) * SGU_CHUNK)
            z = _dot(w, vn[rows, cols]) + bs[:, gi:gi + 1]
            o_ref[rows, cols] = (u[rows] * z).astype(o_ref.dtype)


def _sgu(x_bf, w_uv, ln_g, ln_b, w_s, b_s_t, *, chunks=4):
    m, d = x_bf.shape
    tm = chunks * SGU_CHUNK
    return pl.pallas_call(
        functools.partial(_sgu_kernel, chunks=chunks),
        grid=(m // tm,),
        in_specs=[pl.BlockSpec((tm, d), lambda i: (i, 0)),
                  pl.BlockSpec(w_uv.shape, lambda i: (0, 0), pipeline_mode=pl.Buffered(1)),
                  pl.BlockSpec((1, SGU_WIDTH), lambda i: (0, 0)),
                  pl.BlockSpec((1, SGU_WIDTH), lambda i: (0, 0)),
                  pl.BlockSpec(w_s.shape, lambda i: (0, 0, 0)),
                  pl.BlockSpec(b_s_t.shape, lambda i: (0, 0))],
        out_specs=pl.BlockSpec((tm, SGU_WIDTH), lambda i: (i, 0)),
        out_shape=jax.ShapeDtypeStruct((m, SGU_WIDTH), BF16),
        compiler_params=_params("parallel"),
        name="sgu",
    )(x_bf, w_uv, ln_g, ln_b, w_s, b_s_t)


def _merge_kernel(x_ref, a_ref, s_ref, wga_ref, wgb_ref, wa_ref, wb_ref, o_ref):
    x = x_ref[...]
    ga = _sigmoid(_dot(x, wga_ref[...]))
    gb = _sigmoid(_dot(x, wgb_ref[...]))
    o_ref[...] = (ga * _dot(a_ref[...], wa_ref[...]) + gb * _dot(s_ref[...], wb_ref[...])).astype(o_ref.dtype)


def _merge(x_bf, o_nsa, o_sgu, w_merge, w_a, w_b, *, tm=1024, tn=256):
    m, k = x_bf.shape
    n = w_a.shape[1]
    nb = n // tn
    row = pl.BlockSpec((tm, k), lambda i, j: (i, 0))
    col = pl.BlockSpec((k, tn), lambda i, j: (0, j))
    return pl.pallas_call(
        _merge_kernel,
        grid=(m // tm, nb),
        in_specs=[row, row, row, col, pl.BlockSpec((k, tn), lambda i, j: (0, nb + j)), col, col],
        out_specs=pl.BlockSpec((tm, tn), lambda i, j: (i, j)),
        out_shape=jax.ShapeDtypeStruct((m, n), BF16),
        compiler_params=_params("parallel", "parallel"),
        name="merge",
    )(x_bf, o_nsa, o_sgu, w_merge, w_merge, w_a, w_b)


def _proj_ln_kernel(a_ref, w_ref, x_ref, g_ref, b_ref, o_ref, obf_ref):
    y = DN_ALPHA * x_ref[...] + _dot(a_ref[...], w_ref[...])
    out = _layer_norm(y, g_ref[...], b_ref[...])
    o_ref[...] = out
    obf_ref[...] = out.astype(BF16)


def _proj_ln(a_bf, w, x, ln_g, ln_b, *, tm=512):
    m, k = a_bf.shape
    n = w.shape[1]
    vec = pl.BlockSpec((1, n), lambda i: (0, 0))
    return pl.pallas_call(
        _proj_ln_kernel,
        grid=(m // tm,),
        in_specs=[pl.BlockSpec((tm, k), lambda i: (i, 0)),
                  pl.BlockSpec((k, n), lambda i: (0, 0)),
                  pl.BlockSpec((tm, n), lambda i: (i, 0)), vec, vec],
        out_specs=[pl.BlockSpec((tm, n), lambda i: (i, 0)), pl.BlockSpec((tm, n), lambda i: (i, 0))],
        out_shape=[jax.ShapeDtypeStruct((m, n), F32), jax.ShapeDtypeStruct((m, n), BF16)],
        compiler_params=_params("parallel"),
        name="proj_ln",
    )(a_bf, w, x, ln_g, ln_b)


def _xattn_kernel(xbf_ref, x_ref, wq_ref, kv_ref, wo_ref, g_ref, b_ref, o_ref):
    q = (_dot(xbf_ref[0], wq_ref[...]) * HEAD_DIM ** -0.5).astype(BF16)
    kv = kv_ref[0]
    heads = []
    for h in range(MEM_HEADS):
        cols = slice(h * HEAD_DIM, (h + 1) * HEAD_DIM)
        s = _dot_nt(q[:, cols], kv[:, cols])
        p = jnp.exp(s - jnp.max(s, -1, keepdims=True))
        p = p / jnp.sum(p, -1, keepdims=True)
        heads.append(_dot(p.astype(BF16), kv[:, MEM_WIDTH + h * HEAD_DIM:MEM_WIDTH + (h + 1) * HEAD_DIM]))
    o = jnp.concatenate(heads, axis=1).astype(BF16)
    y = DN_ALPHA * x_ref[0] + _dot(o, wo_ref[...])
    o_ref[0] = _layer_norm(y, g_ref[...], b_ref[...])


def _xattn(x_bf, x, w_xq, mem_kv, w_xo, ln_g, ln_b, *, tm=512):
    bsz, seq, d = x.shape
    mem_len = mem_kv.shape[1]
    vec = pl.BlockSpec((1, d), lambda b, i: (0, 0))
    return pl.pallas_call(
        _xattn_kernel,
        grid=(bsz, seq // tm),
        in_specs=[pl.BlockSpec((1, tm, d), lambda b, i: (b, i, 0)),
                  pl.BlockSpec((1, tm, d), lambda b, i: (b, i, 0)),
                  pl.BlockSpec(w_xq.shape, lambda b, i: (0, 0)),
                  pl.BlockSpec((1, mem_len, 2 * MEM_WIDTH), lambda b, i: (b, 0, 0)),
                  pl.BlockSpec(w_xo.shape, lambda b, i: (0, 0)), vec, vec],
        out_specs=pl.BlockSpec((1, tm, d), lambda b, i: (b, i, 0)),
        out_shape=jax.ShapeDtypeStruct(x.shape, F32),
        compiler_params=_params("parallel", "parallel"),
        name="xattn_ln",
    )(x_bf, x, w_xq, mem_kv, w_xo, ln_g, ln_b)


def _router_kernel(x_ref, whi_ref, wlo_ref, b_ref, gate_ref, exp_ref, rank_ref, cnt_ref, seen_ref):
    @pl.when(pl.program_id(0) == 0)
    def _():
        seen_ref[...] = jnp.zeros_like(seen_ref)

    x = x_ref[...]
    x_hi = x.astype(BF16)
    x_lo = (x - x_hi.astype(F32)).astype(BF16)
    whi = whi_ref[...]
    logits = _dot(x_hi, whi) + _dot(x_hi, wlo_ref[...]) + _dot(x_lo, whi) + b_ref[...]
    rows = logits.shape[0]
    lane = lax.broadcasted_iota(jnp.int32, (rows, LANES), 1)
    lane_f = lane.astype(F32)
    is_grp = lane < N_GROUPS
    lg = jnp.where(is_grp, logits, MASKED)
    eg = jnp.where(is_grp, jnp.exp(lg - jnp.max(lg, -1, keepdims=True)), 0.0)
    p_grp = eg / jnp.sum(eg, -1, keepdims=True)
    g_grp = jnp.max(p_grp, -1, keepdims=True)
    grp = jnp.min(jnp.where((p_grp == g_grp) & is_grp, lane_f, 1e9), -1, keepdims=True).astype(jnp.int32)
    lo = N_GROUPS + grp * EXPERTS_PER_GROUP
    in_grp = (lane >= lo) & (lane < lo + EXPERTS_PER_GROUP)
    le = jnp.where(in_grp, logits, MASKED)
    ee = jnp.where(in_grp, jnp.exp(le - jnp.max(le, -1, keepdims=True)), 0.0)
    p_exp = jnp.where(in_grp, ee / jnp.sum(ee, -1, keepdims=True), -1.0)
    v1 = jnp.max(p_exp, -1, keepdims=True)
    i1 = jnp.min(jnp.where(p_exp == v1, lane_f, 1e9), -1, keepdims=True)
    p_rest = jnp.where(lane_f == i1, -1.0, p_exp)
    v2 = jnp.max(p_rest, -1, keepdims=True)
    i2 = jnp.min(jnp.where(p_rest == v2, lane_f, 1e9), -1, keepdims=True)
    denom = v1 + v2
    gate_ref[...] = jnp.where(lane == 0, g_grp * v1 / denom, jnp.where(lane == 1, g_grp * v2 / denom, 0.0))
    exp_ref[...] = jnp.where(lane == 0, i1, jnp.where(lane == 1, i2, float(N_GROUPS))).astype(jnp.int32) - N_GROUPS

    onehot = jnp.where((lane_f == i1) | (lane_f == i2), 1.0, 0.0)
    tok_r = lax.broadcasted_iota(jnp.int32, (rows, rows), 0)
    tok_c = lax.broadcasted_iota(jnp.int32, (rows, rows), 1)
    earlier = jnp.where(tok_c < tok_r, 1.0, 0.0).astype(BF16)
    before = _dot(earlier, onehot.astype(BF16)) + seen_ref[...]
    r1 = jnp.sum(jnp.where(lane_f == i1, before, 0.0), -1, keepdims=True)
    r2 = jnp.sum(jnp.where(lane_f == i2, before, 0.0), -1, keepdims=True)
    rank_ref[...] = jnp.where(lane == 0, r1, jnp.where(lane == 1, r2, 0.0)).astype(jnp.int32)
    seen_ref[...] += jnp.sum(onehot, axis=0, keepdims=True)
    cnt_ref[...] = seen_ref[...]


def _router(x, w_hi, w_lo, bias, *, tm=512):
    m, d = x.shape
    wspec = pl.BlockSpec((d, LANES), lambda i: (0, 0))
    vec = pl.BlockSpec((1, LANES), lambda i: (0, 0))
    tile = pl.BlockSpec((tm, LANES), lambda i: (i, 0))
    return pl.pallas_call(
        _router_kernel,
        grid=(m // tm,),
        in_specs=[pl.BlockSpec((tm, d), lambda i: (i, 0)), wspec, wspec, vec],
        out_specs=[tile, tile, tile, vec],
        out_shape=[jax.ShapeDtypeStruct((m, LANES), F32), jax.ShapeDtypeStruct((m, LANES), jnp.int32),
                   jax.ShapeDtypeStruct((m, LANES), jnp.int32), jax.ShapeDtypeStruct((1, LANES), F32)],
        scratch_shapes=[pltpu.VMEM((1, LANES), F32)],
        compiler_params=_params("arbitrary"),
        name="router",
    )(x, w_hi, w_lo, bias)


GATHER_UNROLL = 8


def _ffn_kernel(exp_ref, on_ref, first_ref, slot_ref, next_ref, tok_ref, dst_ref, x_hbm, wg_hbm, wu_hbm, wd_hbm,
                y_hbm, wg_buf, wu_buf, wd_buf, h_buf, y_buf, sem, row_sem, out_sem):
    i = pl.program_id(0)
    n_blk = pl.num_programs(0)
    slot = slot_ref[i]
    rows = MOE_BLOCK

    def out_copy(blk, g, u):
        dst = dst_ref[blk * rows + g * GATHER_UNROLL + u]
        return pltpu.make_async_copy(y_buf.at[blk % 2, g, pl.ds(u, 1), :], y_hbm.at[pl.ds(dst, 1), :],
                                     out_sem.at[blk % 2])

    def weight_copies(e, s):
        return (pltpu.make_async_copy(wg_hbm.at[e], wg_buf.at[s], sem.at[s, 0]),
                pltpu.make_async_copy(wu_hbm.at[e], wu_buf.at[s], sem.at[s, 1]),
                pltpu.make_async_copy(wd_hbm.at[e], wd_buf.at[s], sem.at[s, 2]))

    def row_copy(blk, g, u):
        tok = tok_ref[blk * rows + g * GATHER_UNROLL + u]
        return pltpu.make_async_copy(x_hbm.at[pl.ds(tok, 1), :], h_buf.at[blk % 2, g, pl.ds(u, 1), :],
                                     row_sem.at[blk % 2])

    def for_rows(blk, fn):
        def body(g, carry):
            for u in range(GATHER_UNROLL):
                fn(blk, g, u)
            return carry
        lax.fori_loop(0, on_ref[blk], body, 0)

    @pl.when(i == 0)
    def _():
        for cp in weight_copies(exp_ref[0], 0):
            cp.start()
        h_buf[...] = jnp.zeros_like(h_buf)
        for_rows(0, lambda b, g, u: row_copy(b, g, u).start())
        y_buf[...] = jnp.zeros_like(y_buf)
        spare = y_hbm.shape[0] - 2 * rows
        fills = [pltpu.make_async_copy(y_buf.at[s, g],
                                       y_hbm.at[pl.ds(spare + s * rows + g * GATHER_UNROLL, GATHER_UNROLL), :],
                                       out_sem.at[s])
                 for s in range(2) for g in range(rows // GATHER_UNROLL)]
        for cp in fills:
            cp.start()
        for cp in fills:
            cp.wait()

    prev2 = jnp.maximum(i - 2, 0)

    @pl.when((i >= 2) & (on_ref[prev2] != 0))
    def _():
        for_rows(i - 2, lambda b, g, u: out_copy(b, g, u).wait())

    nxt_blk = jnp.minimum(i + 1, n_blk - 1)

    @pl.when((i + 1 < n_blk) & (on_ref[nxt_blk] != 0))
    def _():
        for_rows(i + 1, lambda b, g, u: row_copy(b, g, u).start())

    @pl.when(first_ref[i] != 0)
    def _():
        nxt = next_ref[i]

        @pl.when(nxt >= 0)
        def _():
            for cp in weight_copies(nxt, 1 - slot):
                cp.start()

        for cp in weight_copies(exp_ref[i], slot):
            cp.wait()

    @pl.when(on_ref[i] != 0)
    def _():
        for_rows(i, lambda b, g, u: row_copy(b, g, u).wait())
        h = h_buf[i % 2].reshape(rows, h_buf.shape[-1]).astype(BF16)
        a = _dot(h, wg_buf[slot].astype(BF16))
        a = a * _sigmoid(a) * _dot(h, wu_buf[slot].astype(BF16))
        y_buf[i % 2] = _dot(a.astype(BF16), wd_buf[slot].astype(BF16)).reshape(y_buf.shape[1:])
        for_rows(i, lambda b, g, u: out_copy(b, g, u).start())

    @pl.when(i == n_blk - 1)
    def _():
        prev1 = jnp.maximum(i - 1, 0)

        @pl.when((i >= 1) & (on_ref[prev1] != 0))
        def _():
            for_rows(i - 1, lambda b, g, u: out_copy(b, g, u).wait())

        @pl.when(on_ref[i] != 0)
        def _():
            for_rows(i, lambda b, g, u: out_copy(b, g, u).wait())


def _expert_ffn(blk_exp, blk_on, row_tok, row_dst, x, w_gate, w_up, w_down):
    p_rows = row_tok.shape[0]
    n_tok, d = x.shape
    n_exp, _, ff = w_gate.shape
    n_blocks = p_rows // MOE_BLOCK
    prev_exp = jnp.concatenate([jnp.full((1,), -1, jnp.int32), blk_exp[:-1]])
    first = ((blk_on != 0) & (blk_exp != prev_exp)).astype(jnp.int32)
    seg_id = jnp.cumsum(first) - 1
    seg_ids = jnp.arange(n_exp + 1)
    hit = (first[:, None] != 0) & (seg_id[:, None] == seg_ids[None, :])
    seg_exp = jnp.sum(jnp.where(hit, blk_exp[:, None] + 1, 0), 0) - 1
    blk_next = jnp.sum(jnp.where(seg_ids[None, :] == seg_id[:, None] + 1, seg_exp[None, :] + 1, 0), 1) - 1
    blk_slot = jnp.maximum(seg_id, 0) % 2
    grid_spec = pltpu.PrefetchScalarGridSpec(
        num_scalar_prefetch=7,
        grid=(n_blocks,),
        in_specs=[pl.BlockSpec(memory_space=pl.ANY)] * 4,
        out_specs=pl.BlockSpec(memory_space=pl.ANY),
        scratch_shapes=[pltpu.VMEM((2, d, ff), F32), pltpu.VMEM((2, d, ff), F32), pltpu.VMEM((2, ff, d), F32),
                        pltpu.VMEM((2, MOE_BLOCK // GATHER_UNROLL, GATHER_UNROLL, d), F32),
                        pltpu.VMEM((2, MOE_BLOCK // GATHER_UNROLL, GATHER_UNROLL, d), F32),
                        pltpu.SemaphoreType.DMA((2, 3)), pltpu.SemaphoreType.DMA((2,)),
                        pltpu.SemaphoreType.DMA((2,))],
    )
    return pl.pallas_call(
        _ffn_kernel,
        grid_spec=grid_spec,
        out_shape=jax.ShapeDtypeStruct((EXPERT_TOPK * n_tok + 2 * MOE_BLOCK, d), F32),
        compiler_params=_params("arbitrary"),
        name="expert_ffn",
    )(blk_exp, blk_on, first, blk_slot.astype(jnp.int32), blk_next.astype(jnp.int32), row_tok, row_dst,
      x, w_gate, w_up, w_down)


def _combine_ln_kernel(x_ref, y0_ref, y1_ref, gate_ref, g_ref, b_ref, o_ref):
    gates = gate_ref[...]
    moe = y0_ref[...] * gates[:, 0:1] + y1_ref[...] * gates[:, 1:2]
    o_ref[...] = _layer_norm(DN_ALPHA * x_ref[...] + moe, g_ref[...], b_ref[...])


def _combine_ln(x, y_tok, gates, ln_g, ln_b, *, tm=256):
    m, d = x.shape
    row = pl.BlockSpec((tm, d), lambda i: (i, 0))
    vec = pl.BlockSpec((1, d), lambda i: (0, 0))
    return pl.pallas_call(
        _combine_ln_kernel,
        grid=(m // tm,),
        in_specs=[row, row, pl.BlockSpec((tm, d), lambda i: (m // tm + i, 0)),
                  pl.BlockSpec((tm, LANES), lambda i: (i, 0)), vec, vec],
        out_specs=row,
        out_shape=jax.ShapeDtypeStruct((m, d), F32),
        compiler_params=_params("parallel"),
        name="combine_ln",
    )(x, y_tok, y_tok, gates, ln_g, ln_b)


def _layer(x, mem, w_in, cmp_pe_k, cmp_w1_k, cmp_w2_k, cmp_pe_v, cmp_w1_v, cmp_w2_v,
           sgu_ln_g, sgu_ln_b, sgu_w_s, sgu_b_s, w_branch_a, w_branch_b, w_o, ln1_g, ln1_b,
           w_xq, w_xk, w_xv, w_xo, ln2_g, ln2_b,
           w_router_grp, b_router_grp, w_router_exp, b_router_exp,
           w_exp_gate, w_exp_up, w_exp_down, ln3_g, ln3_b):
    bsz, seq, d = x.shape
    n_tok = bsz * seq
    g = NSA_KV_GROUPS
    hpg = NSA_HPG
    xf = x.reshape(n_tok, d)
    x_bf = xf.astype(BF16)

    c0 = Q_WIDTH
    c1 = c0 + KV_WIDTH
    c2 = c1 + NSA_GATE_WIDTH
    c3 = c2 + 2 * SGU_WIDTH
    w_q = w_in[:, :c0].astype(BF16)
    w_gate = w_in[:, c1:c2].reshape(d, g, hpg, 3).transpose(0, 1, 3, 2).reshape(d, g, 3 * hpg)
    w_gate = jnp.pad(w_gate, ((0, 0), (0, 0), (0, LANES - 3 * hpg))).reshape(d, g * LANES)
    w_kvg = jnp.concatenate([w_in[:, c0:c1], w_gate], axis=1).astype(BF16)
    w_uv = w_in[:, c2:c3].astype(BF16)
    w_merge = w_in[:, c3:].astype(BF16)

    cos, sin = _rope_tables(jnp.arange(seq, dtype=jnp.int32))
    q = _q_proj(x_bf, w_q, cos, sin, seq).reshape(bsz, seq, Q_WIDTH)
    kv_cmp, kv_rest, nsa_gates = _kv_proj(x_bf, w_kvg, cos, sin, seq)

    n_chunks = seq // CMP_STRIDE
    pe = jnp.stack([cmp_pe_k, cmp_pe_v])
    w1 = jnp.stack([cmp_w1_k, cmp_w1_v]).astype(BF16)
    w2 = jnp.stack([cmp_w2_k, cmp_w2_v]).astype(BF16)
    cmp_end = CMP_STRIDE * jnp.arange(n_chunks, dtype=jnp.int32) + CMP_BLOCK - 1
    ccos, csin = _rope_tables(cmp_end)
    cmp_cos = jnp.stack([ccos, jnp.ones_like(ccos)])
    cmp_sin = jnp.stack([csin, jnp.zeros_like(csin)])
    cmp_kv = _compress(kv_cmp.reshape(bsz, seq, _N_CMP_COLS), pe, w1, w2, cmp_cos, cmp_sin)

    n_sel = seq // SEL_BLOCK
    n_pick = min(SEL_TOPK, n_sel)
    ci = CMP_STRIDE * jnp.arange(n_chunks)
    sj = SEL_BLOCK * jnp.arange(LANES)
    overlap = jnp.clip(jnp.minimum(ci[:, None] + CMP_BLOCK, sj[None, :] + SEL_BLOCK)
                       - jnp.maximum(ci[:, None], sj[None, :]), 0, None)
    selmap_t = (overlap.astype(F32) / CMP_BLOCK).astype(BF16).T
    o_nsa = _nsa(q, cmp_kv, kv_rest.reshape(bsz, seq, _N_REST_COLS),
                 nsa_gates.reshape(bsz, seq, _GATE_COLS), selmap_t, n_sel=n_sel, n_pick=n_pick)

    o_sgu = _sgu(x_bf, w_uv, sgu_ln_g.reshape(1, -1), sgu_ln_b.reshape(1, -1), sgu_w_s, sgu_b_s.T)

    merged = _merge(x_bf, o_nsa.reshape(n_tok, Q_WIDTH), o_sgu, w_merge,
                    w_branch_a.astype(BF16), w_branch_b.astype(BF16))
    x1, x1_bf = _proj_ln(merged, w_o.astype(BF16), xf, ln1_g.reshape(1, -1), ln1_b.reshape(1, -1))

    mem_len = mem.shape[1]
    w_mem = jnp.concatenate([w_xk, w_xv], axis=1).astype(BF16)
    mem_kv = _matmul(mem.reshape(bsz * mem_len, d).astype(BF16), w_mem, tm=256, tn=512, out_dtype=BF16)
    x2 = _xattn(x1_bf.reshape(bsz, seq, d), x1.reshape(bsz, seq, d), w_xq.astype(BF16),
                mem_kv.reshape(bsz, mem_len, 2 * MEM_WIDTH), w_xo.astype(BF16),
                ln2_g.reshape(1, -1), ln2_b.reshape(1, -1)).reshape(n_tok, d)

    w_r = jnp.concatenate([w_router_grp, w_router_exp], axis=1)
    w_r = jnp.pad(w_r, ((0, 0), (0, LANES - w_r.shape[1])))
    w_r_hi = w_r.astype(BF16)
    w_r_lo = (w_r - w_r_hi.astype(F32)).astype(BF16)
    b_r = jnp.concatenate([b_router_grp, b_router_exp])
    b_r = jnp.pad(b_r, (0, LANES - b_r.shape[0])).reshape(1, LANES)
    gates, experts, ranks, seen = _router(x2, w_r_hi, w_r_lo, b_r)

    nk = n_tok * EXPERT_TOPK
    counts = seen[0, N_GROUPS:N_GROUPS + N_EXPERTS].astype(jnp.int32)
    pcounts = ((counts + MOE_BLOCK - 1) // MOE_BLOCK) * MOE_BLOCK
    pend = jnp.cumsum(pcounts)
    pstart = pend - pcounts
    e_tok = experts[:, :EXPERT_TOPK]
    seg_start = jnp.sum(jnp.where(e_tok[..., None] == jnp.arange(N_EXPERTS), pstart, 0), -1)
    dest_of = seg_start + ranks[:, :EXPERT_TOPK]
    n_blocks = -(-nk // MOE_BLOCK) + N_EXPERTS
    p_rows = n_blocks * MOE_BLOCK
    row_ids = jnp.arange(p_rows, dtype=jnp.int32)
    row_asg = jnp.full((p_rows,), -1, jnp.int32).at[dest_of.reshape(nk)].set(
        jnp.arange(nk, dtype=jnp.int32), unique_indices=True)
    is_real = row_asg >= 0
    row_tok = jnp.where(is_real, row_asg // EXPERT_TOPK, row_ids % n_tok)
    spare = EXPERT_TOPK * n_tok + ((row_ids // MOE_BLOCK) % 2) * MOE_BLOCK + row_ids % MOE_BLOCK
    row_dst = jnp.where(is_real, (row_asg % EXPERT_TOPK) * n_tok + row_asg // EXPERT_TOPK, spare)
    blk_start = jnp.arange(n_blocks) * MOE_BLOCK
    blk_on = (blk_start < pend[-1]).astype(jnp.int32)
    blk_probe = jnp.minimum(blk_start, pend[-1] - 1)
    blk_exp = jnp.minimum(jnp.sum((pend[None, :] <= blk_probe[:, None]).astype(jnp.int32), -1),
                          N_EXPERTS - 1).astype(jnp.int32)
    real_end = jnp.sum(jnp.where(blk_exp[:, None] == jnp.arange(N_EXPERTS), (pstart + counts)[None, :], 0), -1)
    blk_real = jnp.clip(real_end - blk_start, 0, MOE_BLOCK) * blk_on
    blk_groups = ((blk_real + GATHER_UNROLL - 1) // GATHER_UNROLL).astype(jnp.int32)

    y_tok = _expert_ffn(blk_exp, blk_groups, row_tok, row_dst, x2, w_exp_gate, w_exp_up, w_exp_down)
    out = _combine_ln(x2, y_tok, gates, ln3_g.reshape(1, -1), ln3_b.reshape(1, -1))
    return out.reshape(bsz, seq, d)


def kernel(x, mem, w_in, cmp_pe_k, cmp_w1_k, cmp_w2_k, cmp_pe_v, cmp_w1_v, cmp_w2_v, sgu_ln_g, sgu_ln_b, sgu_w_s, sgu_b_s, w_branch_a, w_branch_b, w_o, ln1_g, ln1_b, w_xq, w_xk, w_xv, w_xo, ln2_g, ln2_b, w_router_grp, b_router_grp, w_router_exp, b_router_exp, w_exp_gate, w_exp_up, w_exp_down, ln3_g, ln3_b):
    params = (w_in, cmp_pe_k, cmp_w1_k, cmp_w2_k, cmp_pe_v, cmp_w1_v, cmp_w2_v,
              sgu_ln_g, sgu_ln_b, sgu_w_s, sgu_b_s, w_branch_a, w_branch_b, w_o, ln1_g, ln1_b,
              w_xq, w_xk, w_xv, w_xo, ln2_g, ln2_b,
              w_router_grp, b_router_grp, w_router_exp, b_router_exp,
              w_exp_gate, w_exp_up, w_exp_down, ln3_g, ln3_b)
    h = x
    for l in range(DEPTH):
        h = _layer(h, mem, *[p[l] for p in params])
    return h
```

```python
import functools
import math

import jax
import jax.numpy as jnp
from jax import lax
from jax.experimental import pallas as pl
from jax.experimental.pallas import tpu as pltpu

HEAD_DIM = 128
NSA_HEADS = 16
NSA_KV_GROUPS = 2
NSA_HPG = NSA_HEADS // NSA_KV_GROUPS
CMP_BLOCK = 32
CMP_STRIDE = 16
CMP_HIDDEN = 256
SEL_BLOCK = 64
SEL_TOPK = 16
WINDOW = 512
SGU_WIDTH = 2048
SGU_GROUPS = 8
SGU_CHUNK = 128
MEM_HEADS = 4
N_GROUPS = 4
EXPERTS_PER_GROUP = 16
N_EXPERTS = N_GROUPS * EXPERTS_PER_GROUP
EXPERT_TOPK = 2
MOE_BLOCK = 128
ROPE_THETA = 10000.0
LN_EPS = 1e-5
SEL_BIG = 1e9
DEPTH = 1
DN_ALPHA = (2.0 * DEPTH) ** 0.25

Q_WIDTH = NSA_HEADS * HEAD_DIM
KV_WIDTH = 3 * 2 * NSA_KV_GROUPS * HEAD_DIM
NSA_GATE_WIDTH = 3 * NSA_HEADS
MEM_WIDTH = MEM_HEADS * HEAD_DIM

LANES = 128
VMEM_LIMIT = 56 * 1024 * 1024
MASKED = -1e30
M_INIT = -1e20
BF16 = jnp.bfloat16
F32 = jnp.float32


def _params(*sem):
    return pltpu.CompilerParams(dimension_semantics=sem, vmem_limit_bytes=VMEM_LIMIT)


def _dot(a, b):
    return jnp.dot(a, b, preferred_element_type=F32)


def _dot_nt(a, b):
    return lax.dot_general(a, b, (((1,), (1,)), ((), ())), preferred_element_type=F32)


def _gelu(x):
    return 0.5 * x * (1.0 + jnp.tanh(math.sqrt(2.0 / math.pi) * (x + 0.044715 * (x * x * x))))


def _sigmoid(x):
    return 1.0 / (1.0 + jnp.exp(-x))


def _layer_norm(x, g, b):
    mu = jnp.mean(x, -1, keepdims=True)
    xc = x - mu
    var = jnp.mean(xc * xc, -1, keepdims=True)
    return xc * lax.rsqrt(var + LN_EPS) * g + b


def _rope(t, cos_full, sin_signed):
    return t * cos_full + pltpu.roll(t, HEAD_DIM // 2, axis=1) * sin_signed


def _rope_tables(pos):
    inv = ROPE_THETA ** (-jnp.arange(0, HEAD_DIM, 2, dtype=F32) / HEAD_DIM)
    ang = pos.astype(F32)[:, None] * inv[None, :]
    c, s = jnp.cos(ang), jnp.sin(ang)
    return jnp.concatenate([c, c], -1), jnp.concatenate([-s, s], -1)


MXU_COLS = 256


def _mm_kernel(a_ref, b_ref, o_ref):
    o_ref[...] = _dot(a_ref[...], b_ref[...]).astype(o_ref.dtype)


def _matmul(a, b, *, tm, tn, out_dtype):
    m, k = a.shape
    n = b.shape[1]
    return pl.pallas_call(
        _mm_kernel,
        grid=(m // tm, n // tn),
        in_specs=[pl.BlockSpec((tm, k), lambda i, j: (i, 0)),
                  pl.BlockSpec((k, tn), lambda i, j: (0, j))],
        out_specs=pl.BlockSpec((tm, tn), lambda i, j: (i, j)),
        out_shape=jax.ShapeDtypeStruct((m, n), out_dtype),
        compiler_params=_params("parallel", "parallel"),
        name="matmul",
    )(a, b)


def _qproj_kernel(a_ref, b_ref, cos_ref, sin_ref, o_ref):
    a = a_ref[...]
    cos, sin = cos_ref[...], sin_ref[...]
    scale = HEAD_DIM ** -0.5 * math.log2(math.e)
    for j in range(o_ref.shape[1] // MXU_COLS):
        y = _dot(a, b_ref[:, j * MXU_COLS:(j + 1) * MXU_COLS])
        for h in range(MXU_COLS // HEAD_DIM):
            sl = slice(j * MXU_COLS + h * HEAD_DIM, j * MXU_COLS + (h + 1) * HEAD_DIM)
            o_ref[:, sl] = (_rope(y[:, h * HEAD_DIM:(h + 1) * HEAD_DIM], cos, sin) * scale).astype(o_ref.dtype)


def _q_proj(x_bf, w_q, cos, sin, seq, *, tm=1024, tn=1024):
    m, k = x_bf.shape
    n = w_q.shape[1]
    per_seq = seq // tm
    return pl.pallas_call(
        _qproj_kernel,
        grid=(m // tm, n // tn),
        in_specs=[pl.BlockSpec((tm, k), lambda i, j: (i, 0)),
                  pl.BlockSpec((k, tn), lambda i, j: (0, j)),
                  pl.BlockSpec((tm, HEAD_DIM), lambda i, j: (i % per_seq, 0)),
                  pl.BlockSpec((tm, HEAD_DIM), lambda i, j: (i % per_seq, 0))],
        out_specs=pl.BlockSpec((tm, tn), lambda i, j: (i, j)),
        out_shape=jax.ShapeDtypeStruct((m, n), BF16),
        compiler_params=_params("parallel", "parallel"),
        name="q_proj",
    )(x_bf, w_q, cos, sin)


_N_CMP_COLS = 2 * NSA_KV_GROUPS * HEAD_DIM
_N_REST_CHUNKS = (KV_WIDTH - _N_CMP_COLS) // HEAD_DIM
_N_REST_COLS = (_N_REST_CHUNKS + NSA_KV_GROUPS) * HEAD_DIM
_GATE_COLS = NSA_KV_GROUPS * LANES
_SEL_SHIFT = SEL_BLOCK.bit_length() - 1
assert 1 << _SEL_SHIFT == SEL_BLOCK


def _kvproj_kernel(a_ref, b_ref, cos_ref, sin_ref, cmp_ref, rest_ref, gate_ref, *, per_seq):
    g = NSA_KV_GROUPS
    y = _dot(a_ref[...], b_ref[...])
    cos, sin = cos_ref[...], sin_ref[...]
    cmp_ref[...] = y[:, :_N_CMP_COLS]
    tm = y.shape[0]
    pos = (pl.program_id(0) % per_seq) * tm + lax.broadcasted_iota(jnp.int32, (tm, LANES), 0)
    lane = lax.broadcasted_iota(jnp.int32, (tm, LANES), 1)
    blk_onehot = jnp.where(lane == (pos >> _SEL_SHIFT), 1.0, 0.0).astype(rest_ref.dtype)
    for c in range(_N_REST_CHUNKS):
        t = y[:, _N_CMP_COLS + c * HEAD_DIM:_N_CMP_COLS + (c + 1) * HEAD_DIM]
        if (c % (2 * g)) < g:
            t = _rope(t, cos, sin)
        dst = 2 * c if c < g else g + c
        rest_ref[:, dst * HEAD_DIM:(dst + 1) * HEAD_DIM] = t.astype(rest_ref.dtype)
        if c < g:
            rest_ref[:, (dst + 1) * HEAD_DIM:(dst + 2) * HEAD_DIM] = blk_onehot
    gate_ref[...] = _sigmoid(y[:, KV_WIDTH:])


def _kv_proj(x_bf, w_kvg, cos, sin, seq, *, tm=512):
    m, k = x_bf.shape
    n = w_kvg.shape[1]
    per_seq = seq // tm
    return pl.pallas_call(
        functools.partial(_kvproj_kernel, per_seq=per_seq),
        grid=(m // tm,),
        in_specs=[pl.BlockSpec((tm, k), lambda i: (i, 0)),
                  pl.BlockSpec((k, n), lambda i: (0, 0)),
                  pl.BlockSpec((tm, HEAD_DIM), lambda i: (i % per_seq, 0)),
                  pl.BlockSpec((tm, HEAD_DIM), lambda i: (i % per_seq, 0))],
        out_specs=[pl.BlockSpec((tm, _N_CMP_COLS), lambda i: (i, 0)),
                   pl.BlockSpec((tm, _N_REST_COLS), lambda i: (i, 0)),
                   pl.BlockSpec((tm, _GATE_COLS), lambda i: (i, 0))],
        out_shape=[jax.ShapeDtypeStruct((m, _N_CMP_COLS), F32),
                   jax.ShapeDtypeStruct((m, _N_REST_COLS), BF16),
                   jax.ShapeDtypeStruct((m, _GATE_COLS), F32)],
        compiler_params=_params("parallel"),
        name="kv_gate_proj",
    )(x_bf, w_kvg, cos, sin)


def _compress_kernel(kc_ref, pe_ref, w1_ref, w2_ref, cos_ref, sin_ref, o_ref):
    n_chunks = o_ref.shape[2]
    top = jnp.zeros((n_chunks, CMP_HIDDEN), F32)
    bot = jnp.zeros((n_chunks, CMP_HIDDEN), F32)
    for p in range(CMP_STRIDE):
        xp = kc_ref[0, pl.ds(p, n_chunks, stride=CMP_STRIDE), :]
        q = CMP_STRIDE + p
        top += _dot((xp + pe_ref[0, p:p + 1, :]).astype(BF16), w1_ref[0, p * HEAD_DIM:(p + 1) * HEAD_DIM, :])
        bot += _dot((xp + pe_ref[0, q:q + 1, :]).astype(BF16), w1_ref[0, q * HEAD_DIM:(q + 1) * HEAD_DIM, :])
    h = top + pltpu.roll(bot, n_chunks - 1, axis=0)
    y = _dot(_gelu(h).astype(BF16), w2_ref[0])
    o_ref[0, 0] = _rope(y, cos_ref[0], sin_ref[0]).astype(o_ref.dtype)


def _compress(kv_cmp, pe, w1, w2, cos, sin):
    bsz, seq, width = kv_cmp.shape
    four = width // HEAD_DIM
    n_chunks = seq // CMP_STRIDE
    g = NSA_KV_GROUPS
    return pl.pallas_call(
        _compress_kernel,
        grid=(four, bsz),
        in_specs=[pl.BlockSpec((1, seq, HEAD_DIM), lambda a, b: (b, 0, a)),
                  pl.BlockSpec((1, CMP_BLOCK, HEAD_DIM), lambda a, b: (a // g, 0, 0)),
                  pl.BlockSpec((1, CMP_BLOCK * HEAD_DIM, CMP_HIDDEN), lambda a, b: (a // g, 0, 0)),
                  pl.BlockSpec((1, CMP_HIDDEN, HEAD_DIM), lambda a, b: (a // g, 0, 0)),
                  pl.BlockSpec((1, n_chunks, HEAD_DIM), lambda a, b: (a // g, 0, 0)),
                  pl.BlockSpec((1, n_chunks, HEAD_DIM), lambda a, b: (a // g, 0, 0))],
        out_specs=pl.BlockSpec((1, 1, n_chunks, HEAD_DIM), lambda a, b: (a, b, 0, 0)),
        out_shape=jax.ShapeDtypeStruct((four, bsz, n_chunks, HEAD_DIM), BF16),
        compiler_params=_params("parallel", "parallel"),
        name="compress",
    )(kv_cmp, pe, w1, w2, cos, sin)


SEL_TILE = 512
SEL_WIDE = 2
NSA_TQ = 256
SUBLANES = 8


def _softmax_parts(s3, bias):
    s3 = s3 + bias[None]
    m = jnp.max(s3, -1, keepdims=True)
    m = jnp.where(m > 0.5 * MASKED, m, 0.0)
    p = jnp.exp2(s3 - m)
    return p, 1.0 / jnp.maximum(jnp.sum(p, -1, keepdims=True), 1e-30)


def _topk_unselected(score_t, n_sel, n_pick):
    tq = score_t.shape[1]
    n_grp = n_sel // SUBLANES
    grp = [score_t[g * SUBLANES:(g + 1) * SUBLANES] for g in range(n_grp)]
    row_in_grp = lax.broadcasted_iota(jnp.int32, (SUBLANES, tq), 0)
    later = [jnp.where(row_in_grp > li, 1.0, 0.0) for li in range(SUBLANES)]
    beaten = [jnp.zeros((SUBLANES, tq), F32) for _ in range(n_grp)]
    for i in range(n_sel):
        gi, li = divmod(i, SUBLANES)
        row = grp[gi][li:li + 1, :]
        for g in range(n_grp):
            if g > gi:
                beaten[g] = beaten[g] + jnp.where(row >= grp[g], 1.0, 0.0)
            elif g < gi:
                beaten[g] = beaten[g] + jnp.where(row > grp[g], 1.0, 0.0)
            else:
                beaten[g] = (beaten[g] + jnp.where(row > grp[g], 1.0, 0.0)
                             + jnp.where(row == grp[g], later[li], 0.0))
    return jnp.concatenate([jnp.where(b < n_pick, 0.0, MASKED) for b in beaten], axis=0)


def _nsa_kernel(q_ref, kc_ref, vc_ref, ks_ref, vs_ref, kw_ref, vw_ref, gate_ref, selmap_t_ref, o_ref,
                *, n_sel, n_pick):
    hpg = NSA_HPG
    tq = NSA_TQ
    c = pl.program_id(2)
    q_t = q_ref[0]
    q8 = jnp.concatenate([q_t[:, n * HEAD_DIM:(n + 1) * HEAD_DIM] for n in range(hpg)], axis=0)
    t_pos = c * tq + lax.broadcasted_iota(jnp.int32, (tq, 1), 0)
    t_lane = c * tq + lax.broadcasted_iota(jnp.int32, (1, tq), 1)

    kc = kc_ref[0, 0]
    n_cmp_pad = kc.shape[0]
    k_idx = lax.broadcasted_iota(jnp.int32, (1, n_cmp_pad), 1)
    vis_c = (CMP_STRIDE * k_idx + (CMP_BLOCK - 1) <= t_pos) & (k_idx < n_cmp_pad - 1)
    p_c, r_c = _softmax_parts(_dot_nt(q8, kc).reshape(hpg, tq, n_cmp_pad), jnp.where(vis_c, 0.0, MASKED))
    o_c = _dot(p_c.reshape(hpg * tq, n_cmp_pad).astype(BF16), vc_ref[0, 0]).reshape(hpg, tq, HEAD_DIM) * r_c

    p_sum = jnp.sum(p_c * r_c, axis=0)
    p_hi = p_sum.astype(BF16)
    p_lo = (p_sum - p_hi.astype(F32)).astype(BF16)
    selmap_t = selmap_t_ref[...]
    imp_t = _dot_nt(selmap_t, p_hi) + _dot_nt(selmap_t, p_lo)
    blk = lax.broadcasted_iota(jnp.int32, (LANES, tq), 0)
    cur = t_lane >> _SEL_SHIFT
    forced = (blk == 0) | (blk == cur) | (blk == cur - 1)
    score_t = jnp.where(forced, SEL_BIG, jnp.where(blk <= cur, imp_t, -SEL_BIG))
    unsel_t = _topk_unselected(score_t, n_sel, n_pick)
    unsel_t = jnp.concatenate([unsel_t, jnp.zeros((LANES - n_sel, tq), F32)], axis=0)
    unsel = unsel_t.T.astype(BF16)
    q_aug = jnp.concatenate([q8, jnp.concatenate([unsel] * hpg, axis=0)], axis=1)

    def sel_tile(kt, size, carry, bias):
        m_old, l_old, acc = carry
        base = pl.multiple_of(kt * size, size)
        k = ks_ref[0, pl.ds(base, size), :]
        v = vs_ref[0, pl.ds(base, size), :]
        s = _dot_nt(q_aug, k).reshape(hpg, tq, size)
        if bias is not None:
            s = s + bias[None]
        m_new = jnp.maximum(m_old, jnp.max(s, -1, keepdims=True))
        alpha = jnp.exp2(m_old - m_new)
        p = jnp.exp2(s - m_new)
        l_new = alpha * l_old + jnp.sum(p, -1, keepdims=True)
        pv = _dot(p.reshape(hpg * tq, size).astype(BF16), v).reshape(hpg, tq, HEAD_DIM)
        return m_new, l_new, alpha * acc + pv

    n_full = c // (SEL_TILE // tq)
    n_wide = n_full // SEL_WIDE
    carry = (jnp.full((hpg, tq, 1), M_INIT, F32), jnp.zeros((hpg, tq, 1), F32),
             jnp.zeros((hpg, tq, HEAD_DIM), F32))
    carry = lax.fori_loop(0, n_wide, lambda kt, cr: sel_tile(kt, SEL_WIDE * SEL_TILE, cr, None), carry)
    carry = lax.fori_loop(n_wide * SEL_WIDE, n_full, lambda kt, cr: sel_tile(kt, SEL_TILE, cr, None), carry)
    def causal_tile(kt, size):
        kp = kt * size + lax.broadcasted_iota(jnp.int32, (1, size), 1)
        return sel_tile(kt, size, carry, jnp.where(kp <= t_pos, 0.0, MASKED))

    _, l_s, acc_s = lax.cond(c % (SEL_TILE // tq) == 0,
                             lambda: causal_tile(c, tq), lambda: causal_tile(n_full, SEL_TILE))
    o_s = acc_s * (1.0 / jnp.maximum(l_s, 1e-30))

    span = WINDOW + tq
    start = pl.multiple_of(jnp.maximum(c * tq - WINDOW, 0), tq)
    kw = kw_ref[0, pl.ds(start, span), :]
    vw = vw_ref[0, pl.ds(start, span), :]
    kp = start + lax.broadcasted_iota(jnp.int32, (1, span), 1)
    vis_w = (kp <= t_pos) & (t_pos - kp < WINDOW)
    p_w, r_w = _softmax_parts(_dot_nt(q8, kw).reshape(hpg, tq, span), jnp.where(vis_w, 0.0, MASKED))
    o_w = _dot(p_w.reshape(hpg * tq, span).astype(BF16), vw).reshape(hpg, tq, HEAD_DIM) * r_w

    gates = gate_ref[0]
    for n in range(hpg):
        out = (gates[:, n:n + 1] * o_c[n]
               + gates[:, hpg + n:hpg + n + 1] * o_s[n]
               + gates[:, 2 * hpg + n:2 * hpg + n + 1] * o_w[n])
        o_ref[0, :, n * HEAD_DIM:(n + 1) * HEAD_DIM] = out.astype(o_ref.dtype)


def _nsa(q, cmp_kv, rest, gates, selmap_t, *, n_sel, n_pick):
    bsz, seq, _ = q.shape
    g = NSA_KV_GROUPS
    n_cmp_pad = cmp_kv.shape[2]
    gw = NSA_HPG * HEAD_DIM
    assert n_sel % SUBLANES == 0 and n_sel <= LANES and seq >= WINDOW + NSA_TQ
    assert SEL_TILE % NSA_TQ == 0 and seq % SEL_TILE == 0
    kv_spec = lambda off: pl.BlockSpec((1, seq, HEAD_DIM), lambda b, gi, c: (b, 0, off + gi))
    return pl.pallas_call(
        functools.partial(_nsa_kernel, n_sel=n_sel, n_pick=n_pick),
        grid=(bsz, g, seq // NSA_TQ),
        in_specs=[pl.BlockSpec((1, NSA_TQ, gw), lambda b, gi, c: (b, c, gi)),
                  pl.BlockSpec((1, 1, n_cmp_pad, HEAD_DIM), lambda b, gi, c: (gi, b, 0, 0)),
                  pl.BlockSpec((1, 1, n_cmp_pad, HEAD_DIM), lambda b, gi, c: (g + gi, b, 0, 0)),
                  pl.BlockSpec((1, seq, 2 * HEAD_DIM), lambda b, gi, c: (b, 0, gi)),
                  kv_spec(2 * g), kv_spec(3 * g), kv_spec(4 * g),
                  pl.BlockSpec((1, NSA_TQ, LANES), lambda b, gi, c: (b, c, gi)),
                  pl.BlockSpec(selmap_t.shape, lambda b, gi, c: (0, 0))],
        out_specs=pl.BlockSpec((1, NSA_TQ, gw), lambda b, gi, c: (b, c, gi)),
        out_shape=jax.ShapeDtypeStruct(q.shape, BF16),
        compiler_params=_params("parallel", "parallel", "arbitrary"),
        name="nsa_attention",
    )(q, cmp_kv, cmp_kv, rest, rest, rest, rest, gates, selmap_t)


def _sgu_kernel(x_ref, w_ref, g_ref, b_ref, ws_ref, bs_ref, o_ref, *, chunks):
    gd = SGU_WIDTH // SGU_GROUPS
    row = lax.broadcasted_iota(jnp.int32, (SGU_CHUNK, SGU_CHUNK), 0)
    col = lax.broadcasted_iota(jnp.int32, (SGU_CHUNK, SGU_CHUNK), 1)
    causal = col <= row
    bs = bs_ref[...]
    x = x_ref[...]
    v = jnp.concatenate(
        [_gelu(_dot(x, w_ref[:, SGU_WIDTH + j * MXU_COLS:SGU_WIDTH + (j + 1) * MXU_COLS]))
         for j in range(SGU_WIDTH // MXU_COLS)], axis=1)
    vn = _layer_norm(v, g_ref[...], b_ref[...]).astype(BF16)
    for gi in range(SGU_GROUPS):
        cols = slice(gi * gd, (gi + 1) * gd)
        u = _gelu(_dot(x, w_ref[:, cols]))
        w = jnp.where(causal, ws_ref[gi], 0.0).astype(BF16)
        for ch in range(chunks):
            rows = slice(ch * SGU_CHUNK, (ch + 1) * SGU_CHUNK)
            z = _dot(w, vn[rows, cols]) + bs[:, gi:gi + 1]
            o_ref[rows, cols] = (u[rows] * z).astype(o_ref.dtype)


def _sgu(x_bf, w_uv, ln_g, ln_b, w_s, b_s_t, *, chunks=4):
    m, d = x_bf.shape
    tm = chunks * SGU_CHUNK
    return pl.pallas_call(
        functools.partial(_sgu_kernel, chunks=chunks),
        grid=(m // tm,),
        in_specs=[pl.BlockSpec((tm, d), lambda i: (i, 0)),
                  pl.BlockSpec(w_uv.shape, lambda i: (0, 0), pipeline_mode=pl.Buffered(1)),
                  pl.BlockSpec((1, SGU_WIDTH), lambda i: (0, 0)),
                  pl.BlockSpec((1, SGU_WIDTH), lambda i: (0, 0)),
                  pl.BlockSpec(w_s.shape, lambda i: (0, 0, 0)),
                  pl.BlockSpec(b_s_t.shape, lambda i: (0, 0))],
        out_specs=pl.BlockSpec((tm, SGU_WIDTH), lambda i: (i, 0)),
        out_shape=jax.ShapeDtypeStruct((m, SGU_WIDTH), BF16),
        compiler_params=_params("parallel"),
        name="sgu",
    )(x_bf, w_uv, ln_g, ln_b, w_s, b_s_t)


def _merge_kernel(x_ref, a_ref, s_ref, wga_ref, wgb_ref, wa_ref, wb_ref, o_ref):
    x = x_ref[...]
    ga = _sigmoid(_dot(x, wga_ref[...]))
    gb = _sigmoid(_dot(x, wgb_ref[...]))
    o_ref[...] = (ga * _dot(a_ref[...], wa_ref[...]) + gb * _dot(s_ref[...], wb_ref[...])).astype(o_ref.dtype)


def _merge(x_bf, o_nsa, o_sgu, w_merge, w_a, w_b, *, tm=1024, tn=256):
    m, k = x_bf.shape
    n = w_a.shape[1]
    nb = n // tn
    row = pl.BlockSpec((tm, k), lambda i, j: (i, 0))
    col = pl.BlockSpec((k, tn), lambda i, j: (0, j))
    return pl.pallas_call(
        _merge_kernel,
        grid=(m // tm, nb),
        in_specs=[row, row, row, col, pl.BlockSpec((k, tn), lambda i, j: (0, nb + j)), col, col],
        out_specs=pl.BlockSpec((tm, tn), lambda i, j: (i, j)),
        out_shape=jax.ShapeDtypeStruct((m, n), BF16),
        compiler_params=_params("parallel", "parallel"),
        name="merge",
    )(x_bf, o_nsa, o_sgu, w_merge, w_merge, w_a, w_b)


def _proj_ln_kernel(a_ref, w_ref, x_ref, g_ref, b_ref, o_ref, obf_ref):
    y = DN_ALPHA * x_ref[...] + _dot(a_ref[...], w_ref[...])
    out = _layer_norm(y, g_ref[...], b_ref[...])
    o_ref[...] = out
    obf_ref[...] = out.astype(BF16)


def _proj_ln(a_bf, w, x, ln_g, ln_b, *, tm=512):
    m, k = a_bf.shape
    n = w.shape[1]
    vec = pl.BlockSpec((1, n), lambda i: (0, 0))
    return pl.pallas_call(
        _proj_ln_kernel,
        grid=(m // tm,),
        in_specs=[pl.BlockSpec((tm, k), lambda i: (i, 0)),
                  pl.BlockSpec((k, n), lambda i: (0, 0)),
                  pl.BlockSpec((tm, n), lambda i: (i, 0)), vec, vec],
        out_specs=[pl.BlockSpec((tm, n), lambda i: (i, 0)), pl.BlockSpec((tm, n), lambda i: (i, 0))],
        out_shape=[jax.ShapeDtypeStruct((m, n), F32), jax.ShapeDtypeStruct((m, n), BF16)],
        compiler_params=_params("parallel"),
        name="proj_ln",
    )(a_bf, w, x, ln_g, ln_b)


def _xattn_kernel(xbf_ref, x_ref, wq_ref, kv_ref, wo_ref, g_ref, b_ref, o_ref):
    q = (_dot(xbf_ref[0], wq_ref[...]) * HEAD_DIM ** -0.5).astype(BF16)
    kv = kv_ref[0]
    heads = []
    for h in range(MEM_HEADS):
        cols = slice(h * HEAD_DIM, (h + 1) * HEAD_DIM)
        s = _dot_nt(q[:, cols], kv[:, cols])
        p = jnp.exp(s - jnp.max(s, -1, keepdims=True))
        p = p / jnp.sum(p, -1, keepdims=True)
        heads.append(_dot(p.astype(BF16), kv[:, MEM_WIDTH + h * HEAD_DIM:MEM_WIDTH + (h + 1) * HEAD_DIM]))
    o = jnp.concatenate(heads, axis=1).astype(BF16)
    y = DN_ALPHA * x_ref[0] + _dot(o, wo_ref[...])
    o_ref[0] = _layer_norm(y, g_ref[...], b_ref[...])


def _xattn(x_bf, x, w_xq, mem_kv, w_xo, ln_g, ln_b, *, tm=512):
    bsz, seq, d = x.shape
    mem_len = mem_kv.shape[1]
    vec = pl.BlockSpec((1, d), lambda b, i: (0, 0))
    return pl.pallas_call(
        _xattn_kernel,
        grid=(bsz, seq // tm),
        in_specs=[pl.BlockSpec((1, tm, d), lambda b, i: (b, i, 0)),
                  pl.BlockSpec((1, tm, d), lambda b, i: (b, i, 0)),
                  pl.BlockSpec(w_xq.shape, lambda b, i: (0, 0)),
                  pl.BlockSpec((1, mem_len, 2 * MEM_WIDTH), lambda b, i: (b, 0, 0)),
                  pl.BlockSpec(w_xo.shape, lambda b, i: (0, 0)), vec, vec],
        out_specs=pl.BlockSpec((1, tm, d), lambda b, i: (b, i, 0)),
        out_shape=jax.ShapeDtypeStruct(x.shape, F32),
        compiler_params=_params("parallel", "parallel"),
        name="xattn_ln",
    )(x_bf, x, w_xq, mem_kv, w_xo, ln_g, ln_b)


def _router_kernel(x_ref, whi_ref, wlo_ref, b_ref, gate_ref, exp_ref, rank_ref, cnt_ref, seen_ref):
    @pl.when(pl.program_id(0) == 0)
    def _():
        seen_ref[...] = jnp.zeros_like(seen_ref)

    x = x_ref[...]
    x_hi = x.astype(BF16)
    x_lo = (x - x_hi.astype(F32)).astype(BF16)
    whi = whi_ref[...]
    logits = _dot(x_hi, whi) + _dot(x_hi, wlo_ref[...]) + _dot(x_lo, whi) + b_ref[...]
    rows = logits.shape[0]
    lane = lax.broadcasted_iota(jnp.int32, (rows, LANES), 1)
    lane_f = lane.astype(F32)
    is_grp = lane < N_GROUPS
    lg = jnp.where(is_grp, logits, MASKED)
    eg = jnp.where(is_grp, jnp.exp(lg - jnp.max(lg, -1, keepdims=True)), 0.0)
    p_grp = eg / jnp.sum(eg, -1, keepdims=True)
    g_grp = jnp.max(p_grp, -1, keepdims=True)
    grp = jnp.min(jnp.where((p_grp == g_grp) & is_grp, lane_f, 1e9), -1, keepdims=True).astype(jnp.int32)
    lo = N_GROUPS + grp * EXPERTS_PER_GROUP
    in_grp = (lane >= lo) & (lane < lo + EXPERTS_PER_GROUP)
    le = jnp.where(in_grp, logits, MASKED)
    ee = jnp.where(in_grp, jnp.exp(le - jnp.max(le, -1, keepdims=True)), 0.0)
    p_exp = jnp.where(in_grp, ee / jnp.sum(ee, -1, keepdims=True), -1.0)
    v1 = jnp.max(p_exp, -1, keepdims=True)
    i1 = jnp.min(jnp.where(p_exp == v1, lane_f, 1e9), -1, keepdims=True)
    p_rest = jnp.where(lane_f == i1, -1.0, p_exp)
    v2 = jnp.max(p_rest, -1, keepdims=True)
    i2 = jnp.min(jnp.where(p_rest == v2, lane_f, 1e9), -1, keepdims=True)
    denom = v1 + v2
    gate_ref[...] = jnp.where(lane == 0, g_grp * v1 / denom, jnp.where(lane == 1, g_grp * v2 / denom, 0.0))
    exp_ref[...] = jnp.where(lane == 0, i1, jnp.where(lane == 1, i2, float(N_GROUPS))).astype(jnp.int32) - N_GROUPS

    onehot = jnp.where((lane_f == i1) | (lane_f == i2), 1.0, 0.0)
    tok_r = lax.broadcasted_iota(jnp.int32, (rows, rows), 0)
    tok_c = lax.broadcasted_iota(jnp.int32, (rows, rows), 1)
    earlier = jnp.where(tok_c < tok_r, 1.0, 0.0).astype(BF16)
    before = _dot(earlier, onehot.astype(BF16)) + seen_ref[...]
    r1 = jnp.sum(jnp.where(lane_f == i1, before, 0.0), -1, keepdims=True)
    r2 = jnp.sum(jnp.where(lane_f == i2, before, 0.0), -1, keepdims=True)
    rank_ref[...] = jnp.where(lane == 0, r1, jnp.where(lane == 1, r2, 0.0)).astype(jnp.int32)
    seen_ref[...] += jnp.sum(onehot, axis=0, keepdims=True)
    cnt_ref[...] = seen_ref[...]


def _router(x, w_hi, w_lo, bias, *, tm=512):
    m, d = x.shape
    wspec = pl.BlockSpec((d, LANES), lambda i: (0, 0))
    vec = pl.BlockSpec((1, LANES), lambda i: (0, 0))
    tile = pl.BlockSpec((tm, LANES), lambda i: (i, 0))
    return pl.pallas_call(
        _router_kernel,
        grid=(m // tm,),
        in_specs=[pl.BlockSpec((tm, d), lambda i: (i, 0)), wspec, wspec, vec],
        out_specs=[tile, tile, tile, vec],
        out_shape=[jax.ShapeDtypeStruct((m, LANES), F32), jax.ShapeDtypeStruct((m, LANES), jnp.int32),
                   jax.ShapeDtypeStruct((m, LANES), jnp.int32), jax.ShapeDtypeStruct((1, LANES), F32)],
        scratch_shapes=[pltpu.VMEM((1, LANES), F32)],
        compiler_params=_params("arbitrary"),
        name="router",
    )(x, w_hi, w_lo, bias)


GATHER_UNROLL = 8


def _ffn_kernel(exp_ref, on_ref, first_ref, slot_ref, next_ref, tok_ref, dst_ref, x_hbm, wg_hbm, wu_hbm, wd_hbm,
                y_hbm, wg_buf, wu_buf, wd_buf, h_buf, y_buf, sem, row_sem, out_sem):
    i = pl.program_id(0)
    n_blk = pl.num_programs(0)
    slot = slot_ref[i]
    rows = MOE_BLOCK

    def out_copy(blk, g, u):
        dst = dst_ref[blk * rows + g * GATHER_UNROLL + u]
        return pltpu.make_async_copy(y_buf.at[blk % 2, g, pl.ds(u, 1), :], y_hbm.at[pl.ds(dst, 1), :],
                                     out_sem.at[blk % 2])

    def weight_copies(e, s):
        return (pltpu.make_async_copy(wg_hbm.at[e], wg_buf.at[s], sem.at[s, 0]),
                pltpu.make_async_copy(wu_hbm.at[e], wu_buf.at[s], sem.at[s, 1]),
                pltpu.make_async_copy(wd_hbm.at[e], wd_buf.at[s], sem.at[s, 2]))

    def row_copy(blk, g, u):
        tok = tok_ref[blk * rows + g * GATHER_UNROLL + u]
        return pltpu.make_async_copy(x_hbm.at[pl.ds(tok, 1), :], h_buf.at[blk % 2, g, pl.ds(u, 1), :],
                                     row_sem.at[blk % 2])

    def for_rows(blk, fn):
        def body(g, carry):
            for u in range(GATHER_UNROLL):
                fn(blk, g, u)
            return carry
        lax.fori_loop(0, on_ref[blk], body, 0)

    @pl.when(i == 0)
    def _():
        for cp in weight_copies(exp_ref[0], 0):
            cp.start()
        h_buf[...] = jnp.zeros_like(h_buf)
        for_rows(0, lambda b, g, u: row_copy(b, g, u).start())
        y_buf[...] = jnp.zeros_like(y_buf)
        spare = y_hbm.shape[0] - 2 * rows
        fills = [pltpu.make_async_copy(y_buf.at[s, g],
                                       y_hbm.at[pl.ds(spare + s * rows + g * GATHER_UNROLL, GATHER_UNROLL), :],
                                       out_sem.at[s])
                 for s in range(2) for g in range(rows // GATHER_UNROLL)]
        for cp in fills:
            cp.start()
        for cp in fills:
            cp.wait()

    prev2 = jnp.maximum(i - 2, 0)

    @pl.when((i >= 2) & (on_ref[prev2] != 0))
    def _():
        for_rows(i - 2, lambda b, g, u: out_copy(b, g, u).wait())

    nxt_blk = jnp.minimum(i + 1, n_blk - 1)

    @pl.when((i + 1 < n_blk) & (on_ref[nxt_blk] != 0))
    def _():
        for_rows(i + 1, lambda b, g, u: row_copy(b, g, u).start())

    @pl.when(first_ref[i] != 0)
    def _():
        nxt = next_ref[i]

        @pl.when(nxt >= 0)
        def _():
            for cp in weight_copies(nxt, 1 - slot):
                cp.start()

        for cp in weight_copies(exp_ref[i], slot):
            cp.wait()

    @pl.when(on_ref[i] != 0)
    def _():
        for_rows(i, lambda b, g, u: row_copy(b, g, u).wait())
        h = h_buf[i % 2].reshape(rows, h_buf.shape[-1]).astype(BF16)
        a = _dot(h, wg_buf[slot].astype(BF16))
        a = a * _sigmoid(a) * _dot(h, wu_buf[slot].astype(BF16))
        y_buf[i % 2] = _dot(a.astype(BF16), wd_buf[slot].astype(BF16)).reshape(y_buf.shape[1:])
        for_rows(i, lambda b, g, u: out_copy(b, g, u).start())

    @pl.when(i == n_blk - 1)
    def _():
        prev1 = jnp.maximum(i - 1, 0)

        @pl.when((i >= 1) & (on_ref[prev1] != 0))
        def _():
            for_rows(i - 1, lambda b, g, u: out_copy(b, g, u).wait())

        @pl.when(on_ref[i] != 0)
        def _():
            for_rows(i, lambda b, g, u: out_copy(b, g, u).wait())


def _expert_ffn(blk_exp, blk_on, row_tok, row_dst, x, w_gate, w_up, w_down):
    p_rows = row_tok.shape[0]
    n_tok, d = x.shape
    n_exp, _, ff = w_gate.shape
    n_blocks = p_rows // MOE_BLOCK
    prev_exp = jnp.concatenate([jnp.full((1,), -1, jnp.int32), blk_exp[:-1]])
    first = ((blk_on != 0) & (blk_exp != prev_exp)).astype(jnp.int32)
    seg_id = jnp.cumsum(first) - 1
    seg_ids = jnp.arange(n_exp + 1)
    hit = (first[:, None] != 0) & (seg_id[:, None] == seg_ids[None, :])
    seg_exp = jnp.sum(jnp.where(hit, blk_exp[:, None] + 1, 0), 0) - 1
    blk_next = jnp.sum(jnp.where(seg_ids[None, :] == seg_id[:, None] + 1, seg_exp[None, :] + 1, 0), 1) - 1
    blk_slot = jnp.maximum(seg_id, 0) % 2
    grid_spec = pltpu.PrefetchScalarGridSpec(
        num_scalar_prefetch=7,
        grid=(n_blocks,),
        in_specs=[pl.BlockSpec(memory_space=pl.ANY)] * 4,
        out_specs=pl.BlockSpec(memory_space=pl.ANY),
        scratch_shapes=[pltpu.VMEM((2, d, ff), F32), pltpu.VMEM((2, d, ff), F32), pltpu.VMEM((2, ff, d), F32),
                        pltpu.VMEM((2, MOE_BLOCK // GATHER_UNROLL, GATHER_UNROLL, d), F32),
                        pltpu.VMEM((2, MOE_BLOCK // GATHER_UNROLL, GATHER_UNROLL, d), F32),
                        pltpu.SemaphoreType.DMA((2, 3)), pltpu.SemaphoreType.DMA((2,)),
                        pltpu.SemaphoreType.DMA((2,))],
    )
    return pl.pallas_call(
        _ffn_kernel,
        grid_spec=grid_spec,
        out_shape=jax.ShapeDtypeStruct((EXPERT_TOPK * n_tok + 2 * MOE_BLOCK, d), F32),
        compiler_params=_params("arbitrary"),
        name="expert_ffn",
    )(blk_exp, blk_on, first, blk_slot.astype(jnp.int32), blk_next.astype(jnp.int32), row_tok, row_dst,
      x, w_gate, w_up, w_down)


def _combine_ln_kernel(x_ref, y0_ref, y1_ref, gate_ref, g_ref, b_ref, o_ref):
    gates = gate_ref[...]
    moe = y0_ref[...] * gates[:, 0:1] + y1_ref[...] * gates[:, 1:2]
    o_ref[...] = _layer_norm(DN_ALPHA * x_ref[...] + moe, g_ref[...], b_ref[...])


def _combine_ln(x, y_tok, gates, ln_g, ln_b, *, tm=256):
    m, d = x.shape
    row = pl.BlockSpec((tm, d), lambda i: (i, 0))
    vec = pl.BlockSpec((1, d), lambda i: (0, 0))
    return pl.pallas_call(
        _combine_ln_kernel,
        grid=(m // tm,),
        in_specs=[row, row, pl.BlockSpec((tm, d), lambda i: (m // tm + i, 0)),
                  pl.BlockSpec((tm, LANES), lambda i: (i, 0)), vec, vec],
        out_specs=row,
        out_shape=jax.ShapeDtypeStruct((m, d), F32),
        compiler_params=_params("parallel"),
        name="combine_ln",
    )(x, y_tok, y_tok, gates, ln_g, ln_b)


def _layer(x, mem, w_in, cmp_pe_k, cmp_w1_k, cmp_w2_k, cmp_pe_v, cmp_w1_v, cmp_w2_v,
           sgu_ln_g, sgu_ln_b, sgu_w_s, sgu_b_s, w_branch_a, w_branch_b, w_o, ln1_g, ln1_b,
           w_xq, w_xk, w_xv, w_xo, ln2_g, ln2_b,
           w_router_grp, b_router_grp, w_router_exp, b_router_exp,
           w_exp_gate, w_exp_up, w_exp_down, ln3_g, ln3_b):
    bsz, seq, d = x.shape
    n_tok = bsz * seq
    g = NSA_KV_GROUPS
    hpg = NSA_HPG
    xf = x.reshape(n_tok, d)
    x_bf = xf.astype(BF16)

    c0 = Q_WIDTH
    c1 = c0 + KV_WIDTH
    c2 = c1 + NSA_GATE_WIDTH
    c3 = c2 + 2 * SGU_WIDTH
    w_q = w_in[:, :c0].astype(BF16)
    w_gate = w_in[:, c1:c2].reshape(d, g, hpg, 3).transpose(0, 1, 3, 2).reshape(d, g, 3 * hpg)
    w_gate = jnp.pad(w_gate, ((0, 0), (0, 0), (0, LANES - 3 * hpg))).reshape(d, g * LANES)
    w_kvg = jnp.concatenate([w_in[:, c0:c1], w_gate], axis=1).astype(BF16)
    w_uv = w_in[:, c2:c3].astype(BF16)
    w_merge = w_in[:, c3:].astype(BF16)

    cos, sin = _rope_tables(jnp.arange(seq, dtype=jnp.int32))
    q = _q_proj(x_bf, w_q, cos, sin, seq).reshape(bsz, seq, Q_WIDTH)
    kv_cmp, kv_rest, nsa_gates = _kv_proj(x_bf, w_kvg, cos, sin, seq)

    n_chunks = seq // CMP_STRIDE
    pe = jnp.stack([cmp_pe_k, cmp_pe_v])
    w1 = jnp.stack([cmp_w1_k, cmp_w1_v]).astype(BF16)
    w2 = jnp.stack([cmp_w2_k, cmp_w2_v]).astype(BF16)
    cmp_end = CMP_STRIDE * jnp.arange(n_chunks, dtype=jnp.int32) + CMP_BLOCK - 1
    ccos, csin = _rope_tables(cmp_end)
    cmp_cos = jnp.stack([ccos, jnp.ones_like(ccos)])
    cmp_sin = jnp.stack([csin, jnp.zeros_like(csin)])
    cmp_kv = _compress(kv_cmp.reshape(bsz, seq, _N_CMP_COLS), pe, w1, w2, cmp_cos, cmp_sin)

    n_sel = seq // SEL_BLOCK
    n_pick = min(SEL_TOPK, n_sel)
    ci = CMP_STRIDE * jnp.arange(n_chunks)
    sj = SEL_BLOCK * jnp.arange(LANES)
    overlap = jnp.clip(jnp.minimum(ci[:, None] + CMP_BLOCK, sj[None, :] + SEL_BLOCK)
                       - jnp.maximum(ci[:, None], sj[None, :]), 0, None)
    selmap_t = (overlap.astype(F32) / CMP_BLOCK).astype(BF16).T
    o_nsa = _nsa(q, cmp_kv, kv_rest.reshape(bsz, seq, _N_REST_COLS),
                 nsa_gates.reshape(bsz, seq, _GATE_COLS), selmap_t, n_sel=n_sel, n_pick=n_pick)

    o_sgu = _sgu(x_bf, w_uv, sgu_ln_g.reshape(1, -1), sgu_ln_b.reshape(1, -1), sgu_w_s, sgu_b_s.T)

    merged = _merge(x_bf, o_nsa.reshape(n_tok, Q_WIDTH), o_sgu, w_merge,
                    w_branch_a.astype(BF16), w_branch_b.astype(BF16))
    x1, x1_bf = _proj_ln(merged, w_o.astype(BF16), xf, ln1_g.reshape(1, -1), ln1_b.reshape(1, -1))

    mem_len = mem.shape[1]
    w_mem = jnp.concatenate([w_xk, w_xv], axis=1).astype(BF16)
    mem_kv = _matmul(mem.reshape(bsz * mem_len, d).astype(BF16), w_mem, tm=256, tn=512, out_dtype=BF16)
    x2 = _xattn(x1_bf.reshape(bsz, seq, d), x1.reshape(bsz, seq, d), w_xq.astype(BF16),
                mem_kv.reshape(bsz, mem_len, 2 * MEM_WIDTH), w_xo.astype(BF16),
                ln2_g.reshape(1, -1), ln2_b.reshape(1, -1)).reshape(n_tok, d)

    w_r = jnp.concatenate([w_router_grp, w_router_exp], axis=1)
    w_r = jnp.pad(w_r, ((0, 0), (0, LANES - w_r.shape[1])))
    w_r_hi = w_r.astype(BF16)
    w_r_lo = (w_r - w_r_hi.astype(F32)).astype(BF16)
    b_r = jnp.concatenate([b_router_grp, b_router_exp])
    b_r = jnp.pad(b_r, (0, LANES - b_r.shape[0])).reshape(1, LANES)
    gates, experts, ranks, seen = _router(x2, w_r_hi, w_r_lo, b_r)

    nk = n_tok * EXPERT_TOPK
    counts = seen[0, N_GROUPS:N_GROUPS + N_EXPERTS].astype(jnp.int32)
    pcounts = ((counts + MOE_BLOCK - 1) // MOE_BLOCK) * MOE_BLOCK
    pend = jnp.cumsum(pcounts)
    pstart = pend - pcounts
    e_tok = experts[:, :EXPERT_TOPK]
    seg_start = jnp.sum(jnp.where(e_tok[..., None] == jnp.arange(N_EXPERTS), pstart, 0), -1)
    dest_of = seg_start + ranks[:, :EXPERT_TOPK]
    n_blocks = -(-nk // MOE_BLOCK) + N_EXPERTS
    p_rows = n_blocks * MOE_BLOCK
    row_ids = jnp.arange(p_rows, dtype=jnp.int32)
    row_asg = jnp.full((p_rows,), -1, jnp.int32).at[dest_of.reshape(nk)].set(
        jnp.arange(nk, dtype=jnp.int32), unique_indices=True)
    is_real = row_asg >= 0
    row_tok = jnp.where(is_real, row_asg // EXPERT_TOPK, row_ids % n_tok)
    spare = EXPERT_TOPK * n_tok + ((row_ids // MOE_BLOCK) % 2) * MOE_BLOCK + row_ids % MOE_BLOCK
    row_dst = jnp.where(is_real, (row_asg % EXPERT_TOPK) * n_tok + row_asg // EXPERT_TOPK, spare)
    blk_start = jnp.arange(n_blocks) * MOE_BLOCK
    blk_on = (blk_start < pend[-1]).astype(jnp.int32)
    blk_probe = jnp.minimum(blk_start, pend[-1] - 1)
    blk_exp = jnp.minimum(jnp.sum((pend[None, :] <= blk_probe[:, None]).astype(jnp.int32), -1),
                          N_EXPERTS - 1).astype(jnp.int32)
    real_end = jnp.sum(jnp.where(blk_exp[:, None] == jnp.arange(N_EXPERTS), (pstart + counts)[None, :], 0), -1)
    blk_real = jnp.clip(real_end - blk_start, 0, MOE_BLOCK) * blk_on
    blk_groups = ((blk_real + GATHER_UNROLL - 1) // GATHER_UNROLL).astype(jnp.int32)

    y_tok = _expert_ffn(blk_exp, blk_groups, row_tok, row_dst, x2, w_exp_gate, w_exp_up, w_exp_down)
    out = _combine_ln(x2, y_tok, gates, ln3_g.reshape(1, -1), ln3_b.reshape(1, -1))
    return out.reshape(bsz, seq, d)


def kernel(x, mem, w_in, cmp_pe_k, cmp_w1_k, cmp_w2_k, cmp_pe_v, cmp_w1_v, cmp_w2_v, sgu_ln_g, sgu_ln_b, sgu_w_s, sgu_b_s, w_branch_a, w_branch_b, w_o, ln1_g, ln1_b, w_xq, w_xk, w_xv, w_xo, ln2_g, ln2_b, w_router_grp, b_router_grp, w_router_exp, b_router_exp, w_exp_gate, w_exp_up, w_exp_down, ln3_g, ln3_b):
    params = (w_in, cmp_pe_k, cmp_w1_k, cmp_w2_k, cmp_pe_v, cmp_w1_v, cmp_w2_v,
              sgu_ln_g, sgu_ln_b, sgu_w_s, sgu_b_s, w_branch_a, w_branch_b, w_o, ln1_g, ln1_b,
              w_xq, w_xk, w_xv, w_xo, ln2_g, ln2_b,
              w_router_grp, b_router_grp, w_router_exp, b_router_exp,
              w_exp_gate, w_exp_up, w_exp_down, ln3_g, ln3_b)
    h = x
    for l in range(DEPTH):
        h = _layer(h, mem, *[p[l] for p in params])
    return h
```

```python
import functools
import math

import jax
import jax.numpy as jnp
from jax import lax
from jax.experimental import pallas as pl
from jax.experimental.pallas import tpu as pltpu

HEAD_DIM = 128
NSA_HEADS = 16
NSA_KV_GROUPS = 2
NSA_HPG = NSA_HEADS // NSA_KV_GROUPS
CMP_BLOCK = 32
CMP_STRIDE = 16
CMP_HIDDEN = 256
SEL_BLOCK = 64
SEL_TOPK = 16
WINDOW = 512
SGU_WIDTH = 2048
SGU_GROUPS = 8
SGU_CHUNK = 128
MEM_HEADS = 4
N_GROUPS = 4
EXPERTS_PER_GROUP = 16
N_EXPERTS = N_GROUPS * EXPERTS_PER_GROUP
EXPERT_TOPK = 2
MOE_BLOCK = 128
ROPE_THETA = 10000.0
LN_EPS = 1e-5
SEL_BIG = 1e9
DEPTH = 1
DN_ALPHA = (2.0 * DEPTH) ** 0.25

Q_WIDTH = NSA_HEADS * HEAD_DIM
KV_WIDTH = 3 * 2 * NSA_KV_GROUPS * HEAD_DIM
NSA_GATE_WIDTH = 3 * NSA_HEADS
MEM_WIDTH = MEM_HEADS * HEAD_DIM

LANES = 128
VMEM_LIMIT = 56 * 1024 * 1024
MASKED = -1e30
M_INIT = -1e20
BF16 = jnp.bfloat16
F32 = jnp.float32


def _params(*sem):
    return pltpu.CompilerParams(dimension_semantics=sem, vmem_limit_bytes=VMEM_LIMIT)


def _dot(a, b):
    return jnp.dot(a, b, preferred_element_type=F32)


def _dot_nt(a, b):
    return lax.dot_general(a, b, (((1,), (1,)), ((), ())), preferred_element_type=F32)


def _gelu(x):
    return 0.5 * x * (1.0 + jnp.tanh(math.sqrt(2.0 / math.pi) * (x + 0.044715 * (x * x * x))))


def _sigmoid(x):
    return 1.0 / (1.0 + jnp.exp(-x))


def _layer_norm(x, g, b):
    mu = jnp.mean(x, -1, keepdims=True)
    xc = x - mu
    var = jnp.mean(xc * xc, -1, keepdims=True)
    return xc * lax.rsqrt(var + LN_EPS) * g + b


def _rope(t, cos_full, sin_signed):
    return t * cos_full + pltpu.roll(t, HEAD_DIM // 2, axis=1) * sin_signed


def _rope_tables(pos):
    inv = ROPE_THETA ** (-jnp.arange(0, HEAD_DIM, 2, dtype=F32) / HEAD_DIM)
    ang = pos.astype(F32)[:, None] * inv[None, :]
    c, s = jnp.cos(ang), jnp.sin(ang)
    return jnp.concatenate([c, c], -1), jnp.concatenate([-s, s], -1)


MXU_COLS = 256


def _mm_kernel(a_ref, b_ref, o_ref):
    o_ref[...] = _dot(a_ref[...], b_ref[...]).astype(o_ref.dtype)


def _matmul(a, b, *, tm, tn, out_dtype):
    m, k = a.shape
    n = b.shape[1]
    return pl.pallas_call(
        _mm_kernel,
        grid=(m // tm, n // tn),
        in_specs=[pl.BlockSpec((tm, k), lambda i, j: (i, 0)),
                  pl.BlockSpec((k, tn), lambda i, j: (0, j))],
        out_specs=pl.BlockSpec((tm, tn), lambda i, j: (i, j)),
        out_shape=jax.ShapeDtypeStruct((m, n), out_dtype),
        compiler_params=_params("parallel", "parallel"),
        name="matmul",
    )(a, b)


def _qproj_kernel(a_ref, b_ref, cos_ref, sin_ref, o_ref):
    a = a_ref[...]
    cos, sin = cos_ref[...], sin_ref[...]
    scale = HEAD_DIM ** -0.5 * math.log2(math.e)
    for j in range(o_ref.shape[1] // MXU_COLS):
        y = _dot(a, b_ref[:, j * MXU_COLS:(j + 1) * MXU_COLS])
        for h in range(MXU_COLS // HEAD_DIM):
            sl = slice(j * MXU_COLS + h * HEAD_DIM, j * MXU_COLS + (h + 1) * HEAD_DIM)
            o_ref[:, sl] = (_rope(y[:, h * HEAD_DIM:(h + 1) * HEAD_DIM], cos, sin) * scale).astype(o_ref.dtype)


def _q_proj(x_bf, w_q, cos, sin, seq, *, tm=1024, tn=1024):
    m, k = x_bf.shape
    n = w_q.shape[1]
    per_seq = seq // tm
    return pl.pallas_call(
        _qproj_kernel,
        grid=(m // tm, n // tn),
        in_specs=[pl.BlockSpec((tm, k), lambda i, j: (i, 0)),
                  pl.BlockSpec((k, tn), lambda i, j: (0, j)),
                  pl.BlockSpec((tm, HEAD_DIM), lambda i, j: (i % per_seq, 0)),
                  pl.BlockSpec((tm, HEAD_DIM), lambda i, j: (i % per_seq, 0))],
        out_specs=pl.BlockSpec((tm, tn), lambda i, j: (i, j)),
        out_shape=jax.ShapeDtypeStruct((m, n), BF16),
        compiler_params=_params("parallel", "parallel"),
        name="q_proj",
    )(x_bf, w_q, cos, sin)


_N_CMP_COLS = 2 * NSA_KV_GROUPS * HEAD_DIM
_N_REST_CHUNKS = (KV_WIDTH - _N_CMP_COLS) // HEAD_DIM
_N_REST_COLS = (_N_REST_CHUNKS + NSA_KV_GROUPS) * HEAD_DIM
_GATE_COLS = NSA_KV_GROUPS * LANES
_SEL_SHIFT = SEL_BLOCK.bit_length() - 1
assert 1 << _SEL_SHIFT == SEL_BLOCK


def _kvproj_kernel(a_ref, b_ref, cos_ref, sin_ref, cmp_ref, rest_ref, gate_ref, *, per_seq):
    g = NSA_KV_GROUPS
    y = _dot(a_ref[...], b_ref[...])
    cos, sin = cos_ref[...], sin_ref[...]
    cmp_ref[...] = y[:, :_N_CMP_COLS]
    tm = y.shape[0]
    pos = (pl.program_id(0) % per_seq) * tm + lax.broadcasted_iota(jnp.int32, (tm, LANES), 0)
    lane = lax.broadcasted_iota(jnp.int32, (tm, LANES), 1)
    blk_onehot = jnp.where(lane == (pos >> _SEL_SHIFT), 1.0, 0.0).astype(rest_ref.dtype)
    for c in range(_N_REST_CHUNKS):
        t = y[:, _N_CMP_COLS + c * HEAD_DIM:_N_CMP_COLS + (c + 1) * HEAD_DIM]
        if (c % (2 * g)) < g:
            t = _rope(t, cos, sin)
        dst = 2 * c if c < g else g + c
        rest_ref[:, dst * HEAD_DIM:(dst + 1) * HEAD_DIM] = t.astype(rest_ref.dtype)
        if c < g:
            rest_ref[:, (dst + 1) * HEAD_DIM:(dst + 2) * HEAD_DIM] = blk_onehot
    gate_ref[...] = _sigmoid(y[:, KV_WIDTH:])


def _kv_proj(x_bf, w_kvg, cos, sin, seq, *, tm=1024):
    m, k = x_bf.shape
    n = w_kvg.shape[1]
    per_seq = seq // tm
    return pl.pallas_call(
        functools.partial(_kvproj_kernel, per_seq=per_seq),
        grid=(m // tm,),
        in_specs=[pl.BlockSpec((tm, k), lambda i: (i, 0)),
                  pl.BlockSpec((k, n), lambda i: (0, 0)),
                  pl.BlockSpec((tm, HEAD_DIM), lambda i: (i % per_seq, 0)),
                  pl.BlockSpec((tm, HEAD_DIM), lambda i: (i % per_seq, 0))],
        out_specs=[pl.BlockSpec((tm, _N_CMP_COLS), lambda i: (i, 0)),
                   pl.BlockSpec((tm, _N_REST_COLS), lambda i: (i, 0)),
                   pl.BlockSpec((tm, _GATE_COLS), lambda i: (i, 0))],
        out_shape=[jax.ShapeDtypeStruct((m, _N_CMP_COLS), F32),
                   jax.ShapeDtypeStruct((m, _N_REST_COLS), BF16),
                   jax.ShapeDtypeStruct((m, _GATE_COLS), F32)],
        compiler_params=_params("parallel"),
        name="kv_gate_proj",
    )(x_bf, w_kvg, cos, sin)


def _compress_kernel(kc_ref, pe_ref, w1_ref, w2_ref, cos_ref, sin_ref, o_ref):
    n_chunks = o_ref.shape[2]
    top = jnp.zeros((n_chunks, CMP_HIDDEN), F32)
    bot = jnp.zeros((n_chunks, CMP_HIDDEN), F32)
    for p in range(CMP_STRIDE):
        xp = kc_ref[0, pl.ds(p, n_chunks, stride=CMP_STRIDE), :]
        q = CMP_STRIDE + p
        top += _dot((xp + pe_ref[0, p:p + 1, :]).astype(BF16), w1_ref[0, p * HEAD_DIM:(p + 1) * HEAD_DIM, :])
        bot += _dot((xp + pe_ref[0, q:q + 1, :]).astype(BF16), w1_ref[0, q * HEAD_DIM:(q + 1) * HEAD_DIM, :])
    h = top + pltpu.roll(bot, n_chunks - 1, axis=0)
    y = _dot(_gelu(h).astype(BF16), w2_ref[0])
    o_ref[0, 0] = _rope(y, cos_ref[0], sin_ref[0]).astype(o_ref.dtype)


def _compress(kv_cmp, pe, w1, w2, cos, sin):
    bsz, seq, width = kv_cmp.shape
    four = width // HEAD_DIM
    n_chunks = seq // CMP_STRIDE
    g = NSA_KV_GROUPS
    return pl.pallas_call(
        _compress_kernel,
        grid=(four, bsz),
        in_specs=[pl.BlockSpec((1, seq, HEAD_DIM), lambda a, b: (b, 0, a)),
                  pl.BlockSpec((1, CMP_BLOCK, HEAD_DIM), lambda a, b: (a // g, 0, 0)),
                  pl.BlockSpec((1, CMP_BLOCK * HEAD_DIM, CMP_HIDDEN), lambda a, b: (a // g, 0, 0)),
                  pl.BlockSpec((1, CMP_HIDDEN, HEAD_DIM), lambda a, b: (a // g, 0, 0)),
                  pl.BlockSpec((1, n_chunks, HEAD_DIM), lambda a, b: (a // g, 0, 0)),
                  pl.BlockSpec((1, n_chunks, HEAD_DIM), lambda a, b: (a // g, 0, 0))],
        out_specs=pl.BlockSpec((1, 1, n_chunks, HEAD_DIM), lambda a, b: (a, b, 0, 0)),
        out_shape=jax.ShapeDtypeStruct((four, bsz, n_chunks, HEAD_DIM), BF16),
        compiler_params=_params("parallel", "parallel"),
        name="compress",
    )(kv_cmp, pe, w1, w2, cos, sin)


SEL_TILE = 512
SEL_WIDE = 2
NSA_TQ = 256
SUBLANES = 8


def _softmax_parts(s3, bias):
    s3 = s3 + bias[None]
    m = jnp.max(s3, -1, keepdims=True)
    m = jnp.where(m > 0.5 * MASKED, m, 0.0)
    p = jnp.exp2(s3 - m)
    return p, 1.0 / jnp.maximum(jnp.sum(p, -1, keepdims=True), 1e-30)


def _topk_unselected(score_t, n_sel, n_pick):
    tq = score_t.shape[1]
    n_grp = n_sel // SUBLANES
    grp = [score_t[g * SUBLANES:(g + 1) * SUBLANES] for g in range(n_grp)]
    row_in_grp = lax.broadcasted_iota(jnp.int32, (SUBLANES, tq), 0)
    later = [jnp.where(row_in_grp > li, 1.0, 0.0) for li in range(SUBLANES)]
    beaten = [jnp.zeros((SUBLANES, tq), F32) for _ in range(n_grp)]
    for i in range(n_sel):
        gi, li = divmod(i, SUBLANES)
        row = grp[gi][li:li + 1, :]
        for g in range(n_grp):
            if g > gi:
                beaten[g] = beaten[g] + jnp.where(row >= grp[g], 1.0, 0.0)
            elif g < gi:
                beaten[g] = beaten[g] + jnp.where(row > grp[g], 1.0, 0.0)
            else:
                beaten[g] = (beaten[g] + jnp.where(row > grp[g], 1.0, 0.0)
                             + jnp.where(row == grp[g], later[li], 0.0))
    return jnp.concatenate([jnp.where(b < n_pick, 0.0, MASKED) for b in beaten], axis=0)


def _nsa_kernel(q_ref, kc_ref, vc_ref, ks_ref, vs_ref, kw_ref, vw_ref, gate_ref, selmap_t_ref, o_ref,
                *, n_sel, n_pick):
    hpg = NSA_HPG
    tq = NSA_TQ
    c = pl.program_id(2)
    q_t = q_ref[0]
    q8 = jnp.concatenate([q_t[:, n * HEAD_DIM:(n + 1) * HEAD_DIM] for n in range(hpg)], axis=0)
    t_pos = c * tq + lax.broadcasted_iota(jnp.int32, (tq, 1), 0)
    t_lane = c * tq + lax.broadcasted_iota(jnp.int32, (1, tq), 1)

    kc = kc_ref[0, 0]
    n_cmp_pad = kc.shape[0]
    k_idx = lax.broadcasted_iota(jnp.int32, (1, n_cmp_pad), 1)
    vis_c = (CMP_STRIDE * k_idx + (CMP_BLOCK - 1) <= t_pos) & (k_idx < n_cmp_pad - 1)
    p_c, r_c = _softmax_parts(_dot_nt(q8, kc).reshape(hpg, tq, n_cmp_pad), jnp.where(vis_c, 0.0, MASKED))
    o_c = _dot(p_c.reshape(hpg * tq, n_cmp_pad).astype(BF16), vc_ref[0, 0]).reshape(hpg, tq, HEAD_DIM) * r_c

    p_sum = jnp.sum(p_c * r_c, axis=0)
    p_hi = p_sum.astype(BF16)
    p_lo = (p_sum - p_hi.astype(F32)).astype(BF16)
    selmap_t = selmap_t_ref[...]
    imp_t = _dot_nt(selmap_t, p_hi) + _dot_nt(selmap_t, p_lo)
    blk = lax.broadcasted_iota(jnp.int32, (LANES, tq), 0)
    cur = t_lane >> _SEL_SHIFT
    forced = (blk == 0) | (blk == cur) | (blk == cur - 1)
    score_t = jnp.where(forced, SEL_BIG, jnp.where(blk <= cur, imp_t, -SEL_BIG))
    unsel_t = _topk_unselected(score_t, n_sel, n_pick)
    unsel_t = jnp.concatenate([unsel_t, jnp.zeros((LANES - n_sel, tq), F32)], axis=0)
    unsel = unsel_t.T.astype(BF16)
    q_aug = jnp.concatenate([q8, jnp.concatenate([unsel] * hpg, axis=0)], axis=1)

    def sel_tile(kt, size, carry, bias):
        m_old, l_old, acc = carry
        base = pl.multiple_of(kt * size, size)
        k = ks_ref[0, pl.ds(base, size), :]
        v = vs_ref[0, pl.ds(base, size), :]
        s = _dot_nt(q_aug, k).reshape(hpg, tq, size)
        if bias is not None:
            s = s + bias[None]
        m_new = jnp.maximum(m_old, jnp.max(s, -1, keepdims=True))
        alpha = jnp.exp2(m_old - m_new)
        p = jnp.exp2(s - m_new)
        l_new = alpha * l_old + jnp.sum(p, -1, keepdims=True)
        pv = _dot(p.reshape(hpg * tq, size).astype(BF16), v).reshape(hpg, tq, HEAD_DIM)
        return m_new, l_new, alpha * acc + pv

    n_full = c // (SEL_TILE // tq)
    n_wide = n_full // SEL_WIDE
    carry = (jnp.full((hpg, tq, 1), M_INIT, F32), jnp.zeros((hpg, tq, 1), F32),
             jnp.zeros((hpg, tq, HEAD_DIM), F32))
    carry = lax.fori_loop(0, n_wide, lambda kt, cr: sel_tile(kt, SEL_WIDE * SEL_TILE, cr, None), carry)
    carry = lax.fori_loop(n_wide * SEL_WIDE, n_full, lambda kt, cr: sel_tile(kt, SEL_TILE, cr, None), carry)
    kp_last = n_full * SEL_TILE + lax.broadcasted_iota(jnp.int32, (1, SEL_TILE), 1)
    _, l_s, acc_s = sel_tile(n_full, SEL_TILE, carry, jnp.where(kp_last <= t_pos, 0.0, MASKED))
    o_s = acc_s * (1.0 / jnp.maximum(l_s, 1e-30))

    span = WINDOW + tq
    start = pl.multiple_of(jnp.maximum(c * tq - WINDOW, 0), tq)
    kw = kw_ref[0, pl.ds(start, span), :]
    vw = vw_ref[0, pl.ds(start, span), :]
    kp = start + lax.broadcasted_iota(jnp.int32, (1, span), 1)
    vis_w = (kp <= t_pos) & (t_pos - kp < WINDOW)
    p_w, r_w = _softmax_parts(_dot_nt(q8, kw).reshape(hpg, tq, span), jnp.where(vis_w, 0.0, MASKED))
    o_w = _dot(p_w.reshape(hpg * tq, span).astype(BF16), vw).reshape(hpg, tq, HEAD_DIM) * r_w

    gates = gate_ref[0]
    for n in range(hpg):
        out = (gates[:, n:n + 1] * o_c[n]
               + gates[:, hpg + n:hpg + n + 1] * o_s[n]
               + gates[:, 2 * hpg + n:2 * hpg + n + 1] * o_w[n])
        o_ref[0, :, n * HEAD_DIM:(n + 1) * HEAD_DIM] = out.astype(o_ref.dtype)


def _nsa(q, cmp_kv, rest, gates, selmap_t, *, n_sel, n_pick):
    bsz, seq, _ = q.shape
    g = NSA_KV_GROUPS
    n_cmp_pad = cmp_kv.shape[2]
    gw = NSA_HPG * HEAD_DIM
    assert n_sel % SUBLANES == 0 and n_sel <= LANES and seq >= WINDOW + NSA_TQ
    assert SEL_TILE % NSA_TQ == 0 and seq % SEL_TILE == 0
    kv_spec = lambda off: pl.BlockSpec((1, seq, HEAD_DIM), lambda b, gi, c: (b, 0, off + gi))
    return pl.pallas_call(
        functools.partial(_nsa_kernel, n_sel=n_sel, n_pick=n_pick),
        grid=(bsz, g, seq // NSA_TQ),
        in_specs=[pl.BlockSpec((1, NSA_TQ, gw), lambda b, gi, c: (b, c, gi)),
                  pl.BlockSpec((1, 1, n_cmp_pad, HEAD_DIM), lambda b, gi, c: (gi, b, 0, 0)),
                  pl.BlockSpec((1, 1, n_cmp_pad, HEAD_DIM), lambda b, gi, c: (g + gi, b, 0, 0)),
                  pl.BlockSpec((1, seq, 2 * HEAD_DIM), lambda b, gi, c: (b, 0, gi)),
                  kv_spec(2 * g), kv_spec(3 * g), kv_spec(4 * g),
                  pl.BlockSpec((1, NSA_TQ, LANES), lambda b, gi, c: (b, c, gi)),
                  pl.BlockSpec(selmap_t.shape, lambda b, gi, c: (0, 0))],
        out_specs=pl.BlockSpec((1, NSA_TQ, gw), lambda b, gi, c: (b, c, gi)),
        out_shape=jax.ShapeDtypeStruct(q.shape, BF16),
        compiler_params=_params("parallel", "parallel", "arbitrary"),
        name="nsa_attention",
    )(q, cmp_kv, cmp_kv, rest, rest, rest, rest, gates, selmap_t)


def _sgu_kernel(x_ref, w_ref, g_ref, b_ref, ws_ref, bs_ref, o_ref, *, chunks):
    gd = SGU_WIDTH // SGU_GROUPS
    row = lax.broadcasted_iota(jnp.int32, (SGU_CHUNK, SGU_CHUNK), 0)
    col = lax.broadcasted_iota(jnp.int32, (SGU_CHUNK, SGU_CHUNK), 1)
    causal = col <= row
    bs = bs_ref[...]
    x = x_ref[...]
    v = jnp.concatenate(
        [_gelu(_dot(x, w_ref[:, SGU_WIDTH + j * MXU_COLS:SGU_WIDTH + (j + 1) * MXU_COLS]))
         for j in range(SGU_WIDTH // MXU_COLS)], axis=1)
    vn = _layer_norm(v, g_ref[...], b_ref[...]).astype(BF16)
    for gi in range(SGU_GROUPS):
        cols = slice(gi * gd, (gi + 1) * gd)
        u = _gelu(_dot(x, w_ref[:, cols]))
        w = jnp.where(causal, ws_ref[gi], 0.0).astype(BF16)
        for ch in range(chunks):
            rows = slice(ch * SGU_CHUNK, (ch + 1) * SGU_CHUNK)
            z = _dot(w, vn[rows, cols]) + bs[:, gi:gi + 1]
            o_ref[rows, cols] = (u[rows] * z).astype(o_ref.dtype)


def _sgu(x_bf, w_uv, ln_g, ln_b, w_s, b_s_t, *, chunks=4):
    m, d = x_bf.shape
    tm = chunks * SGU_CHUNK
    return pl.pallas_call(
        functools.partial(_sgu_kernel, chunks=chunks),
        grid=(m // tm,),
        in_specs=[pl.BlockSpec((tm, d), lambda i: (i, 0)),
                  pl.BlockSpec(w_uv.shape, lambda i: (0, 0), pipeline_mode=pl.Buffered(1)),
                  pl.BlockSpec((1, SGU_WIDTH), lambda i: (0, 0)),
                  pl.BlockSpec((1, SGU_WIDTH), lambda i: (0, 0)),
                  pl.BlockSpec(w_s.shape, lambda i: (0, 0, 0)),
                  pl.BlockSpec(b_s_t.shape, lambda i: (0, 0))],
        out_specs=pl.BlockSpec((tm, SGU_WIDTH), lambda i: (i, 0)),
        out_shape=jax.ShapeDtypeStruct((m, SGU_WIDTH), BF16),
        compiler_params=_params("parallel"),
        name="sgu",
    )(x_bf, w_uv, ln_g, ln_b, w_s, b_s_t)


def _merge_kernel(x_ref, a_ref, s_ref, wga_ref, wgb_ref, wa_ref, wb_ref, o_ref):
    x = x_ref[...]
    ga = _sigmoid(_dot(x, wga_ref[...]))
    gb = _sigmoid(_dot(x, wgb_ref[...]))
    o_ref[...] = (ga * _dot(a_ref[...], wa_ref[...]) + gb * _dot(s_ref[...], wb_ref[...])).astype(o_ref.dtype)


def _merge(x_bf, o_nsa, o_sgu, w_merge, w_a, w_b, *, tm=1024, tn=512):
    m, k = x_bf.shape
    n = w_a.shape[1]
    nb = n // tn
    row = pl.BlockSpec((tm, k), lambda i, j: (i, 0))
    col = pl.BlockSpec((k, tn), lambda i, j: (0, j))
    return pl.pallas_call(
        _merge_kernel,
        grid=(m // tm, nb),
        in_specs=[row, row, row, col, pl.BlockSpec((k, tn), lambda i, j: (0, nb + j)), col, col],
        out_specs=pl.BlockSpec((tm, tn), lambda i, j: (i, j)),
        out_shape=jax.ShapeDtypeStruct((m, n), BF16),
        compiler_params=_params("parallel", "parallel"),
        name="merge",
    )(x_bf, o_nsa, o_sgu, w_merge, w_merge, w_a, w_b)


def _proj_ln_kernel(a_ref, w_ref, x_ref, g_ref, b_ref, o_ref, obf_ref):
    y = DN_ALPHA * x_ref[...] + _dot(a_ref[...], w_ref[...])
    out = _layer_norm(y, g_ref[...], b_ref[...])
    o_ref[...] = out
    obf_ref[...] = out.astype(BF16)


def _proj_ln(a_bf, w, x, ln_g, ln_b, *, tm=512):
    m, k = a_bf.shape
    n = w.shape[1]
    vec = pl.BlockSpec((1, n), lambda i: (0, 0))
    return pl.pallas_call(
        _proj_ln_kernel,
        grid=(m // tm,),
        in_specs=[pl.BlockSpec((tm, k), lambda i: (i, 0)),
                  pl.BlockSpec((k, n), lambda i: (0, 0)),
                  pl.BlockSpec((tm, n), lambda i: (i, 0)), vec, vec],
        out_specs=[pl.BlockSpec((tm, n), lambda i: (i, 0)), pl.BlockSpec((tm, n), lambda i: (i, 0))],
        out_shape=[jax.ShapeDtypeStruct((m, n), F32), jax.ShapeDtypeStruct((m, n), BF16)],
        compiler_params=_params("parallel"),
        name="proj_ln",
    )(a_bf, w, x, ln_g, ln_b)


def _xattn_kernel(xbf_ref, x_ref, wq_ref, kv_ref, wo_ref, g_ref, b_ref, o_ref):
    q = (_dot(xbf_ref[0], wq_ref[...]) * HEAD_DIM ** -0.5).astype(BF16)
    kv = kv_ref[0]
    heads = []
    for h in range(MEM_HEADS):
        cols = slice(h * HEAD_DIM, (h + 1) * HEAD_DIM)
        s = _dot_nt(q[:, cols], kv[:, cols])
        p = jnp.exp(s - jnp.max(s, -1, keepdims=True))
        p = p / jnp.sum(p, -1, keepdims=True)
        heads.append(_dot(p.astype(BF16), kv[:, MEM_WIDTH + h * HEAD_DIM:MEM_WIDTH + (h + 1) * HEAD_DIM]))
    o = jnp.concatenate(heads, axis=1).astype(BF16)
    y = DN_ALPHA * x_ref[0] + _dot(o, wo_ref[...])
    o_ref[0] = _layer_norm(y, g_ref[...], b_ref[...])


def _xattn(x_bf, x, w_xq, mem_kv, w_xo, ln_g, ln_b, *, tm=512):
    bsz, seq, d = x.shape
    mem_len = mem_kv.shape[1]
    vec = pl.BlockSpec((1, d), lambda b, i: (0, 0))
    return pl.pallas_call(
        _xattn_kernel,
        grid=(bsz, seq // tm),
        in_specs=[pl.BlockSpec((1, tm, d), lambda b, i: (b, i, 0)),
                  pl.BlockSpec((1, tm, d), lambda b, i: (b, i, 0)),
                  pl.BlockSpec(w_xq.shape, lambda b, i: (0, 0)),
                  pl.BlockSpec((1, mem_len, 2 * MEM_WIDTH), lambda b, i: (b, 0, 0)),
                  pl.BlockSpec(w_xo.shape, lambda b, i: (0, 0)), vec, vec],
        out_specs=pl.BlockSpec((1, tm, d), lambda b, i: (b, i, 0)),
        out_shape=jax.ShapeDtypeStruct(x.shape, F32),
        compiler_params=_params("parallel", "parallel"),
        name="xattn_ln",
    )(x_bf, x, w_xq, mem_kv, w_xo, ln_g, ln_b)


def _router_kernel(x_ref, whi_ref, wlo_ref, b_ref, gate_ref, exp_ref, rank_ref, cnt_ref, seen_ref):
    @pl.when(pl.program_id(0) == 0)
    def _():
        seen_ref[...] = jnp.zeros_like(seen_ref)

    x = x_ref[...]
    x_hi = x.astype(BF16)
    x_lo = (x - x_hi.astype(F32)).astype(BF16)
    whi = whi_ref[...]
    logits = _dot(x_hi, whi) + _dot(x_hi, wlo_ref[...]) + _dot(x_lo, whi) + b_ref[...]
    rows = logits.shape[0]
    lane = lax.broadcasted_iota(jnp.int32, (rows, LANES), 1)
    lane_f = lane.astype(F32)
    is_grp = lane < N_GROUPS
    lg = jnp.where(is_grp, logits, MASKED)
    eg = jnp.where(is_grp, jnp.exp(lg - jnp.max(lg, -1, keepdims=True)), 0.0)
    p_grp = eg / jnp.sum(eg, -1, keepdims=True)
    g_grp = jnp.max(p_grp, -1, keepdims=True)
    grp = jnp.min(jnp.where((p_grp == g_grp) & is_grp, lane_f, 1e9), -1, keepdims=True).astype(jnp.int32)
    lo = N_GROUPS + grp * EXPERTS_PER_GROUP
    in_grp = (lane >= lo) & (lane < lo + EXPERTS_PER_GROUP)
    le = jnp.where(in_grp, logits, MASKED)
    ee = jnp.where(in_grp, jnp.exp(le - jnp.max(le, -1, keepdims=True)), 0.0)
    p_exp = jnp.where(in_grp, ee / jnp.sum(ee, -1, keepdims=True), -1.0)
    v1 = jnp.max(p_exp, -1, keepdims=True)
    i1 = jnp.min(jnp.where(p_exp == v1, lane_f, 1e9), -1, keepdims=True)
    p_rest = jnp.where(lane_f == i1, -1.0, p_exp)
    v2 = jnp.max(p_rest, -1, keepdims=True)
    i2 = jnp.min(jnp.where(p_rest == v2, lane_f, 1e9), -1, keepdims=True)
    denom = v1 + v2
    gate_ref[...] = jnp.where(lane == 0, g_grp * v1 / denom, jnp.where(lane == 1, g_grp * v2 / denom, 0.0))
    exp_ref[...] = jnp.where(lane == 0, i1, jnp.where(lane == 1, i2, float(N_GROUPS))).astype(jnp.int32) - N_GROUPS

    onehot = jnp.where((lane_f == i1) | (lane_f == i2), 1.0, 0.0)
    tok_r = lax.broadcasted_iota(jnp.int32, (rows, rows), 0)
    tok_c = lax.broadcasted_iota(jnp.int32, (rows, rows), 1)
    earlier = jnp.where(tok_c < tok_r, 1.0, 0.0).astype(BF16)
    before = _dot(earlier, onehot.astype(BF16)) + seen_ref[...]
    r1 = jnp.sum(jnp.where(lane_f == i1, before, 0.0), -1, keepdims=True)
    r2 = jnp.sum(jnp.where(lane_f == i2, before, 0.0), -1, keepdims=True)
    rank_ref[...] = jnp.where(lane == 0, r1, jnp.where(lane == 1, r2, 0.0)).astype(jnp.int32)
    seen_ref[...] += jnp.sum(onehot, axis=0, keepdims=True)
    cnt_ref[...] = seen_ref[...]


def _router(x, w_hi, w_lo, bias, *, tm=512):
    m, d = x.shape
    wspec = pl.BlockSpec((d, LANES), lambda i: (0, 0))
    vec = pl.BlockSpec((1, LANES), lambda i: (0, 0))
    tile = pl.BlockSpec((tm, LANES), lambda i: (i, 0))
    return pl.pallas_call(
        _router_kernel,
        grid=(m // tm,),
        in_specs=[pl.BlockSpec((tm, d), lambda i: (i, 0)), wspec, wspec, vec],
        out_specs=[tile, tile, tile, vec],
        out_shape=[jax.ShapeDtypeStruct((m, LANES), F32), jax.ShapeDtypeStruct((m, LANES), jnp.int32),
                   jax.ShapeDtypeStruct((m, LANES), jnp.int32), jax.ShapeDtypeStruct((1, LANES), F32)],
        scratch_shapes=[pltpu.VMEM((1, LANES), F32)],
        compiler_params=_params("arbitrary"),
        name="router",
    )(x, w_hi, w_lo, bias)


GATHER_UNROLL = 8


def _ffn_kernel(exp_ref, on_ref, first_ref, slot_ref, next_ref, tok_ref, dst_ref, x_hbm, wg_hbm, wu_hbm, wd_hbm,
                y_hbm, wg_buf, wu_buf, wd_buf, h_buf, y_buf, sem, row_sem, out_sem):
    i = pl.program_id(0)
    n_blk = pl.num_programs(0)
    slot = slot_ref[i]
    rows = MOE_BLOCK

    def out_copy(blk, g, u):
        dst = dst_ref[blk * rows + g * GATHER_UNROLL + u]
        return pltpu.make_async_copy(y_buf.at[blk % 2, g, pl.ds(u, 1), :], y_hbm.at[pl.ds(dst, 1), :],
                                     out_sem.at[blk % 2])

    def weight_copies(e, s):
        return (pltpu.make_async_copy(wg_hbm.at[e], wg_buf.at[s], sem.at[s, 0]),
                pltpu.make_async_copy(wu_hbm.at[e], wu_buf.at[s], sem.at[s, 1]),
                pltpu.make_async_copy(wd_hbm.at[e], wd_buf.at[s], sem.at[s, 2]))

    def row_copy(blk, g, u):
        tok = tok_ref[blk * rows + g * GATHER_UNROLL + u]
        return pltpu.make_async_copy(x_hbm.at[pl.ds(tok, 1), :], h_buf.at[blk % 2, g, pl.ds(u, 1), :],
                                     row_sem.at[blk % 2])

    def for_rows(blk, fn):
        def body(g, carry):
            for u in range(GATHER_UNROLL):
                fn(blk, g, u)
            return carry
        lax.fori_loop(0, on_ref[blk], body, 0)

    @pl.when(i == 0)
    def _():
        for cp in weight_copies(exp_ref[0], 0):
            cp.start()
        h_buf[...] = jnp.zeros_like(h_buf)
        for_rows(0, lambda b, g, u: row_copy(b, g, u).start())
        y_buf[...] = jnp.zeros_like(y_buf)
        spare = y_hbm.shape[0] - 2 * rows
        fills = [pltpu.make_async_copy(y_buf.at[s, g],
                                       y_hbm.at[pl.ds(spare + s * rows + g * GATHER_UNROLL, GATHER_UNROLL), :],
                                       out_sem.at[s])
                 for s in range(2) for g in range(rows // GATHER_UNROLL)]
        for cp in fills:
            cp.start()
        for cp in fills:
            cp.wait()

    prev2 = jnp.maximum(i - 2, 0)

    @pl.when((i >= 2) & (on_ref[prev2] != 0))
    def _():
        for_rows(i - 2, lambda b, g, u: out_copy(b, g, u).wait())

    nxt_blk = jnp.minimum(i + 1, n_blk - 1)

    @pl.when((i + 1 < n_blk) & (on_ref[nxt_blk] != 0))
    def _():
        for_rows(i + 1, lambda b, g, u: row_copy(b, g, u).start())

    @pl.when(first_ref[i] != 0)
    def _():
        nxt = next_ref[i]

        @pl.when(nxt >= 0)
        def _():
            for cp in weight_copies(nxt, 1 - slot):
                cp.start()

        for cp in weight_copies(exp_ref[i], slot):
            cp.wait()

    @pl.when(on_ref[i] != 0)
    def _():
        for_rows(i, lambda b, g, u: row_copy(b, g, u).wait())
        h = h_buf[i % 2].reshape(rows, h_buf.shape[-1]).astype(BF16)
        a = _dot(h, wg_buf[slot].astype(BF16))
        a = a * _sigmoid(a) * _dot(h, wu_buf[slot].astype(BF16))
        y_buf[i % 2] = _dot(a.astype(BF16), wd_buf[slot].astype(BF16)).reshape(y_buf.shape[1:])
        for_rows(i, lambda b, g, u: out_copy(b, g, u).start())

    @pl.when(i == n_blk - 1)
    def _():
        prev1 = jnp.maximum(i - 1, 0)

        @pl.when((i >= 1) & (on_ref[prev1] != 0))
        def _():
            for_rows(i - 1, lambda b, g, u: out_copy(b, g, u).wait())

        @pl.when(on_ref[i] != 0)
        def _():
            for_rows(i, lambda b, g, u: out_copy(b, g, u).wait())


def _expert_ffn(blk_exp, blk_on, row_tok, row_dst, x, w_gate, w_up, w_down):
    p_rows = row_tok.shape[0]
    n_tok, d = x.shape
    n_exp, _, ff = w_gate.shape
    n_blocks = p_rows // MOE_BLOCK
    prev_exp = jnp.concatenate([jnp.full((1,), -1, jnp.int32), blk_exp[:-1]])
    first = ((blk_on != 0) & (blk_exp != prev_exp)).astype(jnp.int32)
    seg_id = jnp.cumsum(first) - 1
    seg_ids = jnp.arange(n_exp + 1)
    hit = (first[:, None] != 0) & (seg_id[:, None] == seg_ids[None, :])
    seg_exp = jnp.sum(jnp.where(hit, blk_exp[:, None] + 1, 0), 0) - 1
    blk_next = jnp.sum(jnp.where(seg_ids[None, :] == seg_id[:, None] + 1, seg_exp[None, :] + 1, 0), 1) - 1
    blk_slot = jnp.maximum(seg_id, 0) % 2
    grid_spec = pltpu.PrefetchScalarGridSpec(
        num_scalar_prefetch=7,
        grid=(n_blocks,),
        in_specs=[pl.BlockSpec(memory_space=pl.ANY)] * 4,
        out_specs=pl.BlockSpec(memory_space=pl.ANY),
        scratch_shapes=[pltpu.VMEM((2, d, ff), F32), pltpu.VMEM((2, d, ff), F32), pltpu.VMEM((2, ff, d), F32),
                        pltpu.VMEM((2, MOE_BLOCK // GATHER_UNROLL, GATHER_UNROLL, d), F32),
                        pltpu.VMEM((2, MOE_BLOCK // GATHER_UNROLL, GATHER_UNROLL, d), F32),
                        pltpu.SemaphoreType.DMA((2, 3)), pltpu.SemaphoreType.DMA((2,)),
                        pltpu.SemaphoreType.DMA((2,))],
    )
    return pl.pallas_call(
        _ffn_kernel,
        grid_spec=grid_spec,
        out_shape=jax.ShapeDtypeStruct((EXPERT_TOPK * n_tok + 2 * MOE_BLOCK, d), F32),
        compiler_params=_params("arbitrary"),
        name="expert_ffn",
    )(blk_exp, blk_on, first, blk_slot.astype(jnp.int32), blk_next.astype(jnp.int32), row_tok, row_dst,
      x, w_gate, w_up, w_down)


def _combine_ln_kernel(x_ref, y0_ref, y1_ref, gate_ref, g_ref, b_ref, o_ref):
    gates = gate_ref[...]
    moe = y0_ref[...] * gates[:, 0:1] + y1_ref[...] * gates[:, 1:2]
    o_ref[...] = _layer_norm(DN_ALPHA * x_ref[...] + moe, g_ref[...], b_ref[...])


def _combine_ln(x, y_tok, gates, ln_g, ln_b, *, tm=256):
    m, d = x.shape
    row = pl.BlockSpec((tm, d), lambda i: (i, 0))
    vec = pl.BlockSpec((1, d), lambda i: (0, 0))
    return pl.pallas_call(
        _combine_ln_kernel,
        grid=(m // tm,),
        in_specs=[row, row, pl.BlockSpec((tm, d), lambda i: (m // tm + i, 0)),
                  pl.BlockSpec((tm, LANES), lambda i: (i, 0)), vec, vec],
        out_specs=row,
        out_shape=jax.ShapeDtypeStruct((m, d), F32),
        compiler_params=_params("parallel"),
        name="combine_ln",
    )(x, y_tok, y_tok, gates, ln_g, ln_b)


def _layer(x, mem, w_in, cmp_pe_k, cmp_w1_k, cmp_w2_k, cmp_pe_v, cmp_w1_v, cmp_w2_v,
           sgu_ln_g, sgu_ln_b, sgu_w_s, sgu_b_s, w_branch_a, w_branch_b, w_o, ln1_g, ln1_b,
           w_xq, w_xk, w_xv, w_xo, ln2_g, ln2_b,
           w_router_grp, b_router_grp, w_router_exp, b_router_exp,
           w_exp_gate, w_exp_up, w_exp_down, ln3_g, ln3_b):
    bsz, seq, d = x.shape
    n_tok = bsz * seq
    g = NSA_KV_GROUPS
    hpg = NSA_HPG
    xf = x.reshape(n_tok, d)
    x_bf = xf.astype(BF16)

    c0 = Q_WIDTH
    c1 = c0 + KV_WIDTH
    c2 = c1 + NSA_GATE_WIDTH
    c3 = c2 + 2 * SGU_WIDTH
    w_q = w_in[:, :c0].astype(BF16)
    w_gate = w_in[:, c1:c2].reshape(d, g, hpg, 3).transpose(0, 1, 3, 2).reshape(d, g, 3 * hpg)
    w_gate = jnp.pad(w_gate, ((0, 0), (0, 0), (0, LANES - 3 * hpg))).reshape(d, g * LANES)
    w_kvg = jnp.concatenate([w_in[:, c0:c1], w_gate], axis=1).astype(BF16)
    w_uv = w_in[:, c2:c3].astype(BF16)
    w_merge = w_in[:, c3:].astype(BF16)

    cos, sin = _rope_tables(jnp.arange(seq, dtype=jnp.int32))
    q = _q_proj(x_bf, w_q, cos, sin, seq).reshape(bsz, seq, Q_WIDTH)
    kv_cmp, kv_rest, nsa_gates = _kv_proj(x_bf, w_kvg, cos, sin, seq)

    n_chunks = seq // CMP_STRIDE
    pe = jnp.stack([cmp_pe_k, cmp_pe_v])
    w1 = jnp.stack([cmp_w1_k, cmp_w1_v]).astype(BF16)
    w2 = jnp.stack([cmp_w2_k, cmp_w2_v]).astype(BF16)
    cmp_end = CMP_STRIDE * jnp.arange(n_chunks, dtype=jnp.int32) + CMP_BLOCK - 1
    ccos, csin = _rope_tables(cmp_end)
    cmp_cos = jnp.stack([ccos, jnp.ones_like(ccos)])
    cmp_sin = jnp.stack([csin, jnp.zeros_like(csin)])
    cmp_kv = _compress(kv_cmp.reshape(bsz, seq, _N_CMP_COLS), pe, w1, w2, cmp_cos, cmp_sin)

    n_sel = seq // SEL_BLOCK
    n_pick = min(SEL_TOPK, n_sel)
    ci = CMP_STRIDE * jnp.arange(n_chunks)
    sj = SEL_BLOCK * jnp.arange(LANES)
    overlap = jnp.clip(jnp.minimum(ci[:, None] + CMP_BLOCK, sj[None, :] + SEL_BLOCK)
                       - jnp.maximum(ci[:, None], sj[None, :]), 0, None)
    selmap_t = (overlap.astype(F32) / CMP_BLOCK).astype(BF16).T
    o_nsa = _nsa(q, cmp_kv, kv_rest.reshape(bsz, seq, _N_REST_COLS),
                 nsa_gates.reshape(bsz, seq, _GATE_COLS), selmap_t, n_sel=n_sel, n_pick=n_pick)

    o_sgu = _sgu(x_bf, w_uv, sgu_ln_g.reshape(1, -1), sgu_ln_b.reshape(1, -1), sgu_w_s, sgu_b_s.T)

    merged = _merge(x_bf, o_nsa.reshape(n_tok, Q_WIDTH), o_sgu, w_merge,
                    w_branch_a.astype(BF16), w_branch_b.astype(BF16))
    x1, x1_bf = _proj_ln(merged, w_o.astype(BF16), xf, ln1_g.reshape(1, -1), ln1_b.reshape(1, -1))

    mem_len = mem.shape[1]
    w_mem = jnp.concatenate([w_xk, w_xv], axis=1).astype(BF16)
    mem_kv = _matmul(mem.reshape(bsz * mem_len, d).astype(BF16), w_mem, tm=256, tn=512, out_dtype=BF16)
    x2 = _xattn(x1_bf.reshape(bsz, seq, d), x1.reshape(bsz, seq, d), w_xq.astype(BF16),
                mem_kv.reshape(bsz, mem_len, 2 * MEM_WIDTH), w_xo.astype(BF16),
                ln2_g.reshape(1, -1), ln2_b.reshape(1, -1)).reshape(n_tok, d)

    w_r = jnp.concatenate([w_router_grp, w_router_exp], axis=1)
    w_r = jnp.pad(w_r, ((0, 0), (0, LANES - w_r.shape[1])))
    w_r_hi = w_r.astype(BF16)
    w_r_lo = (w_r - w_r_hi.astype(F32)).astype(BF16)
    b_r = jnp.concatenate([b_router_grp, b_router_exp])
    b_r = jnp.pad(b_r, (0, LANES - b_r.shape[0])).reshape(1, LANES)
    gates, experts, ranks, seen = _router(x2, w_r_hi, w_r_lo, b_r)

    nk = n_tok * EXPERT_TOPK
    counts = seen[0, N_GROUPS:N_GROUPS + N_EXPERTS].astype(jnp.int32)
    pcounts = ((counts + MOE_BLOCK - 1) // MOE_BLOCK) * MOE_BLOCK
    pend = jnp.cumsum(pcounts)
    pstart = pend - pcounts
    e_tok = experts[:, :EXPERT_TOPK]
    seg_start = jnp.sum(jnp.where(e_tok[..., None] == jnp.arange(N_EXPERTS), pstart, 0), -1)
    dest_of = seg_start + ranks[:, :EXPERT_TOPK]
    n_blocks = -(-nk // MOE_BLOCK) + N_EXPERTS
    p_rows = n_blocks * MOE_BLOCK
    row_ids = jnp.arange(p_rows, dtype=jnp.int32)
    row_asg = jnp.full((p_rows,), -1, jnp.int32).at[dest_of.reshape(nk)].set(
        jnp.arange(nk, dtype=jnp.int32), unique_indices=True)
    is_real = row_asg >= 0
    row_tok = jnp.where(is_real, row_asg // EXPERT_TOPK, row_ids % n_tok)
    spare = EXPERT_TOPK * n_tok + ((row_ids // MOE_BLOCK) % 2) * MOE_BLOCK + row_ids % MOE_BLOCK
    row_dst = jnp.where(is_real, (row_asg % EXPERT_TOPK) * n_tok + row_asg // EXPERT_TOPK, spare)
    blk_start = jnp.arange(n_blocks) * MOE_BLOCK
    blk_on = (blk_start < pend[-1]).astype(jnp.int32)
    blk_probe = jnp.minimum(blk_start, pend[-1] - 1)
    blk_exp = jnp.minimum(jnp.sum((pend[None, :] <= blk_probe[:, None]).astype(jnp.int32), -1),
                          N_EXPERTS - 1).astype(jnp.int32)
    real_end = jnp.sum(jnp.where(blk_exp[:, None] == jnp.arange(N_EXPERTS), (pstart + counts)[None, :], 0), -1)
    blk_real = jnp.clip(real_end - blk_start, 0, MOE_BLOCK) * blk_on
    blk_groups = ((blk_real + GATHER_UNROLL - 1) // GATHER_UNROLL).astype(jnp.int32)

    y_tok = _expert_ffn(blk_exp, blk_groups, row_tok, row_dst, x2, w_exp_gate, w_exp_up, w_exp_down)
    out = _combine_ln(x2, y_tok, gates, ln3_g.reshape(1, -1), ln3_b.reshape(1, -1))
    return out.reshape(bsz, seq, d)


def kernel(x, mem, w_in, cmp_pe_k, cmp_w1_k, cmp_w2_k, cmp_pe_v, cmp_w1_v, cmp_w2_v, sgu_ln_g, sgu_ln_b, sgu_w_s, sgu_b_s, w_branch_a, w_branch_b, w_o, ln1_g, ln1_b, w_xq, w_xk, w_xv, w_xo, ln2_g, ln2_b, w_router_grp, b_router_grp, w_router_exp, b_router_exp, w_exp_gate, w_exp_up, w_exp_down, ln3_g, ln3_b):
    params = (w_in, cmp_pe_k, cmp_w1_k, cmp_w2_k, cmp_pe_v, cmp_w1_v, cmp_w2_v,
              sgu_ln_g, sgu_ln_b, sgu_w_s, sgu_b_s, w_branch_a, w_branch_b, w_o, ln1_g, ln1_b,
              w_xq, w_xk, w_xv, w_xo, ln2_g, ln2_b,
              w_router_grp, b_router_grp, w_router_exp, b_router_exp,
              w_exp_gate, w_exp_up, w_exp_down, ln3_g, ln3_b)
    h = x
    for l in range(DEPTH):
        h = _layer(h, mem, *[p[l] for p in params])
    return h
```

```python
import functools
import math

import jax
import jax.numpy as jnp
from jax import lax
from jax.experimental import pallas as pl
from jax.experimental.pallas import tpu as pltpu

HEAD_DIM = 128
NSA_HEADS = 16
NSA_KV_GROUPS = 2
NSA_HPG = NSA_HEADS // NSA_KV_GROUPS
CMP_BLOCK = 32
CMP_STRIDE = 16
CMP_HIDDEN = 256
SEL_BLOCK = 64
SEL_TOPK = 16
WINDOW = 512
SGU_WIDTH = 2048
SGU_GROUPS = 8
SGU_CHUNK = 128
MEM_HEADS = 4
N_GROUPS = 4
EXPERTS_PER_GROUP = 16
N_EXPERTS = N_GROUPS * EXPERTS_PER_GROUP
EXPERT_TOPK = 2
MOE_BLOCK = 128
ROPE_THETA = 10000.0
LN_EPS = 1e-5
SEL_BIG = 1e9
DEPTH = 1
DN_ALPHA = (2.0 * DEPTH) ** 0.25

Q_WIDTH = NSA_HEADS * HEAD_DIM
KV_WIDTH = 3 * 2 * NSA_KV_GROUPS * HEAD_DIM
NSA_GATE_WIDTH = 3 * NSA_HEADS
MEM_WIDTH = MEM_HEADS * HEAD_DIM

LANES = 128
VMEM_LIMIT = 56 * 1024 * 1024
MASKED = -1e30
M_INIT = -1e20
BF16 = jnp.bfloat16
F32 = jnp.float32


def _params(*sem):
    return pltpu.CompilerParams(dimension_semantics=sem, vmem_limit_bytes=VMEM_LIMIT)


def _dot(a, b):
    return jnp.dot(a, b, preferred_element_type=F32)


def _dot_nt(a, b):
    return lax.dot_general(a, b, (((1,), (1,)), ((), ())), preferred_element_type=F32)


def _gelu(x):
    return 0.5 * x * (1.0 + jnp.tanh(math.sqrt(2.0 / math.pi) * (x + 0.044715 * (x * x * x))))


def _sigmoid(x):
    return 1.0 / (1.0 + jnp.exp(-x))


def _layer_norm(x, g, b):
    mu = jnp.mean(x, -1, keepdims=True)
    xc = x - mu
    var = jnp.mean(xc * xc, -1, keepdims=True)
    return xc * lax.rsqrt(var + LN_EPS) * g + b


def _rope(t, cos_full, sin_signed):
    return t * cos_full + pltpu.roll(t, HEAD_DIM // 2, axis=1) * sin_signed


def _rope_tables(pos):
    inv = ROPE_THETA ** (-jnp.arange(0, HEAD_DIM, 2, dtype=F32) / HEAD_DIM)
    ang = pos.astype(F32)[:, None] * inv[None, :]
    c, s = jnp.cos(ang), jnp.sin(ang)
    return jnp.concatenate([c, c], -1), jnp.concatenate([-s, s], -1)


MXU_COLS = 256


def _mm_kernel(a_ref, b_ref, o_ref):
    o_ref[...] = _dot(a_ref[...], b_ref[...]).astype(o_ref.dtype)


def _matmul(a, b, *, tm, tn, out_dtype):
    m, k = a.shape
    n = b.shape[1]
    return pl.pallas_call(
        _mm_kernel,
        grid=(m // tm, n // tn),
        in_specs=[pl.BlockSpec((tm, k), lambda i, j: (i, 0)),
                  pl.BlockSpec((k, tn), lambda i, j: (0, j))],
        out_specs=pl.BlockSpec((tm, tn), lambda i, j: (i, j)),
        out_shape=jax.ShapeDtypeStruct((m, n), out_dtype),
        compiler_params=_params("parallel", "parallel"),
        name="matmul",
    )(a, b)


def _qproj_kernel(a_ref, b_ref, cos_ref, sin_ref, o_ref, abf_ref):
    a = a_ref[...].astype(BF16)

    @pl.when(pl.program_id(1) == 0)
    def _():
        abf_ref[...] = a

    cos, sin = cos_ref[...], sin_ref[...]
    scale = HEAD_DIM ** -0.5 * math.log2(math.e)
    for j in range(o_ref.shape[1] // MXU_COLS):
        y = _dot(a, b_ref[:, j * MXU_COLS:(j + 1) * MXU_COLS])
        for h in range(MXU_COLS // HEAD_DIM):
            sl = slice(j * MXU_COLS + h * HEAD_DIM, j * MXU_COLS + (h + 1) * HEAD_DIM)
            o_ref[:, sl] = (_rope(y[:, h * HEAD_DIM:(h + 1) * HEAD_DIM], cos, sin) * scale).astype(o_ref.dtype)


def _q_proj(x, w_q, cos, sin, seq, *, tm=1024, tn=1024):
    m, k = x.shape
    n = w_q.shape[1]
    per_seq = seq // tm
    return pl.pallas_call(
        _qproj_kernel,
        grid=(m // tm, n // tn),
        in_specs=[pl.BlockSpec((tm, k), lambda i, j: (i, 0)),
                  pl.BlockSpec((k, tn), lambda i, j: (0, j)),
                  pl.BlockSpec((tm, HEAD_DIM), lambda i, j: (i % per_seq, 0)),
                  pl.BlockSpec((tm, HEAD_DIM), lambda i, j: (i % per_seq, 0))],
        out_specs=[pl.BlockSpec((tm, tn), lambda i, j: (i, j)), pl.BlockSpec((tm, k), lambda i, j: (i, 0))],
        out_shape=[jax.ShapeDtypeStruct((m, n), BF16), jax.ShapeDtypeStruct((m, k), BF16)],
        compiler_params=_params("parallel", "arbitrary"),
        name="q_proj",
    )(x, w_q, cos, sin)


_N_CMP_COLS = 2 * NSA_KV_GROUPS * HEAD_DIM
_N_REST_CHUNKS = (KV_WIDTH - _N_CMP_COLS) // HEAD_DIM
_N_REST_COLS = (_N_REST_CHUNKS + NSA_KV_GROUPS) * HEAD_DIM
_GATE_COLS = NSA_KV_GROUPS * LANES
_SEL_SHIFT = SEL_BLOCK.bit_length() - 1
assert 1 << _SEL_SHIFT == SEL_BLOCK


def _kvproj_kernel(a_ref, b_ref, cos_ref, sin_ref, cmp_ref, rest_ref, gate_ref, *, per_seq):
    g = NSA_KV_GROUPS
    y = _dot(a_ref[...], b_ref[...])
    cos, sin = cos_ref[...], sin_ref[...]
    cmp_ref[...] = y[:, :_N_CMP_COLS]
    tm = y.shape[0]
    pos = (pl.program_id(0) % per_seq) * tm + lax.broadcasted_iota(jnp.int32, (tm, LANES), 0)
    lane = lax.broadcasted_iota(jnp.int32, (tm, LANES), 1)
    blk_onehot = jnp.where(lane == (pos >> _SEL_SHIFT), 1.0, 0.0).astype(rest_ref.dtype)
    for c in range(_N_REST_CHUNKS):
        t = y[:, _N_CMP_COLS + c * HEAD_DIM:_N_CMP_COLS + (c + 1) * HEAD_DIM]
        if (c % (2 * g)) < g:
            t = _rope(t, cos, sin)
        dst = 2 * c if c < g else g + c
        rest_ref[:, dst * HEAD_DIM:(dst + 1) * HEAD_DIM] = t.astype(rest_ref.dtype)
        if c < g:
            rest_ref[:, (dst + 1) * HEAD_DIM:(dst + 2) * HEAD_DIM] = blk_onehot
    gate_ref[...] = _sigmoid(y[:, KV_WIDTH:])


def _kv_proj(x_bf, w_kvg, cos, sin, seq, *, tm=1024):
    m, k = x_bf.shape
    n = w_kvg.shape[1]
    per_seq = seq // tm
    return pl.pallas_call(
        functools.partial(_kvproj_kernel, per_seq=per_seq),
        grid=(m // tm,),
        in_specs=[pl.BlockSpec((tm, k), lambda i: (i, 0)),
                  pl.BlockSpec((k, n), lambda i: (0, 0)),
                  pl.BlockSpec((tm, HEAD_DIM), lambda i: (i % per_seq, 0)),
                  pl.BlockSpec((tm, HEAD_DIM), lambda i: (i % per_seq, 0))],
        out_specs=[pl.BlockSpec((tm, _N_CMP_COLS), lambda i: (i, 0)),
                   pl.BlockSpec((tm, _N_REST_COLS), lambda i: (i, 0)),
                   pl.BlockSpec((tm, _GATE_COLS), lambda i: (i, 0))],
        out_shape=[jax.ShapeDtypeStruct((m, _N_CMP_COLS), F32),
                   jax.ShapeDtypeStruct((m, _N_REST_COLS), BF16),
                   jax.ShapeDtypeStruct((m, _GATE_COLS), F32)],
        compiler_params=_params("parallel"),
        name="kv_gate_proj",
    )(x_bf, w_kvg, cos, sin)


def _compress_kernel(kc_ref, pe_ref, w1_ref, w2_ref, cos_ref, sin_ref, o_ref):
    n_chunks = o_ref.shape[2]
    top = jnp.zeros((n_chunks, CMP_HIDDEN), F32)
    bot = jnp.zeros((n_chunks, CMP_HIDDEN), F32)
    for p in range(CMP_STRIDE):
        xp = kc_ref[0, pl.ds(p, n_chunks, stride=CMP_STRIDE), :]
        q = CMP_STRIDE + p
        top += _dot((xp + pe_ref[0, p:p + 1, :]).astype(BF16), w1_ref[0, p * HEAD_DIM:(p + 1) * HEAD_DIM, :])
        bot += _dot((xp + pe_ref[0, q:q + 1, :]).astype(BF16), w1_ref[0, q * HEAD_DIM:(q + 1) * HEAD_DIM, :])
    h = top + pltpu.roll(bot, n_chunks - 1, axis=0)
    y = _dot(_gelu(h).astype(BF16), w2_ref[0])
    o_ref[0, 0] = _rope(y, cos_ref[0], sin_ref[0]).astype(o_ref.dtype)


def _compress(kv_cmp, pe, w1, w2, cos, sin):
    bsz, seq, width = kv_cmp.shape
    four = width // HEAD_DIM
    n_chunks = seq // CMP_STRIDE
    g = NSA_KV_GROUPS
    return pl.pallas_call(
        _compress_kernel,
        grid=(four, bsz),
        in_specs=[pl.BlockSpec((1, seq, HEAD_DIM), lambda a, b: (b, 0, a)),
                  pl.BlockSpec((1, CMP_BLOCK, HEAD_DIM), lambda a, b: (a // g, 0, 0)),
                  pl.BlockSpec((1, CMP_BLOCK * HEAD_DIM, CMP_HIDDEN), lambda a, b: (a // g, 0, 0)),
                  pl.BlockSpec((1, CMP_HIDDEN, HEAD_DIM), lambda a, b: (a // g, 0, 0)),
                  pl.BlockSpec((1, n_chunks, HEAD_DIM), lambda a, b: (a // g, 0, 0)),
                  pl.BlockSpec((1, n_chunks, HEAD_DIM), lambda a, b: (a // g, 0, 0))],
        out_specs=pl.BlockSpec((1, 1, n_chunks, HEAD_DIM), lambda a, b: (a, b, 0, 0)),
        out_shape=jax.ShapeDtypeStruct((four, bsz, n_chunks, HEAD_DIM), BF16),
        compiler_params=_params("parallel", "parallel"),
        name="compress",
    )(kv_cmp, pe, w1, w2, cos, sin)


SEL_TILE = 512
SEL_WIDE = 2
NSA_TQ = 256
SUBLANES = 8


def _softmax_parts(s3, bias):
    s3 = s3 + bias[None]
    m = jnp.max(s3, -1, keepdims=True)
    m = jnp.where(m > 0.5 * MASKED, m, 0.0)
    p = jnp.exp2(s3 - m)
    return p, 1.0 / jnp.maximum(jnp.sum(p, -1, keepdims=True), 1e-30)


def _topk_unselected(score_t, n_sel, n_pick):
    tq = score_t.shape[1]
    n_grp = n_sel // SUBLANES
    grp = [score_t[g * SUBLANES:(g + 1) * SUBLANES] for g in range(n_grp)]
    row_in_grp = lax.broadcasted_iota(jnp.int32, (SUBLANES, tq), 0)
    later = [jnp.where(row_in_grp > li, 1.0, 0.0) for li in range(SUBLANES)]
    beaten = [jnp.zeros((SUBLANES, tq), F32) for _ in range(n_grp)]
    for i in range(n_sel):
        gi, li = divmod(i, SUBLANES)
        row = grp[gi][li:li + 1, :]
        for g in range(n_grp):
            if g > gi:
                beaten[g] = beaten[g] + jnp.where(row >= grp[g], 1.0, 0.0)
            elif g < gi:
                beaten[g] = beaten[g] + jnp.where(row > grp[g], 1.0, 0.0)
            else:
                beaten[g] = (beaten[g] + jnp.where(row > grp[g], 1.0, 0.0)
                             + jnp.where(row == grp[g], later[li], 0.0))
    return jnp.concatenate([jnp.where(b < n_pick, 0.0, MASKED) for b in beaten], axis=0)


def _nsa_kernel(q_ref, kc_ref, vc_ref, ks_ref, vs_ref, kw_ref, vw_ref, gate_ref, selmap_t_ref, o_ref,
                *, n_sel, n_pick):
    hpg = NSA_HPG
    tq = NSA_TQ
    c = pl.program_id(2)
    q_t = q_ref[0]
    q8 = jnp.concatenate([q_t[:, n * HEAD_DIM:(n + 1) * HEAD_DIM] for n in range(hpg)], axis=0)
    t_pos = c * tq + lax.broadcasted_iota(jnp.int32, (tq, 1), 0)
    t_lane = c * tq + lax.broadcasted_iota(jnp.int32, (1, tq), 1)

    kc = kc_ref[0, 0]
    n_cmp_pad = kc.shape[0]
    k_idx = lax.broadcasted_iota(jnp.int32, (1, n_cmp_pad), 1)
    vis_c = (CMP_STRIDE * k_idx + (CMP_BLOCK - 1) <= t_pos) & (k_idx < n_cmp_pad - 1)
    p_c, r_c = _softmax_parts(_dot_nt(q8, kc).reshape(hpg, tq, n_cmp_pad), jnp.where(vis_c, 0.0, MASKED))
    o_c = _dot(p_c.reshape(hpg * tq, n_cmp_pad).astype(BF16), vc_ref[0, 0]).reshape(hpg, tq, HEAD_DIM) * r_c

    p_sum = jnp.sum(p_c * r_c, axis=0)
    p_hi = p_sum.astype(BF16)
    p_lo = (p_sum - p_hi.astype(F32)).astype(BF16)
    selmap_t = selmap_t_ref[...]
    imp_t = _dot_nt(selmap_t, p_hi) + _dot_nt(selmap_t, p_lo)
    blk = lax.broadcasted_iota(jnp.int32, (LANES, tq), 0)
    cur = t_lane >> _SEL_SHIFT
    forced = (blk == 0) | (blk == cur) | (blk == cur - 1)
    score_t = jnp.where(forced, SEL_BIG, jnp.where(blk <= cur, imp_t, -SEL_BIG))
    unsel_t = _topk_unselected(score_t, n_sel, n_pick)
    unsel_t = jnp.concatenate([unsel_t, jnp.zeros((LANES - n_sel, tq), F32)], axis=0)
    unsel = unsel_t.T.astype(BF16)
    q_aug = jnp.concatenate([q8, jnp.concatenate([unsel] * hpg, axis=0)], axis=1)

    def sel_tile(kt, size, carry, bias):
        m_old, l_old, acc = carry
        base = pl.multiple_of(kt * size, size)
        k = ks_ref[0, pl.ds(base, size), :]
        v = vs_ref[0, pl.ds(base, size), :]
        s = _dot_nt(q_aug, k).reshape(hpg, tq, size)
        if bias is not None:
            s = s + bias[None]
        m_new = jnp.maximum(m_old, jnp.max(s, -1, keepdims=True))
        alpha = jnp.exp2(m_old - m_new)
        p = jnp.exp2(s - m_new)
        l_new = alpha * l_old + jnp.sum(p, -1, keepdims=True)
        pv = _dot(p.reshape(hpg * tq, size).astype(BF16), v).reshape(hpg, tq, HEAD_DIM)
        return m_new, l_new, alpha * acc + pv

    n_full = c // (SEL_TILE // tq)
    n_wide = n_full // SEL_WIDE
    carry = (jnp.full((hpg, tq, 1), M_INIT, F32), jnp.zeros((hpg, tq, 1), F32),
             jnp.zeros((hpg, tq, HEAD_DIM), F32))
    carry = lax.fori_loop(0, n_wide, lambda kt, cr: sel_tile(kt, SEL_WIDE * SEL_TILE, cr, None), carry)
    carry = lax.fori_loop(n_wide * SEL_WIDE, n_full, lambda kt, cr: sel_tile(kt, SEL_TILE, cr, None), carry)
    kp_last = n_full * SEL_TILE + lax.broadcasted_iota(jnp.int32, (1, SEL_TILE), 1)
    _, l_s, acc_s = sel_tile(n_full, SEL_TILE, carry, jnp.where(kp_last <= t_pos, 0.0, MASKED))
    o_s = acc_s * (1.0 / jnp.maximum(l_s, 1e-30))

    span = WINDOW + tq
    start = pl.multiple_of(jnp.maximum(c * tq - WINDOW, 0), tq)
    kw = kw_ref[0, pl.ds(start, span), :]
    vw = vw_ref[0, pl.ds(start, span), :]
    kp = start + lax.broadcasted_iota(jnp.int32, (1, span), 1)
    vis_w = (kp <= t_pos) & (t_pos - kp < WINDOW)
    p_w, r_w = _softmax_parts(_dot_nt(q8, kw).reshape(hpg, tq, span), jnp.where(vis_w, 0.0, MASKED))
    o_w = _dot(p_w.reshape(hpg * tq, span).astype(BF16), vw).reshape(hpg, tq, HEAD_DIM) * r_w

    gates = gate_ref[0]
    for n in range(hpg):
        out = (gates[:, n:n + 1] * o_c[n]
               + gates[:, hpg + n:hpg + n + 1] * o_s[n]
               + gates[:, 2 * hpg + n:2 * hpg + n + 1] * o_w[n])
        o_ref[0, :, n * HEAD_DIM:(n + 1) * HEAD_DIM] = out.astype(o_ref.dtype)


def _nsa(q, cmp_kv, rest, gates, selmap_t, *, n_sel, n_pick):
    bsz, seq, _ = q.shape
    g = NSA_KV_GROUPS
    n_cmp_pad = cmp_kv.shape[2]
    gw = NSA_HPG * HEAD_DIM
    assert n_sel % SUBLANES == 0 and n_sel <= LANES and seq >= WINDOW + NSA_TQ
    assert SEL_TILE % NSA_TQ == 0 and seq % SEL_TILE == 0
    kv_spec = lambda off: pl.BlockSpec((1, seq, HEAD_DIM), lambda b, gi, c: (b, 0, off + gi))
    return pl.pallas_call(
        functools.partial(_nsa_kernel, n_sel=n_sel, n_pick=n_pick),
        grid=(bsz, g, seq // NSA_TQ),
        in_specs=[pl.BlockSpec((1, NSA_TQ, gw), lambda b, gi, c: (b, c, gi)),
                  pl.BlockSpec((1, 1, n_cmp_pad, HEAD_DIM), lambda b, gi, c: (gi, b, 0, 0)),
                  pl.BlockSpec((1, 1, n_cmp_pad, HEAD_DIM), lambda b, gi, c: (g + gi, b, 0, 0)),
                  pl.BlockSpec((1, seq, 2 * HEAD_DIM), lambda b, gi, c: (b, 0, gi)),
                  kv_spec(2 * g), kv_spec(3 * g), kv_spec(4 * g),
                  pl.BlockSpec((1, NSA_TQ, LANES), lambda b, gi, c: (b, c, gi)),
                  pl.BlockSpec(selmap_t.shape, lambda b, gi, c: (0, 0))],
        out_specs=pl.BlockSpec((1, NSA_TQ, gw), lambda b, gi, c: (b, c, gi)),
        out_shape=jax.ShapeDtypeStruct(q.shape, BF16),
        compiler_params=_params("parallel", "parallel", "arbitrary"),
        name="nsa_attention",
    )(q, cmp_kv, cmp_kv, rest, rest, rest, rest, gates, selmap_t)


def _sgu_kernel(x_ref, w_ref, g_ref, b_ref, ws_ref, bs_ref, o_ref, *, chunks):
    gd = SGU_WIDTH // SGU_GROUPS
    row = lax.broadcasted_iota(jnp.int32, (SGU_CHUNK, SGU_CHUNK), 0)
    col = lax.broadcasted_iota(jnp.int32, (SGU_CHUNK, SGU_CHUNK), 1)
    causal = col <= row
    bs = bs_ref[...]
    x = x_ref[...]
    v = jnp.concatenate(
        [_gelu(_dot(x, w_ref[:, SGU_WIDTH + j * MXU_COLS:SGU_WIDTH + (j + 1) * MXU_COLS]))
         for j in range(SGU_WIDTH // MXU_COLS)], axis=1)
    vn = _layer_norm(v, g_ref[...], b_ref[...]).astype(BF16)
    for gi in range(SGU_GROUPS):
        cols = slice(gi * gd, (gi + 1) * gd)
        u = _gelu(_dot(x, w_ref[:, cols]))
        w = jnp.where(causal, ws_ref[gi], 0.0).astype(BF16)
        for ch in range(chunks):
            rows = slice(ch * SGU_CHUNK, (ch + 1) * SGU_CHUNK)
            z = _dot(w, vn[rows, cols]) + bs[:, gi:gi + 1]
            o_ref[rows, cols] = (u[rows] * z).astype(o_ref.dtype)


def _sgu(x_bf, w_uv, ln_g, ln_b, w_s, b_s_t, *, chunks=4):
    m, d = x_bf.shape
    tm = chunks * SGU_CHUNK
    return pl.pallas_call(
        functools.partial(_sgu_kernel, chunks=chunks),
        grid=(m // tm,),
        in_specs=[pl.BlockSpec((tm, d), lambda i: (i, 0)),
                  pl.BlockSpec(w_uv.shape, lambda i: (0, 0), pipeline_mode=pl.Buffered(1)),
                  pl.BlockSpec((1, SGU_WIDTH), lambda i: (0, 0)),
                  pl.BlockSpec((1, SGU_WIDTH), lambda i: (0, 0)),
                  pl.BlockSpec(w_s.shape, lambda i: (0, 0, 0)),
                  pl.BlockSpec(b_s_t.shape, lambda i: (0, 0))],
        out_specs=pl.BlockSpec((tm, SGU_WIDTH), lambda i: (i, 0)),
        out_shape=jax.ShapeDtypeStruct((m, SGU_WIDTH), BF16),
        compiler_params=_params("parallel"),
        name="sgu",
    )(x_bf, w_uv, ln_g, ln_b, w_s, b_s_t)


def _merge_kernel(x_ref, a_ref, s_ref, wga_ref, wgb_ref, wa_ref, wb_ref, o_ref):
    x = x_ref[...]
    ga = _sigmoid(_dot(x, wga_ref[...]))
    gb = _sigmoid(_dot(x, wgb_ref[...]))
    o_ref[...] = (ga * _dot(a_ref[...], wa_ref[...]) + gb * _dot(s_ref[...], wb_ref[...])).astype(o_ref.dtype)


def _merge(x_bf, o_nsa, o_sgu, w_merge, w_a, w_b, *, tm=1024, tn=512):
    m, k = x_bf.shape
    n = w_a.shape[1]
    nb = n // tn
    row = pl.BlockSpec((tm, k), lambda i, j: (i, 0))
    col = pl.BlockSpec((k, tn), lambda i, j: (0, j))
    return pl.pallas_call(
        _merge_kernel,
        grid=(m // tm, nb),
        in_specs=[row, row, row, col, pl.BlockSpec((k, tn), lambda i, j: (0, nb + j)), col, col],
        out_specs=pl.BlockSpec((tm, tn), lambda i, j: (i, j)),
        out_shape=jax.ShapeDtypeStruct((m, n), BF16),
        compiler_params=_params("parallel", "parallel"),
        name="merge",
    )(x_bf, o_nsa, o_sgu, w_merge, w_merge, w_a, w_b)


def _proj_ln_kernel(a_ref, w_ref, x_ref, g_ref, b_ref, o_ref, obf_ref):
    y = DN_ALPHA * x_ref[...] + _dot(a_ref[...], w_ref[...])
    out = _layer_norm(y, g_ref[...], b_ref[...])
    o_ref[...] = out
    obf_ref[...] = out.astype(BF16)


def _proj_ln(a_bf, w, x, ln_g, ln_b, *, tm=512):
    m, k = a_bf.shape
    n = w.shape[1]
    vec = pl.BlockSpec((1, n), lambda i: (0, 0))
    return pl.pallas_call(
        _proj_ln_kernel,
        grid=(m // tm,),
        in_specs=[pl.BlockSpec((tm, k), lambda i: (i, 0)),
                  pl.BlockSpec((k, n), lambda i: (0, 0)),
                  pl.BlockSpec((tm, n), lambda i: (i, 0)), vec, vec],
        out_specs=[pl.BlockSpec((tm, n), lambda i: (i, 0)), pl.BlockSpec((tm, n), lambda i: (i, 0))],
        out_shape=[jax.ShapeDtypeStruct((m, n), F32), jax.ShapeDtypeStruct((m, n), BF16)],
        compiler_params=_params("parallel"),
        name="proj_ln",
    )(a_bf, w, x, ln_g, ln_b)


def _xattn_kernel(xbf_ref, x_ref, wq_ref, kv_ref, wo_ref, g_ref, b_ref, o_ref):
    q = (_dot(xbf_ref[0], wq_ref[...]) * HEAD_DIM ** -0.5).astype(BF16)
    kv = kv_ref[0]
    heads = []
    for h in range(MEM_HEADS):
        cols = slice(h * HEAD_DIM, (h + 1) * HEAD_DIM)
        s = _dot_nt(q[:, cols], kv[:, cols])
        p = jnp.exp(s - jnp.max(s, -1, keepdims=True))
        p = p / jnp.sum(p, -1, keepdims=True)
        heads.append(_dot(p.astype(BF16), kv[:, MEM_WIDTH + h * HEAD_DIM:MEM_WIDTH + (h + 1) * HEAD_DIM]))
    o = jnp.concatenate(heads, axis=1).astype(BF16)
    y = DN_ALPHA * x_ref[0] + _dot(o, wo_ref[...])
    o_ref[0] = _layer_norm(y, g_ref[...], b_ref[...])


def _xattn(x_bf, x, w_xq, mem_kv, w_xo, ln_g, ln_b, *, tm=512):
    bsz, seq, d = x.shape
    mem_len = mem_kv.shape[1]
    vec = pl.BlockSpec((1, d), lambda b, i: (0, 0))
    return pl.pallas_call(
        _xattn_kernel,
        grid=(bsz, seq // tm),
        in_specs=[pl.BlockSpec((1, tm, d), lambda b, i: (b, i, 0)),
                  pl.BlockSpec((1, tm, d), lambda b, i: (b, i, 0)),
                  pl.BlockSpec(w_xq.shape, lambda b, i: (0, 0)),
                  pl.BlockSpec((1, mem_len, 2 * MEM_WIDTH), lambda b, i: (b, 0, 0)),
                  pl.BlockSpec(w_xo.shape, lambda b, i: (0, 0)), vec, vec],
        out_specs=pl.BlockSpec((1, tm, d), lambda b, i: (b, i, 0)),
        out_shape=jax.ShapeDtypeStruct(x.shape, F32),
        compiler_params=_params("parallel", "parallel"),
        name="xattn_ln",
    )(x_bf, x, w_xq, mem_kv, w_xo, ln_g, ln_b)


def _router_kernel(x_ref, whi_ref, wlo_ref, b_ref, gate_ref, exp_ref, rank_ref, cnt_ref, seen_ref):
    @pl.when(pl.program_id(0) == 0)
    def _():
        seen_ref[...] = jnp.zeros_like(seen_ref)

    x = x_ref[...]
    x_hi = x.astype(BF16)
    x_lo = (x - x_hi.astype(F32)).astype(BF16)
    whi = whi_ref[...]
    logits = _dot(x_hi, whi) + _dot(x_hi, wlo_ref[...]) + _dot(x_lo, whi) + b_ref[...]
    rows = logits.shape[0]
    lane = lax.broadcasted_iota(jnp.int32, (rows, LANES), 1)
    lane_f = lane.astype(F32)
    is_grp = lane < N_GROUPS
    lg = jnp.where(is_grp, logits, MASKED)
    eg = jnp.where(is_grp, jnp.exp(lg - jnp.max(lg, -1, keepdims=True)), 0.0)
    p_grp = eg / jnp.sum(eg, -1, keepdims=True)
    g_grp = jnp.max(p_grp, -1, keepdims=True)
    grp = jnp.min(jnp.where((p_grp == g_grp) & is_grp, lane_f, 1e9), -1, keepdims=True).astype(jnp.int32)
    lo = N_GROUPS + grp * EXPERTS_PER_GROUP
    in_grp = (lane >= lo) & (lane < lo + EXPERTS_PER_GROUP)
    le = jnp.where(in_grp, logits, MASKED)
    ee = jnp.where(in_grp, jnp.exp(le - jnp.max(le, -1, keepdims=True)), 0.0)
    p_exp = jnp.where(in_grp, ee / jnp.sum(ee, -1, keepdims=True), -1.0)
    v1 = jnp.max(p_exp, -1, keepdims=True)
    i1 = jnp.min(jnp.where(p_exp == v1, lane_f, 1e9), -1, keepdims=True)
    p_rest = jnp.where(lane_f == i1, -1.0, p_exp)
    v2 = jnp.max(p_rest, -1, keepdims=True)
    i2 = jnp.min(jnp.where(p_rest == v2, lane_f, 1e9), -1, keepdims=True)
    denom = v1 + v2
    gate_ref[...] = jnp.where(lane == 0, g_grp * v1 / denom, jnp.where(lane == 1, g_grp * v2 / denom, 0.0))
    exp_ref[...] = jnp.where(lane == 0, i1, jnp.where(lane == 1, i2, float(N_GROUPS))).astype(jnp.int32) - N_GROUPS

    onehot = jnp.where((lane_f == i1) | (lane_f == i2), 1.0, 0.0)
    tok_r = lax.broadcasted_iota(jnp.int32, (rows, rows), 0)
    tok_c = lax.broadcasted_iota(jnp.int32, (rows, rows), 1)
    earlier = jnp.where(tok_c < tok_r, 1.0, 0.0).astype(BF16)
    before = _dot(earlier, onehot.astype(BF16)) + seen_ref[...]
    r1 = jnp.sum(jnp.where(lane_f == i1, before, 0.0), -1, keepdims=True)
    r2 = jnp.sum(jnp.where(lane_f == i2, before, 0.0), -1, keepdims=True)
    rank_ref[...] = jnp.where(lane == 0, r1, jnp.where(lane == 1, r2, 0.0)).astype(jnp.int32)
    seen_ref[...] += jnp.sum(onehot, axis=0, keepdims=True)
    cnt_ref[...] = seen_ref[...]


def _router(x, w_hi, w_lo, bias, *, tm=512):
    m, d = x.shape
    wspec = pl.BlockSpec((d, LANES), lambda i: (0, 0))
    vec = pl.BlockSpec((1, LANES), lambda i: (0, 0))
    tile = pl.BlockSpec((tm, LANES), lambda i: (i, 0))
    return pl.pallas_call(
        _router_kernel,
        grid=(m // tm,),
        in_specs=[pl.BlockSpec((tm, d), lambda i: (i, 0)), wspec, wspec, vec],
        out_specs=[tile, tile, tile, vec],
        out_shape=[jax.ShapeDtypeStruct((m, LANES), F32), jax.ShapeDtypeStruct((m, LANES), jnp.int32),
                   jax.ShapeDtypeStruct((m, LANES), jnp.int32), jax.ShapeDtypeStruct((1, LANES), F32)],
        scratch_shapes=[pltpu.VMEM((1, LANES), F32)],
        compiler_params=_params("arbitrary"),
        name="router",
    )(x, w_hi, w_lo, bias)


GATHER_UNROLL = 8


def _ffn_kernel(exp_ref, on_ref, first_ref, slot_ref, next_ref, tok_ref, dst_ref, x_hbm, wg_hbm, wu_hbm, wd_hbm,
                y_hbm, wg_buf, wu_buf, wd_buf, h_buf, y_buf, sem, row_sem, out_sem):
    i = pl.program_id(0)
    n_blk = pl.num_programs(0)
    slot = slot_ref[i]
    rows = MOE_BLOCK

    def out_copy(blk, g, u):
        dst = dst_ref[blk * rows + g * GATHER_UNROLL + u]
        return pltpu.make_async_copy(y_buf.at[blk % 2, g, pl.ds(u, 1), :], y_hbm.at[pl.ds(dst, 1), :],
                                     out_sem.at[blk % 2])

    def weight_copies(e, s):
        return (pltpu.make_async_copy(wg_hbm.at[e], wg_buf.at[s], sem.at[s, 0]),
                pltpu.make_async_copy(wu_hbm.at[e], wu_buf.at[s], sem.at[s, 1]),
                pltpu.make_async_copy(wd_hbm.at[e], wd_buf.at[s], sem.at[s, 2]))

    def row_copy(blk, g, u):
        tok = tok_ref[blk * rows + g * GATHER_UNROLL + u]
        return pltpu.make_async_copy(x_hbm.at[pl.ds(tok, 1), :], h_buf.at[blk % 2, g, pl.ds(u, 1), :],
                                     row_sem.at[blk % 2])

    def for_rows(blk, fn):
        def body(g, carry):
            for u in range(GATHER_UNROLL):
                fn(blk, g, u)
            return carry
        lax.fori_loop(0, on_ref[blk], body, 0)

    @pl.when(i == 0)
    def _():
        for cp in weight_copies(exp_ref[0], 0):
            cp.start()
        h_buf[...] = jnp.zeros_like(h_buf)
        for_rows(0, lambda b, g, u: row_copy(b, g, u).start())
        y_buf[...] = jnp.zeros_like(y_buf)
        spare = y_hbm.shape[0] - 2 * rows
        fills = [pltpu.make_async_copy(y_buf.at[s, g],
                                       y_hbm.at[pl.ds(spare + s * rows + g * GATHER_UNROLL, GATHER_UNROLL), :],
                                       out_sem.at[s])
                 for s in range(2) for g in range(rows // GATHER_UNROLL)]
        for cp in fills:
            cp.start()
        for cp in fills:
            cp.wait()

    prev2 = jnp.maximum(i - 2, 0)

    @pl.when((i >= 2) & (on_ref[prev2] != 0))
    def _():
        for_rows(i - 2, lambda b, g, u: out_copy(b, g, u).wait())

    nxt_blk = jnp.minimum(i + 1, n_blk - 1)

    @pl.when((i + 1 < n_blk) & (on_ref[nxt_blk] != 0))
    def _():
        for_rows(i + 1, lambda b, g, u: row_copy(b, g, u).start())

    @pl.when(first_ref[i] != 0)
    def _():
        nxt = next_ref[i]

        @pl.when(nxt >= 0)
        def _():
            for cp in weight_copies(nxt, 1 - slot):
                cp.start()

        for cp in weight_copies(exp_ref[i], slot):
            cp.wait()

    @pl.when(on_ref[i] != 0)
    def _():
        for_rows(i, lambda b, g, u: row_copy(b, g, u).wait())
        h = h_buf[i % 2].reshape(rows, h_buf.shape[-1]).astype(BF16)
        a = _dot(h, wg_buf[slot].astype(BF16))
        a = a * _sigmoid(a) * _dot(h, wu_buf[slot].astype(BF16))
        y_buf[i % 2] = _dot(a.astype(BF16), wd_buf[slot].astype(BF16)).reshape(y_buf.shape[1:])
        for_rows(i, lambda b, g, u: out_copy(b, g, u).start())

    @pl.when(i == n_blk - 1)
    def _():
        prev1 = jnp.maximum(i - 1, 0)

        @pl.when((i >= 1) & (on_ref[prev1] != 0))
        def _():
            for_rows(i - 1, lambda b, g, u: out_copy(b, g, u).wait())

        @pl.when(on_ref[i] != 0)
        def _():
            for_rows(i, lambda b, g, u: out_copy(b, g, u).wait())


def _expert_ffn(blk_exp, blk_on, row_tok, row_dst, x, w_gate, w_up, w_down):
    p_rows = row_tok.shape[0]
    n_tok, d = x.shape
    n_exp, _, ff = w_gate.shape
    n_blocks = p_rows // MOE_BLOCK
    prev_exp = jnp.concatenate([jnp.full((1,), -1, jnp.int32), blk_exp[:-1]])
    first = ((blk_on != 0) & (blk_exp != prev_exp)).astype(jnp.int32)
    seg_id = jnp.cumsum(first) - 1
    seg_ids = jnp.arange(n_exp + 1)
    hit = (first[:, None] != 0) & (seg_id[:, None] == seg_ids[None, :])
    seg_exp = jnp.sum(jnp.where(hit, blk_exp[:, None] + 1, 0), 0) - 1
    blk_next = jnp.sum(jnp.where(seg_ids[None, :] == seg_id[:, None] + 1, seg_exp[None, :] + 1, 0), 1) - 1
    blk_slot = jnp.maximum(seg_id, 0) % 2
    grid_spec = pltpu.PrefetchScalarGridSpec(
        num_scalar_prefetch=7,
        grid=(n_blocks,),
        in_specs=[pl.BlockSpec(memory_space=pl.ANY)] * 4,
        out_specs=pl.BlockSpec(memory_space=pl.ANY),
        scratch_shapes=[pltpu.VMEM((2, d, ff), F32), pltpu.VMEM((2, d, ff), F32), pltpu.VMEM((2, ff, d), F32),
                        pltpu.VMEM((2, MOE_BLOCK // GATHER_UNROLL, GATHER_UNROLL, d), F32),
                        pltpu.VMEM((2, MOE_BLOCK // GATHER_UNROLL, GATHER_UNROLL, d), F32),
                        pltpu.SemaphoreType.DMA((2, 3)), pltpu.SemaphoreType.DMA((2,)),
                        pltpu.SemaphoreType.DMA((2,))],
    )
    return pl.pallas_call(
        _ffn_kernel,
        grid_spec=grid_spec,
        out_shape=jax.ShapeDtypeStruct((EXPERT_TOPK * n_tok + 2 * MOE_BLOCK, d), F32),
        compiler_params=_params("arbitrary"),
        name="expert_ffn",
    )(blk_exp, blk_on, first, blk_slot.astype(jnp.int32), blk_next.astype(jnp.int32), row_tok, row_dst,
      x, w_gate, w_up, w_down)


def _combine_ln_kernel(x_ref, y0_ref, y1_ref, gate_ref, g_ref, b_ref, o_ref):
    gates = gate_ref[...]
    moe = y0_ref[...] * gates[:, 0:1] + y1_ref[...] * gates[:, 1:2]
    o_ref[...] = _layer_norm(DN_ALPHA * x_ref[...] + moe, g_ref[...], b_ref[...])


def _combine_ln(x, y_tok, gates, ln_g, ln_b, *, tm=256):
    m, d = x.shape
    row = pl.BlockSpec((tm, d), lambda i: (i, 0))
    vec = pl.BlockSpec((1, d), lambda i: (0, 0))
    return pl.pallas_call(
        _combine_ln_kernel,
        grid=(m // tm,),
        in_specs=[row, row, pl.BlockSpec((tm, d), lambda i: (m // tm + i, 0)),
                  pl.BlockSpec((tm, LANES), lambda i: (i, 0)), vec, vec],
        out_specs=row,
        out_shape=jax.ShapeDtypeStruct((m, d), F32),
        compiler_params=_params("parallel"),
        name="combine_ln",
    )(x, y_tok, y_tok, gates, ln_g, ln_b)


def _layer(x, mem, w_in, cmp_pe_k, cmp_w1_k, cmp_w2_k, cmp_pe_v, cmp_w1_v, cmp_w2_v,
           sgu_ln_g, sgu_ln_b, sgu_w_s, sgu_b_s, w_branch_a, w_branch_b, w_o, ln1_g, ln1_b,
           w_xq, w_xk, w_xv, w_xo, ln2_g, ln2_b,
           w_router_grp, b_router_grp, w_router_exp, b_router_exp,
           w_exp_gate, w_exp_up, w_exp_down, ln3_g, ln3_b):
    bsz, seq, d = x.shape
    n_tok = bsz * seq
    g = NSA_KV_GROUPS
    hpg = NSA_HPG
    xf = x.reshape(n_tok, d)

    c0 = Q_WIDTH
    c1 = c0 + KV_WIDTH
    c2 = c1 + NSA_GATE_WIDTH
    c3 = c2 + 2 * SGU_WIDTH
    w_q = w_in[:, :c0].astype(BF16)
    w_gate = w_in[:, c1:c2].reshape(d, g, hpg, 3).transpose(0, 1, 3, 2).reshape(d, g, 3 * hpg)
    w_gate = jnp.pad(w_gate, ((0, 0), (0, 0), (0, LANES - 3 * hpg))).reshape(d, g * LANES)
    w_kvg = jnp.concatenate([w_in[:, c0:c1], w_gate], axis=1).astype(BF16)
    w_uv = w_in[:, c2:c3].astype(BF16)
    w_merge = w_in[:, c3:].astype(BF16)

    cos, sin = _rope_tables(jnp.arange(seq, dtype=jnp.int32))
    q, x_bf = _q_proj(xf, w_q, cos, sin, seq)
    q = q.reshape(bsz, seq, Q_WIDTH)
    kv_cmp, kv_rest, nsa_gates = _kv_proj(x_bf, w_kvg, cos, sin, seq)

    n_chunks = seq // CMP_STRIDE
    pe = jnp.stack([cmp_pe_k, cmp_pe_v])
    w1 = jnp.stack([cmp_w1_k, cmp_w1_v]).astype(BF16)
    w2 = jnp.stack([cmp_w2_k, cmp_w2_v]).astype(BF16)
    cmp_end = CMP_STRIDE * jnp.arange(n_chunks, dtype=jnp.int32) + CMP_BLOCK - 1
    ccos, csin = _rope_tables(cmp_end)
    cmp_cos = jnp.stack([ccos, jnp.ones_like(ccos)])
    cmp_sin = jnp.stack([csin, jnp.zeros_like(csin)])
    cmp_kv = _compress(kv_cmp.reshape(bsz, seq, _N_CMP_COLS), pe, w1, w2, cmp_cos, cmp_sin)

    n_sel = seq // SEL_BLOCK
    n_pick = min(SEL_TOPK, n_sel)
    ci = CMP_STRIDE * jnp.arange(n_chunks)
    sj = SEL_BLOCK * jnp.arange(LANES)
    overlap = jnp.clip(jnp.minimum(ci[:, None] + CMP_BLOCK, sj[None, :] + SEL_BLOCK)
                       - jnp.maximum(ci[:, None], sj[None, :]), 0, None)
    selmap_t = (overlap.astype(F32) / CMP_BLOCK).astype(BF16).T
    o_nsa = _nsa(q, cmp_kv, kv_rest.reshape(bsz, seq, _N_REST_COLS),
                 nsa_gates.reshape(bsz, seq, _GATE_COLS), selmap_t, n_sel=n_sel, n_pick=n_pick)

    o_sgu = _sgu(x_bf, w_uv, sgu_ln_g.reshape(1, -1), sgu_ln_b.reshape(1, -1), sgu_w_s, sgu_b_s.T)

    merged = _merge(x_bf, o_nsa.reshape(n_tok, Q_WIDTH), o_sgu, w_merge,
                    w_branch_a.astype(BF16), w_branch_b.astype(BF16))
    x1, x1_bf = _proj_ln(merged, w_o.astype(BF16), xf, ln1_g.reshape(1, -1), ln1_b.reshape(1, -1))

    mem_len = mem.shape[1]
    w_mem = jnp.concatenate([w_xk, w_xv], axis=1).astype(BF16)
    mem_kv = _matmul(mem.reshape(bsz * mem_len, d).astype(BF16), w_mem, tm=256, tn=512, out_dtype=BF16)
    x2 = _xattn(x1_bf.reshape(bsz, seq, d), x1.reshape(bsz, seq, d), w_xq.astype(BF16),
                mem_kv.reshape(bsz, mem_len, 2 * MEM_WIDTH), w_xo.astype(BF16),
                ln2_g.reshape(1, -1), ln2_b.reshape(1, -1)).reshape(n_tok, d)

    w_r = jnp.concatenate([w_router_grp, w_router_exp], axis=1)
    w_r = jnp.pad(w_r, ((0, 0), (0, LANES - w_r.shape[1])))
    w_r_hi = w_r.astype(BF16)
    w_r_lo = (w_r - w_r_hi.astype(F32)).astype(BF16)
    b_r = jnp.concatenate([b_router_grp, b_router_exp])
    b_r = jnp.pad(b_r, (0, LANES - b_r.shape[0])).reshape(1, LANES)
    gates, experts, ranks, seen = _router(x2, w_r_hi, w_r_lo, b_r)

    nk = n_tok * EXPERT_TOPK
    counts = seen[0, N_GROUPS:N_GROUPS + N_EXPERTS].astype(jnp.int32)
    pcounts = ((counts + MOE_BLOCK - 1) // MOE_BLOCK) * MOE_BLOCK
    pend = jnp.cumsum(pcounts)
    pstart = pend - pcounts
    e_tok = experts[:, :EXPERT_TOPK]
    seg_start = jnp.sum(jnp.where(e_tok[..., None] == jnp.arange(N_EXPERTS), pstart, 0), -1)
    dest_of = seg_start + ranks[:, :EXPERT_TOPK]
    n_blocks = -(-nk // MOE_BLOCK) + N_EXPERTS
    p_rows = n_blocks * MOE_BLOCK
    row_ids = jnp.arange(p_rows, dtype=jnp.int32)
    row_asg = jnp.full((p_rows,), -1, jnp.int32).at[dest_of.reshape(nk)].set(
        jnp.arange(nk, dtype=jnp.int32), unique_indices=True)
    is_real = row_asg >= 0
    row_tok = jnp.where(is_real, row_asg // EXPERT_TOPK, row_ids % n_tok)
    spare = EXPERT_TOPK * n_tok + ((row_ids // MOE_BLOCK) % 2) * MOE_BLOCK + row_ids % MOE_BLOCK
    row_dst = jnp.where(is_real, (row_asg % EXPERT_TOPK) * n_tok + row_asg // EXPERT_TOPK, spare)
    blk_start = jnp.arange(n_blocks) * MOE_BLOCK
    blk_on = (blk_start < pend[-1]).astype(jnp.int32)
    blk_probe = jnp.minimum(blk_start, pend[-1] - 1)
    blk_exp = jnp.minimum(jnp.sum((pend[None, :] <= blk_probe[:, None]).astype(jnp.int32), -1),
                          N_EXPERTS - 1).astype(jnp.int32)
    real_end = jnp.sum(jnp.where(blk_exp[:, None] == jnp.arange(N_EXPERTS), (pstart + counts)[None, :], 0), -1)
    blk_real = jnp.clip(real_end - blk_start, 0, MOE_BLOCK) * blk_on
    blk_groups = ((blk_real + GATHER_UNROLL - 1) // GATHER_UNROLL).astype(jnp.int32)

    y_tok = _expert_ffn(blk_exp, blk_groups, row_tok, row_dst, x2, w_exp_gate, w_exp_up, w_exp_down)
    out = _combine_ln(x2, y_tok, gates, ln3_g.reshape(1, -1), ln3_b.reshape(1, -1))
    return out.reshape(bsz, seq, d)


def kernel(x, mem, w_in, cmp_pe_k, cmp_w1_k, cmp_w2_k, cmp_pe_v, cmp_w1_v, cmp_w2_v, sgu_ln_g, sgu_ln_b, sgu_w_s, sgu_b_s, w_branch_a, w_branch_b, w_o, ln1_g, ln1_b, w_xq, w_xk, w_xv, w_xo, ln2_g, ln2_b, w_router_grp, b_router_grp, w_router_exp, b_router_exp, w_exp_gate, w_exp_up, w_exp_down, ln3_g, ln3_b):
    params = (w_in, cmp_pe_k, cmp_w1_k, cmp_w2_k, cmp_pe_v, cmp_w1_v, cmp_w2_v,
              sgu_ln_g, sgu_ln_b, sgu_w_s, sgu_b_s, w_branch_a, w_branch_b, w_o, ln1_g, ln1_b,
              w_xq, w_xk, w_xv, w_xo, ln2_g, ln2_b,
              w_router_grp, b_router_grp, w_router_exp, b_router_exp,
              w_exp_gate, w_exp_up, w_exp_down, ln3_g, ln3_b)
    h = x
    for l in range(DEPTH):
        h = _layer(h, mem, *[p[l] for p in params])
    return h
```

```python
import functools
import math

import jax
import jax.numpy as jnp
from jax import lax
from jax.experimental import pallas as pl
from jax.experimental.pallas import tpu as pltpu

HEAD_DIM = 128
NSA_HEADS = 16
NSA_KV_GROUPS = 2
NSA_HPG = NSA_HEADS // NSA_KV_GROUPS
CMP_BLOCK = 32
CMP_STRIDE = 16
CMP_HIDDEN = 256
SEL_BLOCK = 64
SEL_TOPK = 16
WINDOW = 512
SGU_WIDTH = 2048
SGU_GROUPS = 8
SGU_CHUNK = 128
MEM_HEADS = 4
N_GROUPS = 4
EXPERTS_PER_GROUP = 16
N_EXPERTS = N_GROUPS * EXPERTS_PER_GROUP
EXPERT_TOPK = 2
MOE_BLOCK = 128
ROPE_THETA = 10000.0
LN_EPS = 1e-5
SEL_BIG = 1e9
DEPTH = 1
DN_ALPHA = (2.0 * DEPTH) ** 0.25

Q_WIDTH = NSA_HEADS * HEAD_DIM
KV_WIDTH = 3 * 2 * NSA_KV_GROUPS * HEAD_DIM
NSA_GATE_WIDTH = 3 * NSA_HEADS
MEM_WIDTH = MEM_HEADS * HEAD_DIM

LANES = 128
VMEM_LIMIT = 56 * 1024 * 1024
MASKED = -1e30
M_INIT = -1e20
BF16 = jnp.bfloat16
F32 = jnp.float32


def _params(*sem):
    return pltpu.CompilerParams(dimension_semantics=sem, vmem_limit_bytes=VMEM_LIMIT)


def _dot(a, b):
    return jnp.dot(a, b, preferred_element_type=F32)


def _dot_nt(a, b):
    return lax.dot_general(a, b, (((1,), (1,)), ((), ())), preferred_element_type=F32)


def _gelu(x):
    return 0.5 * x * (1.0 + jnp.tanh(math.sqrt(2.0 / math.pi) * (x + 0.044715 * (x * x * x))))


def _sigmoid(x):
    return 1.0 / (1.0 + jnp.exp(-x))


def _layer_norm(x, g, b):
    mu = jnp.mean(x, -1, keepdims=True)
    xc = x - mu
    var = jnp.mean(xc * xc, -1, keepdims=True)
    return xc * lax.rsqrt(var + LN_EPS) * g + b


def _rope(t, cos_full, sin_signed):
    return t * cos_full + pltpu.roll(t, HEAD_DIM // 2, axis=1) * sin_signed


def _rope_tables(pos):
    inv = ROPE_THETA ** (-jnp.arange(0, HEAD_DIM, 2, dtype=F32) / HEAD_DIM)
    ang = pos.astype(F32)[:, None] * inv[None, :]
    c, s = jnp.cos(ang), jnp.sin(ang)
    return jnp.concatenate([c, c], -1), jnp.concatenate([-s, s], -1)


MXU_COLS = 256


def _mm_kernel(a_ref, b_ref, o_ref):
    o_ref[...] = _dot(a_ref[...], b_ref[...]).astype(o_ref.dtype)


def _matmul(a, b, *, tm, tn, out_dtype):
    m, k = a.shape
    n = b.shape[1]
    return pl.pallas_call(
        _mm_kernel,
        grid=(m // tm, n // tn),
        in_specs=[pl.BlockSpec((tm, k), lambda i, j: (i, 0)),
                  pl.BlockSpec((k, tn), lambda i, j: (0, j))],
        out_specs=pl.BlockSpec((tm, tn), lambda i, j: (i, j)),
        out_shape=jax.ShapeDtypeStruct((m, n), out_dtype),
        compiler_params=_params("parallel", "parallel"),
        name="matmul",
    )(a, b)


def _qproj_kernel(a_ref, b_ref, cos_ref, sin_ref, o_ref, abf_ref):
    a = a_ref[...].astype(BF16)

    @pl.when(pl.program_id(1) == 0)
    def _():
        abf_ref[...] = a

    cos, sin = cos_ref[...], sin_ref[...]
    scale = HEAD_DIM ** -0.5 * math.log2(math.e)
    for j in range(o_ref.shape[1] // MXU_COLS):
        y = _dot(a, b_ref[:, j * MXU_COLS:(j + 1) * MXU_COLS])
        for h in range(MXU_COLS // HEAD_DIM):
            sl = slice(j * MXU_COLS + h * HEAD_DIM, j * MXU_COLS + (h + 1) * HEAD_DIM)
            o_ref[:, sl] = (_rope(y[:, h * HEAD_DIM:(h + 1) * HEAD_DIM], cos, sin) * scale).astype(o_ref.dtype)


def _q_proj(x, w_q, cos, sin, seq, *, tm=1024, tn=1024):
    m, k = x.shape
    n = w_q.shape[1]
    per_seq = seq // tm
    return pl.pallas_call(
        _qproj_kernel,
        grid=(m // tm, n // tn),
        in_specs=[pl.BlockSpec((tm, k), lambda i, j: (i, 0)),
                  pl.BlockSpec((k, tn), lambda i, j: (0, j)),
                  pl.BlockSpec((tm, HEAD_DIM), lambda i, j: (i % per_seq, 0)),
                  pl.BlockSpec((tm, HEAD_DIM), lambda i, j: (i % per_seq, 0))],
        out_specs=[pl.BlockSpec((tm, tn), lambda i, j: (i, j)), pl.BlockSpec((tm, k), lambda i, j: (i, 0))],
        out_shape=[jax.ShapeDtypeStruct((m, n), BF16), jax.ShapeDtypeStruct((m, k), BF16)],
        compiler_params=_params("parallel", "arbitrary"),
        name="q_proj",
    )(x, w_q, cos, sin)


_N_CMP_COLS = 2 * NSA_KV_GROUPS * HEAD_DIM
_N_REST_CHUNKS = (KV_WIDTH - _N_CMP_COLS) // HEAD_DIM
_N_REST_COLS = (_N_REST_CHUNKS + NSA_KV_GROUPS) * HEAD_DIM
_GATE_COLS = NSA_KV_GROUPS * LANES
_SEL_SHIFT = SEL_BLOCK.bit_length() - 1
assert 1 << _SEL_SHIFT == SEL_BLOCK


def _kvproj_kernel(a_ref, b_ref, cos_ref, sin_ref, cmp_ref, rest_ref, gate_ref, *, per_seq):
    g = NSA_KV_GROUPS
    y = _dot(a_ref[...], b_ref[...])
    cos, sin = cos_ref[...], sin_ref[...]
    cmp_ref[...] = y[:, :_N_CMP_COLS]
    tm = y.shape[0]
    pos = (pl.program_id(0) % per_seq) * tm + lax.broadcasted_iota(jnp.int32, (tm, LANES), 0)
    lane = lax.broadcasted_iota(jnp.int32, (tm, LANES), 1)
    blk_onehot = jnp.where(lane == (pos >> _SEL_SHIFT), 1.0, 0.0).astype(rest_ref.dtype)
    for c in range(_N_REST_CHUNKS):
        t = y[:, _N_CMP_COLS + c * HEAD_DIM:_N_CMP_COLS + (c + 1) * HEAD_DIM]
        if (c % (2 * g)) < g:
            t = _rope(t, cos, sin)
        dst = 2 * c if c < g else g + c
        rest_ref[:, dst * HEAD_DIM:(dst + 1) * HEAD_DIM] = t.astype(rest_ref.dtype)
        if c < g:
            rest_ref[:, (dst + 1) * HEAD_DIM:(dst + 2) * HEAD_DIM] = blk_onehot
    gate_ref[...] = _sigmoid(y[:, KV_WIDTH:])


def _kv_proj(x_bf, w_kvg, cos, sin, seq, *, tm=1024):
    m, k = x_bf.shape
    n = w_kvg.shape[1]
    per_seq = seq // tm
    return pl.pallas_call(
        functools.partial(_kvproj_kernel, per_seq=per_seq),
        grid=(m // tm,),
        in_specs=[pl.BlockSpec((tm, k), lambda i: (i, 0)),
                  pl.BlockSpec((k, n), lambda i: (0, 0)),
                  pl.BlockSpec((tm, HEAD_DIM), lambda i: (i % per_seq, 0)),
                  pl.BlockSpec((tm, HEAD_DIM), lambda i: (i % per_seq, 0))],
        out_specs=[pl.BlockSpec((tm, _N_CMP_COLS), lambda i: (i, 0)),
                   pl.BlockSpec((tm, _N_REST_COLS), lambda i: (i, 0)),
                   pl.BlockSpec((tm, _GATE_COLS), lambda i: (i, 0))],
        out_shape=[jax.ShapeDtypeStruct((m, _N_CMP_COLS), F32),
                   jax.ShapeDtypeStruct((m, _N_REST_COLS), BF16),
                   jax.ShapeDtypeStruct((m, _GATE_COLS), F32)],
        compiler_params=_params("parallel"),
        name="kv_gate_proj",
    )(x_bf, w_kvg, cos, sin)


def _compress_kernel(kc_ref, pe_ref, w1_ref, w2_ref, cos_ref, sin_ref, o_ref):
    n_chunks = o_ref.shape[2]
    top = jnp.zeros((n_chunks, CMP_HIDDEN), F32)
    bot = jnp.zeros((n_chunks, CMP_HIDDEN), F32)
    for p in range(CMP_STRIDE):
        xp = kc_ref[0, pl.ds(p, n_chunks, stride=CMP_STRIDE), :]
        q = CMP_STRIDE + p
        top += _dot((xp + pe_ref[0, p:p + 1, :]).astype(BF16), w1_ref[0, p * HEAD_DIM:(p + 1) * HEAD_DIM, :])
        bot += _dot((xp + pe_ref[0, q:q + 1, :]).astype(BF16), w1_ref[0, q * HEAD_DIM:(q + 1) * HEAD_DIM, :])
    h = top + pltpu.roll(bot, n_chunks - 1, axis=0)
    y = _dot(_gelu(h).astype(BF16), w2_ref[0])
    o_ref[0, 0] = _rope(y, cos_ref[0], sin_ref[0]).astype(o_ref.dtype)


def _compress(kv_cmp, pe, w1, w2, cos, sin):
    bsz, seq, width = kv_cmp.shape
    four = width // HEAD_DIM
    n_chunks = seq // CMP_STRIDE
    g = NSA_KV_GROUPS
    return pl.pallas_call(
        _compress_kernel,
        grid=(four, bsz),
        in_specs=[pl.BlockSpec((1, seq, HEAD_DIM), lambda a, b: (b, 0, a)),
                  pl.BlockSpec((1, CMP_BLOCK, HEAD_DIM), lambda a, b: (a // g, 0, 0)),
                  pl.BlockSpec((1, CMP_BLOCK * HEAD_DIM, CMP_HIDDEN), lambda a, b: (a // g, 0, 0)),
                  pl.BlockSpec((1, CMP_HIDDEN, HEAD_DIM), lambda a, b: (a // g, 0, 0)),
                  pl.BlockSpec((1, n_chunks, HEAD_DIM), lambda a, b: (a // g, 0, 0)),
                  pl.BlockSpec((1, n_chunks, HEAD_DIM), lambda a, b: (a // g, 0, 0))],
        out_specs=pl.BlockSpec((1, 1, n_chunks, HEAD_DIM), lambda a, b: (a, b, 0, 0)),
        out_shape=jax.ShapeDtypeStruct((four, bsz, n_chunks, HEAD_DIM), BF16),
        compiler_params=_params("parallel", "parallel"),
        name="compress",
    )(kv_cmp, pe, w1, w2, cos, sin)


SEL_TILE = 1024
SEL_WIDE = 1
NSA_TQ = 256
SUBLANES = 8


def _softmax_parts(s3, bias):
    s3 = s3 + bias[None]
    m = jnp.max(s3, -1, keepdims=True)
    m = jnp.where(m > 0.5 * MASKED, m, 0.0)
    p = jnp.exp2(s3 - m)
    return p, 1.0 / jnp.maximum(jnp.sum(p, -1, keepdims=True), 1e-30)


def _topk_unselected(score_t, n_sel, n_pick):
    tq = score_t.shape[1]
    n_grp = n_sel // SUBLANES
    grp = [score_t[g * SUBLANES:(g + 1) * SUBLANES] for g in range(n_grp)]
    row_in_grp = lax.broadcasted_iota(jnp.int32, (SUBLANES, tq), 0)
    later = [jnp.where(row_in_grp > li, 1.0, 0.0) for li in range(SUBLANES)]
    beaten = [jnp.zeros((SUBLANES, tq), F32) for _ in range(n_grp)]
    for i in range(n_sel):
        gi, li = divmod(i, SUBLANES)
        row = grp[gi][li:li + 1, :]
        for g in range(n_grp):
            if g > gi:
                beaten[g] = beaten[g] + jnp.where(row >= grp[g], 1.0, 0.0)
            elif g < gi:
                beaten[g] = beaten[g] + jnp.where(row > grp[g], 1.0, 0.0)
            else:
                beaten[g] = (beaten[g] + jnp.where(row > grp[g], 1.0, 0.0)
                             + jnp.where(row == grp[g], later[li], 0.0))
    return jnp.concatenate([jnp.where(b < n_pick, 0.0, MASKED) for b in beaten], axis=0)


def _nsa_kernel(q_ref, kc_ref, vc_ref, ks_ref, vs_ref, kw_ref, vw_ref, gate_ref, selmap_t_ref, o_ref,
                *, n_sel, n_pick):
    hpg = NSA_HPG
    tq = NSA_TQ
    c = pl.program_id(2)
    q_t = q_ref[0]
    q8 = jnp.concatenate([q_t[:, n * HEAD_DIM:(n + 1) * HEAD_DIM] for n in range(hpg)], axis=0)
    t_pos = c * tq + lax.broadcasted_iota(jnp.int32, (tq, 1), 0)
    t_lane = c * tq + lax.broadcasted_iota(jnp.int32, (1, tq), 1)

    kc = kc_ref[0, 0]
    n_cmp_pad = kc.shape[0]
    k_idx = lax.broadcasted_iota(jnp.int32, (1, n_cmp_pad), 1)
    vis_c = (CMP_STRIDE * k_idx + (CMP_BLOCK - 1) <= t_pos) & (k_idx < n_cmp_pad - 1)
    p_c, r_c = _softmax_parts(_dot_nt(q8, kc).reshape(hpg, tq, n_cmp_pad), jnp.where(vis_c, 0.0, MASKED))
    o_c = _dot(p_c.reshape(hpg * tq, n_cmp_pad).astype(BF16), vc_ref[0, 0]).reshape(hpg, tq, HEAD_DIM) * r_c

    p_sum = jnp.sum(p_c * r_c, axis=0)
    p_hi = p_sum.astype(BF16)
    p_lo = (p_sum - p_hi.astype(F32)).astype(BF16)
    selmap_t = selmap_t_ref[...]
    imp_t = _dot_nt(selmap_t, p_hi) + _dot_nt(selmap_t, p_lo)
    blk = lax.broadcasted_iota(jnp.int32, (LANES, tq), 0)
    cur = t_lane >> _SEL_SHIFT
    forced = (blk == 0) | (blk == cur) | (blk == cur - 1)
    score_t = jnp.where(forced, SEL_BIG, jnp.where(blk <= cur, imp_t, -SEL_BIG))
    unsel_t = _topk_unselected(score_t, n_sel, n_pick)
    unsel_t = jnp.concatenate([unsel_t, jnp.zeros((LANES - n_sel, tq), F32)], axis=0)
    unsel = unsel_t.T.astype(BF16)
    q_aug = jnp.concatenate([q8, jnp.concatenate([unsel] * hpg, axis=0)], axis=1)

    def sel_tile(kt, size, carry, bias):
        m_old, l_old, acc = carry
        base = pl.multiple_of(kt * size, size)
        k = ks_ref[0, pl.ds(base, size), :]
        v = vs_ref[0, pl.ds(base, size), :]
        s = _dot_nt(q_aug, k).reshape(hpg, tq, size)
        if bias is not None:
            s = s + bias[None]
        m_new = jnp.maximum(m_old, jnp.max(s, -1, keepdims=True))
        alpha = jnp.exp2(m_old - m_new)
        p = jnp.exp2(s - m_new)
        l_new = alpha * l_old + jnp.sum(p, -1, keepdims=True)
        pv = _dot(p.reshape(hpg * tq, size).astype(BF16), v).reshape(hpg, tq, HEAD_DIM)
        return m_new, l_new, alpha * acc + pv

    n_full = c // (SEL_TILE // tq)
    n_wide = n_full // SEL_WIDE
    carry = (jnp.full((hpg, tq, 1), M_INIT, F32), jnp.zeros((hpg, tq, 1), F32),
             jnp.zeros((hpg, tq, HEAD_DIM), F32))
    carry = lax.fori_loop(0, n_wide, lambda kt, cr: sel_tile(kt, SEL_WIDE * SEL_TILE, cr, None), carry)
    carry = lax.fori_loop(n_wide * SEL_WIDE, n_full, lambda kt, cr: sel_tile(kt, SEL_TILE, cr, None), carry)
    kp_last = n_full * SEL_TILE + lax.broadcasted_iota(jnp.int32, (1, SEL_TILE), 1)
    _, l_s, acc_s = sel_tile(n_full, SEL_TILE, carry, jnp.where(kp_last <= t_pos, 0.0, MASKED))
    o_s = acc_s * (1.0 / jnp.maximum(l_s, 1e-30))

    span = WINDOW + tq
    start = pl.multiple_of(jnp.maximum(c * tq - WINDOW, 0), tq)
    kw = kw_ref[0, pl.ds(start, span), :]
    vw = vw_ref[0, pl.ds(start, span), :]
    kp = start + lax.broadcasted_iota(jnp.int32, (1, span), 1)
    vis_w = (kp <= t_pos) & (t_pos - kp < WINDOW)
    p_w, r_w = _softmax_parts(_dot_nt(q8, kw).reshape(hpg, tq, span), jnp.where(vis_w, 0.0, MASKED))
    o_w = _dot(p_w.reshape(hpg * tq, span).astype(BF16), vw).reshape(hpg, tq, HEAD_DIM) * r_w

    gates = gate_ref[0]
    for n in range(hpg):
        out = (gates[:, n:n + 1] * o_c[n]
               + gates[:, hpg + n:hpg + n + 1] * o_s[n]
               + gates[:, 2 * hpg + n:2 * hpg + n + 1] * o_w[n])
        o_ref[0, :, n * HEAD_DIM:(n + 1) * HEAD_DIM] = out.astype(o_ref.dtype)


def _nsa(q, cmp_kv, rest, gates, selmap_t, *, n_sel, n_pick):
    bsz, seq, _ = q.shape
    g = NSA_KV_GROUPS
    n_cmp_pad = cmp_kv.shape[2]
    gw = NSA_HPG * HEAD_DIM
    assert n_sel % SUBLANES == 0 and n_sel <= LANES and seq >= WINDOW + NSA_TQ
    assert SEL_TILE % NSA_TQ == 0 and seq % SEL_TILE == 0
    kv_spec = lambda off: pl.BlockSpec((1, seq, HEAD_DIM), lambda b, gi, c: (b, 0, off + gi))
    return pl.pallas_call(
        functools.partial(_nsa_kernel, n_sel=n_sel, n_pick=n_pick),
        grid=(bsz, g, seq // NSA_TQ),
        in_specs=[pl.BlockSpec((1, NSA_TQ, gw), lambda b, gi, c: (b, c, gi)),
                  pl.BlockSpec((1, 1, n_cmp_pad, HEAD_DIM), lambda b, gi, c: (gi, b, 0, 0)),
                  pl.BlockSpec((1, 1, n_cmp_pad, HEAD_DIM), lambda b, gi, c: (g + gi, b, 0, 0)),
                  pl.BlockSpec((1, seq, 2 * HEAD_DIM), lambda b, gi, c: (b, 0, gi)),
                  kv_spec(2 * g), kv_spec(3 * g), kv_spec(4 * g),
                  pl.BlockSpec((1, NSA_TQ, LANES), lambda b, gi, c: (b, c, gi)),
                  pl.BlockSpec(selmap_t.shape, lambda b, gi, c: (0, 0))],
        out_specs=pl.BlockSpec((1, NSA_TQ, gw), lambda b, gi, c: (b, c, gi)),
        out_shape=jax.ShapeDtypeStruct(q.shape, BF16),
        compiler_params=_params("parallel", "parallel", "arbitrary"),
        name="nsa_attention",
    )(q, cmp_kv, cmp_kv, rest, rest, rest, rest, gates, selmap_t)


def _sgu_kernel(x_ref, w_ref, g_ref, b_ref, ws_ref, bs_ref, o_ref, *, chunks):
    gd = SGU_WIDTH // SGU_GROUPS
    row = lax.broadcasted_iota(jnp.int32, (SGU_CHUNK, SGU_CHUNK), 0)
    col = lax.broadcasted_iota(jnp.int32, (SGU_CHUNK, SGU_CHUNK), 1)
    causal = col <= row
    bs = bs_ref[...]
    x = x_ref[...]
    v = jnp.concatenate(
        [_gelu(_dot(x, w_ref[:, SGU_WIDTH + j * MXU_COLS:SGU_WIDTH + (j + 1) * MXU_COLS]))
         for j in range(SGU_WIDTH // MXU_COLS)], axis=1)
    vn = _layer_norm(v, g_ref[...], b_ref[...]).astype(BF16)
    for gi in range(SGU_GROUPS):
        cols = slice(gi * gd, (gi + 1) * gd)
        u = _gelu(_dot(x, w_ref[:, cols]))
        w = jnp.where(causal, ws_ref[gi], 0.0).astype(BF16)
        for ch in range(chunks):
            rows = slice(ch * SGU_CHUNK, (ch + 1) * SGU_CHUNK)
            z = _dot(w, vn[rows, cols]) + bs[:, gi:gi + 1]
            o_ref[rows, cols] = (u[rows] * z).astype(o_ref.dtype)


def _sgu(x_bf, w_uv, ln_g, ln_b, w_s, b_s_t, *, chunks=4):
    m, d = x_bf.shape
    tm = chunks * SGU_CHUNK
    return pl.pallas_call(
        functools.partial(_sgu_kernel, chunks=chunks),
        grid=(m // tm,),
        in_specs=[pl.BlockSpec((tm, d), lambda i: (i, 0)),
                  pl.BlockSpec(w_uv.shape, lambda i: (0, 0), pipeline_mode=pl.Buffered(1)),
                  pl.BlockSpec((1, SGU_WIDTH), lambda i: (0, 0)),
                  pl.BlockSpec((1, SGU_WIDTH), lambda i: (0, 0)),
                  pl.BlockSpec(w_s.shape, lambda i: (0, 0, 0)),
                  pl.BlockSpec(b_s_t.shape, lambda i: (0, 0))],
        out_specs=pl.BlockSpec((tm, SGU_WIDTH), lambda i: (i, 0)),
        out_shape=jax.ShapeDtypeStruct((m, SGU_WIDTH), BF16),
        compiler_params=_params("parallel"),
        name="sgu",
    )(x_bf, w_uv, ln_g, ln_b, w_s, b_s_t)


def _merge_kernel(x_ref, a_ref, s_ref, wga_ref, wgb_ref, wa_ref, wb_ref, o_ref):
    x = x_ref[...]
    ga = _sigmoid(_dot(x, wga_ref[...]))
    gb = _sigmoid(_dot(x, wgb_ref[...]))
    o_ref[...] = (ga * _dot(a_ref[...], wa_ref[...]) + gb * _dot(s_ref[...], wb_ref[...])).astype(o_ref.dtype)


def _merge(x_bf, o_nsa, o_sgu, w_merge, w_a, w_b, *, tm=1024, tn=512):
    m, k = x_bf.shape
    n = w_a.shape[1]
    nb = n // tn
    row = pl.BlockSpec((tm, k), lambda i, j: (i, 0))
    col = pl.BlockSpec((k, tn), lambda i, j: (0, j))
    return pl.pallas_call(
        _merge_kernel,
        grid=(m // tm, nb),
        in_specs=[row, row, row, col, pl.BlockSpec((k, tn), lambda i, j: (0, nb + j)), col, col],
        out_specs=pl.BlockSpec((tm, tn), lambda i, j: (i, j)),
        out_shape=jax.ShapeDtypeStruct((m, n), BF16),
        compiler_params=_params("parallel", "parallel"),
        name="merge",
    )(x_bf, o_nsa, o_sgu, w_merge, w_merge, w_a, w_b)


def _proj_ln_kernel(a_ref, w_ref, x_ref, g_ref, b_ref, o_ref, obf_ref):
    y = DN_ALPHA * x_ref[...] + _dot(a_ref[...], w_ref[...])
    out = _layer_norm(y, g_ref[...], b_ref[...])
    o_ref[...] = out
    obf_ref[...] = out.astype(BF16)


def _proj_ln(a_bf, w, x, ln_g, ln_b, *, tm=512):
    m, k = a_bf.shape
    n = w.shape[1]
    vec = pl.BlockSpec((1, n), lambda i: (0, 0))
    return pl.pallas_call(
        _proj_ln_kernel,
        grid=(m // tm,),
        in_specs=[pl.BlockSpec((tm, k), lambda i: (i, 0)),
                  pl.BlockSpec((k, n), lambda i: (0, 0)),
                  pl.BlockSpec((tm, n), lambda i: (i, 0)), vec, vec],
        out_specs=[pl.BlockSpec((tm, n), lambda i: (i, 0)), pl.BlockSpec((tm, n), lambda i: (i, 0))],
        out_shape=[jax.ShapeDtypeStruct((m, n), F32), jax.ShapeDtypeStruct((m, n), BF16)],
        compiler_params=_params("parallel"),
        name="proj_ln",
    )(a_bf, w, x, ln_g, ln_b)


def _xattn_kernel(xbf_ref, x_ref, wq_ref, kv_ref, wo_ref, g_ref, b_ref, o_ref):
    q = (_dot(xbf_ref[0], wq_ref[...]) * HEAD_DIM ** -0.5).astype(BF16)
    kv = kv_ref[0]
    heads = []
    for h in range(MEM_HEADS):
        cols = slice(h * HEAD_DIM, (h + 1) * HEAD_DIM)
        s = _dot_nt(q[:, cols], kv[:, cols])
        p = jnp.exp(s - jnp.max(s, -1, keepdims=True))
        p = p / jnp.sum(p, -1, keepdims=True)
        heads.append(_dot(p.astype(BF16), kv[:, MEM_WIDTH + h * HEAD_DIM:MEM_WIDTH + (h + 1) * HEAD_DIM]))
    o = jnp.concatenate(heads, axis=1).astype(BF16)
    y = DN_ALPHA * x_ref[0] + _dot(o, wo_ref[...])
    o_ref[0] = _layer_norm(y, g_ref[...], b_ref[...])


def _xattn(x_bf, x, w_xq, mem_kv, w_xo, ln_g, ln_b, *, tm=512):
    bsz, seq, d = x.shape
    mem_len = mem_kv.shape[1]
    vec = pl.BlockSpec((1, d), lambda b, i: (0, 0))
    return pl.pallas_call(
        _xattn_kernel,
        grid=(bsz, seq // tm),
        in_specs=[pl.BlockSpec((1, tm, d), lambda b, i: (b, i, 0)),
                  pl.BlockSpec((1, tm, d), lambda b, i: (b, i, 0)),
                  pl.BlockSpec(w_xq.shape, lambda b, i: (0, 0)),
                  pl.BlockSpec((1, mem_len, 2 * MEM_WIDTH), lambda b, i: (b, 0, 0)),
                  pl.BlockSpec(w_xo.shape, lambda b, i: (0, 0)), vec, vec],
        out_specs=pl.BlockSpec((1, tm, d), lambda b, i: (b, i, 0)),
        out_shape=jax.ShapeDtypeStruct(x.shape, F32),
        compiler_params=_params("parallel", "parallel"),
        name="xattn_ln",
    )(x_bf, x, w_xq, mem_kv, w_xo, ln_g, ln_b)


def _router_kernel(x_ref, whi_ref, wlo_ref, b_ref, gate_ref, exp_ref, rank_ref, cnt_ref, seen_ref):
    @pl.when(pl.program_id(0) == 0)
    def _():
        seen_ref[...] = jnp.zeros_like(seen_ref)

    x = x_ref[...]
    x_hi = x.astype(BF16)
    x_lo = (x - x_hi.astype(F32)).astype(BF16)
    whi = whi_ref[...]
    logits = _dot(x_hi, whi) + _dot(x_hi, wlo_ref[...]) + _dot(x_lo, whi) + b_ref[...]
    rows = logits.shape[0]
    lane = lax.broadcasted_iota(jnp.int32, (rows, LANES), 1)
    lane_f = lane.astype(F32)
    is_grp = lane < N_GROUPS
    lg = jnp.where(is_grp, logits, MASKED)
    eg = jnp.where(is_grp, jnp.exp(lg - jnp.max(lg, -1, keepdims=True)), 0.0)
    p_grp = eg / jnp.sum(eg, -1, keepdims=True)
    g_grp = jnp.max(p_grp, -1, keepdims=True)
    grp = jnp.min(jnp.where((p_grp == g_grp) & is_grp, lane_f, 1e9), -1, keepdims=True).astype(jnp.int32)
    lo = N_GROUPS + grp * EXPERTS_PER_GROUP
    in_grp = (lane >= lo) & (lane < lo + EXPERTS_PER_GROUP)
    le = jnp.where(in_grp, logits, MASKED)
    ee = jnp.where(in_grp, jnp.exp(le - jnp.max(le, -1, keepdims=True)), 0.0)
    p_exp = jnp.where(in_grp, ee / jnp.sum(ee, -1, keepdims=True), -1.0)
    v1 = jnp.max(p_exp, -1, keepdims=True)
    i1 = jnp.min(jnp.where(p_exp == v1, lane_f, 1e9), -1, keepdims=True)
    p_rest = jnp.where(lane_f == i1, -1.0, p_exp)
    v2 = jnp.max(p_rest, -1, keepdims=True)
    i2 = jnp.min(jnp.where(p_rest == v2, lane_f, 1e9), -1, keepdims=True)
    denom = v1 + v2
    gate_ref[...] = jnp.where(lane == 0, g_grp * v1 / denom, jnp.where(lane == 1, g_grp * v2 / denom, 0.0))
    exp_ref[...] = jnp.where(lane == 0, i1, jnp.where(lane == 1, i2, float(N_GROUPS))).astype(jnp.int32) - N_GROUPS

    onehot = jnp.where((lane_f == i1) | (lane_f == i2), 1.0, 0.0)
    tok_r = lax.broadcasted_iota(jnp.int32, (rows, rows), 0)
    tok_c = lax.broadcasted_iota(jnp.int32, (rows, rows), 1)
    earlier = jnp.where(tok_c < tok_r, 1.0, 0.0).astype(BF16)
    before = _dot(earlier, onehot.astype(BF16)) + seen_ref[...]
    r1 = jnp.sum(jnp.where(lane_f == i1, before, 0.0), -1, keepdims=True)
    r2 = jnp.sum(jnp.where(lane_f == i2, before, 0.0), -1, keepdims=True)
    rank_ref[...] = jnp.where(lane == 0, r1, jnp.where(lane == 1, r2, 0.0)).astype(jnp.int32)
    seen_ref[...] += jnp.sum(onehot, axis=0, keepdims=True)
    cnt_ref[...] = seen_ref[...]


def _router(x, w_hi, w_lo, bias, *, tm=512):
    m, d = x.shape
    wspec = pl.BlockSpec((d, LANES), lambda i: (0, 0))
    vec = pl.BlockSpec((1, LANES), lambda i: (0, 0))
    tile = pl.BlockSpec((tm, LANES), lambda i: (i, 0))
    return pl.pallas_call(
        _router_kernel,
        grid=(m // tm,),
        in_specs=[pl.BlockSpec((tm, d), lambda i: (i, 0)), wspec, wspec, vec],
        out_specs=[tile, tile, tile, vec],
        out_shape=[jax.ShapeDtypeStruct((m, LANES), F32), jax.ShapeDtypeStruct((m, LANES), jnp.int32),
                   jax.ShapeDtypeStruct((m, LANES), jnp.int32), jax.ShapeDtypeStruct((1, LANES), F32)],
        scratch_shapes=[pltpu.VMEM((1, LANES), F32)],
        compiler_params=_params("arbitrary"),
        name="router",
    )(x, w_hi, w_lo, bias)


GATHER_UNROLL = 8


def _ffn_kernel(exp_ref, on_ref, first_ref, slot_ref, next_ref, tok_ref, dst_ref, x_hbm, wg_hbm, wu_hbm, wd_hbm,
                y_hbm, wg_buf, wu_buf, wd_buf, h_buf, y_buf, sem, row_sem, out_sem):
    i = pl.program_id(0)
    n_blk = pl.num_programs(0)
    slot = slot_ref[i]
    rows = MOE_BLOCK

    def out_copy(blk, g, u):
        dst = dst_ref[blk * rows + g * GATHER_UNROLL + u]
        return pltpu.make_async_copy(y_buf.at[blk % 2, g, pl.ds(u, 1), :], y_hbm.at[pl.ds(dst, 1), :],
                                     out_sem.at[blk % 2])

    def weight_copies(e, s):
        return (pltpu.make_async_copy(wg_hbm.at[e], wg_buf.at[s], sem.at[s, 0]),
                pltpu.make_async_copy(wu_hbm.at[e], wu_buf.at[s], sem.at[s, 1]),
                pltpu.make_async_copy(wd_hbm.at[e], wd_buf.at[s], sem.at[s, 2]))

    def row_copy(blk, g, u):
        tok = tok_ref[blk * rows + g * GATHER_UNROLL + u]
        return pltpu.make_async_copy(x_hbm.at[pl.ds(tok, 1), :], h_buf.at[blk % 2, g, pl.ds(u, 1), :],
                                     row_sem.at[blk % 2])

    def for_rows(blk, fn):
        def body(g, carry):
            for u in range(GATHER_UNROLL):
                fn(blk, g, u)
            return carry
        lax.fori_loop(0, on_ref[blk], body, 0)

    @pl.when(i == 0)
    def _():
        for cp in weight_copies(exp_ref[0], 0):
            cp.start()
        h_buf[...] = jnp.zeros_like(h_buf)
        for_rows(0, lambda b, g, u: row_copy(b, g, u).start())
        y_buf[...] = jnp.zeros_like(y_buf)
        spare = y_hbm.shape[0] - 2 * rows
        fills = [pltpu.make_async_copy(y_buf.at[s, g],
                                       y_hbm.at[pl.ds(spare + s * rows + g * GATHER_UNROLL, GATHER_UNROLL), :],
                                       out_sem.at[s])
                 for s in range(2) for g in range(rows // GATHER_UNROLL)]
        for cp in fills:
            cp.start()
        for cp in fills:
            cp.wait()

    prev2 = jnp.maximum(i - 2, 0)

    @pl.when((i >= 2) & (on_ref[prev2] != 0))
    def _():
        for_rows(i - 2, lambda b, g, u: out_copy(b, g, u).wait())

    nxt_blk = jnp.minimum(i + 1, n_blk - 1)

    @pl.when((i + 1 < n_blk) & (on_ref[nxt_blk] != 0))
    def _():
        for_rows(i + 1, lambda b, g, u: row_copy(b, g, u).start())

    @pl.when(first_ref[i] != 0)
    def _():
        nxt = next_ref[i]

        @pl.when(nxt >= 0)
        def _():
            for cp in weight_copies(nxt, 1 - slot):
                cp.start()

        for cp in weight_copies(exp_ref[i], slot):
            cp.wait()

    @pl.when(on_ref[i] != 0)
    def _():
        for_rows(i, lambda b, g, u: row_copy(b, g, u).wait())
        h = h_buf[i % 2].reshape(rows, h_buf.shape[-1]).astype(BF16)
        a = _dot(h, wg_buf[slot].astype(BF16))
        a = a * _sigmoid(a) * _dot(h, wu_buf[slot].astype(BF16))
        y_buf[i % 2] = _dot(a.astype(BF16), wd_buf[slot].astype(BF16)).reshape(y_buf.shape[1:])
        for_rows(i, lambda b, g, u: out_copy(b, g, u).start())

    @pl.when(i == n_blk - 1)
    def _():
        prev1 = jnp.maximum(i - 1, 0)

        @pl.when((i >= 1) & (on_ref[prev1] != 0))
        def _():
            for_rows(i - 1, lambda b, g, u: out_copy(b, g, u).wait())

        @pl.when(on_ref[i] != 0)
        def _():
            for_rows(i, lambda b, g, u: out_copy(b, g, u).wait())


def _expert_ffn(blk_exp, blk_on, row_tok, row_dst, x, w_gate, w_up, w_down):
    p_rows = row_tok.shape[0]
    n_tok, d = x.shape
    n_exp, _, ff = w_gate.shape
    n_blocks = p_rows // MOE_BLOCK
    prev_exp = jnp.concatenate([jnp.full((1,), -1, jnp.int32), blk_exp[:-1]])
    first = ((blk_on != 0) & (blk_exp != prev_exp)).astype(jnp.int32)
    seg_id = jnp.cumsum(first) - 1
    seg_ids = jnp.arange(n_exp + 1)
    hit = (first[:, None] != 0) & (seg_id[:, None] == seg_ids[None, :])
    seg_exp = jnp.sum(jnp.where(hit, blk_exp[:, None] + 1, 0), 0) - 1
    blk_next = jnp.sum(jnp.where(seg_ids[None, :] == seg_id[:, None] + 1, seg_exp[None, :] + 1, 0), 1) - 1
    blk_slot = jnp.maximum(seg_id, 0) % 2
    grid_spec = pltpu.PrefetchScalarGridSpec(
        num_scalar_prefetch=7,
        grid=(n_blocks,),
        in_specs=[pl.BlockSpec(memory_space=pl.ANY)] * 4,
        out_specs=pl.BlockSpec(memory_space=pl.ANY),
        scratch_shapes=[pltpu.VMEM((2, d, ff), F32), pltpu.VMEM((2, d, ff), F32), pltpu.VMEM((2, ff, d), F32),
                        pltpu.VMEM((2, MOE_BLOCK // GATHER_UNROLL, GATHER_UNROLL, d), F32),
                        pltpu.VMEM((2, MOE_BLOCK // GATHER_UNROLL, GATHER_UNROLL, d), F32),
                        pltpu.SemaphoreType.DMA((2, 3)), pltpu.SemaphoreType.DMA((2,)),
                        pltpu.SemaphoreType.DMA((2,))],
    )
    return pl.pallas_call(
        _ffn_kernel,
        grid_spec=grid_spec,
        out_shape=jax.ShapeDtypeStruct((EXPERT_TOPK * n_tok + 2 * MOE_BLOCK, d), F32),
        compiler_params=_params("arbitrary"),
        name="expert_ffn",
    )(blk_exp, blk_on, first, blk_slot.astype(jnp.int32), blk_next.astype(jnp.int32), row_tok, row_dst,
      x, w_gate, w_up, w_down)


def _combine_ln_kernel(x_ref, y0_ref, y1_ref, gate_ref, g_ref, b_ref, o_ref):
    gates = gate_ref[...]
    moe = y0_ref[...] * gates[:, 0:1] + y1_ref[...] * gates[:, 1:2]
    o_ref[...] = _layer_norm(DN_ALPHA * x_ref[...] + moe, g_ref[...], b_ref[...])


def _combine_ln(x, y_tok, gates, ln_g, ln_b, *, tm=256):
    m, d = x.shape
    row = pl.BlockSpec((tm, d), lambda i: (i, 0))
    vec = pl.BlockSpec((1, d), lambda i: (0, 0))
    return pl.pallas_call(
        _combine_ln_kernel,
        grid=(m // tm,),
        in_specs=[row, row, pl.BlockSpec((tm, d), lambda i: (m // tm + i, 0)),
                  pl.BlockSpec((tm, LANES), lambda i: (i, 0)), vec, vec],
        out_specs=row,
        out_shape=jax.ShapeDtypeStruct((m, d), F32),
        compiler_params=_params("parallel"),
        name="combine_ln",
    )(x, y_tok, y_tok, gates, ln_g, ln_b)


def _layer(x, mem, w_in, cmp_pe_k, cmp_w1_k, cmp_w2_k, cmp_pe_v, cmp_w1_v, cmp_w2_v,
           sgu_ln_g, sgu_ln_b, sgu_w_s, sgu_b_s, w_branch_a, w_branch_b, w_o, ln1_g, ln1_b,
           w_xq, w_xk, w_xv, w_xo, ln2_g, ln2_b,
           w_router_grp, b_router_grp, w_router_exp, b_router_exp,
           w_exp_gate, w_exp_up, w_exp_down, ln3_g, ln3_b):
    bsz, seq, d = x.shape
    n_tok = bsz * seq
    g = NSA_KV_GROUPS
    hpg = NSA_HPG
    xf = x.reshape(n_tok, d)

    c0 = Q_WIDTH
    c1 = c0 + KV_WIDTH
    c2 = c1 + NSA_GATE_WIDTH
    c3 = c2 + 2 * SGU_WIDTH
    w_q = w_in[:, :c0].astype(BF16)
    w_gate = w_in[:, c1:c2].reshape(d, g, hpg, 3).transpose(0, 1, 3, 2).reshape(d, g, 3 * hpg)
    w_gate = jnp.pad(w_gate, ((0, 0), (0, 0), (0, LANES - 3 * hpg))).reshape(d, g * LANES)
    w_kvg = jnp.concatenate([w_in[:, c0:c1], w_gate], axis=1).astype(BF16)
    w_uv = w_in[:, c2:c3].astype(BF16)
    w_merge = w_in[:, c3:].astype(BF16)

    cos, sin = _rope_tables(jnp.arange(seq, dtype=jnp.int32))
    q, x_bf = _q_proj(xf, w_q, cos, sin, seq)
    q = q.reshape(bsz, seq, Q_WIDTH)
    kv_cmp, kv_rest, nsa_gates = _kv_proj(x_bf, w_kvg, cos, sin, seq)

    n_chunks = seq // CMP_STRIDE
    pe = jnp.stack([cmp_pe_k, cmp_pe_v])
    w1 = jnp.stack([cmp_w1_k, cmp_w1_v]).astype(BF16)
    w2 = jnp.stack([cmp_w2_k, cmp_w2_v]).astype(BF16)
    cmp_end = CMP_STRIDE * jnp.arange(n_chunks, dtype=jnp.int32) + CMP_BLOCK - 1
    ccos, csin = _rope_tables(cmp_end)
    cmp_cos = jnp.stack([ccos, jnp.ones_like(ccos)])
    cmp_sin = jnp.stack([csin, jnp.zeros_like(csin)])
    cmp_kv = _compress(kv_cmp.reshape(bsz, seq, _N_CMP_COLS), pe, w1, w2, cmp_cos, cmp_sin)

    n_sel = seq // SEL_BLOCK
    n_pick = min(SEL_TOPK, n_sel)
    ci = CMP_STRIDE * jnp.arange(n_chunks)
    sj = SEL_BLOCK * jnp.arange(LANES)
    overlap = jnp.clip(jnp.minimum(ci[:, None] + CMP_BLOCK, sj[None, :] + SEL_BLOCK)
                       - jnp.maximum(ci[:, None], sj[None, :]), 0, None)
    selmap_t = (overlap.astype(F32) / CMP_BLOCK).astype(BF16).T
    o_nsa = _nsa(q, cmp_kv, kv_rest.reshape(bsz, seq, _N_REST_COLS),
                 nsa_gates.reshape(bsz, seq, _GATE_COLS), selmap_t, n_sel=n_sel, n_pick=n_pick)

    o_sgu = _sgu(x_bf, w_uv, sgu_ln_g.reshape(1, -1), sgu_ln_b.reshape(1, -1), sgu_w_s, sgu_b_s.T)

    merged = _merge(x_bf, o_nsa.reshape(n_tok, Q_WIDTH), o_sgu, w_merge,
                    w_branch_a.astype(BF16), w_branch_b.astype(BF16))
    x1, x1_bf = _proj_ln(merged, w_o.astype(BF16), xf, ln1_g.reshape(1, -1), ln1_b.reshape(1, -1))

    mem_len = mem.shape[1]
    w_mem = jnp.concatenate([w_xk, w_xv], axis=1).astype(BF16)
    mem_kv = _matmul(mem.reshape(bsz * mem_len, d).astype(BF16), w_mem, tm=256, tn=512, out_dtype=BF16)
    x2 = _xattn(x1_bf.reshape(bsz, seq, d), x1.reshape(bsz, seq, d), w_xq.astype(BF16),
                mem_kv.reshape(bsz, mem_len, 2 * MEM_WIDTH), w_xo.astype(BF16),
                ln2_g.reshape(1, -1), ln2_b.reshape(1, -1)).reshape(n_tok, d)

    w_r = jnp.concatenate([w_router_grp, w_router_exp], axis=1)
    w_r = jnp.pad(w_r, ((0, 0), (0, LANES - w_r.shape[1])))
    w_r_hi = w_r.astype(BF16)
    w_r_lo = (w_r - w_r_hi.astype(F32)).astype(BF16)
    b_r = jnp.concatenate([b_router_grp, b_router_exp])
    b_r = jnp.pad(b_r, (0, LANES - b_r.shape[0])).reshape(1, LANES)
    gates, experts, ranks, seen = _router(x2, w_r_hi, w_r_lo, b_r)

    nk = n_tok * EXPERT_TOPK
    counts = seen[0, N_GROUPS:N_GROUPS + N_EXPERTS].astype(jnp.int32)
    pcounts = ((counts + MOE_BLOCK - 1) // MOE_BLOCK) * MOE_BLOCK
    pend = jnp.cumsum(pcounts)
    pstart = pend - pcounts
    e_tok = experts[:, :EXPERT_TOPK]
    seg_start = jnp.sum(jnp.where(e_tok[..., None] == jnp.arange(N_EXPERTS), pstart, 0), -1)
    dest_of = seg_start + ranks[:, :EXPERT_TOPK]
    n_blocks = -(-nk // MOE_BLOCK) + N_EXPERTS
    p_rows = n_blocks * MOE_BLOCK
    row_ids = jnp.arange(p_rows, dtype=jnp.int32)
    row_asg = jnp.full((p_rows,), -1, jnp.int32).at[dest_of.reshape(nk)].set(
        jnp.arange(nk, dtype=jnp.int32), unique_indices=True)
    is_real = row_asg >= 0
    row_tok = jnp.where(is_real, row_asg // EXPERT_TOPK, row_ids % n_tok)
    spare = EXPERT_TOPK * n_tok + ((row_ids // MOE_BLOCK) % 2) * MOE_BLOCK + row_ids % MOE_BLOCK
    row_dst = jnp.where(is_real, (row_asg % EXPERT_TOPK) * n_tok + row_asg // EXPERT_TOPK, spare)
    blk_start = jnp.arange(n_blocks) * MOE_BLOCK
    blk_on = (blk_start < pend[-1]).astype(jnp.int32)
    blk_probe = jnp.minimum(blk_start, pend[-1] - 1)
    blk_exp = jnp.minimum(jnp.sum((pend[None, :] <= blk_probe[:, None]).astype(jnp.int32), -1),
                          N_EXPERTS - 1).astype(jnp.int32)
    real_end = jnp.sum(jnp.where(blk_exp[:, None] == jnp.arange(N_EXPERTS), (pstart + counts)[None, :], 0), -1)
    blk_real = jnp.clip(real_end - blk_start, 0, MOE_BLOCK) * blk_on
    blk_groups = ((blk_real + GATHER_UNROLL - 1) // GATHER_UNROLL).astype(jnp.int32)

    y_tok = _expert_ffn(blk_exp, blk_groups, row_tok, row_dst, x2, w_exp_gate, w_exp_up, w_exp_down)
    out = _combine_ln(x2, y_tok, gates, ln3_g.reshape(1, -1), ln3_b.reshape(1, -1))
    return out.reshape(bsz, seq, d)


def kernel(x, mem, w_in, cmp_pe_k, cmp_w1_k, cmp_w2_k, cmp_pe_v, cmp_w1_v, cmp_w2_v, sgu_ln_g, sgu_ln_b, sgu_w_s, sgu_b_s, w_branch_a, w_branch_b, w_o, ln1_g, ln1_b, w_xq, w_xk, w_xv, w_xo, ln2_g, ln2_b, w_router_grp, b_router_grp, w_router_exp, b_router_exp, w_exp_gate, w_exp_up, w_exp_down, ln3_g, ln3_b):
    params = (w_in, cmp_pe_k, cmp_w1_k, cmp_w2_k, cmp_pe_v, cmp_w1_v, cmp_w2_v,
              sgu_ln_g, sgu_ln_b, sgu_w_s, sgu_b_s, w_branch_a, w_branch_b, w_o, ln1_g, ln1_b,
              w_xq, w_xk, w_xv, w_xo, ln2_g, ln2_b,
              w_router_grp, b_router_grp, w_router_exp, b_router_exp,
              w_exp_gate, w_exp_up, w_exp_down, ln3_g, ln3_b)
    h = x
    for l in range(DEPTH):
        h = _layer(h, mem, *[p[l] for p in params])
    return h
```

```python
import functools
import math

import jax
import jax.numpy as jnp
from jax import lax
from jax.experimental import pallas as pl
from jax.experimental.pallas import tpu as pltpu

HEAD_DIM = 128
NSA_HEADS = 16
NSA_KV_GROUPS = 2
NSA_HPG = NSA_HEADS // NSA_KV_GROUPS
CMP_BLOCK = 32
CMP_STRIDE = 16
CMP_HIDDEN = 256
SEL_BLOCK = 64
SEL_TOPK = 16
WINDOW = 512
SGU_WIDTH = 2048
SGU_GROUPS = 8
SGU_CHUNK = 128
MEM_HEADS = 4
N_GROUPS = 4
EXPERTS_PER_GROUP = 16
N_EXPERTS = N_GROUPS * EXPERTS_PER_GROUP
EXPERT_TOPK = 2
MOE_BLOCK = 128
ROPE_THETA = 10000.0
LN_EPS = 1e-5
SEL_BIG = 1e9
DEPTH = 1
DN_ALPHA = (2.0 * DEPTH) ** 0.25

Q_WIDTH = NSA_HEADS * HEAD_DIM
KV_WIDTH = 3 * 2 * NSA_KV_GROUPS * HEAD_DIM
NSA_GATE_WIDTH = 3 * NSA_HEADS
MEM_WIDTH = MEM_HEADS * HEAD_DIM

LANES = 128
VMEM_LIMIT = 56 * 1024 * 1024
MASKED = -1e30
M_INIT = -1e20
BF16 = jnp.bfloat16
F32 = jnp.float32


def _params(*sem):
    return pltpu.CompilerParams(dimension_semantics=sem, vmem_limit_bytes=VMEM_LIMIT)


def _dot(a, b):
    return jnp.dot(a, b, preferred_element_type=F32)


def _dot_nt(a, b):
    return lax.dot_general(a, b, (((1,), (1,)), ((), ())), preferred_element_type=F32)


def _gelu(x):
    return 0.5 * x * (1.0 + jnp.tanh(math.sqrt(2.0 / math.pi) * (x + 0.044715 * (x * x * x))))


def _sigmoid(x):
    return 1.0 / (1.0 + jnp.exp(-x))


def _layer_norm(x, g, b):
    mu = jnp.mean(x, -1, keepdims=True)
    xc = x - mu
    var = jnp.mean(xc * xc, -1, keepdims=True)
    return xc * lax.rsqrt(var + LN_EPS) * g + b


def _rope(t, cos_full, sin_signed):
    return t * cos_full + pltpu.roll(t, HEAD_DIM // 2, axis=1) * sin_signed


def _rope_tables(pos):
    inv = ROPE_THETA ** (-jnp.arange(0, HEAD_DIM, 2, dtype=F32) / HEAD_DIM)
    ang = pos.astype(F32)[:, None] * inv[None, :]
    c, s = jnp.cos(ang), jnp.sin(ang)
    return jnp.concatenate([c, c], -1), jnp.concatenate([-s, s], -1)


MXU_COLS = 256


def _mm_kernel(a_ref, b_ref, o_ref):
    o_ref[...] = _dot(a_ref[...], b_ref[...]).astype(o_ref.dtype)


def _matmul(a, b, *, tm, tn, out_dtype):
    m, k = a.shape
    n = b.shape[1]
    return pl.pallas_call(
        _mm_kernel,
        grid=(m // tm, n // tn),
        in_specs=[pl.BlockSpec((tm, k), lambda i, j: (i, 0)),
                  pl.BlockSpec((k, tn), lambda i, j: (0, j))],
        out_specs=pl.BlockSpec((tm, tn), lambda i, j: (i, j)),
        out_shape=jax.ShapeDtypeStruct((m, n), out_dtype),
        compiler_params=_params("parallel", "parallel"),
        name="matmul",
    )(a, b)


def _qproj_kernel(a_ref, b_ref, cos_ref, sin_ref, o_ref):
    a = a_ref[...]
    cos, sin = cos_ref[...], sin_ref[...]
    scale = HEAD_DIM ** -0.5 * math.log2(math.e)
    for j in range(o_ref.shape[1] // MXU_COLS):
        y = _dot(a, b_ref[:, j * MXU_COLS:(j + 1) * MXU_COLS])
        for h in range(MXU_COLS // HEAD_DIM):
            sl = slice(j * MXU_COLS + h * HEAD_DIM, j * MXU_COLS + (h + 1) * HEAD_DIM)
            o_ref[:, sl] = (_rope(y[:, h * HEAD_DIM:(h + 1) * HEAD_DIM], cos, sin) * scale).astype(o_ref.dtype)


def _q_proj(x_bf, w_q, cos, sin, seq, *, tm=1024, tn=1024):
    m, k = x_bf.shape
    n = w_q.shape[1]
    per_seq = seq // tm
    return pl.pallas_call(
        _qproj_kernel,
        grid=(m // tm, n // tn),
        in_specs=[pl.BlockSpec((tm, k), lambda i, j: (i, 0)),
                  pl.BlockSpec((k, tn), lambda i, j: (0, j)),
                  pl.BlockSpec((tm, HEAD_DIM), lambda i, j: (i % per_seq, 0)),
                  pl.BlockSpec((tm, HEAD_DIM), lambda i, j: (i % per_seq, 0))],
        out_specs=pl.BlockSpec((tm, tn), lambda i, j: (i, j)),
        out_shape=jax.ShapeDtypeStruct((m, n), BF16),
        compiler_params=_params("parallel", "parallel"),
        name="q_proj",
    )(x_bf, w_q, cos, sin)


_N_CMP_COLS = 2 * NSA_KV_GROUPS * HEAD_DIM
_N_REST_CHUNKS = (KV_WIDTH - _N_CMP_COLS) // HEAD_DIM
_N_REST_COLS = (_N_REST_CHUNKS + NSA_KV_GROUPS) * HEAD_DIM
_GATE_COLS = NSA_KV_GROUPS * LANES
_SEL_SHIFT = SEL_BLOCK.bit_length() - 1
assert 1 << _SEL_SHIFT == SEL_BLOCK


def _kvproj_kernel(a_ref, b_ref, cos_ref, sin_ref, cmp_ref, rest_ref, gate_ref, abf_ref, *, per_seq):
    g = NSA_KV_GROUPS
    a = a_ref[...].astype(BF16)
    abf_ref[...] = a
    y = _dot(a, b_ref[...])
    cos, sin = cos_ref[...], sin_ref[...]
    cmp_ref[...] = y[:, :_N_CMP_COLS]
    tm = y.shape[0]
    pos = (pl.program_id(0) % per_seq) * tm + lax.broadcasted_iota(jnp.int32, (tm, LANES), 0)
    lane = lax.broadcasted_iota(jnp.int32, (tm, LANES), 1)
    blk_onehot = jnp.where(lane == (pos >> _SEL_SHIFT), 1.0, 0.0).astype(rest_ref.dtype)
    for c in range(_N_REST_CHUNKS):
        t = y[:, _N_CMP_COLS + c * HEAD_DIM:_N_CMP_COLS + (c + 1) * HEAD_DIM]
        if (c % (2 * g)) < g:
            t = _rope(t, cos, sin)
        dst = 2 * c if c < g else g + c
        rest_ref[:, dst * HEAD_DIM:(dst + 1) * HEAD_DIM] = t.astype(rest_ref.dtype)
        if c < g:
            rest_ref[:, (dst + 1) * HEAD_DIM:(dst + 2) * HEAD_DIM] = blk_onehot
    gate_ref[...] = _sigmoid(y[:, KV_WIDTH:])


def _kv_proj(x, w_kvg, cos, sin, seq, *, tm=512):
    m, k = x.shape
    n = w_kvg.shape[1]
    per_seq = seq // tm
    return pl.pallas_call(
        functools.partial(_kvproj_kernel, per_seq=per_seq),
        grid=(m // tm,),
        in_specs=[pl.BlockSpec((tm, k), lambda i: (i, 0)),
                  pl.BlockSpec((k, n), lambda i: (0, 0)),
                  pl.BlockSpec((tm, HEAD_DIM), lambda i: (i % per_seq, 0)),
                  pl.BlockSpec((tm, HEAD_DIM), lambda i: (i % per_seq, 0))],
        out_specs=[pl.BlockSpec((tm, _N_CMP_COLS), lambda i: (i, 0)),
                   pl.BlockSpec((tm, _N_REST_COLS), lambda i: (i, 0)),
                   pl.BlockSpec((tm, _GATE_COLS), lambda i: (i, 0)),
                   pl.BlockSpec((tm, k), lambda i: (i, 0))],
        out_shape=[jax.ShapeDtypeStruct((m, _N_CMP_COLS), F32),
                   jax.ShapeDtypeStruct((m, _N_REST_COLS), BF16),
                   jax.ShapeDtypeStruct((m, _GATE_COLS), F32),
                   jax.ShapeDtypeStruct((m, k), BF16)],
        compiler_params=_params("parallel"),
        name="kv_gate_proj",
    )(x, w_kvg, cos, sin)


def _compress_kernel(kc_ref, pe_ref, w1_ref, w2_ref, cos_ref, sin_ref, o_ref):
    n_chunks = o_ref.shape[2]
    top = jnp.zeros((n_chunks, CMP_HIDDEN), F32)
    bot = jnp.zeros((n_chunks, CMP_HIDDEN), F32)
    for p in range(CMP_STRIDE):
        xp = kc_ref[0, pl.ds(p, n_chunks, stride=CMP_STRIDE), :]
        q = CMP_STRIDE + p
        top += _dot((xp + pe_ref[0, p:p + 1, :]).astype(BF16), w1_ref[0, p * HEAD_DIM:(p + 1) * HEAD_DIM, :])
        bot += _dot((xp + pe_ref[0, q:q + 1, :]).astype(BF16), w1_ref[0, q * HEAD_DIM:(q + 1) * HEAD_DIM, :])
    h = top + pltpu.roll(bot, n_chunks - 1, axis=0)
    y = _dot(_gelu(h).astype(BF16), w2_ref[0])
    o_ref[0, 0] = _rope(y, cos_ref[0], sin_ref[0]).astype(o_ref.dtype)


def _compress(kv_cmp, pe, w1, w2, cos, sin):
    bsz, seq, width = kv_cmp.shape
    four = width // HEAD_DIM
    n_chunks = seq // CMP_STRIDE
    g = NSA_KV_GROUPS
    return pl.pallas_call(
        _compress_kernel,
        grid=(four, bsz),
        in_specs=[pl.BlockSpec((1, seq, HEAD_DIM), lambda a, b: (b, 0, a)),
                  pl.BlockSpec((1, CMP_BLOCK, HEAD_DIM), lambda a, b: (a // g, 0, 0)),
                  pl.BlockSpec((1, CMP_BLOCK * HEAD_DIM, CMP_HIDDEN), lambda a, b: (a // g, 0, 0)),
                  pl.BlockSpec((1, CMP_HIDDEN, HEAD_DIM), lambda a, b: (a // g, 0, 0)),
                  pl.BlockSpec((1, n_chunks, HEAD_DIM), lambda a, b: (a // g, 0, 0)),
                  pl.BlockSpec((1, n_chunks, HEAD_DIM), lambda a, b: (a // g, 0, 0))],
        out_specs=pl.BlockSpec((1, 1, n_chunks, HEAD_DIM), lambda a, b: (a, b, 0, 0)),
        out_shape=jax.ShapeDtypeStruct((four, bsz, n_chunks, HEAD_DIM), BF16),
        compiler_params=_params("parallel", "parallel"),
        name="compress",
    )(kv_cmp, pe, w1, w2, cos, sin)


SEL_TILE = 512
SEL_WIDE = 2
NSA_TQ = 256
SUBLANES = 8


def _softmax_parts(s3, bias):
    s3 = s3 + bias[None]
    m = jnp.max(s3, -1, keepdims=True)
    m = jnp.where(m > 0.5 * MASKED, m, 0.0)
    p = jnp.exp2(s3 - m)
    return p, 1.0 / jnp.maximum(jnp.sum(p, -1, keepdims=True), 1e-30)


def _topk_unselected(score_t, n_sel, n_pick):
    tq = score_t.shape[1]
    n_grp = n_sel // SUBLANES
    grp = [score_t[g * SUBLANES:(g + 1) * SUBLANES] for g in range(n_grp)]
    row_in_grp = lax.broadcasted_iota(jnp.int32, (SUBLANES, tq), 0)
    later = [jnp.where(row_in_grp > li, 1.0, 0.0) for li in range(SUBLANES)]
    beaten = [jnp.zeros((SUBLANES, tq), F32) for _ in range(n_grp)]
    for i in range(n_sel):
        gi, li = divmod(i, SUBLANES)
        row = grp[gi][li:li + 1, :]
        for g in range(n_grp):
            if g > gi:
                beaten[g] = beaten[g] + jnp.where(row >= grp[g], 1.0, 0.0)
            elif g < gi:
                beaten[g] = beaten[g] + jnp.where(row > grp[g], 1.0, 0.0)
            else:
                beaten[g] = (beaten[g] + jnp.where(row > grp[g], 1.0, 0.0)
                             + jnp.where(row == grp[g], later[li], 0.0))
    return jnp.concatenate([jnp.where(b < n_pick, 0.0, MASKED) for b in beaten], axis=0)


def _nsa_kernel(q_ref, kc_ref, vc_ref, ks_ref, vs_ref, kw_ref, vw_ref, gate_ref, selmap_t_ref, o_ref,
                *, n_sel, n_pick):
    hpg = NSA_HPG
    tq = NSA_TQ
    c = pl.program_id(2)
    q_t = q_ref[0]
    q8 = jnp.concatenate([q_t[:, n * HEAD_DIM:(n + 1) * HEAD_DIM] for n in range(hpg)], axis=0)
    t_pos = c * tq + lax.broadcasted_iota(jnp.int32, (tq, 1), 0)
    t_lane = c * tq + lax.broadcasted_iota(jnp.int32, (1, tq), 1)

    kc = kc_ref[0, 0]
    n_cmp_pad = kc.shape[0]
    k_idx = lax.broadcasted_iota(jnp.int32, (1, n_cmp_pad), 1)
    vis_c = (CMP_STRIDE * k_idx + (CMP_BLOCK - 1) <= t_pos) & (k_idx < n_cmp_pad - 1)
    p_c, r_c = _softmax_parts(_dot_nt(q8, kc).reshape(hpg, tq, n_cmp_pad), jnp.where(vis_c, 0.0, MASKED))
    o_c = _dot(p_c.reshape(hpg * tq, n_cmp_pad).astype(BF16), vc_ref[0, 0]).reshape(hpg, tq, HEAD_DIM) * r_c

    p_sum = jnp.sum(p_c * r_c, axis=0)
    p_hi = p_sum.astype(BF16)
    p_lo = (p_sum - p_hi.astype(F32)).astype(BF16)
    selmap_t = selmap_t_ref[...]
    imp_t = _dot_nt(selmap_t, p_hi) + _dot_nt(selmap_t, p_lo)
    blk = lax.broadcasted_iota(jnp.int32, (LANES, tq), 0)
    cur = t_lane >> _SEL_SHIFT
    forced = (blk == 0) | (blk == cur) | (blk == cur - 1)
    score_t = jnp.where(forced, SEL_BIG, jnp.where(blk <= cur, imp_t, -SEL_BIG))
    unsel_t = _topk_unselected(score_t, n_sel, n_pick)
    unsel_t = jnp.concatenate([unsel_t, jnp.zeros((LANES - n_sel, tq), F32)], axis=0)
    unsel = unsel_t.T.astype(BF16)
    q_aug = jnp.concatenate([q8, jnp.concatenate([unsel] * hpg, axis=0)], axis=1)

    def sel_tile(kt, size, carry, bias):
        m_old, l_old, acc = carry
        base = pl.multiple_of(kt * size, size)
        k = ks_ref[0, pl.ds(base, size), :]
        v = vs_ref[0, pl.ds(base, size), :]
        s = _dot_nt(q_aug, k).reshape(hpg, tq, size)
        if bias is not None:
            s = s + bias[None]
        m_new = jnp.maximum(m_old, jnp.max(s, -1, keepdims=True))
        alpha = jnp.exp2(m_old - m_new)
        p = jnp.exp2(s - m_new)
        l_new = alpha * l_old + jnp.sum(p, -1, keepdims=True)
        pv = _dot(p.reshape(hpg * tq, size).astype(BF16), v).reshape(hpg, tq, HEAD_DIM)
        return m_new, l_new, alpha * acc + pv

    n_full = c // (SEL_TILE // tq)
    n_wide = n_full // SEL_WIDE
    carry = (jnp.full((hpg, tq, 1), M_INIT, F32), jnp.zeros((hpg, tq, 1), F32),
             jnp.zeros((hpg, tq, HEAD_DIM), F32))
    carry = lax.fori_loop(0, n_wide, lambda kt, cr: sel_tile(kt, SEL_WIDE * SEL_TILE, cr, None), carry)
    carry = lax.fori_loop(n_wide * SEL_WIDE, n_full, lambda kt, cr: sel_tile(kt, SEL_TILE, cr, None), carry)
    kp_last = n_full * SEL_TILE + lax.broadcasted_iota(jnp.int32, (1, SEL_TILE), 1)
    _, l_s, acc_s = sel_tile(n_full, SEL_TILE, carry, jnp.where(kp_last <= t_pos, 0.0, MASKED))
    o_s = acc_s * (1.0 / jnp.maximum(l_s, 1e-30))

    span = WINDOW + tq
    start = pl.multiple_of(jnp.maximum(c * tq - WINDOW, 0), tq)
    kw = kw_ref[0, pl.ds(start, span), :]
    vw = vw_ref[0, pl.ds(start, span), :]
    kp = start + lax.broadcasted_iota(jnp.int32, (1, span), 1)
    vis_w = (kp <= t_pos) & (t_pos - kp < WINDOW)
    p_w, r_w = _softmax_parts(_dot_nt(q8, kw).reshape(hpg, tq, span), jnp.where(vis_w, 0.0, MASKED))
    o_w = _dot(p_w.reshape(hpg * tq, span).astype(BF16), vw).reshape(hpg, tq, HEAD_DIM) * r_w

    gates = gate_ref[0]
    for n in range(hpg):
        out = (gates[:, n:n + 1] * o_c[n]
               + gates[:, hpg + n:hpg + n + 1] * o_s[n]
               + gates[:, 2 * hpg + n:2 * hpg + n + 1] * o_w[n])
        o_ref[0, :, n * HEAD_DIM:(n + 1) * HEAD_DIM] = out.astype(o_ref.dtype)


def _nsa(q, cmp_kv, rest, gates, selmap_t, *, n_sel, n_pick):
    bsz, seq, _ = q.shape
    g = NSA_KV_GROUPS
    n_cmp_pad = cmp_kv.shape[2]
    gw = NSA_HPG * HEAD_DIM
    assert n_sel % SUBLANES == 0 and n_sel <= LANES and seq >= WINDOW + NSA_TQ
    assert SEL_TILE % NSA_TQ == 0 and seq % SEL_TILE == 0
    kv_spec = lambda off: pl.BlockSpec((1, seq, HEAD_DIM), lambda b, gi, c: (b, 0, off + gi))
    return pl.pallas_call(
        functools.partial(_nsa_kernel, n_sel=n_sel, n_pick=n_pick),
        grid=(bsz, g, seq // NSA_TQ),
        in_specs=[pl.BlockSpec((1, NSA_TQ, gw), lambda b, gi, c: (b, c, gi)),
                  pl.BlockSpec((1, 1, n_cmp_pad, HEAD_DIM), lambda b, gi, c: (gi, b, 0, 0)),
                  pl.BlockSpec((1, 1, n_cmp_pad, HEAD_DIM), lambda b, gi, c: (g + gi, b, 0, 0)),
                  pl.BlockSpec((1, seq, 2 * HEAD_DIM), lambda b, gi, c: (b, 0, gi)),
                  kv_spec(2 * g), kv_spec(3 * g), kv_spec(4 * g),
                  pl.BlockSpec((1, NSA_TQ, LANES), lambda b, gi, c: (b, c, gi)),
                  pl.BlockSpec(selmap_t.shape, lambda b, gi, c: (0, 0))],
        out_specs=pl.BlockSpec((1, NSA_TQ, gw), lambda b, gi, c: (b, c, gi)),
        out_shape=jax.ShapeDtypeStruct(q.shape, BF16),
        compiler_params=_params("parallel", "parallel", "arbitrary"),
        name="nsa_attention",
    )(q, cmp_kv, cmp_kv, rest, rest, rest, rest, gates, selmap_t)


def _sgu_kernel(x_ref, w_ref, g_ref, b_ref, ws_ref, bs_ref, o_ref, *, chunks):
    gd = SGU_WIDTH // SGU_GROUPS
    row = lax.broadcasted_iota(jnp.int32, (SGU_CHUNK, SGU_CHUNK), 0)
    col = lax.broadcasted_iota(jnp.int32, (SGU_CHUNK, SGU_CHUNK), 1)
    causal = col <= row
    bs = bs_ref[...]
    x = x_ref[...]
    v = jnp.concatenate(
        [_gelu(_dot(x, w_ref[:, SGU_WIDTH + j * MXU_COLS:SGU_WIDTH + (j + 1) * MXU_COLS]))
         for j in range(SGU_WIDTH // MXU_COLS)], axis=1)
    vn = _layer_norm(v, g_ref[...], b_ref[...]).astype(BF16)
    for gi in range(SGU_GROUPS):
        cols = slice(gi * gd, (gi + 1) * gd)
        u = _gelu(_dot(x, w_ref[:, cols]))
        w = jnp.where(causal, ws_ref[gi], 0.0).astype(BF16)
        for ch in range(chunks):
            rows = slice(ch * SGU_CHUNK, (ch + 1) * SGU_CHUNK)
            z = _dot(w, vn[rows, cols]) + bs[:, gi:gi + 1]
            o_ref[rows, cols] = (u[rows] * z).astype(o_ref.dtype)


def _sgu(x_bf, w_uv, ln_g, ln_b, w_s, b_s_t, *, chunks=4):
    m, d = x_bf.shape
    tm = chunks * SGU_CHUNK
    return pl.pallas_call(
        functools.partial(_sgu_kernel, chunks=chunks),
        grid=(m // tm,),
        in_specs=[pl.BlockSpec((tm, d), lambda i: (i, 0)),
                  pl.BlockSpec(w_uv.shape, lambda i: (0, 0), pipeline_mode=pl.Buffered(1)),
                  pl.BlockSpec((1, SGU_WIDTH), lambda i: (0, 0)),
                  pl.BlockSpec((1, SGU_WIDTH), lambda i: (0, 0)),
                  pl.BlockSpec(w_s.shape, lambda i: (0, 0, 0)),
                  pl.BlockSpec(b_s_t.shape, lambda i: (0, 0))],
        out_specs=pl.BlockSpec((tm, SGU_WIDTH), lambda i: (i, 0)),
        out_shape=jax.ShapeDtypeStruct((m, SGU_WIDTH), BF16),
        compiler_params=_params("parallel"),
        name="sgu",
    )(x_bf, w_uv, ln_g, ln_b, w_s, b_s_t)


def _merge_kernel(x_ref, a_ref, s_ref, wga_ref, wgb_ref, wa_ref, wb_ref, o_ref):
    x = x_ref[...]
    ga = _sigmoid(_dot(x, wga_ref[...]))
    gb = _sigmoid(_dot(x, wgb_ref[...]))
    o_ref[...] = (ga * _dot(a_ref[...], wa_ref[...]) + gb * _dot(s_ref[...], wb_ref[...])).astype(o_ref.dtype)


def _merge(x_bf, o_nsa, o_sgu, w_merge, w_a, w_b, *, tm=1024, tn=512):
    m, k = x_bf.shape
    n = w_a.shape[1]
    nb = n // tn
    row = pl.BlockSpec((tm, k), lambda i, j: (i, 0))
    col = pl.BlockSpec((k, tn), lambda i, j: (0, j))
    return pl.pallas_call(
        _merge_kernel,
        grid=(m // tm, nb),
        in_specs=[row, row, row, col, pl.BlockSpec((k, tn), lambda i, j: (0, nb + j)), col, col],
        out_specs=pl.BlockSpec((tm, tn), lambda i, j: (i, j)),
        out_shape=jax.ShapeDtypeStruct((m, n), BF16),
        compiler_params=_params("parallel", "parallel"),
        name="merge",
    )(x_bf, o_nsa, o_sgu, w_merge, w_merge, w_a, w_b)


def _proj_ln_kernel(a_ref, w_ref, x_ref, g_ref, b_ref, o_ref, obf_ref):
    y = DN_ALPHA * x_ref[...] + _dot(a_ref[...], w_ref[...])
    out = _layer_norm(y, g_ref[...], b_ref[...])
    o_ref[...] = out
    obf_ref[...] = out.astype(BF16)


def _proj_ln(a_bf, w, x, ln_g, ln_b, *, tm=512):
    m, k = a_bf.shape
    n = w.shape[1]
    vec = pl.BlockSpec((1, n), lambda i: (0, 0))
    return pl.pallas_call(
        _proj_ln_kernel,
        grid=(m // tm,),
        in_specs=[pl.BlockSpec((tm, k), lambda i: (i, 0)),
                  pl.BlockSpec((k, n), lambda i: (0, 0)),
                  pl.BlockSpec((tm, n), lambda i: (i, 0)), vec, vec],
        out_specs=[pl.BlockSpec((tm, n), lambda i: (i, 0)), pl.BlockSpec((tm, n), lambda i: (i, 0))],
        out_shape=[jax.ShapeDtypeStruct((m, n), F32), jax.ShapeDtypeStruct((m, n), BF16)],
        compiler_params=_params("parallel"),
        name="proj_ln",
    )(a_bf, w, x, ln_g, ln_b)


def _xattn_kernel(xbf_ref, x_ref, wq_ref, kv_ref, wo_ref, g_ref, b_ref, o_ref):
    q = (_dot(xbf_ref[0], wq_ref[...]) * HEAD_DIM ** -0.5).astype(BF16)
    kv = kv_ref[0]
    heads = []
    for h in range(MEM_HEADS):
        cols = slice(h * HEAD_DIM, (h + 1) * HEAD_DIM)
        s = _dot_nt(q[:, cols], kv[:, cols])
        p = jnp.exp(s - jnp.max(s, -1, keepdims=True))
        p = p / jnp.sum(p, -1, keepdims=True)
        heads.append(_dot(p.astype(BF16), kv[:, MEM_WIDTH + h * HEAD_DIM:MEM_WIDTH + (h + 1) * HEAD_DIM]))
    o = jnp.concatenate(heads, axis=1).astype(BF16)
    y = DN_ALPHA * x_ref[0] + _dot(o, wo_ref[...])
    o_ref[0] = _layer_norm(y, g_ref[...], b_ref[...])


def _xattn(x_bf, x, w_xq, mem_kv, w_xo, ln_g, ln_b, *, tm=512):
    bsz, seq, d = x.shape
    mem_len = mem_kv.shape[1]
    vec = pl.BlockSpec((1, d), lambda b, i: (0, 0))
    return pl.pallas_call(
        _xattn_kernel,
        grid=(bsz, seq // tm),
        in_specs=[pl.BlockSpec((1, tm, d), lambda b, i: (b, i, 0)),
                  pl.BlockSpec((1, tm, d), lambda b, i: (b, i, 0)),
                  pl.BlockSpec(w_xq.shape, lambda b, i: (0, 0)),
                  pl.BlockSpec((1, mem_len, 2 * MEM_WIDTH), lambda b, i: (b, 0, 0)),
                  pl.BlockSpec(w_xo.shape, lambda b, i: (0, 0)), vec, vec],
        out_specs=pl.BlockSpec((1, tm, d), lambda b, i: (b, i, 0)),
        out_shape=jax.ShapeDtypeStruct(x.shape, F32),
        compiler_params=_params("parallel", "parallel"),
        name="xattn_ln",
    )(x_bf, x, w_xq, mem_kv, w_xo, ln_g, ln_b)


def _router_kernel(x_ref, whi_ref, wlo_ref, b_ref, gate_ref, exp_ref, rank_ref, cnt_ref, seen_ref):
    @pl.when(pl.program_id(0) == 0)
    def _():
        seen_ref[...] = jnp.zeros_like(seen_ref)

    x = x_ref[...]
    x_hi = x.astype(BF16)
    x_lo = (x - x_hi.astype(F32)).astype(BF16)
    whi = whi_ref[...]
    logits = _dot(x_hi, whi) + _dot(x_hi, wlo_ref[...]) + _dot(x_lo, whi) + b_ref[...]
    rows = logits.shape[0]
    lane = lax.broadcasted_iota(jnp.int32, (rows, LANES), 1)
    lane_f = lane.astype(F32)
    is_grp = lane < N_GROUPS
    lg = jnp.where(is_grp, logits, MASKED)
    eg = jnp.where(is_grp, jnp.exp(lg - jnp.max(lg, -1, keepdims=True)), 0.0)
    p_grp = eg / jnp.sum(eg, -1, keepdims=True)
    g_grp = jnp.max(p_grp, -1, keepdims=True)
    grp = jnp.min(jnp.where((p_grp == g_grp) & is_grp, lane_f, 1e9), -1, keepdims=True).astype(jnp.int32)
    lo = N_GROUPS + grp * EXPERTS_PER_GROUP
    in_grp = (lane >= lo) & (lane < lo + EXPERTS_PER_GROUP)
    le = jnp.where(in_grp, logits, MASKED)
    ee = jnp.where(in_grp, jnp.exp(le - jnp.max(le, -1, keepdims=True)), 0.0)
    p_exp = jnp.where(in_grp, ee / jnp.sum(ee, -1, keepdims=True), -1.0)
    v1 = jnp.max(p_exp, -1, keepdims=True)
    i1 = jnp.min(jnp.where(p_exp == v1, lane_f, 1e9), -1, keepdims=True)
    p_rest = jnp.where(lane_f == i1, -1.0, p_exp)
    v2 = jnp.max(p_rest, -1, keepdims=True)
    i2 = jnp.min(jnp.where(p_rest == v2, lane_f, 1e9), -1, keepdims=True)
    denom = v1 + v2
    gate_ref[...] = jnp.where(lane == 0, g_grp * v1 / denom, jnp.where(lane == 1, g_grp * v2 / denom, 0.0))
    exp_ref[...] = jnp.where(lane == 0, i1, jnp.where(lane == 1, i2, float(N_GROUPS))).astype(jnp.int32) - N_GROUPS

    onehot = jnp.where((lane_f == i1) | (lane_f == i2), 1.0, 0.0)
    tok_r = lax.broadcasted_iota(jnp.int32, (rows, rows), 0)
    tok_c = lax.broadcasted_iota(jnp.int32, (rows, rows), 1)
    earlier = jnp.where(tok_c < tok_r, 1.0, 0.0).astype(BF16)
    before = _dot(earlier, onehot.astype(BF16)) + seen_ref[...]
    r1 = jnp.sum(jnp.where(lane_f == i1, before, 0.0), -1, keepdims=True)
    r2 = jnp.sum(jnp.where(lane_f == i2, before, 0.0), -1, keepdims=True)
    rank_ref[...] = jnp.where(lane == 0, r1, jnp.where(lane == 1, r2, 0.0)).astype(jnp.int32)
    seen_ref[...] += jnp.sum(onehot, axis=0, keepdims=True)
    cnt_ref[...] = seen_ref[...]


def _router(x, w_hi, w_lo, bias, *, tm=512):
    m, d = x.shape
    wspec = pl.BlockSpec((d, LANES), lambda i: (0, 0))
    vec = pl.BlockSpec((1, LANES), lambda i: (0, 0))
    tile = pl.BlockSpec((tm, LANES), lambda i: (i, 0))
    return pl.pallas_call(
        _router_kernel,
        grid=(m // tm,),
        in_specs=[pl.BlockSpec((tm, d), lambda i: (i, 0)), wspec, wspec, vec],
        out_specs=[tile, tile, tile, vec],
        out_shape=[jax.ShapeDtypeStruct((m, LANES), F32), jax.ShapeDtypeStruct((m, LANES), jnp.int32),
                   jax.ShapeDtypeStruct((m, LANES), jnp.int32), jax.ShapeDtypeStruct((1, LANES), F32)],
        scratch_shapes=[pltpu.VMEM((1, LANES), F32)],
        compiler_params=_params("arbitrary"),
        name="router",
    )(x, w_hi, w_lo, bias)


GATHER_UNROLL = 8


def _ffn_kernel(exp_ref, on_ref, first_ref, slot_ref, next_ref, tok_ref, dst_ref, x_hbm, wg_hbm, wu_hbm, wd_hbm,
                y_hbm, wg_buf, wu_buf, wd_buf, h_buf, y_buf, sem, row_sem, out_sem):
    i = pl.program_id(0)
    n_blk = pl.num_programs(0)
    slot = slot_ref[i]
    rows = MOE_BLOCK

    def out_copy(blk, g, u):
        dst = dst_ref[blk * rows + g * GATHER_UNROLL + u]
        return pltpu.make_async_copy(y_buf.at[blk % 2, g, pl.ds(u, 1), :], y_hbm.at[pl.ds(dst, 1), :],
                                     out_sem.at[blk % 2])

    def weight_copies(e, s):
        return (pltpu.make_async_copy(wg_hbm.at[e], wg_buf.at[s], sem.at[s, 0]),
                pltpu.make_async_copy(wu_hbm.at[e], wu_buf.at[s], sem.at[s, 1]),
                pltpu.make_async_copy(wd_hbm.at[e], wd_buf.at[s], sem.at[s, 2]))

    def row_copy(blk, g, u):
        tok = tok_ref[blk * rows + g * GATHER_UNROLL + u]
        return pltpu.make_async_copy(x_hbm.at[pl.ds(tok, 1), :], h_buf.at[blk % 2, g, pl.ds(u, 1), :],
                                     row_sem.at[blk % 2])

    def for_rows(blk, fn):
        def body(g, carry):
            for u in range(GATHER_UNROLL):
                fn(blk, g, u)
            return carry
        lax.fori_loop(0, on_ref[blk], body, 0)

    @pl.when(i == 0)
    def _():
        for cp in weight_copies(exp_ref[0], 0):
            cp.start()
        h_buf[...] = jnp.zeros_like(h_buf)
        for_rows(0, lambda b, g, u: row_copy(b, g, u).start())
        y_buf[...] = jnp.zeros_like(y_buf)
        spare = y_hbm.shape[0] - 2 * rows
        fills = [pltpu.make_async_copy(y_buf.at[s, g],
                                       y_hbm.at[pl.ds(spare + s * rows + g * GATHER_UNROLL, GATHER_UNROLL), :],
                                       out_sem.at[s])
                 for s in range(2) for g in range(rows // GATHER_UNROLL)]
        for cp in fills:
            cp.start()
        for cp in fills:
            cp.wait()

    prev2 = jnp.maximum(i - 2, 0)

    @pl.when((i >= 2) & (on_ref[prev2] != 0))
    def _():
        for_rows(i - 2, lambda b, g, u: out_copy(b, g, u).wait())

    nxt_blk = jnp.minimum(i + 1, n_blk - 1)

    @pl.when((i + 1 < n_blk) & (on_ref[nxt_blk] != 0))
    def _():
        for_rows(i + 1, lambda b, g, u: row_copy(b, g, u).start())

    @pl.when(first_ref[i] != 0)
    def _():
        nxt = next_ref[i]

        @pl.when(nxt >= 0)
        def _():
            for cp in weight_copies(nxt, 1 - slot):
                cp.start()

        for cp in weight_copies(exp_ref[i], slot):
            cp.wait()

    @pl.when(on_ref[i] != 0)
    def _():
        for_rows(i, lambda b, g, u: row_copy(b, g, u).wait())
        h = h_buf[i % 2].reshape(rows, h_buf.shape[-1]).astype(BF16)
        a = _dot(h, wg_buf[slot].astype(BF16))
        a = a * _sigmoid(a) * _dot(h, wu_buf[slot].astype(BF16))
        y_buf[i % 2] = _dot(a.astype(BF16), wd_buf[slot].astype(BF16)).reshape(y_buf.shape[1:])
        for_rows(i, lambda b, g, u: out_copy(b, g, u).start())

    @pl.when(i == n_blk - 1)
    def _():
        prev1 = jnp.maximum(i - 1, 0)

        @pl.when((i >= 1) & (on_ref[prev1] != 0))
        def _():
            for_rows(i - 1, lambda b, g, u: out_copy(b, g, u).wait())

        @pl.when(on_ref[i] != 0)
        def _():
            for_rows(i, lambda b, g, u: out_copy(b, g, u).wait())


def _expert_ffn(blk_exp, blk_on, row_tok, row_dst, x, w_gate, w_up, w_down):
    p_rows = row_tok.shape[0]
    n_tok, d = x.shape
    n_exp, _, ff = w_gate.shape
    n_blocks = p_rows // MOE_BLOCK
    prev_exp = jnp.concatenate([jnp.full((1,), -1, jnp.int32), blk_exp[:-1]])
    first = ((blk_on != 0) & (blk_exp != prev_exp)).astype(jnp.int32)
    seg_id = jnp.cumsum(first) - 1
    seg_ids = jnp.arange(n_exp + 1)
    hit = (first[:, None] != 0) & (seg_id[:, None] == seg_ids[None, :])
    seg_exp = jnp.sum(jnp.where(hit, blk_exp[:, None] + 1, 0), 0) - 1
    blk_next = jnp.sum(jnp.where(seg_ids[None, :] == seg_id[:, None] + 1, seg_exp[None, :] + 1, 0), 1) - 1
    blk_slot = jnp.maximum(seg_id, 0) % 2
    grid_spec = pltpu.PrefetchScalarGridSpec(
        num_scalar_prefetch=7,
        grid=(n_blocks,),
        in_specs=[pl.BlockSpec(memory_space=pl.ANY)] * 4,
        out_specs=pl.BlockSpec(memory_space=pl.ANY),
        scratch_shapes=[pltpu.VMEM((2, d, ff), F32), pltpu.VMEM((2, d, ff), F32), pltpu.VMEM((2, ff, d), F32),
                        pltpu.VMEM((2, MOE_BLOCK // GATHER_UNROLL, GATHER_UNROLL, d), F32),
                        pltpu.VMEM((2, MOE_BLOCK // GATHER_UNROLL, GATHER_UNROLL, d), F32),
                        pltpu.SemaphoreType.DMA((2, 3)), pltpu.SemaphoreType.DMA((2,)),
                        pltpu.SemaphoreType.DMA((2,))],
    )
    return pl.pallas_call(
        _ffn_kernel,
        grid_spec=grid_spec,
        out_shape=jax.ShapeDtypeStruct((EXPERT_TOPK * n_tok + 2 * MOE_BLOCK, d), F32),
        compiler_params=_params("arbitrary"),
        name="expert_ffn",
    )(blk_exp, blk_on, first, blk_slot.astype(jnp.int32), blk_next.astype(jnp.int32), row_tok, row_dst,
      x, w_gate, w_up, w_down)


def _combine_ln_kernel(x_ref, y0_ref, y1_ref, gate_ref, g_ref, b_ref, o_ref):
    gates = gate_ref[...]
    moe = y0_ref[...] * gates[:, 0:1] + y1_ref[...] * gates[:, 1:2]
    o_ref[...] = _layer_norm(DN_ALPHA * x_ref[...] + moe, g_ref[...], b_ref[...])


def _combine_ln(x, y_tok, gates, ln_g, ln_b, *, tm=256):
    m, d = x.shape
    row = pl.BlockSpec((tm, d), lambda i: (i, 0))
    vec = pl.BlockSpec((1, d), lambda i: (0, 0))
    return pl.pallas_call(
        _combine_ln_kernel,
        grid=(m // tm,),
        in_specs=[row, row, pl.BlockSpec((tm, d), lambda i: (m // tm + i, 0)),
                  pl.BlockSpec((tm, LANES), lambda i: (i, 0)), vec, vec],
        out_specs=row,
        out_shape=jax.ShapeDtypeStruct((m, d), F32),
        compiler_params=_params("parallel"),
        name="combine_ln",
    )(x, y_tok, y_tok, gates, ln_g, ln_b)


def _layer(x, mem, w_in, cmp_pe_k, cmp_w1_k, cmp_w2_k, cmp_pe_v, cmp_w1_v, cmp_w2_v,
           sgu_ln_g, sgu_ln_b, sgu_w_s, sgu_b_s, w_branch_a, w_branch_b, w_o, ln1_g, ln1_b,
           w_xq, w_xk, w_xv, w_xo, ln2_g, ln2_b,
           w_router_grp, b_router_grp, w_router_exp, b_router_exp,
           w_exp_gate, w_exp_up, w_exp_down, ln3_g, ln3_b):
    bsz, seq, d = x.shape
    n_tok = bsz * seq
    g = NSA_KV_GROUPS
    hpg = NSA_HPG
    xf = x.reshape(n_tok, d)

    c0 = Q_WIDTH
    c1 = c0 + KV_WIDTH
    c2 = c1 + NSA_GATE_WIDTH
    c3 = c2 + 2 * SGU_WIDTH
    w_q = w_in[:, :c0].astype(BF16)
    w_gate = w_in[:, c1:c2].reshape(d, g, hpg, 3).transpose(0, 1, 3, 2).reshape(d, g, 3 * hpg)
    w_gate = jnp.pad(w_gate, ((0, 0), (0, 0), (0, LANES - 3 * hpg))).reshape(d, g * LANES)
    w_kvg = jnp.concatenate([w_in[:, c0:c1], w_gate], axis=1).astype(BF16)
    w_uv = w_in[:, c2:c3].astype(BF16)
    w_merge = w_in[:, c3:].astype(BF16)

    cos, sin = _rope_tables(jnp.arange(seq, dtype=jnp.int32))
    kv_cmp, kv_rest, nsa_gates, x_bf = _kv_proj(xf, w_kvg, cos, sin, seq)
    q = _q_proj(x_bf, w_q, cos, sin, seq).reshape(bsz, seq, Q_WIDTH)

    n_chunks = seq // CMP_STRIDE
    pe = jnp.stack([cmp_pe_k, cmp_pe_v])
    w1 = jnp.stack([cmp_w1_k, cmp_w1_v]).astype(BF16)
    w2 = jnp.stack([cmp_w2_k, cmp_w2_v]).astype(BF16)
    cmp_end = CMP_STRIDE * jnp.arange(n_chunks, dtype=jnp.int32) + CMP_BLOCK - 1
    ccos, csin = _rope_tables(cmp_end)
    cmp_cos = jnp.stack([ccos, jnp.ones_like(ccos)])
    cmp_sin = jnp.stack([csin, jnp.zeros_like(csin)])
    cmp_kv = _compress(kv_cmp.reshape(bsz, seq, _N_CMP_COLS), pe, w1, w2, cmp_cos, cmp_sin)

    n_sel = seq // SEL_BLOCK
    n_pick = min(SEL_TOPK, n_sel)
    ci = CMP_STRIDE * jnp.arange(n_chunks)
    sj = SEL_BLOCK * jnp.arange(LANES)
    overlap = jnp.clip(jnp.minimum(ci[:, None] + CMP_BLOCK, sj[None, :] + SEL_BLOCK)
                       - jnp.maximum(ci[:, None], sj[None, :]), 0, None)
    selmap_t = (overlap.astype(F32) / CMP_BLOCK).astype(BF16).T
    o_nsa = _nsa(q, cmp_kv, kv_rest.reshape(bsz, seq, _N_REST_COLS),
                 nsa_gates.reshape(bsz, seq, _GATE_COLS), selmap_t, n_sel=n_sel, n_pick=n_pick)

    o_sgu = _sgu(x_bf, w_uv, sgu_ln_g.reshape(1, -1), sgu_ln_b.reshape(1, -1), sgu_w_s, sgu_b_s.T)

    merged = _merge(x_bf, o_nsa.reshape(n_tok, Q_WIDTH), o_sgu, w_merge,
                    w_branch_a.astype(BF16), w_branch_b.astype(BF16))
    x1, x1_bf = _proj_ln(merged, w_o.astype(BF16), xf, ln1_g.reshape(1, -1), ln1_b.reshape(1, -1))

    mem_len = mem.shape[1]
    w_mem = jnp.concatenate([w_xk, w_xv], axis=1).astype(BF16)
    mem_kv = _matmul(mem.reshape(bsz * mem_len, d).astype(BF16), w_mem, tm=256, tn=512, out_dtype=BF16)
    x2 = _xattn(x1_bf.reshape(bsz, seq, d), x1.reshape(bsz, seq, d), w_xq.astype(BF16),
                mem_kv.reshape(bsz, mem_len, 2 * MEM_WIDTH), w_xo.astype(BF16),
                ln2_g.reshape(1, -1), ln2_b.reshape(1, -1)).reshape(n_tok, d)

    w_r = jnp.concatenate([w_router_grp, w_router_exp], axis=1)
    w_r = jnp.pad(w_r, ((0, 0), (0, LANES - w_r.shape[1])))
    w_r_hi = w_r.astype(BF16)
    w_r_lo = (w_r - w_r_hi.astype(F32)).astype(BF16)
    b_r = jnp.concatenate([b_router_grp, b_router_exp])
    b_r = jnp.pad(b_r, (0, LANES - b_r.shape[0])).reshape(1, LANES)
    gates, experts, ranks, seen = _router(x2, w_r_hi, w_r_lo, b_r)

    nk = n_tok * EXPERT_TOPK
    counts = seen[0, N_GROUPS:N_GROUPS + N_EXPERTS].astype(jnp.int32)
    pcounts = ((counts + MOE_BLOCK - 1) // MOE_BLOCK) * MOE_BLOCK
    pend = jnp.cumsum(pcounts)
    pstart = pend - pcounts
    e_tok = experts[:, :EXPERT_TOPK]
    seg_start = jnp.sum(jnp.where(e_tok[..., None] == jnp.arange(N_EXPERTS), pstart, 0), -1)
    dest_of = seg_start + ranks[:, :EXPERT_TOPK]
    n_blocks = -(-nk // MOE_BLOCK) + N_EXPERTS
    p_rows = n_blocks * MOE_BLOCK
    row_ids = jnp.arange(p_rows, dtype=jnp.int32)
    row_asg = jnp.full((p_rows,), -1, jnp.int32).at[dest_of.reshape(nk)].set(
        jnp.arange(nk, dtype=jnp.int32), unique_indices=True)
    is_real = row_asg >= 0
    row_tok = jnp.where(is_real, row_asg // EXPERT_TOPK, row_ids % n_tok)
    spare = EXPERT_TOPK * n_tok + ((row_ids // MOE_BLOCK) % 2) * MOE_BLOCK + row_ids % MOE_BLOCK
    row_dst = jnp.where(is_real, (row_asg % EXPERT_TOPK) * n_tok + row_asg // EXPERT_TOPK, spare)
    blk_start = jnp.arange(n_blocks) * MOE_BLOCK
    blk_on = (blk_start < pend[-1]).astype(jnp.int32)
    blk_probe = jnp.minimum(blk_start, pend[-1] - 1)
    blk_exp = jnp.minimum(jnp.sum((pend[None, :] <= blk_probe[:, None]).astype(jnp.int32), -1),
                          N_EXPERTS - 1).astype(jnp.int32)
    real_end = jnp.sum(jnp.where(blk_exp[:, None] == jnp.arange(N_EXPERTS), (pstart + counts)[None, :], 0), -1)
    blk_real = jnp.clip(real_end - blk_start, 0, MOE_BLOCK) * blk_on
    blk_groups = ((blk_real + GATHER_UNROLL - 1) // GATHER_UNROLL).astype(jnp.int32)

    y_tok = _expert_ffn(blk_exp, blk_groups, row_tok, row_dst, x2, w_exp_gate, w_exp_up, w_exp_down)
    out = _combine_ln(x2, y_tok, gates, ln3_g.reshape(1, -1), ln3_b.reshape(1, -1))
    return out.reshape(bsz, seq, d)


def kernel(x, mem, w_in, cmp_pe_k, cmp_w1_k, cmp_w2_k, cmp_pe_v, cmp_w1_v, cmp_w2_v, sgu_ln_g, sgu_ln_b, sgu_w_s, sgu_b_s, w_branch_a, w_branch_b, w_o, ln1_g, ln1_b, w_xq, w_xk, w_xv, w_xo, ln2_g, ln2_b, w_router_grp, b_router_grp, w_router_exp, b_router_exp, w_exp_gate, w_exp_up, w_exp_down, ln3_g, ln3_b):
    params = (w_in, cmp_pe_k, cmp_w1_k, cmp_w2_k, cmp_pe_v, cmp_w1_v, cmp_w2_v,
              sgu_ln_g, sgu_ln_b, sgu_w_s, sgu_b_s, w_branch_a, w_branch_b, w_o, ln1_g, ln1_b,
              w_xq, w_xk, w_xv, w_xo, ln2_g, ln2_b,
              w_router_grp, b_router_grp, w_router_exp, b_router_exp,
              w_exp_gate, w_exp_up, w_exp_down, ln3_g, ln3_b)
    h = x
    for l in range(DEPTH):
        h = _layer(h, mem, *[p[l] for p in params])
    return h
```

```python
import functools
import math

import jax
import jax.numpy as jnp
from jax import lax
from jax.experimental import pallas as pl
from jax.experimental.pallas import tpu as pltpu

HEAD_DIM = 128
NSA_HEADS = 16
NSA_KV_GROUPS = 2
NSA_HPG = NSA_HEADS // NSA_KV_GROUPS
CMP_BLOCK = 32
CMP_STRIDE = 16
CMP_HIDDEN = 256
SEL_BLOCK = 64
SEL_TOPK = 16
WINDOW = 512
SGU_WIDTH = 2048
SGU_GROUPS = 8
SGU_CHUNK = 128
MEM_HEADS = 4
N_GROUPS = 4
EXPERTS_PER_GROUP = 16
N_EXPERTS = N_GROUPS * EXPERTS_PER_GROUP
EXPERT_TOPK = 2
MOE_BLOCK = 128
ROPE_THETA = 10000.0
LN_EPS = 1e-5
SEL_BIG = 1e9
DEPTH = 1
DN_ALPHA = (2.0 * DEPTH) ** 0.25

Q_WIDTH = NSA_HEADS * HEAD_DIM
KV_WIDTH = 3 * 2 * NSA_KV_GROUPS * HEAD_DIM
NSA_GATE_WIDTH = 3 * NSA_HEADS
MEM_WIDTH = MEM_HEADS * HEAD_DIM

LANES = 128
VMEM_LIMIT = 56 * 1024 * 1024
MASKED = -1e30
M_INIT = -1e20
BF16 = jnp.bfloat16
F32 = jnp.float32


def _params(*sem):
    return pltpu.CompilerParams(dimension_semantics=sem, vmem_limit_bytes=VMEM_LIMIT)


def _dot(a, b):
    return jnp.dot(a, b, preferred_element_type=F32)


def _dot_nt(a, b):
    return lax.dot_general(a, b, (((1,), (1,)), ((), ())), preferred_element_type=F32)


def _gelu(x):
    return 0.5 * x * (1.0 + jnp.tanh(math.sqrt(2.0 / math.pi) * (x + 0.044715 * (x * x * x))))


def _sigmoid(x):
    return 1.0 / (1.0 + jnp.exp(-x))


def _layer_norm(x, g, b):
    mu = jnp.mean(x, -1, keepdims=True)
    xc = x - mu
    var = jnp.mean(xc * xc, -1, keepdims=True)
    return xc * lax.rsqrt(var + LN_EPS) * g + b


def _rope(t, cos_full, sin_signed):
    return t * cos_full + pltpu.roll(t, HEAD_DIM // 2, axis=1) * sin_signed


def _rope_tables(pos):
    inv = ROPE_THETA ** (-jnp.arange(0, HEAD_DIM, 2, dtype=F32) / HEAD_DIM)
    ang = pos.astype(F32)[:, None] * inv[None, :]
    c, s = jnp.cos(ang), jnp.sin(ang)
    return jnp.concatenate([c, c], -1), jnp.concatenate([-s, s], -1)


MXU_COLS = 256


def _mm_kernel(a_ref, b_ref, o_ref):
    o_ref[...] = _dot(a_ref[...], b_ref[...]).astype(o_ref.dtype)


def _matmul(a, b, *, tm, tn, out_dtype):
    m, k = a.shape
    n = b.shape[1]
    return pl.pallas_call(
        _mm_kernel,
        grid=(m // tm, n // tn),
        in_specs=[pl.BlockSpec((tm, k), lambda i, j: (i, 0)),
                  pl.BlockSpec((k, tn), lambda i, j: (0, j))],
        out_specs=pl.BlockSpec((tm, tn), lambda i, j: (i, j)),
        out_shape=jax.ShapeDtypeStruct((m, n), out_dtype),
        compiler_params=_params("parallel", "parallel"),
        name="matmul",
    )(a, b)


def _qproj_kernel(a_ref, b_ref, cos_ref, sin_ref, o_ref):
    a = a_ref[...]
    cos, sin = cos_ref[...], sin_ref[...]
    scale = HEAD_DIM ** -0.5 * math.log2(math.e)
    for j in range(o_ref.shape[1] // MXU_COLS):
        y = _dot(a, b_ref[:, j * MXU_COLS:(j + 1) * MXU_COLS])
        for h in range(MXU_COLS // HEAD_DIM):
            sl = slice(j * MXU_COLS + h * HEAD_DIM, j * MXU_COLS + (h + 1) * HEAD_DIM)
            o_ref[:, sl] = (_rope(y[:, h * HEAD_DIM:(h + 1) * HEAD_DIM], cos, sin) * scale).astype(o_ref.dtype)


def _q_proj(x_bf, w_q, cos, sin, seq, *, tm=1024, tn=1024):
    m, k = x_bf.shape
    n = w_q.shape[1]
    per_seq = seq // tm
    return pl.pallas_call(
        _qproj_kernel,
        grid=(m // tm, n // tn),
        in_specs=[pl.BlockSpec((tm, k), lambda i, j: (i, 0)),
                  pl.BlockSpec((k, tn), lambda i, j: (0, j)),
                  pl.BlockSpec((tm, HEAD_DIM), lambda i, j: (i % per_seq, 0)),
                  pl.BlockSpec((tm, HEAD_DIM), lambda i, j: (i % per_seq, 0))],
        out_specs=pl.BlockSpec((tm, tn), lambda i, j: (i, j)),
        out_shape=jax.ShapeDtypeStruct((m, n), BF16),
        compiler_params=_params("parallel", "parallel"),
        name="q_proj",
    )(x_bf, w_q, cos, sin)


_N_CMP_COLS = 2 * NSA_KV_GROUPS * HEAD_DIM
_N_REST_CHUNKS = (KV_WIDTH - _N_CMP_COLS) // HEAD_DIM
_N_REST_COLS = (_N_REST_CHUNKS + NSA_KV_GROUPS) * HEAD_DIM
_GATE_COLS = NSA_KV_GROUPS * LANES
_SEL_SHIFT = SEL_BLOCK.bit_length() - 1
assert 1 << _SEL_SHIFT == SEL_BLOCK


def _kvproj_kernel(a_ref, b_ref, cos_ref, sin_ref, cmp_ref, rest_ref, gate_ref, abf_ref, *, per_seq):
    g = NSA_KV_GROUPS
    a = a_ref[...].astype(BF16)
    abf_ref[...] = a
    y = _dot(a, b_ref[...])
    cos, sin = cos_ref[...], sin_ref[...]
    cmp_ref[...] = y[:, :_N_CMP_COLS]
    tm = y.shape[0]
    pos = (pl.program_id(0) % per_seq) * tm + lax.broadcasted_iota(jnp.int32, (tm, LANES), 0)
    lane = lax.broadcasted_iota(jnp.int32, (tm, LANES), 1)
    blk_onehot = jnp.where(lane == (pos >> _SEL_SHIFT), 1.0, 0.0).astype(rest_ref.dtype)
    for c in range(_N_REST_CHUNKS):
        t = y[:, _N_CMP_COLS + c * HEAD_DIM:_N_CMP_COLS + (c + 1) * HEAD_DIM]
        if (c % (2 * g)) < g:
            t = _rope(t, cos, sin)
        dst = 2 * c if c < g else g + c
        rest_ref[:, dst * HEAD_DIM:(dst + 1) * HEAD_DIM] = t.astype(rest_ref.dtype)
        if c < g:
            rest_ref[:, (dst + 1) * HEAD_DIM:(dst + 2) * HEAD_DIM] = blk_onehot
    gate_ref[...] = _sigmoid(y[:, KV_WIDTH:])


def _kv_proj(x, w_kvg, cos, sin, seq, *, tm=512):
    m, k = x.shape
    n = w_kvg.shape[1]
    per_seq = seq // tm
    return pl.pallas_call(
        functools.partial(_kvproj_kernel, per_seq=per_seq),
        grid=(m // tm,),
        in_specs=[pl.BlockSpec((tm, k), lambda i: (i, 0)),
                  pl.BlockSpec((k, n), lambda i: (0, 0)),
                  pl.BlockSpec((tm, HEAD_DIM), lambda i: (i % per_seq, 0)),
                  pl.BlockSpec((tm, HEAD_DIM), lambda i: (i % per_seq, 0))],
        out_specs=[pl.BlockSpec((tm, _N_CMP_COLS), lambda i: (i, 0)),
                   pl.BlockSpec((tm, _N_REST_COLS), lambda i: (i, 0)),
                   pl.BlockSpec((tm, _GATE_COLS), lambda i: (i, 0)),
                   pl.BlockSpec((tm, k), lambda i: (i, 0))],
        out_shape=[jax.ShapeDtypeStruct((m, _N_CMP_COLS), F32),
                   jax.ShapeDtypeStruct((m, _N_REST_COLS), BF16),
                   jax.ShapeDtypeStruct((m, _GATE_COLS), F32),
                   jax.ShapeDtypeStruct((m, k), BF16)],
        compiler_params=_params("parallel"),
        name="kv_gate_proj",
    )(x, w_kvg, cos, sin)


def _compress_kernel(kc_ref, pe_ref, w1_ref, w2_ref, cos_ref, sin_ref, o_ref):
    n_chunks = o_ref.shape[2]
    top = jnp.zeros((n_chunks, CMP_HIDDEN), F32)
    bot = jnp.zeros((n_chunks, CMP_HIDDEN), F32)
    for p in range(CMP_STRIDE):
        xp = kc_ref[0, pl.ds(p, n_chunks, stride=CMP_STRIDE), :]
        q = CMP_STRIDE + p
        top += _dot((xp + pe_ref[0, p:p + 1, :]).astype(BF16), w1_ref[0, p * HEAD_DIM:(p + 1) * HEAD_DIM, :])
        bot += _dot((xp + pe_ref[0, q:q + 1, :]).astype(BF16), w1_ref[0, q * HEAD_DIM:(q + 1) * HEAD_DIM, :])
    h = top + pltpu.roll(bot, n_chunks - 1, axis=0)
    y = _dot(_gelu(h).astype(BF16), w2_ref[0])
    o_ref[0, 0] = _rope(y, cos_ref[0], sin_ref[0]).astype(o_ref.dtype)


def _compress(kv_cmp, pe, w1, w2, cos, sin):
    bsz, seq, width = kv_cmp.shape
    four = width // HEAD_DIM
    n_chunks = seq // CMP_STRIDE
    g = NSA_KV_GROUPS
    return pl.pallas_call(
        _compress_kernel,
        grid=(four, bsz),
        in_specs=[pl.BlockSpec((1, seq, HEAD_DIM), lambda a, b: (b, 0, a)),
                  pl.BlockSpec((1, CMP_BLOCK, HEAD_DIM), lambda a, b: (a // g, 0, 0)),
                  pl.BlockSpec((1, CMP_BLOCK * HEAD_DIM, CMP_HIDDEN), lambda a, b: (a // g, 0, 0)),
                  pl.BlockSpec((1, CMP_HIDDEN, HEAD_DIM), lambda a, b: (a // g, 0, 0)),
                  pl.BlockSpec((1, n_chunks, HEAD_DIM), lambda a, b: (a // g, 0, 0)),
                  pl.BlockSpec((1, n_chunks, HEAD_DIM), lambda a, b: (a // g, 0, 0))],
        out_specs=pl.BlockSpec((1, 1, n_chunks, HEAD_DIM), lambda a, b: (a, b, 0, 0)),
        out_shape=jax.ShapeDtypeStruct((four, bsz, n_chunks, HEAD_DIM), BF16),
        compiler_params=_params("parallel", "parallel"),
        name="compress",
    )(kv_cmp, pe, w1, w2, cos, sin)


SEL_TILE = 512
SEL_WIDE = 2
NSA_TQ = 256
SUBLANES = 8


def _softmax_parts(s3, bias):
    s3 = s3 + bias[None]
    m = jnp.max(s3, -1, keepdims=True)
    m = jnp.where(m > 0.5 * MASKED, m, 0.0)
    p = jnp.exp2(s3 - m)
    return p, 1.0 / jnp.maximum(jnp.sum(p, -1, keepdims=True), 1e-30)


def _topk_unselected(score_t, n_sel, n_pick):
    tq = score_t.shape[1]
    n_grp = n_sel // SUBLANES
    grp = [score_t[g * SUBLANES:(g + 1) * SUBLANES] for g in range(n_grp)]
    row_in_grp = lax.broadcasted_iota(jnp.int32, (SUBLANES, tq), 0)
    later = [jnp.where(row_in_grp > li, 1.0, 0.0) for li in range(SUBLANES)]
    beaten = [jnp.zeros((SUBLANES, tq), F32) for _ in range(n_grp)]
    for i in range(n_sel):
        gi, li = divmod(i, SUBLANES)
        row = grp[gi][li:li + 1, :]
        for g in range(n_grp):
            if g > gi:
                beaten[g] = beaten[g] + jnp.where(row >= grp[g], 1.0, 0.0)
            elif g < gi:
                beaten[g] = beaten[g] + jnp.where(row > grp[g], 1.0, 0.0)
            else:
                beaten[g] = (beaten[g] + jnp.where(row > grp[g], 1.0, 0.0)
                             + jnp.where(row == grp[g], later[li], 0.0))
    return jnp.concatenate([jnp.where(b < n_pick, 0.0, MASKED) for b in beaten], axis=0)


def _nsa_kernel(q_ref, kc_ref, vc_ref, ks_ref, vs_ref, kw_ref, vw_ref, gate_ref, selmap_t_ref, o_ref,
                *, n_sel, n_pick):
    hpg = NSA_HPG
    tq = NSA_TQ
    c = pl.program_id(2)
    q_t = q_ref[0]
    q8 = jnp.concatenate([q_t[:, n * HEAD_DIM:(n + 1) * HEAD_DIM] for n in range(hpg)], axis=0)
    t_pos = c * tq + lax.broadcasted_iota(jnp.int32, (tq, 1), 0)
    t_lane = c * tq + lax.broadcasted_iota(jnp.int32, (1, tq), 1)

    kc = kc_ref[0, 0]
    n_cmp_pad = kc.shape[0]
    k_idx = lax.broadcasted_iota(jnp.int32, (1, n_cmp_pad), 1)
    vis_c = (CMP_STRIDE * k_idx + (CMP_BLOCK - 1) <= t_pos) & (k_idx < n_cmp_pad - 1)
    p_c, r_c = _softmax_parts(_dot_nt(q8, kc).reshape(hpg, tq, n_cmp_pad), jnp.where(vis_c, 0.0, MASKED))
    o_c = _dot(p_c.reshape(hpg * tq, n_cmp_pad).astype(BF16), vc_ref[0, 0]).reshape(hpg, tq, HEAD_DIM) * r_c

    p_sum = jnp.sum(p_c * r_c, axis=0)
    p_hi = p_sum.astype(BF16)
    p_lo = (p_sum - p_hi.astype(F32)).astype(BF16)
    selmap_t = selmap_t_ref[...]
    imp_t = _dot_nt(selmap_t, p_hi) + _dot_nt(selmap_t, p_lo)
    blk = lax.broadcasted_iota(jnp.int32, (LANES, tq), 0)
    cur = t_lane >> _SEL_SHIFT
    forced = (blk == 0) | (blk == cur) | (blk == cur - 1)
    score_t = jnp.where(forced, SEL_BIG, jnp.where(blk <= cur, imp_t, -SEL_BIG))
    unsel_t = _topk_unselected(score_t, n_sel, n_pick)
    unsel_t = jnp.concatenate([unsel_t, jnp.zeros((LANES - n_sel, tq), F32)], axis=0)
    unsel = unsel_t.T.astype(BF16)
    q_aug = jnp.concatenate([q8, jnp.concatenate([unsel] * hpg, axis=0)], axis=1)

    def sel_tile(kt, size, carry, bias):
        m_old, l_old, acc = carry
        base = pl.multiple_of(kt * size, size)
        k = ks_ref[0, pl.ds(base, size), :]
        v = vs_ref[0, pl.ds(base, size), :]
        s = _dot_nt(q_aug, k).reshape(hpg, tq, size)
        if bias is not None:
            s = s + bias[None]
        m_new = jnp.maximum(m_old, jnp.max(s, -1, keepdims=True))
        alpha = jnp.exp2(m_old - m_new)
        p = jnp.exp2(s - m_new)
        l_new = alpha * l_old + jnp.sum(p, -1, keepdims=True)
        pv = _dot(p.reshape(hpg * tq, size).astype(BF16), v).reshape(hpg, tq, HEAD_DIM)
        return m_new, l_new, alpha * acc + pv

    n_full = c // (SEL_TILE // tq)
    n_wide = n_full // SEL_WIDE
    carry = (jnp.full((hpg, tq, 1), M_INIT, F32), jnp.zeros((hpg, tq, 1), F32),
             jnp.zeros((hpg, tq, HEAD_DIM), F32))
    carry = lax.fori_loop(0, n_wide, lambda kt, cr: sel_tile(kt, SEL_WIDE * SEL_TILE, cr, None), carry)
    carry = lax.fori_loop(n_wide * SEL_WIDE, n_full, lambda kt, cr: sel_tile(kt, SEL_TILE, cr, None), carry)
    kp_last = n_full * SEL_TILE + lax.broadcasted_iota(jnp.int32, (1, SEL_TILE), 1)
    _, l_s, acc_s = sel_tile(n_full, SEL_TILE, carry, jnp.where(kp_last <= t_pos, 0.0, MASKED))
    o_s = acc_s * (1.0 / jnp.maximum(l_s, 1e-30))

    span = WINDOW + tq
    start = pl.multiple_of(jnp.maximum(c * tq - WINDOW, 0), tq)
    kw = kw_ref[0, pl.ds(start, span), :]
    vw = vw_ref[0, pl.ds(start, span), :]
    kp = start + lax.broadcasted_iota(jnp.int32, (1, span), 1)
    vis_w = (kp <= t_pos) & (t_pos - kp < WINDOW)
    p_w, r_w = _softmax_parts(_dot_nt(q8, kw).reshape(hpg, tq, span), jnp.where(vis_w, 0.0, MASKED))
    o_w = _dot(p_w.reshape(hpg * tq, span).astype(BF16), vw).reshape(hpg, tq, HEAD_DIM) * r_w

    gates = gate_ref[0]
    for n in range(hpg):
        out = (gates[:, n:n + 1] * o_c[n]
               + gates[:, hpg + n:hpg + n + 1] * o_s[n]
               + gates[:, 2 * hpg + n:2 * hpg + n + 1] * o_w[n])
        o_ref[0, :, n * HEAD_DIM:(n + 1) * HEAD_DIM] = out.astype(o_ref.dtype)


def _nsa(q, cmp_kv, rest, gates, selmap_t, *, n_sel, n_pick):
    bsz, seq, _ = q.shape
    g = NSA_KV_GROUPS
    n_cmp_pad = cmp_kv.shape[2]
    gw = NSA_HPG * HEAD_DIM
    assert n_sel % SUBLANES == 0 and n_sel <= LANES and seq >= WINDOW + NSA_TQ
    assert SEL_TILE % NSA_TQ == 0 and seq % SEL_TILE == 0
    kv_spec = lambda off: pl.BlockSpec((1, seq, HEAD_DIM), lambda b, gi, c: (b, 0, off + gi))
    return pl.pallas_call(
        functools.partial(_nsa_kernel, n_sel=n_sel, n_pick=n_pick),
        grid=(bsz, g, seq // NSA_TQ),
        in_specs=[pl.BlockSpec((1, NSA_TQ, gw), lambda b, gi, c: (b, c, gi)),
                  pl.BlockSpec((1, 1, n_cmp_pad, HEAD_DIM), lambda b, gi, c: (gi, b, 0, 0)),
                  pl.BlockSpec((1, 1, n_cmp_pad, HEAD_DIM), lambda b, gi, c: (g + gi, b, 0, 0)),
                  pl.BlockSpec((1, seq, 2 * HEAD_DIM), lambda b, gi, c: (b, 0, gi)),
                  kv_spec(2 * g), kv_spec(3 * g), kv_spec(4 * g),
                  pl.BlockSpec((1, NSA_TQ, LANES), lambda b, gi, c: (b, c, gi)),
                  pl.BlockSpec(selmap_t.shape, lambda b, gi, c: (0, 0))],
        out_specs=pl.BlockSpec((1, NSA_TQ, gw), lambda b, gi, c: (b, c, gi)),
        out_shape=jax.ShapeDtypeStruct(q.shape, BF16),
        compiler_params=_params("parallel", "parallel", "arbitrary"),
        name="nsa_attention",
    )(q, cmp_kv, cmp_kv, rest, rest, rest, rest, gates, selmap_t)


def _sgu_kernel(x_ref, w_ref, g_ref, b_ref, ws_ref, bs_ref, o_ref, *, chunks):
    gd = SGU_WIDTH // SGU_GROUPS
    row = lax.broadcasted_iota(jnp.int32, (SGU_CHUNK, SGU_CHUNK), 0)
    col = lax.broadcasted_iota(jnp.int32, (SGU_CHUNK, SGU_CHUNK), 1)
    causal = col <= row
    bs = bs_ref[...]
    x = x_ref[...]
    v = jnp.concatenate(
        [_gelu(_dot(x, w_ref[:, SGU_WIDTH + j * MXU_COLS:SGU_WIDTH + (j + 1) * MXU_COLS]))
         for j in range(SGU_WIDTH // MXU_COLS)], axis=1)
    vn = _layer_norm(v, g_ref[...], b_ref[...]).astype(BF16)
    for gi in range(SGU_GROUPS):
        cols = slice(gi * gd, (gi + 1) * gd)
        u = _gelu(_dot(x, w_ref[:, cols]))
        w = jnp.where(causal, ws_ref[gi], 0.0).astype(BF16)
        for ch in range(chunks):
            rows = slice(ch * SGU_CHUNK, (ch + 1) * SGU_CHUNK)
            z = _dot(w, vn[rows, cols]) + bs[:, gi:gi + 1]
            o_ref[rows, cols] = (u[rows] * z).astype(o_ref.dtype)


def _sgu(x_bf, w_uv, ln_g, ln_b, w_s, b_s_t, *, chunks=4):
    m, d = x_bf.shape
    tm = chunks * SGU_CHUNK
    return pl.pallas_call(
        functools.partial(_sgu_kernel, chunks=chunks),
        grid=(m // tm,),
        in_specs=[pl.BlockSpec((tm, d), lambda i: (i, 0)),
                  pl.BlockSpec(w_uv.shape, lambda i: (0, 0), pipeline_mode=pl.Buffered(1)),
                  pl.BlockSpec((1, SGU_WIDTH), lambda i: (0, 0)),
                  pl.BlockSpec((1, SGU_WIDTH), lambda i: (0, 0)),
                  pl.BlockSpec(w_s.shape, lambda i: (0, 0, 0)),
                  pl.BlockSpec(b_s_t.shape, lambda i: (0, 0))],
        out_specs=pl.BlockSpec((tm, SGU_WIDTH), lambda i: (i, 0)),
        out_shape=jax.ShapeDtypeStruct((m, SGU_WIDTH), BF16),
        compiler_params=_params("parallel"),
        name="sgu",
    )(x_bf, w_uv, ln_g, ln_b, w_s, b_s_t)


def _merge_kernel(x_ref, a_ref, s_ref, wga_ref, wgb_ref, wa_ref, wb_ref, o_ref):
    x = x_ref[...]
    ga = _sigmoid(_dot(x, wga_ref[...]))
    gb = _sigmoid(_dot(x, wgb_ref[...]))
    o_ref[...] = (ga * _dot(a_ref[...], wa_ref[...]) + gb * _dot(s_ref[...], wb_ref[...])).astype(o_ref.dtype)


def _merge(x_bf, o_nsa, o_sgu, w_merge, w_a, w_b, *, tm=1024, tn=512):
    m, k = x_bf.shape
    n = w_a.shape[1]
    nb = n // tn
    row = pl.BlockSpec((tm, k), lambda i, j: (i, 0))
    col = pl.BlockSpec((k, tn), lambda i, j: (0, j))
    return pl.pallas_call(
        _merge_kernel,
        grid=(m // tm, nb),
        in_specs=[row, row, row, col, pl.BlockSpec((k, tn), lambda i, j: (0, nb + j)), col, col],
        out_specs=pl.BlockSpec((tm, tn), lambda i, j: (i, j)),
        out_shape=jax.ShapeDtypeStruct((m, n), BF16),
        compiler_params=_params("parallel", "parallel"),
        name="merge",
    )(x_bf, o_nsa, o_sgu, w_merge, w_merge, w_a, w_b)


def _proj_ln_kernel(a_ref, w_ref, x_ref, g_ref, b_ref, o_ref, obf_ref):
    y = DN_ALPHA * x_ref[...] + _dot(a_ref[...], w_ref[...])
    out = _layer_norm(y, g_ref[...], b_ref[...])
    o_ref[...] = out
    obf_ref[...] = out.astype(BF16)


def _proj_ln(a_bf, w, x, ln_g, ln_b, *, tm=512):
    m, k = a_bf.shape
    n = w.shape[1]
    vec = pl.BlockSpec((1, n), lambda i: (0, 0))
    return pl.pallas_call(
        _proj_ln_kernel,
        grid=(m // tm,),
        in_specs=[pl.BlockSpec((tm, k), lambda i: (i, 0)),
                  pl.BlockSpec((k, n), lambda i: (0, 0)),
                  pl.BlockSpec((tm, n), lambda i: (i, 0)), vec, vec],
        out_specs=[pl.BlockSpec((tm, n), lambda i: (i, 0)), pl.BlockSpec((tm, n), lambda i: (i, 0))],
        out_shape=[jax.ShapeDtypeStruct((m, n), F32), jax.ShapeDtypeStruct((m, n), BF16)],
        compiler_params=_params("parallel"),
        name="proj_ln",
    )(a_bf, w, x, ln_g, ln_b)


def _xattn_kernel(xbf_ref, x_ref, wq_ref, kv_ref, wo_ref, g_ref, b_ref, o_ref):
    q = (_dot(xbf_ref[0], wq_ref[...]) * HEAD_DIM ** -0.5).astype(BF16)
    kv = kv_ref[0]
    heads = []
    for h in range(MEM_HEADS):
        cols = slice(h * HEAD_DIM, (h + 1) * HEAD_DIM)
        s = _dot_nt(q[:, cols], kv[:, cols])
        p = jnp.exp(s - jnp.max(s, -1, keepdims=True))
        p = p / jnp.sum(p, -1, keepdims=True)
        heads.append(_dot(p.astype(BF16), kv[:, MEM_WIDTH + h * HEAD_DIM:MEM_WIDTH + (h + 1) * HEAD_DIM]))
    o = jnp.concatenate(heads, axis=1).astype(BF16)
    y = DN_ALPHA * x_ref[0] + _dot(o, wo_ref[...])
    o_ref[0] = _layer_norm(y, g_ref[...], b_ref[...])


def _xattn(x_bf, x, w_xq, mem_kv, w_xo, ln_g, ln_b, *, tm=512):
    bsz, seq, d = x.shape
    mem_len = mem_kv.shape[1]
    vec = pl.BlockSpec((1, d), lambda b, i: (0, 0))
    return pl.pallas_call(
        _xattn_kernel,
        grid=(bsz, seq // tm),
        in_specs=[pl.BlockSpec((1, tm, d), lambda b, i: (b, i, 0)),
                  pl.BlockSpec((1, tm, d), lambda b, i: (b, i, 0)),
                  pl.BlockSpec(w_xq.shape, lambda b, i: (0, 0)),
                  pl.BlockSpec((1, mem_len, 2 * MEM_WIDTH), lambda b, i: (b, 0, 0)),
                  pl.BlockSpec(w_xo.shape, lambda b, i: (0, 0)), vec, vec],
        out_specs=pl.BlockSpec((1, tm, d), lambda b, i: (b, i, 0)),
        out_shape=jax.ShapeDtypeStruct(x.shape, F32),
        compiler_params=_params("parallel", "parallel"),
        name="xattn_ln",
    )(x_bf, x, w_xq, mem_kv, w_xo, ln_g, ln_b)


def _router_kernel(x_ref, whi_ref, wlo_ref, b_ref, gate_ref, exp_ref, rank_ref, cnt_ref, seen_ref):
    @pl.when(pl.program_id(0) == 0)
    def _():
        seen_ref[...] = jnp.zeros_like(seen_ref)

    x = x_ref[...]
    x_hi = x.astype(BF16)
    x_lo = (x - x_hi.astype(F32)).astype(BF16)
    whi = whi_ref[...]
    logits = _dot(x_hi, whi) + _dot(x_hi, wlo_ref[...]) + _dot(x_lo, whi) + b_ref[...]
    rows = logits.shape[0]
    lane = lax.broadcasted_iota(jnp.int32, (rows, LANES), 1)
    lane_f = lane.astype(F32)
    is_grp = lane < N_GROUPS
    lg = jnp.where(is_grp, logits, MASKED)
    eg = jnp.where(is_grp, jnp.exp(lg - jnp.max(lg, -1, keepdims=True)), 0.0)
    p_grp = eg / jnp.sum(eg, -1, keepdims=True)
    g_grp = jnp.max(p_grp, -1, keepdims=True)
    grp = jnp.min(jnp.where((p_grp == g_grp) & is_grp, lane_f, 1e9), -1, keepdims=True).astype(jnp.int32)
    lo = N_GROUPS + grp * EXPERTS_PER_GROUP
    in_grp = (lane >= lo) & (lane < lo + EXPERTS_PER_GROUP)
    le = jnp.where(in_grp, logits, MASKED)
    ee = jnp.where(in_grp, jnp.exp(le - jnp.max(le, -1, keepdims=True)), 0.0)
    p_exp = jnp.where(in_grp, ee / jnp.sum(ee, -1, keepdims=True), -1.0)
    v1 = jnp.max(p_exp, -1, keepdims=True)
    i1 = jnp.min(jnp.where(p_exp == v1, lane_f, 1e9), -1, keepdims=True)
    p_rest = jnp.where(lane_f == i1, -1.0, p_exp)
    v2 = jnp.max(p_rest, -1, keepdims=True)
    i2 = jnp.min(jnp.where(p_rest == v2, lane_f, 1e9), -1, keepdims=True)
    denom = v1 + v2
    gate_ref[...] = jnp.where(lane == 0, g_grp * v1 / denom, jnp.where(lane == 1, g_grp * v2 / denom, 0.0))
    exp_ref[...] = jnp.where(lane == 0, i1, jnp.where(lane == 1, i2, float(N_GROUPS))).astype(jnp.int32) - N_GROUPS

    onehot = jnp.where((lane_f == i1) | (lane_f == i2), 1.0, 0.0)
    tok_r = lax.broadcasted_iota(jnp.int32, (rows, rows), 0)
    tok_c = lax.broadcasted_iota(jnp.int32, (rows, rows), 1)
    earlier = jnp.where(tok_c < tok_r, 1.0, 0.0).astype(BF16)
    before = _dot(earlier, onehot.astype(BF16)) + seen_ref[...]
    r1 = jnp.sum(jnp.where(lane_f == i1, before, 0.0), -1, keepdims=True)
    r2 = jnp.sum(jnp.where(lane_f == i2, before, 0.0), -1, keepdims=True)
    rank_ref[...] = jnp.where(lane == 0, r1, jnp.where(lane == 1, r2, 0.0)).astype(jnp.int32)
    seen_ref[...] += jnp.sum(onehot, axis=0, keepdims=True)
    cnt_ref[...] = seen_ref[...]


def _router(x, w_hi, w_lo, bias, *, tm=1024):
    m, d = x.shape
    wspec = pl.BlockSpec((d, LANES), lambda i: (0, 0))
    vec = pl.BlockSpec((1, LANES), lambda i: (0, 0))
    tile = pl.BlockSpec((tm, LANES), lambda i: (i, 0))
    return pl.pallas_call(
        _router_kernel,
        grid=(m // tm,),
        in_specs=[pl.BlockSpec((tm, d), lambda i: (i, 0)), wspec, wspec, vec],
        out_specs=[tile, tile, tile, vec],
        out_shape=[jax.ShapeDtypeStruct((m, LANES), F32), jax.ShapeDtypeStruct((m, LANES), jnp.int32),
                   jax.ShapeDtypeStruct((m, LANES), jnp.int32), jax.ShapeDtypeStruct((1, LANES), F32)],
        scratch_shapes=[pltpu.VMEM((1, LANES), F32)],
        compiler_params=_params("arbitrary"),
        name="router",
    )(x, w_hi, w_lo, bias)


GATHER_UNROLL = 8


def _ffn_kernel(exp_ref, on_ref, first_ref, slot_ref, next_ref, tok_ref, dst_ref, x_hbm, wg_hbm, wu_hbm, wd_hbm,
                y_hbm, wg_buf, wu_buf, wd_buf, h_buf, y_buf, sem, row_sem, out_sem):
    i = pl.program_id(0)
    n_blk = pl.num_programs(0)
    slot = slot_ref[i]
    rows = MOE_BLOCK

    def out_copy(blk, g, u):
        dst = dst_ref[blk * rows + g * GATHER_UNROLL + u]
        return pltpu.make_async_copy(y_buf.at[blk % 2, g, pl.ds(u, 1), :], y_hbm.at[pl.ds(dst, 1), :],
                                     out_sem.at[blk % 2])

    def weight_copies(e, s):
        return (pltpu.make_async_copy(wg_hbm.at[e], wg_buf.at[s], sem.at[s, 0]),
                pltpu.make_async_copy(wu_hbm.at[e], wu_buf.at[s], sem.at[s, 1]),
                pltpu.make_async_copy(wd_hbm.at[e], wd_buf.at[s], sem.at[s, 2]))

    def row_copy(blk, g, u):
        tok = tok_ref[blk * rows + g * GATHER_UNROLL + u]
        return pltpu.make_async_copy(x_hbm.at[pl.ds(tok, 1), :], h_buf.at[blk % 2, g, pl.ds(u, 1), :],
                                     row_sem.at[blk % 2])

    def for_rows(blk, fn):
        def body(g, carry):
            for u in range(GATHER_UNROLL):
                fn(blk, g, u)
            return carry
        lax.fori_loop(0, on_ref[blk], body, 0)

    @pl.when(i == 0)
    def _():
        for cp in weight_copies(exp_ref[0], 0):
            cp.start()
        h_buf[...] = jnp.zeros_like(h_buf)
        for_rows(0, lambda b, g, u: row_copy(b, g, u).start())
        y_buf[...] = jnp.zeros_like(y_buf)
        spare = y_hbm.shape[0] - 2 * rows
        fills = [pltpu.make_async_copy(y_buf.at[s, g],
                                       y_hbm.at[pl.ds(spare + s * rows + g * GATHER_UNROLL, GATHER_UNROLL), :],
                                       out_sem.at[s])
                 for s in range(2) for g in range(rows // GATHER_UNROLL)]
        for cp in fills:
            cp.start()
        for cp in fills:
            cp.wait()

    prev2 = jnp.maximum(i - 2, 0)

    @pl.when((i >= 2) & (on_ref[prev2] != 0))
    def _():
        for_rows(i - 2, lambda b, g, u: out_copy(b, g, u).wait())

    nxt_blk = jnp.minimum(i + 1, n_blk - 1)

    @pl.when((i + 1 < n_blk) & (on_ref[nxt_blk] != 0))
    def _():
        for_rows(i + 1, lambda b, g, u: row_copy(b, g, u).start())

    @pl.when(first_ref[i] != 0)
    def _():
        nxt = next_ref[i]

        @pl.when(nxt >= 0)
        def _():
            for cp in weight_copies(nxt, 1 - slot):
                cp.start()

        for cp in weight_copies(exp_ref[i], slot):
            cp.wait()

    @pl.when(on_ref[i] != 0)
    def _():
        for_rows(i, lambda b, g, u: row_copy(b, g, u).wait())
        h = h_buf[i % 2].reshape(rows, h_buf.shape[-1]).astype(BF16)
        a = _dot(h, wg_buf[slot].astype(BF16))
        a = a * _sigmoid(a) * _dot(h, wu_buf[slot].astype(BF16))
        y_buf[i % 2] = _dot(a.astype(BF16), wd_buf[slot].astype(BF16)).reshape(y_buf.shape[1:])
        for_rows(i, lambda b, g, u: out_copy(b, g, u).start())

    @pl.when(i == n_blk - 1)
    def _():
        prev1 = jnp.maximum(i - 1, 0)

        @pl.when((i >= 1) & (on_ref[prev1] != 0))
        def _():
            for_rows(i - 1, lambda b, g, u: out_copy(b, g, u).wait())

        @pl.when(on_ref[i] != 0)
        def _():
            for_rows(i, lambda b, g, u: out_copy(b, g, u).wait())


def _expert_ffn(blk_exp, blk_on, row_tok, row_dst, x, w_gate, w_up, w_down):
    p_rows = row_tok.shape[0]
    n_tok, d = x.shape
    n_exp, _, ff = w_gate.shape
    n_blocks = p_rows // MOE_BLOCK
    prev_exp = jnp.concatenate([jnp.full((1,), -1, jnp.int32), blk_exp[:-1]])
    first = ((blk_on != 0) & (blk_exp != prev_exp)).astype(jnp.int32)
    seg_id = jnp.cumsum(first) - 1
    seg_ids = jnp.arange(n_exp + 1)
    hit = (first[:, None] != 0) & (seg_id[:, None] == seg_ids[None, :])
    seg_exp = jnp.sum(jnp.where(hit, blk_exp[:, None] + 1, 0), 0) - 1
    blk_next = jnp.sum(jnp.where(seg_ids[None, :] == seg_id[:, None] + 1, seg_exp[None, :] + 1, 0), 1) - 1
    blk_slot = jnp.maximum(seg_id, 0) % 2
    grid_spec = pltpu.PrefetchScalarGridSpec(
        num_scalar_prefetch=7,
        grid=(n_blocks,),
        in_specs=[pl.BlockSpec(memory_space=pl.ANY)] * 4,
        out_specs=pl.BlockSpec(memory_space=pl.ANY),
        scratch_shapes=[pltpu.VMEM((2, d, ff), F32), pltpu.VMEM((2, d, ff), F32), pltpu.VMEM((2, ff, d), F32),
                        pltpu.VMEM((2, MOE_BLOCK // GATHER_UNROLL, GATHER_UNROLL, d), F32),
                        pltpu.VMEM((2, MOE_BLOCK // GATHER_UNROLL, GATHER_UNROLL, d), F32),
                        pltpu.SemaphoreType.DMA((2, 3)), pltpu.SemaphoreType.DMA((2,)),
                        pltpu.SemaphoreType.DMA((2,))],
    )
    return pl.pallas_call(
        _ffn_kernel,
        grid_spec=grid_spec,
        out_shape=jax.ShapeDtypeStruct((EXPERT_TOPK * n_tok + 2 * MOE_BLOCK, d), F32),
        compiler_params=_params("arbitrary"),
        name="expert_ffn",
    )(blk_exp, blk_on, first, blk_slot.astype(jnp.int32), blk_next.astype(jnp.int32), row_tok, row_dst,
      x, w_gate, w_up, w_down)


def _combine_ln_kernel(x_ref, y0_ref, y1_ref, gate_ref, g_ref, b_ref, o_ref):
    gates = gate_ref[...]
    moe = y0_ref[...] * gates[:, 0:1] + y1_ref[...] * gates[:, 1:2]
    o_ref[...] = _layer_norm(DN_ALPHA * x_ref[...] + moe, g_ref[...], b_ref[...])


def _combine_ln(x, y_tok, gates, ln_g, ln_b, *, tm=512):
    m, d = x.shape
    row = pl.BlockSpec((tm, d), lambda i: (i, 0))
    vec = pl.BlockSpec((1, d), lambda i: (0, 0))
    return pl.pallas_call(
        _combine_ln_kernel,
        grid=(m // tm,),
        in_specs=[row, row, pl.BlockSpec((tm, d), lambda i: (m // tm + i, 0)),
                  pl.BlockSpec((tm, LANES), lambda i: (i, 0)), vec, vec],
        out_specs=row,
        out_shape=jax.ShapeDtypeStruct((m, d), F32),
        compiler_params=_params("parallel"),
        name="combine_ln",
    )(x, y_tok, y_tok, gates, ln_g, ln_b)


def _layer(x, mem, w_in, cmp_pe_k, cmp_w1_k, cmp_w2_k, cmp_pe_v, cmp_w1_v, cmp_w2_v,
           sgu_ln_g, sgu_ln_b, sgu_w_s, sgu_b_s, w_branch_a, w_branch_b, w_o, ln1_g, ln1_b,
           w_xq, w_xk, w_xv, w_xo, ln2_g, ln2_b,
           w_router_grp, b_router_grp, w_router_exp, b_router_exp,
           w_exp_gate, w_exp_up, w_exp_down, ln3_g, ln3_b):
    bsz, seq, d = x.shape
    n_tok = bsz * seq
    g = NSA_KV_GROUPS
    hpg = NSA_HPG
    xf = x.reshape(n_tok, d)

    c0 = Q_WIDTH
    c1 = c0 + KV_WIDTH
    c2 = c1 + NSA_GATE_WIDTH
    c3 = c2 + 2 * SGU_WIDTH
    w_q = w_in[:, :c0].astype(BF16)
    w_gate = w_in[:, c1:c2].reshape(d, g, hpg, 3).transpose(0, 1, 3, 2).reshape(d, g, 3 * hpg)
    w_gate = jnp.pad(w_gate, ((0, 0), (0, 0), (0, LANES - 3 * hpg))).reshape(d, g * LANES)
    w_kvg = jnp.concatenate([w_in[:, c0:c1], w_gate], axis=1).astype(BF16)
    w_uv = w_in[:, c2:c3].astype(BF16)
    w_merge = w_in[:, c3:].astype(BF16)

    cos, sin = _rope_tables(jnp.arange(seq, dtype=jnp.int32))
    kv_cmp, kv_rest, nsa_gates, x_bf = _kv_proj(xf, w_kvg, cos, sin, seq)
    q = _q_proj(x_bf, w_q, cos, sin, seq).reshape(bsz, seq, Q_WIDTH)

    n_chunks = seq // CMP_STRIDE
    pe = jnp.stack([cmp_pe_k, cmp_pe_v])
    w1 = jnp.stack([cmp_w1_k, cmp_w1_v]).astype(BF16)
    w2 = jnp.stack([cmp_w2_k, cmp_w2_v]).astype(BF16)
    cmp_end = CMP_STRIDE * jnp.arange(n_chunks, dtype=jnp.int32) + CMP_BLOCK - 1
    ccos, csin = _rope_tables(cmp_end)
    cmp_cos = jnp.stack([ccos, jnp.ones_like(ccos)])
    cmp_sin = jnp.stack([csin, jnp.zeros_like(csin)])
    cmp_kv = _compress(kv_cmp.reshape(bsz, seq, _N_CMP_COLS), pe, w1, w2, cmp_cos, cmp_sin)

    n_sel = seq // SEL_BLOCK
    n_pick = min(SEL_TOPK, n_sel)
    ci = CMP_STRIDE * jnp.arange(n_chunks)
    sj = SEL_BLOCK * jnp.arange(LANES)
    overlap = jnp.clip(jnp.minimum(ci[:, None] + CMP_BLOCK, sj[None, :] + SEL_BLOCK)
                       - jnp.maximum(ci[:, None], sj[None, :]), 0, None)
    selmap_t = (overlap.astype(F32) / CMP_BLOCK).astype(BF16).T
    o_nsa = _nsa(q, cmp_kv, kv_rest.reshape(bsz, seq, _N_REST_COLS),
                 nsa_gates.reshape(bsz, seq, _GATE_COLS), selmap_t, n_sel=n_sel, n_pick=n_pick)

    o_sgu = _sgu(x_bf, w_uv, sgu_ln_g.reshape(1, -1), sgu_ln_b.reshape(1, -1), sgu_w_s, sgu_b_s.T)

    merged = _merge(x_bf, o_nsa.reshape(n_tok, Q_WIDTH), o_sgu, w_merge,
                    w_branch_a.astype(BF16), w_branch_b.astype(BF16))
    x1, x1_bf = _proj_ln(merged, w_o.astype(BF16), xf, ln1_g.reshape(1, -1), ln1_b.reshape(1, -1))

    mem_len = mem.shape[1]
    w_mem = jnp.concatenate([w_xk, w_xv], axis=1).astype(BF16)
    mem_kv = _matmul(mem.reshape(bsz * mem_len, d).astype(BF16), w_mem, tm=256, tn=512, out_dtype=BF16)
    x2 = _xattn(x1_bf.reshape(bsz, seq, d), x1.reshape(bsz, seq, d), w_xq.astype(BF16),
                mem_kv.reshape(bsz, mem_len, 2 * MEM_WIDTH), w_xo.astype(BF16),
                ln2_g.reshape(1, -1), ln2_b.reshape(1, -1)).reshape(n_tok, d)

    w_r = jnp.concatenate([w_router_grp, w_router_exp], axis=1)
    w_r = jnp.pad(w_r, ((0, 0), (0, LANES - w_r.shape[1])))
    w_r_hi = w_r.astype(BF16)
    w_r_lo = (w_r - w_r_hi.astype(F32)).astype(BF16)
    b_r = jnp.concatenate([b_router_grp, b_router_exp])
    b_r = jnp.pad(b_r, (0, LANES - b_r.shape[0])).reshape(1, LANES)
    gates, experts, ranks, seen = _router(x2, w_r_hi, w_r_lo, b_r)

    nk = n_tok * EXPERT_TOPK
    counts = seen[0, N_GROUPS:N_GROUPS + N_EXPERTS].astype(jnp.int32)
    pcounts = ((counts + MOE_BLOCK - 1) // MOE_BLOCK) * MOE_BLOCK
    pend = jnp.cumsum(pcounts)
    pstart = pend - pcounts
    e_tok = experts[:, :EXPERT_TOPK]
    seg_start = jnp.sum(jnp.where(e_tok[..., None] == jnp.arange(N_EXPERTS), pstart, 0), -1)
    dest_of = seg_start + ranks[:, :EXPERT_TOPK]
    n_blocks = -(-nk // MOE_BLOCK) + N_EXPERTS
    p_rows = n_blocks * MOE_BLOCK
    row_ids = jnp.arange(p_rows, dtype=jnp.int32)
    row_asg = jnp.full((p_rows,), -1, jnp.int32).at[dest_of.reshape(nk)].set(
        jnp.arange(nk, dtype=jnp.int32), unique_indices=True)
    is_real = row_asg >= 0
    row_tok = jnp.where(is_real, row_asg // EXPERT_TOPK, row_ids % n_tok)
    spare = EXPERT_TOPK * n_tok + ((row_ids // MOE_BLOCK) % 2) * MOE_BLOCK + row_ids % MOE_BLOCK
    row_dst = jnp.where(is_real, (row_asg % EXPERT_TOPK) * n_tok + row_asg // EXPERT_TOPK, spare)
    blk_start = jnp.arange(n_blocks) * MOE_BLOCK
    blk_on = (blk_start < pend[-1]).astype(jnp.int32)
    blk_probe = jnp.minimum(blk_start, pend[-1] - 1)
    blk_exp = jnp.minimum(jnp.sum((pend[None, :] <= blk_probe[:, None]).astype(jnp.int32), -1),
                          N_EXPERTS - 1).astype(jnp.int32)
    real_end = jnp.sum(jnp.where(blk_exp[:, None] == jnp.arange(N_EXPERTS), (pstart + counts)[None, :], 0), -1)
    blk_real = jnp.clip(real_end - blk_start, 0, MOE_BLOCK) * blk_on
    blk_groups = ((blk_real + GATHER_UNROLL - 1) // GATHER_UNROLL).astype(jnp.int32)

    y_tok = _expert_ffn(blk_exp, blk_groups, row_tok, row_dst, x2, w_exp_gate, w_exp_up, w_exp_down)
    out = _combine_ln(x2, y_tok, gates, ln3_g.reshape(1, -1), ln3_b.reshape(1, -1))
    return out.reshape(bsz, seq, d)


def kernel(x, mem, w_in, cmp_pe_k, cmp_w1_k, cmp_w2_k, cmp_pe_v, cmp_w1_v, cmp_w2_v, sgu_ln_g, sgu_ln_b, sgu_w_s, sgu_b_s, w_branch_a, w_branch_b, w_o, ln1_g, ln1_b, w_xq, w_xk, w_xv, w_xo, ln2_g, ln2_b, w_router_grp, b_router_grp, w_router_exp, b_router_exp, w_exp_gate, w_exp_up, w_exp_down, ln3_g, ln3_b):
    params = (w_in, cmp_pe_k, cmp_w1_k, cmp_w2_k, cmp_pe_v, cmp_w1_v, cmp_w2_v,
              sgu_ln_g, sgu_ln_b, sgu_w_s, sgu_b_s, w_branch_a, w_branch_b, w_o, ln1_g, ln1_b,
              w_xq, w_xk, w_xv, w_xo, ln2_g, ln2_b,
              w_router_grp, b_router_grp, w_router_exp, b_router_exp,
              w_exp_gate, w_exp_up, w_exp_down, ln3_g, ln3_b)
    h = x
    for l in range(DEPTH):
        h = _layer(h, mem, *[p[l] for p in params])
    return h
```

```python
import functools
import math

import jax
import jax.numpy as jnp
from jax import lax
from jax.experimental import pallas as pl
from jax.experimental.pallas import tpu as pltpu

HEAD_DIM = 128
NSA_HEADS = 16
NSA_KV_GROUPS = 2
NSA_HPG = NSA_HEADS // NSA_KV_GROUPS
CMP_BLOCK = 32
CMP_STRIDE = 16
CMP_HIDDEN = 256
SEL_BLOCK = 64
SEL_TOPK = 16
WINDOW = 512
SGU_WIDTH = 2048
SGU_GROUPS = 8
SGU_CHUNK = 128
MEM_HEADS = 4
N_GROUPS = 4
EXPERTS_PER_GROUP = 16
N_EXPERTS = N_GROUPS * EXPERTS_PER_GROUP
EXPERT_TOPK = 2
MOE_BLOCK = 128
ROPE_THETA = 10000.0
LN_EPS = 1e-5
SEL_BIG = 1e9
DEPTH = 1
DN_ALPHA = (2.0 * DEPTH) ** 0.25

Q_WIDTH = NSA_HEADS * HEAD_DIM
KV_WIDTH = 3 * 2 * NSA_KV_GROUPS * HEAD_DIM
NSA_GATE_WIDTH = 3 * NSA_HEADS
MEM_WIDTH = MEM_HEADS * HEAD_DIM

LANES = 128
VMEM_LIMIT = 56 * 1024 * 1024
MASKED = -1e30
M_INIT = -1e20
BF16 = jnp.bfloat16
F32 = jnp.float32


def _params(*sem):
    return pltpu.CompilerParams(dimension_semantics=sem, vmem_limit_bytes=VMEM_LIMIT)


def _dot(a, b):
    return jnp.dot(a, b, preferred_element_type=F32)


def _dot_nt(a, b):
    return lax.dot_general(a, b, (((1,), (1,)), ((), ())), preferred_element_type=F32)


def _gelu(x):
    return 0.5 * x * (1.0 + jnp.tanh(math.sqrt(2.0 / math.pi) * (x + 0.044715 * (x * x * x))))


def _sigmoid(x):
    return 1.0 / (1.0 + jnp.exp(-x))


def _layer_norm(x, g, b):
    mu = jnp.mean(x, -1, keepdims=True)
    xc = x - mu
    var = jnp.mean(xc * xc, -1, keepdims=True)
    return xc * lax.rsqrt(var + LN_EPS) * g + b


def _rope(t, cos_full, sin_signed):
    return t * cos_full + pltpu.roll(t, HEAD_DIM // 2, axis=1) * sin_signed


def _rope_tables(pos):
    inv = ROPE_THETA ** (-jnp.arange(0, HEAD_DIM, 2, dtype=F32) / HEAD_DIM)
    ang = pos.astype(F32)[:, None] * inv[None, :]
    c, s = jnp.cos(ang), jnp.sin(ang)
    return jnp.concatenate([c, c], -1), jnp.concatenate([-s, s], -1)


MXU_COLS = 256


def _mm_kernel(a_ref, b_ref, o_ref):
    o_ref[...] = _dot(a_ref[...], b_ref[...]).astype(o_ref.dtype)


def _matmul(a, b, *, tm, tn, out_dtype):
    m, k = a.shape
    n = b.shape[1]
    return pl.pallas_call(
        _mm_kernel,
        grid=(m // tm, n // tn),
        in_specs=[pl.BlockSpec((tm, k), lambda i, j: (i, 0)),
                  pl.BlockSpec((k, tn), lambda i, j: (0, j))],
        out_specs=pl.BlockSpec((tm, tn), lambda i, j: (i, j)),
        out_shape=jax.ShapeDtypeStruct((m, n), out_dtype),
        compiler_params=_params("parallel", "parallel"),
        name="matmul",
    )(a, b)


def _qproj_kernel(a_ref, b_ref, cos_ref, sin_ref, o_ref):
    a = a_ref[...]
    cos, sin = cos_ref[...], sin_ref[...]
    scale = HEAD_DIM ** -0.5 * math.log2(math.e)
    for j in range(o_ref.shape[1] // MXU_COLS):
        y = _dot(a, b_ref[:, j * MXU_COLS:(j + 1) * MXU_COLS])
        for h in range(MXU_COLS // HEAD_DIM):
            sl = slice(j * MXU_COLS + h * HEAD_DIM, j * MXU_COLS + (h + 1) * HEAD_DIM)
            o_ref[:, sl] = (_rope(y[:, h * HEAD_DIM:(h + 1) * HEAD_DIM], cos, sin) * scale).astype(o_ref.dtype)


def _q_proj(x_bf, w_q, cos, sin, seq, *, tm=1024, tn=1024):
    m, k = x_bf.shape
    n = w_q.shape[1]
    per_seq = seq // tm
    return pl.pallas_call(
        _qproj_kernel,
        grid=(m // tm, n // tn),
        in_specs=[pl.BlockSpec((tm, k), lambda i, j: (i, 0)),
                  pl.BlockSpec((k, tn), lambda i, j: (0, j)),
                  pl.BlockSpec((tm, HEAD_DIM), lambda i, j: (i % per_seq, 0)),
                  pl.BlockSpec((tm, HEAD_DIM), lambda i, j: (i % per_seq, 0))],
        out_specs=pl.BlockSpec((tm, tn), lambda i, j: (i, j)),
        out_shape=jax.ShapeDtypeStruct((m, n), BF16),
        compiler_params=_params("parallel", "parallel"),
        name="q_proj",
    )(x_bf, w_q, cos, sin)


_N_CMP_COLS = 2 * NSA_KV_GROUPS * HEAD_DIM
_N_REST_CHUNKS = (KV_WIDTH - _N_CMP_COLS) // HEAD_DIM
_N_REST_COLS = (_N_REST_CHUNKS + NSA_KV_GROUPS) * HEAD_DIM
_GATE_COLS = NSA_KV_GROUPS * LANES
_SEL_SHIFT = SEL_BLOCK.bit_length() - 1
assert 1 << _SEL_SHIFT == SEL_BLOCK


def _kvproj_kernel(a_ref, b_ref, cos_ref, sin_ref, cmp_ref, rest_ref, gate_ref, abf_ref, *, per_seq):
    g = NSA_KV_GROUPS
    a = a_ref[...].astype(BF16)
    abf_ref[...] = a
    y = _dot(a, b_ref[...])
    cos, sin = cos_ref[...], sin_ref[...]
    cmp_ref[...] = y[:, :_N_CMP_COLS]
    tm = y.shape[0]
    pos = (pl.program_id(0) % per_seq) * tm + lax.broadcasted_iota(jnp.int32, (tm, LANES), 0)
    lane = lax.broadcasted_iota(jnp.int32, (tm, LANES), 1)
    blk_onehot = jnp.where(lane == (pos >> _SEL_SHIFT), 1.0, 0.0).astype(rest_ref.dtype)
    for c in range(_N_REST_CHUNKS):
        t = y[:, _N_CMP_COLS + c * HEAD_DIM:_N_CMP_COLS + (c + 1) * HEAD_DIM]
        if (c % (2 * g)) < g:
            t = _rope(t, cos, sin)
        dst = 2 * c if c < g else g + c
        rest_ref[:, dst * HEAD_DIM:(dst + 1) * HEAD_DIM] = t.astype(rest_ref.dtype)
        if c < g:
            rest_ref[:, (dst + 1) * HEAD_DIM:(dst + 2) * HEAD_DIM] = blk_onehot
    gate_ref[...] = _sigmoid(y[:, KV_WIDTH:])


def _kv_proj(x, w_kvg, cos, sin, seq, *, tm=512):
    m, k = x.shape
    n = w_kvg.shape[1]
    per_seq = seq // tm
    return pl.pallas_call(
        functools.partial(_kvproj_kernel, per_seq=per_seq),
        grid=(m // tm,),
        in_specs=[pl.BlockSpec((tm, k), lambda i: (i, 0)),
                  pl.BlockSpec((k, n), lambda i: (0, 0)),
                  pl.BlockSpec((tm, HEAD_DIM), lambda i: (i % per_seq, 0)),
                  pl.BlockSpec((tm, HEAD_DIM), lambda i: (i % per_seq, 0))],
        out_specs=[pl.BlockSpec((tm, _N_CMP_COLS), lambda i: (i, 0)),
                   pl.BlockSpec((tm, _N_REST_COLS), lambda i: (i, 0)),
                   pl.BlockSpec((tm, _GATE_COLS), lambda i: (i, 0)),
                   pl.BlockSpec((tm, k), lambda i: (i, 0))],
        out_shape=[jax.ShapeDtypeStruct((m, _N_CMP_COLS), F32),
                   jax.ShapeDtypeStruct((m, _N_REST_COLS), BF16),
                   jax.ShapeDtypeStruct((m, _GATE_COLS), F32),
                   jax.ShapeDtypeStruct((m, k), BF16)],
        compiler_params=_params("parallel"),
        name="kv_gate_proj",
    )(x, w_kvg, cos, sin)


def _compress_kernel(kc_ref, pe_ref, w1_ref, w2_ref, cos_ref, sin_ref, o_ref):
    n_chunks = o_ref.shape[2]
    top = jnp.zeros((n_chunks, CMP_HIDDEN), F32)
    bot = jnp.zeros((n_chunks, CMP_HIDDEN), F32)
    for p in range(CMP_STRIDE):
        xp = kc_ref[0, pl.ds(p, n_chunks, stride=CMP_STRIDE), :]
        q = CMP_STRIDE + p
        top += _dot((xp + pe_ref[0, p:p + 1, :]).astype(BF16), w1_ref[0, p * HEAD_DIM:(p + 1) * HEAD_DIM, :])
        bot += _dot((xp + pe_ref[0, q:q + 1, :]).astype(BF16), w1_ref[0, q * HEAD_DIM:(q + 1) * HEAD_DIM, :])
    h = top + pltpu.roll(bot, n_chunks - 1, axis=0)
    y = _dot(_gelu(h).astype(BF16), w2_ref[0])
    o_ref[0, 0] = _rope(y, cos_ref[0], sin_ref[0]).astype(o_ref.dtype)


def _compress(kv_cmp, pe, w1, w2, cos, sin):
    bsz, seq, width = kv_cmp.shape
    four = width // HEAD_DIM
    n_chunks = seq // CMP_STRIDE
    g = NSA_KV_GROUPS
    return pl.pallas_call(
        _compress_kernel,
        grid=(four, bsz),
        in_specs=[pl.BlockSpec((1, seq, HEAD_DIM), lambda a, b: (b, 0, a)),
                  pl.BlockSpec((1, CMP_BLOCK, HEAD_DIM), lambda a, b: (a // g, 0, 0)),
                  pl.BlockSpec((1, CMP_BLOCK * HEAD_DIM, CMP_HIDDEN), lambda a, b: (a // g, 0, 0)),
                  pl.BlockSpec((1, CMP_HIDDEN, HEAD_DIM), lambda a, b: (a // g, 0, 0)),
                  pl.BlockSpec((1, n_chunks, HEAD_DIM), lambda a, b: (a // g, 0, 0)),
                  pl.BlockSpec((1, n_chunks, HEAD_DIM), lambda a, b: (a // g, 0, 0))],
        out_specs=pl.BlockSpec((1, 1, n_chunks, HEAD_DIM), lambda a, b: (a, b, 0, 0)),
        out_shape=jax.ShapeDtypeStruct((four, bsz, n_chunks, HEAD_DIM), BF16),
        compiler_params=_params("parallel", "parallel"),
        name="compress",
    )(kv_cmp, pe, w1, w2, cos, sin)


SEL_TILE = 512
SEL_WIDE = 2
NSA_TQ = 256
SUBLANES = 8


def _softmax_parts(s3, bias):
    s3 = s3 + bias[None]
    m = jnp.max(s3, -1, keepdims=True)
    m = jnp.where(m > 0.5 * MASKED, m, 0.0)
    p = jnp.exp2(s3 - m)
    return p, 1.0 / jnp.maximum(jnp.sum(p, -1, keepdims=True), 1e-30)


def _topk_unselected(score_t, n_sel, n_pick):
    tq = score_t.shape[1]
    n_grp = n_sel // SUBLANES
    grp = [score_t[g * SUBLANES:(g + 1) * SUBLANES] for g in range(n_grp)]
    row_in_grp = lax.broadcasted_iota(jnp.int32, (SUBLANES, tq), 0)
    later = [jnp.where(row_in_grp > li, 1.0, 0.0) for li in range(SUBLANES)]
    beaten = [jnp.zeros((SUBLANES, tq), F32) for _ in range(n_grp)]
    for i in range(n_sel):
        gi, li = divmod(i, SUBLANES)
        row = grp[gi][li:li + 1, :]
        for g in range(n_grp):
            if g > gi:
                beaten[g] = beaten[g] + jnp.where(row >= grp[g], 1.0, 0.0)
            elif g < gi:
                beaten[g] = beaten[g] + jnp.where(row > grp[g], 1.0, 0.0)
            else:
                beaten[g] = (beaten[g] + jnp.where(row > grp[g], 1.0, 0.0)
                             + jnp.where(row == grp[g], later[li], 0.0))
    return jnp.concatenate([jnp.where(b < n_pick, 0.0, MASKED) for b in beaten], axis=0)


def _nsa_kernel(q_ref, kc_ref, vc_ref, ks_ref, vs_ref, kw_ref, vw_ref, gate_ref, selmap_t_ref, o_ref,
                *, n_sel, n_pick):
    hpg = NSA_HPG
    tq = NSA_TQ
    c = pl.program_id(2)
    q_t = q_ref[0]
    q8 = jnp.concatenate([q_t[:, n * HEAD_DIM:(n + 1) * HEAD_DIM] for n in range(hpg)], axis=0)
    t_pos = c * tq + lax.broadcasted_iota(jnp.int32, (tq, 1), 0)
    t_lane = c * tq + lax.broadcasted_iota(jnp.int32, (1, tq), 1)

    kc = kc_ref[0, 0]
    n_cmp_pad = kc.shape[0]
    k_idx = lax.broadcasted_iota(jnp.int32, (1, n_cmp_pad), 1)
    vis_c = (CMP_STRIDE * k_idx + (CMP_BLOCK - 1) <= t_pos) & (k_idx < n_cmp_pad - 1)
    p_c, r_c = _softmax_parts(_dot_nt(q8, kc).reshape(hpg, tq, n_cmp_pad), jnp.where(vis_c, 0.0, MASKED))
    o_c = _dot(p_c.reshape(hpg * tq, n_cmp_pad).astype(BF16), vc_ref[0, 0]).reshape(hpg, tq, HEAD_DIM) * r_c

    p_sum = jnp.sum(p_c * r_c, axis=0)
    p_hi = p_sum.astype(BF16)
    p_lo = (p_sum - p_hi.astype(F32)).astype(BF16)
    selmap_t = selmap_t_ref[...]
    imp_t = _dot_nt(selmap_t, p_hi) + _dot_nt(selmap_t, p_lo)
    blk = lax.broadcasted_iota(jnp.int32, (LANES, tq), 0)
    cur = t_lane >> _SEL_SHIFT
    forced = (blk == 0) | (blk == cur) | (blk == cur - 1)
    score_t = jnp.where(forced, SEL_BIG, jnp.where(blk <= cur, imp_t, -SEL_BIG))
    unsel_t = _topk_unselected(score_t, n_sel, n_pick)
    unsel_t = jnp.concatenate([unsel_t, jnp.zeros((LANES - n_sel, tq), F32)], axis=0)
    unsel = unsel_t.T.astype(BF16)
    q_aug = jnp.concatenate([q8, jnp.concatenate([unsel] * hpg, axis=0)], axis=1)

    def sel_tile(kt, size, carry, bias):
        m_old, l_old, acc = carry
        base = pl.multiple_of(kt * size, size)
        k = ks_ref[0, pl.ds(base, size), :]
        v = vs_ref[0, pl.ds(base, size), :]
        s = _dot_nt(q_aug, k).reshape(hpg, tq, size)
        if bias is not None:
            s = s + bias[None]
        m_new = jnp.maximum(m_old, jnp.max(s, -1, keepdims=True))
        alpha = jnp.exp2(m_old - m_new)
        p = jnp.exp2(s - m_new)
        l_new = alpha * l_old + jnp.sum(p, -1, keepdims=True)
        pv = _dot(p.reshape(hpg * tq, size).astype(BF16), v).reshape(hpg, tq, HEAD_DIM)
        return m_new, l_new, alpha * acc + pv

    n_full = c // (SEL_TILE // tq)
    n_wide = n_full // SEL_WIDE
    carry = (jnp.full((hpg, tq, 1), M_INIT, F32), jnp.zeros((hpg, tq, 1), F32),
             jnp.zeros((hpg, tq, HEAD_DIM), F32))
    carry = lax.fori_loop(0, n_wide, lambda kt, cr: sel_tile(kt, SEL_WIDE * SEL_TILE, cr, None), carry)
    carry = lax.fori_loop(n_wide * SEL_WIDE, n_full, lambda kt, cr: sel_tile(kt, SEL_TILE, cr, None), carry)
    kp_last = n_full * SEL_TILE + lax.broadcasted_iota(jnp.int32, (1, SEL_TILE), 1)
    _, l_s, acc_s = sel_tile(n_full, SEL_TILE, carry, jnp.where(kp_last <= t_pos, 0.0, MASKED))
    o_s = acc_s * (1.0 / jnp.maximum(l_s, 1e-30))

    span = WINDOW + tq
    start = pl.multiple_of(jnp.maximum(c * tq - WINDOW, 0), tq)
    kw = kw_ref[0, pl.ds(start, span), :]
    vw = vw_ref[0, pl.ds(start, span), :]
    kp = start + lax.broadcasted_iota(jnp.int32, (1, span), 1)
    vis_w = (kp <= t_pos) & (t_pos - kp < WINDOW)
    p_w, r_w = _softmax_parts(_dot_nt(q8, kw).reshape(hpg, tq, span), jnp.where(vis_w, 0.0, MASKED))
    o_w = _dot(p_w.reshape(hpg * tq, span).astype(BF16), vw).reshape(hpg, tq, HEAD_DIM) * r_w

    gates = gate_ref[0]
    for n in range(hpg):
        out = (gates[:, n:n + 1] * o_c[n]
               + gates[:, hpg + n:hpg + n + 1] * o_s[n]
               + gates[:, 2 * hpg + n:2 * hpg + n + 1] * o_w[n])
        o_ref[0, :, n * HEAD_DIM:(n + 1) * HEAD_DIM] = out.astype(o_ref.dtype)


def _nsa(q, cmp_kv, rest, gates, selmap_t, *, n_sel, n_pick):
    bsz, seq, _ = q.shape
    g = NSA_KV_GROUPS
    n_cmp_pad = cmp_kv.shape[2]
    gw = NSA_HPG * HEAD_DIM
    assert n_sel % SUBLANES == 0 and n_sel <= LANES and seq >= WINDOW + NSA_TQ
    assert SEL_TILE % NSA_TQ == 0 and seq % SEL_TILE == 0
    kv_spec = lambda off: pl.BlockSpec((1, seq, HEAD_DIM), lambda b, gi, c: (b, 0, off + gi))
    return pl.pallas_call(
        functools.partial(_nsa_kernel, n_sel=n_sel, n_pick=n_pick),
        grid=(bsz, g, seq // NSA_TQ),
        in_specs=[pl.BlockSpec((1, NSA_TQ, gw), lambda b, gi, c: (b, c, gi)),
                  pl.BlockSpec((1, 1, n_cmp_pad, HEAD_DIM), lambda b, gi, c: (gi, b, 0, 0)),
                  pl.BlockSpec((1, 1, n_cmp_pad, HEAD_DIM), lambda b, gi, c: (g + gi, b, 0, 0)),
                  pl.BlockSpec((1, seq, 2 * HEAD_DIM), lambda b, gi, c: (b, 0, gi)),
                  kv_spec(2 * g), kv_spec(3 * g), kv_spec(4 * g),
                  pl.BlockSpec((1, NSA_TQ, LANES), lambda b, gi, c: (b, c, gi)),
                  pl.BlockSpec(selmap_t.shape, lambda b, gi, c: (0, 0))],
        out_specs=pl.BlockSpec((1, NSA_TQ, gw), lambda b, gi, c: (b, c, gi)),
        out_shape=jax.ShapeDtypeStruct(q.shape, BF16),
        compiler_params=_params("parallel", "parallel", "arbitrary"),
        name="nsa_attention",
    )(q, cmp_kv, cmp_kv, rest, rest, rest, rest, gates, selmap_t)


def _sgu_kernel(x_ref, w_ref, g_ref, b_ref, ws_ref, bs_ref, o_ref, *, chunks):
    gd = SGU_WIDTH // SGU_GROUPS
    row = lax.broadcasted_iota(jnp.int32, (SGU_CHUNK, SGU_CHUNK), 0)
    col = lax.broadcasted_iota(jnp.int32, (SGU_CHUNK, SGU_CHUNK), 1)
    causal = col <= row
    bs = bs_ref[...]
    x = x_ref[...]
    v = jnp.concatenate(
        [_gelu(_dot(x, w_ref[:, SGU_WIDTH + j * MXU_COLS:SGU_WIDTH + (j + 1) * MXU_COLS]))
         for j in range(SGU_WIDTH // MXU_COLS)], axis=1)
    vn = _layer_norm(v, g_ref[...], b_ref[...]).astype(BF16)
    for gi in range(SGU_GROUPS):
        cols = slice(gi * gd, (gi + 1) * gd)
        u = _gelu(_dot(x, w_ref[:, cols]))
        w = jnp.where(causal, ws_ref[gi], 0.0).astype(BF16)
        for ch in range(chunks):
            rows = slice(ch * SGU_CHUNK, (ch + 1) * SGU_CHUNK)
            z = _dot(w, vn[rows, cols]) + bs[:, gi:gi + 1]
            o_ref[rows, cols] = (u[rows] * z).astype(o_ref.dtype)


def _sgu(x_bf, w_uv, ln_g, ln_b, w_s, b_s_t, *, chunks=4):
    m, d = x_bf.shape
    tm = chunks * SGU_CHUNK
    return pl.pallas_call(
        functools.partial(_sgu_kernel, chunks=chunks),
        grid=(m // tm,),
        in_specs=[pl.BlockSpec((tm, d), lambda i: (i, 0)),
                  pl.BlockSpec(w_uv.shape, lambda i: (0, 0), pipeline_mode=pl.Buffered(1)),
                  pl.BlockSpec((1, SGU_WIDTH), lambda i: (0, 0)),
                  pl.BlockSpec((1, SGU_WIDTH), lambda i: (0, 0)),
                  pl.BlockSpec(w_s.shape, lambda i: (0, 0, 0)),
                  pl.BlockSpec(b_s_t.shape, lambda i: (0, 0))],
        out_specs=pl.BlockSpec((tm, SGU_WIDTH), lambda i: (i, 0)),
        out_shape=jax.ShapeDtypeStruct((m, SGU_WIDTH), BF16),
        compiler_params=_params("parallel"),
        name="sgu",
    )(x_bf, w_uv, ln_g, ln_b, w_s, b_s_t)


def _merge_kernel(x_ref, a_ref, s_ref, wga_ref, wgb_ref, wa_ref, wb_ref, o_ref):
    x = x_ref[...]
    ga = _sigmoid(_dot(x, wga_ref[...]))
    gb = _sigmoid(_dot(x, wgb_ref[...]))
    o_ref[...] = (ga * _dot(a_ref[...], wa_ref[...]) + gb * _dot(s_ref[...], wb_ref[...])).astype(o_ref.dtype)


def _merge(x_bf, o_nsa, o_sgu, w_merge, w_a, w_b, *, tm=1024, tn=512):
    m, k = x_bf.shape
    n = w_a.shape[1]
    nb = n // tn
    row = pl.BlockSpec((tm, k), lambda i, j: (i, 0))
    col = pl.BlockSpec((k, tn), lambda i, j: (0, j))
    return pl.pallas_call(
        _merge_kernel,
        grid=(m // tm, nb),
        in_specs=[row, row, row, col, pl.BlockSpec((k, tn), lambda i, j: (0, nb + j)), col, col],
        out_specs=pl.BlockSpec((tm, tn), lambda i, j: (i, j)),
        out_shape=jax.ShapeDtypeStruct((m, n), BF16),
        compiler_params=_params("parallel", "parallel"),
        name="merge",
    )(x_bf, o_nsa, o_sgu, w_merge, w_merge, w_a, w_b)


def _proj_ln_kernel(a_ref, w_ref, x_ref, g_ref, b_ref, o_ref, obf_ref):
    y = DN_ALPHA * x_ref[...] + _dot(a_ref[...], w_ref[...])
    out = _layer_norm(y, g_ref[...], b_ref[...])
    o_ref[...] = out
    obf_ref[...] = out.astype(BF16)


def _proj_ln(a_bf, w, x, ln_g, ln_b, *, tm=512):
    m, k = a_bf.shape
    n = w.shape[1]
    vec = pl.BlockSpec((1, n), lambda i: (0, 0))
    return pl.pallas_call(
        _proj_ln_kernel,
        grid=(m // tm,),
        in_specs=[pl.BlockSpec((tm, k), lambda i: (i, 0)),
                  pl.BlockSpec((k, n), lambda i: (0, 0)),
                  pl.BlockSpec((tm, n), lambda i: (i, 0)), vec, vec],
        out_specs=[pl.BlockSpec((tm, n), lambda i: (i, 0)), pl.BlockSpec((tm, n), lambda i: (i, 0))],
        out_shape=[jax.ShapeDtypeStruct((m, n), F32), jax.ShapeDtypeStruct((m, n), BF16)],
        compiler_params=_params("parallel"),
        name="proj_ln",
    )(a_bf, w, x, ln_g, ln_b)


def _xattn_kernel(xbf_ref, x_ref, wq_ref, kv_ref, wo_ref, g_ref, b_ref, o_ref):
    q = (_dot(xbf_ref[0], wq_ref[...]) * HEAD_DIM ** -0.5).astype(BF16)
    kv = kv_ref[0]
    heads = []
    for h in range(MEM_HEADS):
        cols = slice(h * HEAD_DIM, (h + 1) * HEAD_DIM)
        s = _dot_nt(q[:, cols], kv[:, cols])
        p = jnp.exp(s - jnp.max(s, -1, keepdims=True))
        p = p / jnp.sum(p, -1, keepdims=True)
        heads.append(_dot(p.astype(BF16), kv[:, MEM_WIDTH + h * HEAD_DIM:MEM_WIDTH + (h + 1) * HEAD_DIM]))
    o = jnp.concatenate(heads, axis=1).astype(BF16)
    y = DN_ALPHA * x_ref[0] + _dot(o, wo_ref[...])
    o_ref[0] = _layer_norm(y, g_ref[...], b_ref[...])


def _xattn(x_bf, x, w_xq, mem_kv, w_xo, ln_g, ln_b, *, tm=512):
    bsz, seq, d = x.shape
    mem_len = mem_kv.shape[1]
    vec = pl.BlockSpec((1, d), lambda b, i: (0, 0))
    return pl.pallas_call(
        _xattn_kernel,
        grid=(bsz, seq // tm),
        in_specs=[pl.BlockSpec((1, tm, d), lambda b, i: (b, i, 0)),
                  pl.BlockSpec((1, tm, d), lambda b, i: (b, i, 0)),
                  pl.BlockSpec(w_xq.shape, lambda b, i: (0, 0)),
                  pl.BlockSpec((1, mem_len, 2 * MEM_WIDTH), lambda b, i: (b, 0, 0)),
                  pl.BlockSpec(w_xo.shape, lambda b, i: (0, 0)), vec, vec],
        out_specs=pl.BlockSpec((1, tm, d), lambda b, i: (b, i, 0)),
        out_shape=jax.ShapeDtypeStruct(x.shape, F32),
        compiler_params=_params("parallel", "parallel"),
        name="xattn_ln",
    )(x_bf, x, w_xq, mem_kv, w_xo, ln_g, ln_b)


def _router_kernel(x_ref, whi_ref, wlo_ref, b_ref, gate_ref, exp_ref, rank_ref, cnt_ref, seen_ref):
    @pl.when(pl.program_id(0) == 0)
    def _():
        seen_ref[...] = jnp.zeros_like(seen_ref)

    x = x_ref[...]
    x_hi = x.astype(BF16)
    x_lo = (x - x_hi.astype(F32)).astype(BF16)
    whi = whi_ref[...]
    logits = _dot(x_hi, whi) + _dot(x_hi, wlo_ref[...]) + _dot(x_lo, whi) + b_ref[...]
    rows = logits.shape[0]
    lane = lax.broadcasted_iota(jnp.int32, (rows, LANES), 1)
    lane_f = lane.astype(F32)
    is_grp = lane < N_GROUPS
    lg = jnp.where(is_grp, logits, MASKED)
    eg = jnp.where(is_grp, jnp.exp(lg - jnp.max(lg, -1, keepdims=True)), 0.0)
    p_grp = eg / jnp.sum(eg, -1, keepdims=True)
    g_grp = jnp.max(p_grp, -1, keepdims=True)
    grp = jnp.min(jnp.where((p_grp == g_grp) & is_grp, lane_f, 1e9), -1, keepdims=True).astype(jnp.int32)
    lo = N_GROUPS + grp * EXPERTS_PER_GROUP
    in_grp = (lane >= lo) & (lane < lo + EXPERTS_PER_GROUP)
    le = jnp.where(in_grp, logits, MASKED)
    ee = jnp.where(in_grp, jnp.exp(le - jnp.max(le, -1, keepdims=True)), 0.0)
    p_exp = jnp.where(in_grp, ee / jnp.sum(ee, -1, keepdims=True), -1.0)
    v1 = jnp.max(p_exp, -1, keepdims=True)
    i1 = jnp.min(jnp.where(p_exp == v1, lane_f, 1e9), -1, keepdims=True)
    p_rest = jnp.where(lane_f == i1, -1.0, p_exp)
    v2 = jnp.max(p_rest, -1, keepdims=True)
    i2 = jnp.min(jnp.where(p_rest == v2, lane_f, 1e9), -1, keepdims=True)
    denom = v1 + v2
    gate_ref[...] = jnp.where(lane == 0, g_grp * v1 / denom, jnp.where(lane == 1, g_grp * v2 / denom, 0.0))
    exp_ref[...] = jnp.where(lane == 0, i1, jnp.where(lane == 1, i2, float(N_GROUPS))).astype(jnp.int32) - N_GROUPS

    onehot = jnp.where((lane_f == i1) | (lane_f == i2), 1.0, 0.0)
    tok_r = lax.broadcasted_iota(jnp.int32, (rows, rows), 0)
    tok_c = lax.broadcasted_iota(jnp.int32, (rows, rows), 1)
    earlier = jnp.where(tok_c < tok_r, 1.0, 0.0).astype(BF16)
    before = _dot(earlier, onehot.astype(BF16)) + seen_ref[...]
    r1 = jnp.sum(jnp.where(lane_f == i1, before, 0.0), -1, keepdims=True)
    r2 = jnp.sum(jnp.where(lane_f == i2, before, 0.0), -1, keepdims=True)
    rank_ref[...] = jnp.where(lane == 0, r1, jnp.where(lane == 1, r2, 0.0)).astype(jnp.int32)
    seen_ref[...] += jnp.sum(onehot, axis=0, keepdims=True)
    cnt_ref[...] = seen_ref[...]


def _router(x, w_hi, w_lo, bias, *, tm=512):
    m, d = x.shape
    wspec = pl.BlockSpec((d, LANES), lambda i: (0, 0))
    vec = pl.BlockSpec((1, LANES), lambda i: (0, 0))
    tile = pl.BlockSpec((tm, LANES), lambda i: (i, 0))
    return pl.pallas_call(
        _router_kernel,
        grid=(m // tm,),
        in_specs=[pl.BlockSpec((tm, d), lambda i: (i, 0)), wspec, wspec, vec],
        out_specs=[tile, tile, tile, vec],
        out_shape=[jax.ShapeDtypeStruct((m, LANES), F32), jax.ShapeDtypeStruct((m, LANES), jnp.int32),
                   jax.ShapeDtypeStruct((m, LANES), jnp.int32), jax.ShapeDtypeStruct((1, LANES), F32)],
        scratch_shapes=[pltpu.VMEM((1, LANES), F32)],
        compiler_params=_params("arbitrary"),
        name="router",
    )(x, w_hi, w_lo, bias)


GATHER_UNROLL = 8
WEIGHT_DMA_PRIORITY = 1


def _ffn_kernel(exp_ref, on_ref, first_ref, slot_ref, next_ref, tok_ref, dst_ref, x_hbm, wg_hbm, wu_hbm, wd_hbm,
                y_hbm, wg_buf, wu_buf, wd_buf, h_buf, y_buf, sem, row_sem, out_sem):
    i = pl.program_id(0)
    n_blk = pl.num_programs(0)
    slot = slot_ref[i]
    rows = MOE_BLOCK

    def out_copy(blk, g, u):
        dst = dst_ref[blk * rows + g * GATHER_UNROLL + u]
        return pltpu.make_async_copy(y_buf.at[blk % 2, g, pl.ds(u, 1), :], y_hbm.at[pl.ds(dst, 1), :],
                                     out_sem.at[blk % 2])

    def weight_copies(e, s):
        return (pltpu.make_async_copy(wg_hbm.at[e], wg_buf.at[s], sem.at[s, 0]),
                pltpu.make_async_copy(wu_hbm.at[e], wu_buf.at[s], sem.at[s, 1]),
                pltpu.make_async_copy(wd_hbm.at[e], wd_buf.at[s], sem.at[s, 2]))

    def row_copy(blk, g, u):
        tok = tok_ref[blk * rows + g * GATHER_UNROLL + u]
        return pltpu.make_async_copy(x_hbm.at[pl.ds(tok, 1), :], h_buf.at[blk % 2, g, pl.ds(u, 1), :],
                                     row_sem.at[blk % 2])

    def for_rows(blk, fn):
        def body(g, carry):
            for u in range(GATHER_UNROLL):
                fn(blk, g, u)
            return carry
        lax.fori_loop(0, on_ref[blk], body, 0)

    @pl.when(i == 0)
    def _():
        for cp in weight_copies(exp_ref[0], 0):
            cp.start(priority=WEIGHT_DMA_PRIORITY)
        h_buf[...] = jnp.zeros_like(h_buf)
        for_rows(0, lambda b, g, u: row_copy(b, g, u).start())
        y_buf[...] = jnp.zeros_like(y_buf)
        spare = y_hbm.shape[0] - 2 * rows
        fills = [pltpu.make_async_copy(y_buf.at[s, g],
                                       y_hbm.at[pl.ds(spare + s * rows + g * GATHER_UNROLL, GATHER_UNROLL), :],
                                       out_sem.at[s])
                 for s in range(2) for g in range(rows // GATHER_UNROLL)]
        for cp in fills:
            cp.start()
        for cp in fills:
            cp.wait()

    prev2 = jnp.maximum(i - 2, 0)

    @pl.when((i >= 2) & (on_ref[prev2] != 0))
    def _():
        for_rows(i - 2, lambda b, g, u: out_copy(b, g, u).wait())

    nxt_blk = jnp.minimum(i + 1, n_blk - 1)

    @pl.when((i + 1 < n_blk) & (on_ref[nxt_blk] != 0))
    def _():
        for_rows(i + 1, lambda b, g, u: row_copy(b, g, u).start())

    @pl.when(first_ref[i] != 0)
    def _():
        nxt = next_ref[i]

        @pl.when(nxt >= 0)
        def _():
            for cp in weight_copies(nxt, 1 - slot):
                cp.start(priority=WEIGHT_DMA_PRIORITY)

        for cp in weight_copies(exp_ref[i], slot):
            cp.wait()

    @pl.when(on_ref[i] != 0)
    def _():
        for_rows(i, lambda b, g, u: row_copy(b, g, u).wait())
        h = h_buf[i % 2].reshape(rows, h_buf.shape[-1]).astype(BF16)
        a = _dot(h, wg_buf[slot].astype(BF16))
        a = a * _sigmoid(a) * _dot(h, wu_buf[slot].astype(BF16))
        y_buf[i % 2] = _dot(a.astype(BF16), wd_buf[slot].astype(BF16)).reshape(y_buf.shape[1:])
        for_rows(i, lambda b, g, u: out_copy(b, g, u).start(priority=u % 2))

    @pl.when(i == n_blk - 1)
    def _():
        prev1 = jnp.maximum(i - 1, 0)

        @pl.when((i >= 1) & (on_ref[prev1] != 0))
        def _():
            for_rows(i - 1, lambda b, g, u: out_copy(b, g, u).wait())

        @pl.when(on_ref[i] != 0)
        def _():
            for_rows(i, lambda b, g, u: out_copy(b, g, u).wait())


def _expert_ffn(blk_exp, blk_on, row_tok, row_dst, x, w_gate, w_up, w_down):
    p_rows = row_tok.shape[0]
    n_tok, d = x.shape
    n_exp, _, ff = w_gate.shape
    n_blocks = p_rows // MOE_BLOCK
    prev_exp = jnp.concatenate([jnp.full((1,), -1, jnp.int32), blk_exp[:-1]])
    first = ((blk_on != 0) & (blk_exp != prev_exp)).astype(jnp.int32)
    seg_id = jnp.cumsum(first) - 1
    seg_ids = jnp.arange(n_exp + 1)
    hit = (first[:, None] != 0) & (seg_id[:, None] == seg_ids[None, :])
    seg_exp = jnp.sum(jnp.where(hit, blk_exp[:, None] + 1, 0), 0) - 1
    blk_next = jnp.sum(jnp.where(seg_ids[None, :] == seg_id[:, None] + 1, seg_exp[None, :] + 1, 0), 1) - 1
    blk_slot = jnp.maximum(seg_id, 0) % 2
    grid_spec = pltpu.PrefetchScalarGridSpec(
        num_scalar_prefetch=7,
        grid=(n_blocks,),
        in_specs=[pl.BlockSpec(memory_space=pl.ANY)] * 4,
        out_specs=pl.BlockSpec(memory_space=pl.ANY),
        scratch_shapes=[pltpu.VMEM((2, d, ff), F32), pltpu.VMEM((2, d, ff), F32), pltpu.VMEM((2, ff, d), F32),
                        pltpu.VMEM((2, MOE_BLOCK // GATHER_UNROLL, GATHER_UNROLL, d), F32),
                        pltpu.VMEM((2, MOE_BLOCK // GATHER_UNROLL, GATHER_UNROLL, d), F32),
                        pltpu.SemaphoreType.DMA((2, 3)), pltpu.SemaphoreType.DMA((2,)),
                        pltpu.SemaphoreType.DMA((2,))],
    )
    return pl.pallas_call(
        _ffn_kernel,
        grid_spec=grid_spec,
        out_shape=jax.ShapeDtypeStruct((EXPERT_TOPK * n_tok + 2 * MOE_BLOCK, d), F32),
        compiler_params=_params("arbitrary"),
        name="expert_ffn",
    )(blk_exp, blk_on, first, blk_slot.astype(jnp.int32), blk_next.astype(jnp.int32), row_tok, row_dst,
      x, w_gate, w_up, w_down)


def _combine_ln_kernel(x_ref, y0_ref, y1_ref, gate_ref, g_ref, b_ref, o_ref):
    gates = gate_ref[...]
    moe = y0_ref[...] * gates[:, 0:1] + y1_ref[...] * gates[:, 1:2]
    o_ref[...] = _layer_norm(DN_ALPHA * x_ref[...] + moe, g_ref[...], b_ref[...])


def _combine_ln(x, y_tok, gates, ln_g, ln_b, *, tm=256):
    m, d = x.shape
    row = pl.BlockSpec((tm, d), lambda i: (i, 0))
    vec = pl.BlockSpec((1, d), lambda i: (0, 0))
    return pl.pallas_call(
        _combine_ln_kernel,
        grid=(m // tm,),
        in_specs=[row, row, pl.BlockSpec((tm, d), lambda i: (m // tm + i, 0)),
                  pl.BlockSpec((tm, LANES), lambda i: (i, 0)), vec, vec],
        out_specs=row,
        out_shape=jax.ShapeDtypeStruct((m, d), F32),
        compiler_params=_params("parallel"),
        name="combine_ln",
    )(x, y_tok, y_tok, gates, ln_g, ln_b)


def _layer(x, mem, w_in, cmp_pe_k, cmp_w1_k, cmp_w2_k, cmp_pe_v, cmp_w1_v, cmp_w2_v,
           sgu_ln_g, sgu_ln_b, sgu_w_s, sgu_b_s, w_branch_a, w_branch_b, w_o, ln1_g, ln1_b,
           w_xq, w_xk, w_xv, w_xo, ln2_g, ln2_b,
           w_router_grp, b_router_grp, w_router_exp, b_router_exp,
           w_exp_gate, w_exp_up, w_exp_down, ln3_g, ln3_b):
    bsz, seq, d = x.shape
    n_tok = bsz * seq
    g = NSA_KV_GROUPS
    hpg = NSA_HPG
    xf = x.reshape(n_tok, d)

    c0 = Q_WIDTH
    c1 = c0 + KV_WIDTH
    c2 = c1 + NSA_GATE_WIDTH
    c3 = c2 + 2 * SGU_WIDTH
    w_q = w_in[:, :c0].astype(BF16)
    w_gate = w_in[:, c1:c2].reshape(d, g, hpg, 3).transpose(0, 1, 3, 2).reshape(d, g, 3 * hpg)
    w_gate = jnp.pad(w_gate, ((0, 0), (0, 0), (0, LANES - 3 * hpg))).reshape(d, g * LANES)
    w_kvg = jnp.concatenate([w_in[:, c0:c1], w_gate], axis=1).astype(BF16)
    w_uv = w_in[:, c2:c3].astype(BF16)
    w_merge = w_in[:, c3:].astype(BF16)

    cos, sin = _rope_tables(jnp.arange(seq, dtype=jnp.int32))
    kv_cmp, kv_rest, nsa_gates, x_bf = _kv_proj(xf, w_kvg, cos, sin, seq)
    q = _q_proj(x_bf, w_q, cos, sin, seq).reshape(bsz, seq, Q_WIDTH)

    n_chunks = seq // CMP_STRIDE
    pe = jnp.stack([cmp_pe_k, cmp_pe_v])
    w1 = jnp.stack([cmp_w1_k, cmp_w1_v]).astype(BF16)
    w2 = jnp.stack([cmp_w2_k, cmp_w2_v]).astype(BF16)
    cmp_end = CMP_STRIDE * jnp.arange(n_chunks, dtype=jnp.int32) + CMP_BLOCK - 1
    ccos, csin = _rope_tables(cmp_end)
    cmp_cos = jnp.stack([ccos, jnp.ones_like(ccos)])
    cmp_sin = jnp.stack([csin, jnp.zeros_like(csin)])
    cmp_kv = _compress(kv_cmp.reshape(bsz, seq, _N_CMP_COLS), pe, w1, w2, cmp_cos, cmp_sin)

    n_sel = seq // SEL_BLOCK
    n_pick = min(SEL_TOPK, n_sel)
    ci = CMP_STRIDE * jnp.arange(n_chunks)
    sj = SEL_BLOCK * jnp.arange(LANES)
    overlap = jnp.clip(jnp.minimum(ci[:, None] + CMP_BLOCK, sj[None, :] + SEL_BLOCK)
                       - jnp.maximum(ci[:, None], sj[None, :]), 0, None)
    selmap_t = (overlap.astype(F32) / CMP_BLOCK).astype(BF16).T
    o_nsa = _nsa(q, cmp_kv, kv_rest.reshape(bsz, seq, _N_REST_COLS),
                 nsa_gates.reshape(bsz, seq, _GATE_COLS), selmap_t, n_sel=n_sel, n_pick=n_pick)

    o_sgu = _sgu(x_bf, w_uv, sgu_ln_g.reshape(1, -1), sgu_ln_b.reshape(1, -1), sgu_w_s, sgu_b_s.T)

    merged = _merge(x_bf, o_nsa.reshape(n_tok, Q_WIDTH), o_sgu, w_merge,
                    w_branch_a.astype(BF16), w_branch_b.astype(BF16))
    x1, x1_bf = _proj_ln(merged, w_o.astype(BF16), xf, ln1_g.reshape(1, -1), ln1_b.reshape(1, -1))

    mem_len = mem.shape[1]
    w_mem = jnp.concatenate([w_xk, w_xv], axis=1).astype(BF16)
    mem_kv = _matmul(mem.reshape(bsz * mem_len, d).astype(BF16), w_mem, tm=256, tn=512, out_dtype=BF16)
    x2 = _xattn(x1_bf.reshape(bsz, seq, d), x1.reshape(bsz, seq, d), w_xq.astype(BF16),
                mem_kv.reshape(bsz, mem_len, 2 * MEM_WIDTH), w_xo.astype(BF16),
                ln2_g.reshape(1, -1), ln2_b.reshape(1, -1)).reshape(n_tok, d)

    w_r = jnp.concatenate([w_router_grp, w_router_exp], axis=1)
    w_r = jnp.pad(w_r, ((0, 0), (0, LANES - w_r.shape[1])))
    w_r_hi = w_r.astype(BF16)
    w_r_lo = (w_r - w_r_hi.astype(F32)).astype(BF16)
    b_r = jnp.concatenate([b_router_grp, b_router_exp])
    b_r = jnp.pad(b_r, (0, LANES - b_r.shape[0])).reshape(1, LANES)
    gates, experts, ranks, seen = _router(x2, w_r_hi, w_r_lo, b_r)

    nk = n_tok * EXPERT_TOPK
    counts = seen[0, N_GROUPS:N_GROUPS + N_EXPERTS].astype(jnp.int32)
    pcounts = ((counts + MOE_BLOCK - 1) // MOE_BLOCK) * MOE_BLOCK
    pend = jnp.cumsum(pcounts)
    pstart = pend - pcounts
    e_tok = experts[:, :EXPERT_TOPK]
    seg_start = jnp.sum(jnp.where(e_tok[..., None] == jnp.arange(N_EXPERTS), pstart, 0), -1)
    dest_of = seg_start + ranks[:, :EXPERT_TOPK]
    n_blocks = -(-nk // MOE_BLOCK) + N_EXPERTS
    p_rows = n_blocks * MOE_BLOCK
    row_ids = jnp.arange(p_rows, dtype=jnp.int32)
    row_asg = jnp.full((p_rows,), -1, jnp.int32).at[dest_of.reshape(nk)].set(
        jnp.arange(nk, dtype=jnp.int32), unique_indices=True)
    is_real = row_asg >= 0
    row_tok = jnp.where(is_real, row_asg // EXPERT_TOPK, row_ids % n_tok)
    spare = EXPERT_TOPK * n_tok + ((row_ids // MOE_BLOCK) % 2) * MOE_BLOCK + row_ids % MOE_BLOCK
    row_dst = jnp.where(is_real, (row_asg % EXPERT_TOPK) * n_tok + row_asg // EXPERT_TOPK, spare)
    blk_start = jnp.arange(n_blocks) * MOE_BLOCK
    blk_on = (blk_start < pend[-1]).astype(jnp.int32)
    blk_probe = jnp.minimum(blk_start, pend[-1] - 1)
    blk_exp = jnp.minimum(jnp.sum((pend[None, :] <= blk_probe[:, None]).astype(jnp.int32), -1),
                          N_EXPERTS - 1).astype(jnp.int32)
    real_end = jnp.sum(jnp.where(blk_exp[:, None] == jnp.arange(N_EXPERTS), (pstart + counts)[None, :], 0), -1)
    blk_real = jnp.clip(real_end - blk_start, 0, MOE_BLOCK) * blk_on
    blk_groups = ((blk_real + GATHER_UNROLL - 1) // GATHER_UNROLL).astype(jnp.int32)

    y_tok = _expert_ffn(blk_exp, blk_groups, row_tok, row_dst, x2, w_exp_gate, w_exp_up, w_exp_down)
    out = _combine_ln(x2, y_tok, gates, ln3_g.reshape(1, -1), ln3_b.reshape(1, -1))
    return out.reshape(bsz, seq, d)


def kernel(x, mem, w_in, cmp_pe_k, cmp_w1_k, cmp_w2_k, cmp_pe_v, cmp_w1_v, cmp_w2_v, sgu_ln_g, sgu_ln_b, sgu_w_s, sgu_b_s, w_branch_a, w_branch_b, w_o, ln1_g, ln1_b, w_xq, w_xk, w_xv, w_xo, ln2_g, ln2_b, w_router_grp, b_router_grp, w_router_exp, b_router_exp, w_exp_gate, w_exp_up, w_exp_down, ln3_g, ln3_b):
    params = (w_in, cmp_pe_k, cmp_w1_k, cmp_w2_k, cmp_pe_v, cmp_w1_v, cmp_w2_v,
              sgu_ln_g, sgu_ln_b, sgu_w_s, sgu_b_s, w_branch_a, w_branch_b, w_o, ln1_g, ln1_b,
              w_xq, w_xk, w_xv, w_xo, ln2_g, ln2_b,
              w_router_grp, b_router_grp, w_router_exp, b_router_exp,
              w_exp_gate, w_exp_up, w_exp_down, ln3_g, ln3_b)
    h = x
    for l in range(DEPTH):
        h = _layer(h, mem, *[p[l] for p in params])
    return h
```
